```python
import math
import jax, jax.numpy as jnp
from jax import lax
import numpy as np

D_MODEL = 1024
BATCH = 8
SEQ = 2048
DEPTH = 1
DEC_BATCH = 128
DEC_SEQ = 1
PAST_LEN = 16384
PAGE_SIZE = 128

D_MIX = D_MODEL
CONV_CH = D_MIX // 2
CONV_WIDTH = 31
GDN_HEADS = 4
GDN_DK = 128
GDN_DV = 128
GDN_QK = GDN_HEADS * GDN_DK
GDN_V = GDN_HEADS * GDN_DV
QKV_CH = 2 * GDN_QK + GDN_V
SHORT_CONV = 4
GDN_CHUNK = 64
N_MEM = 256
X_HEADS = 4
X_HEAD_DIM = D_MODEL // X_HEADS
N_EXPERTS = 32
TOP_K = 4
D_EXPERT = D_MODEL
SWIGLU_LIMIT = 7.0
SWIGLU_ALPHA = 1.702
NORM_EPS = 1e-6
OFF_QKV = 2 * CONV_CH
OFF_Z = OFF_QKV + QKV_CH
OFF_A = OFF_Z + GDN_V
OFF_B = OFF_A + GDN_HEADS
P_IN = OFF_B + GDN_HEADS

kernel_name = 'hymba_conformer_gdn_moe_step'


def rmsnorm(x, g):
    xf = x.astype(jnp.float32)
    y = xf * lax.rsqrt(jnp.mean(xf * xf, axis=-1, keepdims=True) + NORM_EPS)
    return (y * g.astype(jnp.float32)).astype(x.dtype)


def layernorm(x, g, b):
    xf = x.astype(jnp.float32)
    mu = jnp.mean(xf, axis=-1, keepdims=True)
    var = jnp.mean(jnp.square(xf - mu), axis=-1, keepdims=True)
    y = (xf - mu) * lax.rsqrt(var + NORM_EPS) * g.astype(jnp.float32) + b.astype(jnp.float32)
    return y.astype(x.dtype)


def l2norm(x):
    return x * lax.rsqrt(jnp.sum(x * x, axis=-1, keepdims=True) + NORM_EPS)


def causal_depthwise(xh, w):
    return lax.conv_general_dilated(xh, w[:, None, :].astype(xh.dtype), (1,), 'VALID',
                                    dimension_numbers=('NWC', 'WIO', 'NWC'),
                                    feature_group_count=xh.shape[-1])


def conformer_mixer(u_glu, hist, w_dw, b_dw, ln_g, ln_b):
    glu = u_glu[..., :CONV_CH] * jax.nn.sigmoid(u_glu[..., CONV_CH:])
    xh = jnp.concatenate([hist.astype(glu.dtype), glu], axis=1)
    c = causal_depthwise(xh, w_dw) + b_dw.astype(glu.dtype)
    c = layernorm(c, ln_g, ln_b)
    return jax.nn.silu(c), xh[:, -(CONV_WIDTH - 1):]


def gdn_inputs(u, hist, w_sc, a_log, dt_bias):
    n, t, _ = u.shape
    xh = jnp.concatenate([hist.astype(u.dtype), u[..., OFF_QKV:OFF_Z]], axis=1)
    qkv = jax.nn.silu(causal_depthwise(xh, w_sc)).astype(jnp.float32)
    q = l2norm(qkv[..., :GDN_QK].reshape(n, t, GDN_HEADS, GDN_DK)) * (GDN_DK ** -0.5)
    k = l2norm(qkv[..., GDN_QK:2 * GDN_QK].reshape(n, t, GDN_HEADS, GDN_DK))
    v = qkv[..., 2 * GDN_QK:].reshape(n, t, GDN_HEADS, GDN_DV)
    a = u[..., OFF_A:OFF_B].astype(jnp.float32)
    b = u[..., OFF_B:P_IN].astype(jnp.float32)
    g = -jnp.exp(a_log.astype(jnp.float32)) * jax.nn.softplus(a + dt_bias.astype(jnp.float32))
    beta = jax.nn.sigmoid(b)
    return q, k, v, g, beta, xh[:, -(SHORT_CONV - 1):]


def gdn_chunked(q, k, v, g, beta, s0):
    n, t, h, _ = q.shape
    dv = v.shape[-1]
    nc = t // GDN_CHUNK

    def chunks(a):
        a = a.reshape((n, nc, GDN_CHUNK, h) + a.shape[3:])
        return jnp.moveaxis(jnp.moveaxis(a, 1, 0), 3, 2)

    qc, kc, vc = chunks(q), chunks(k), chunks(v)
    gc = jnp.cumsum(chunks(g), axis=-1)
    bc = chunks(beta)
    causal = jnp.tril(jnp.ones((GDN_CHUNK, GDN_CHUNK), dtype=bool))
    strict = jnp.tril(jnp.ones((GDN_CHUNK, GDN_CHUNK), dtype=bool), k=-1)
    diff = gc[..., :, None] - gc[..., None, :]
    decay = jnp.where(causal, jnp.exp(jnp.where(causal, diff, 0.0)), 0.0)
    k_beta = kc * bc[..., None]
    v_beta = vc * bc[..., None]
    lower = jnp.where(strict, jnp.einsum('...ik,...jk->...ij', k_beta, kc) * decay, 0.0)
    eye = jnp.eye(GDN_CHUNK, dtype=jnp.float32)
    t_inv = lax.linalg.triangular_solve(eye + lower, jnp.broadcast_to(eye, lower.shape),
                                        left_side=True, lower=True, unit_diagonal=True)
    u = t_inv @ v_beta
    w = t_inv @ (k_beta * jnp.exp(gc)[..., None])
    intra = jnp.where(causal, jnp.einsum('...ik,...jk->...ij', qc, kc) * decay, 0.0)

    def step(s, inp):
        q_c, k_c, u_c, w_c, g_c, a_c = inp
        v_new = u_c - w_c @ s
        o = (q_c * jnp.exp(g_c)[..., None]) @ s + a_c @ v_new
        g_last = g_c[..., -1:]
        s = s * jnp.exp(g_last)[..., None] + jnp.einsum(
            'nhck,nhcv->nhkv', k_c * jnp.exp(g_last - g_c)[..., None], v_new)
        return s, o

    s_fin, o = lax.scan(step, s0, (qc, kc, u, w, gc, intra))
    o = jnp.moveaxis(jnp.moveaxis(o, 2, 3), 0, 1).reshape(n, t, h, dv)
    return o, s_fin


def gdn_recurrent(q, k, v, g, beta, s0):
    def step(s, inp):
        q_t, k_t, v_t, g_t, b_t = inp
        s = s * jnp.exp(g_t)[..., None, None]
        v_t = (v_t - jnp.einsum('nhkv,nhk->nhv', s, k_t)) * b_t[..., None]
        s = s + jnp.einsum('nhk,nhv->nhkv', k_t, v_t)
        return s, jnp.einsum('nhkv,nhk->nhv', s, q_t)

    xs = tuple(jnp.moveaxis(a, 1, 0) for a in (q, k, v, g, beta))
    s_fin, o = lax.scan(step, s0, xs)
    return jnp.moveaxis(o, 0, 1), s_fin


def memory_kv(mem, g_mem, w_mk, w_mv):
    n = mem.shape[0]
    m = rmsnorm(mem, g_mem)
    mk = (m @ w_mk).reshape(n, N_MEM, X_HEADS, X_HEAD_DIM)
    mv = (m @ w_mv).reshape(n, N_MEM, X_HEADS, X_HEAD_DIM)
    return mk, mv


def cross_attn(h, mem_k, mem_v, w_q, w_o):
    n, t, _ = h.shape
    q = (h @ w_q).reshape(n, t, X_HEADS, X_HEAD_DIM)
    s = jnp.einsum('bthd,bmhd->bhtm', q, mem_k.astype(q.dtype)).astype(jnp.float32) * (X_HEAD_DIM ** -0.5)
    p = jax.nn.softmax(s, axis=-1).astype(h.dtype)
    o = jnp.einsum('bhtm,bmhd->bthd', p, mem_v.astype(h.dtype)).reshape(n, t, X_HEADS * X_HEAD_DIM)
    return (o @ w_o).astype(h.dtype)


def moe(h, w_r, b_r, w_gu, b_gu, w_dn, b_dn):
    n, t, d = h.shape
    ht = h.reshape(n * t, d)
    logits = (ht @ w_r + b_r).astype(jnp.float32)
    top_val, top_idx = lax.top_k(logits, TOP_K)
    gate = jax.nn.softmax(top_val, axis=-1)
    dense_gate = jnp.einsum('mk,mke->me', gate, jax.nn.one_hot(top_idx, N_EXPERTS, dtype=jnp.float32))
    out = jnp.zeros((n * t, d), jnp.float32)
    for e in range(N_EXPERTS):
        gu = ht @ w_gu[e] + b_gu[e]
        x_glu = jnp.minimum(gu[:, :D_EXPERT], SWIGLU_LIMIT)
        x_lin = jnp.clip(gu[:, D_EXPERT:], -SWIGLU_LIMIT, SWIGLU_LIMIT)
        act = x_glu * jax.nn.sigmoid(SWIGLU_ALPHA * x_glu) * (x_lin + 1.0)
        out = out + dense_gate[:, e:e + 1] * (act @ w_dn[e] + b_dn[e]).astype(jnp.float32)
    return out.reshape(n, t, d).astype(h.dtype)


def decoder_layer(x, mem_k, mem_v, conv_hist, sc_hist, s0, p, chunked):
    n, t, _ = x.shape
    h = rmsnorm(x, p['g_mix'])
    u = h @ p['w_in']
    conv_out, conv_new = conformer_mixer(u[..., :OFF_QKV] + p['b_glu'], conv_hist,
                                         p['w_dw'], p['b_dw'], p['ln_g'], p['ln_b'])
    q, k, v, g, beta, sc_new = gdn_inputs(u, sc_hist, p['w_sc'], p['a_log'], p['dt_bias'])
    if chunked:
        o, s_new = gdn_chunked(q, k, v, g, beta, s0)
    else:
        o, s_new = gdn_recurrent(q, k, v, g, beta, s0)
    z = u[..., OFF_Z:OFF_A].reshape(n, t, GDN_HEADS, GDN_DV).astype(jnp.float32)
    o = rmsnorm(o, p['g_onorm']) * jax.nn.silu(z)
    mixed = jnp.concatenate([conv_out, o.reshape(n, t, GDN_V).astype(x.dtype)], axis=-1) @ p['w_out']
    x = x + mixed
    x = x + cross_attn(rmsnorm(x, p['g_xattn']), mem_k, mem_v, p['w_xq'], p['w_xo'])
    x = x + moe(rmsnorm(x, p['g_moe']), p['w_router'], p['b_router'], p['w_gu'], p['b_gu'],
                p['w_dn'], p['b_dn'])
    return x, conv_new, sc_new, s_new


def setup_inputs(seed: int = 0) -> dict:
    key = jax.random.key(seed)
    ks = iter(jax.random.split(key, 48))

    def nrm(shape, scale):
        return scale * jax.random.normal(next(ks), shape, jnp.float32)

    def gain(shape):
        return 1.0 + nrm(shape, 0.02)

    a_log = jnp.log(jax.random.uniform(next(ks), (DEPTH, GDN_HEADS), jnp.float32, 1.0, 16.0))
    dt = jnp.exp(jax.random.uniform(next(ks), (DEPTH, GDN_HEADS), jnp.float32,
                                    math.log(1e-3), math.log(1e-1)))
    dt_bias = dt + jnp.log(-jnp.expm1(-dt))
    return {
        'x_prompt': nrm((BATCH, SEQ, D_MODEL), 1.0),
        'mem_prompt': nrm((BATCH, N_MEM, D_MODEL), 1.0),
        'x_sample': nrm((DEC_BATCH, DEC_SEQ, D_MODEL), 1.0),
        'state_conformer_conv': nrm((DEPTH, DEC_BATCH, CONV_WIDTH - 1, CONV_CH), 0.5),
        'state_gdn_conv': nrm((DEPTH, DEC_BATCH, SHORT_CONV - 1, QKV_CH), 1.0),
        'state_gdn': nrm((DEPTH, DEC_BATCH, GDN_HEADS, GDN_DK, GDN_DV), 0.2),
        'cache_mem_k': nrm((DEPTH, DEC_BATCH, N_MEM, X_HEADS, X_HEAD_DIM), 1.0),
        'cache_mem_v': nrm((DEPTH, DEC_BATCH, N_MEM, X_HEADS, X_HEAD_DIM), 1.0),
        'w_in': nrm((DEPTH, D_MODEL, P_IN), D_MODEL ** -0.5),
        'b_glu': nrm((DEPTH, 2 * CONV_CH), 0.02),
        'w_dw': nrm((DEPTH, CONV_WIDTH, CONV_CH), CONV_WIDTH ** -0.5),
        'b_dw': nrm((DEPTH, CONV_CH), 0.02),
        'ln_g': gain((DEPTH, CONV_CH)),
        'ln_b': nrm((DEPTH, CONV_CH), 0.02),
        'w_sc': nrm((DEPTH, SHORT_CONV, QKV_CH), SHORT_CONV ** -0.5),
        'a_log': a_log,
        'dt_bias': dt_bias,
        'g_onorm': gain((DEPTH, GDN_DV)),
        'w_out': nrm((DEPTH, D_MIX, D_MODEL), D_MIX ** -0.5),
        'g_mix': gain((DEPTH, D_MODEL)),
        'g_xattn': gain((DEPTH, D_MODEL)),
        'g_mem': gain((DEPTH, D_MODEL)),
        'w_xq': nrm((DEPTH, D_MODEL, D_MODEL), D_MODEL ** -0.5),
        'w_mk': nrm((DEPTH, D_MODEL, D_MODEL), D_MODEL ** -0.5),
        'w_mv': nrm((DEPTH, D_MODEL, D_MODEL), D_MODEL ** -0.5),
        'w_xo': nrm((DEPTH, D_MODEL, D_MODEL), D_MODEL ** -0.5),
        'g_moe': gain((DEPTH, D_MODEL)),
        'w_router': nrm((DEPTH, D_MODEL, N_EXPERTS), D_MODEL ** -0.5),
        'b_router': nrm((DEPTH, N_EXPERTS), 0.01),
        'w_gu': nrm((DEPTH, N_EXPERTS, D_MODEL, 2 * D_EXPERT), D_MODEL ** -0.5),
        'b_gu': nrm((DEPTH, N_EXPERTS, 2 * D_EXPERT), 0.01),
        'w_dn': nrm((DEPTH, N_EXPERTS, D_EXPERT, D_MODEL), D_EXPERT ** -0.5),
        'b_dn': nrm((DEPTH, N_EXPERTS, D_MODEL), 0.01),
        'g_final': gain((D_MODEL,)),
    }


def reference(x_prompt, mem_prompt, x_sample, state_conformer_conv, state_gdn_conv, state_gdn,
              cache_mem_k, cache_mem_v, w_in, b_glu, w_dw, b_dw, ln_g, ln_b, w_sc, a_log, dt_bias,
              g_onorm, w_out, g_mix, g_xattn, g_mem, w_xq, w_mk, w_mv, w_xo, g_moe, w_router,
              b_router, w_gu, b_gu, w_dn, b_dn, g_final):
    yp, ys = x_prompt, x_sample
    nb = x_prompt.shape[0]
    pc, pg, ps, pk, pv, sc, sg, ss = [], [], [], [], [], [], [], []
    for l in range(DEPTH):
        p = {'w_in': w_in[l], 'b_glu': b_glu[l], 'w_dw': w_dw[l], 'b_dw': b_dw[l],
             'ln_g': ln_g[l], 'ln_b': ln_b[l], 'w_sc': w_sc[l], 'a_log': a_log[l],
             'dt_bias': dt_bias[l], 'g_onorm': g_onorm[l], 'w_out': w_out[l], 'g_mix': g_mix[l],
             'g_xattn': g_xattn[l], 'w_xq': w_xq[l], 'w_xo': w_xo[l], 'g_moe': g_moe[l],
             'w_router': w_router[l], 'b_router': b_router[l], 'w_gu': w_gu[l], 'b_gu': b_gu[l],
             'w_dn': w_dn[l], 'b_dn': b_dn[l]}
        mk, mv = memory_kv(mem_prompt, g_mem[l], w_mk[l], w_mv[l])
        yp, c_new, g_new, s_new = decoder_layer(
            yp, mk, mv,
            jnp.zeros((nb, CONV_WIDTH - 1, CONV_CH), yp.dtype),
            jnp.zeros((nb, SHORT_CONV - 1, QKV_CH), yp.dtype),
            jnp.zeros((nb, GDN_HEADS, GDN_DK, GDN_DV), jnp.float32),
            p, True)
        pc.append(c_new)
        pg.append(g_new)
        ps.append(s_new.astype(x_prompt.dtype))
        pk.append(mk)
        pv.append(mv)
        ys, c2, g2, s2 = decoder_layer(
            ys, cache_mem_k[l], cache_mem_v[l], state_conformer_conv[l], state_gdn_conv[l],
            state_gdn[l].astype(jnp.float32), p, False)
        sc.append(c2.astype(state_conformer_conv.dtype))
        sg.append(g2.astype(state_gdn_conv.dtype))
        ss.append(s2.astype(state_gdn.dtype))
    y_prompt = rmsnorm(yp, g_final)
    y_sample = rmsnorm(ys, g_final)
    return (y_prompt, y_sample, jnp.stack(pc), jnp.stack(pg), jnp.stack(ps), jnp.stack(pk),
            jnp.stack(pv), jnp.stack(sc), jnp.stack(sg), jnp.stack(ss))
```

```python
import functools

import jax
import jax.numpy as jnp
from jax import lax
from jax.experimental import pallas as pl
from jax.experimental.pallas import tpu as pltpu

F32, BF16, I32 = jnp.float32, jnp.bfloat16, jnp.int32

D_MODEL = 1024
CONV_CH = 512
CONV_WIDTH = 31
GDN_HEADS = 4
GDN_DK = 128
GDN_V = 512
QKV_CH = 1536
SHORT_CONV = 4
N_MEM = 256
X_HEADS = 4
X_HEAD_DIM = 256
N_EXPERTS = 32
TOP_K = 4
D_EXPERT = 1024
SWIGLU_LIMIT = 7.0
SWIGLU_ALPHA = 1.702
NORM_EPS = 1e-6
OFF_QKV = 2 * CONV_CH
OFF_Z = OFF_QKV + QKV_CH
OFF_A = OFF_Z + GDN_V

LANES = 128
SUBLANES = 8
ROW_TILES = D_MODEL // LANES
GDN_BLOCK = 128
TOKEN_TILE = 256
MOE_TILE = 256
COMBINE_TILE = 128
FLAT_BITS = 17
VMEM_LIMIT = 48 * 1024 * 1024


def _cparams(sem, vmem=VMEM_LIMIT):
    return pltpu.CompilerParams(dimension_semantics=sem, vmem_limit_bytes=vmem)


def _mm(a, b):
    return jnp.dot(a.astype(BF16), b.astype(BF16), preferred_element_type=F32)


def _mm_nt(a, b):
    return lax.dot_general(a.astype(BF16), b.astype(BF16), (((1,), (1,)), ((), ())),
                           preferred_element_type=F32)


def _mm_tn(a, b):
    return lax.dot_general(a.astype(BF16), b.astype(BF16), (((0,), (0,)), ((), ())),
                           preferred_element_type=F32)


def _rms(x, g):
    return x * lax.rsqrt(jnp.mean(x * x, axis=-1, keepdims=True) + NORM_EPS) * g


def _silu(x):
    return x * jax.nn.sigmoid(x)


def _full(shape):
    return pl.BlockSpec(shape, lambda *_: (0,) * len(shape))


def _project(x, gmix, w_ref, wab_ref, bglu):
    h = _rms(x, gmix).astype(BF16)
    u_glu = jnp.dot(h, w_ref[:, 0:OFF_QKV], preferred_element_type=F32) + bglu
    glu = u_glu[:, :CONV_CH] * jax.nn.sigmoid(u_glu[:, CONV_CH:])
    qkv_pre = jnp.dot(h, w_ref[:, OFF_QKV:OFF_Z], preferred_element_type=F32)
    z = jnp.dot(h, w_ref[:, OFF_Z:OFF_A], preferred_element_type=F32)
    uab = jnp.dot(h, wab_ref[...], preferred_element_type=F32)
    return glu, qkv_pre, z, uab


def _gate_beta(uab, cst):
    lane = lax.broadcasted_iota(I32, uab.shape, 1)
    g = -jnp.exp(cst[0:1, :]) * jax.nn.softplus(uab + cst[1:2, :])
    return jnp.where(lane < GDN_HEADS, g, jax.nn.sigmoid(uab))


def _conv_post(c, b_dw, ln_g, ln_b):
    c = c + b_dw
    mu = jnp.mean(c, axis=-1, keepdims=True)
    d = c - mu
    var = jnp.mean(d * d, axis=-1, keepdims=True)
    return _silu(d * lax.rsqrt(var + NORM_EPS) * ln_g + ln_b)


def _qkv_post(cs):
    a = _silu(cs)
    parts = []
    for h in range(2 * GDN_HEADS):
        seg = a[:, h * GDN_DK:(h + 1) * GDN_DK]
        n = seg * lax.rsqrt(jnp.sum(seg * seg, axis=-1, keepdims=True) + NORM_EPS)
        if h < GDN_HEADS:
            n = n * (GDN_DK ** -0.5)
        parts.append(n)
    q = jnp.concatenate(parts[:GDN_HEADS], axis=1)
    k = jnp.concatenate(parts[GDN_HEADS:], axis=1)
    return q, k, a[:, 2 * GDN_HEADS * GDN_DK:]


def _mix_out(conv_b, o, z, gon, wout_ref, x):
    parts = []
    for h in range(GDN_HEADS):
        oh = o[:, h * 128:(h + 1) * 128]
        parts.append(oh * lax.rsqrt(jnp.mean(oh * oh, axis=-1, keepdims=True) + NORM_EPS) * gon)
    on = jnp.concatenate(parts, axis=1) * _silu(z)
    mixed = (jnp.dot(conv_b, wout_ref[0:CONV_CH, :], preferred_element_type=F32)
             + jnp.dot(on.astype(BF16), wout_ref[CONV_CH:, :], preferred_element_type=F32))
    return x + mixed


def _route(x2, gmoe, wr_ref, br, carry):
    m = x2.shape[0]
    h3 = _rms(x2, gmoe)
    h_hi = h3.astype(BF16)
    r1 = h3 - h_hi.astype(F32)
    h_mid = r1.astype(BF16)
    h_lo = (r1 - h_mid.astype(F32)).astype(BF16)
    w = wr_ref[...]
    w_hi = w.astype(BF16)
    w_lo = (w - w_hi.astype(F32)).astype(BF16)
    logits = (jnp.dot(h_hi, w_hi, preferred_element_type=F32)
              + jnp.dot(h_hi, w_lo, preferred_element_type=F32)
              + jnp.dot(h_mid, w_hi, preferred_element_type=F32)
              + jnp.dot(h_lo, w_hi, preferred_element_type=F32)) + br
    lane = lax.broadcasted_iota(I32, (m, LANES), 1)
    neg = jnp.float32(-jnp.inf)
    work = jnp.where(lane < N_EXPERTS, logits, neg)
    vals, idxs = [], []
    for _ in range(TOP_K):
        mx = jnp.max(work, axis=-1, keepdims=True)
        ix = jnp.min(jnp.where(work == mx, lane, LANES), axis=-1, keepdims=True)
        vals.append(mx)
        idxs.append(ix)
        work = jnp.where(lane == ix, neg, work)
    es = [jnp.exp(v - vals[0]) for v in vals]
    den = es[0] + es[1] + es[2] + es[3]
    gates = [e / den for e in es]
    sel = jnp.zeros((m, LANES), F32)
    for ix in idxs:
        sel = sel + jnp.where(lane == ix, 1.0, 0.0)
    row = lax.broadcasted_iota(I32, (m, m), 0)
    col = lax.broadcasted_iota(I32, (m, m), 1)
    before = jnp.where(row > col, 1.0, 0.0).astype(BF16)
    rank_full = jnp.dot(before, sel.astype(BF16), preferred_element_type=F32) + carry
    route = jnp.zeros((m, LANES), F32)
    for r in range(TOP_K):
        rk = jnp.sum(jnp.where(lane == idxs[r], rank_full, 0.0), axis=-1, keepdims=True)
        route = (route + jnp.where(lane == r, idxs[r].astype(F32), 0.0)
                 + jnp.where(lane == TOP_K + r, gates[r], 0.0)
                 + jnp.where(lane == 2 * TOP_K + r, rk, 0.0))
    new_carry = carry + jnp.sum(sel, axis=0, keepdims=True)
    return h3, route, new_carry


def _store_rows(h3r_ref, h3):
    for s in range(ROW_TILES):
        h3r_ref[:, s, :] = h3[:, s * LANES:(s + 1) * LANES]


def _load_rows(ref):
    return jnp.concatenate([ref[:, s, :] for s in range(ROW_TILES)], axis=1)


def _mem_kv_kernel(mem_ref, g_ref, wk_ref, wv_ref, mk_ref, mv_ref, mkb_ref, mvb_ref):
    m = _rms(mem_ref[...], g_ref[...]).astype(BF16)
    mk = jnp.dot(m, wk_ref[...], preferred_element_type=F32)
    mv = jnp.dot(m, wv_ref[...], preferred_element_type=F32)
    mk_ref[...] = mk
    mv_ref[...] = mv
    mkb_ref[...] = mk.astype(BF16)
    mvb_ref[...] = mv.astype(BF16)


def _mem_kv(mem, g_mem, wk_b, wv_b):
    rows = mem.shape[0]
    tm = TOKEN_TILE
    row_spec = pl.BlockSpec((tm, D_MODEL), lambda i: (i, 0))
    return pl.pallas_call(
        _mem_kv_kernel,
        grid=(rows // tm,),
        in_specs=[row_spec, _full((1, D_MODEL)), _full((D_MODEL, D_MODEL)), _full((D_MODEL, D_MODEL))],
        out_specs=[row_spec] * 4,
        out_shape=[jax.ShapeDtypeStruct((rows, D_MODEL), F32)] * 2
        + [jax.ShapeDtypeStruct((rows, D_MODEL), BF16)] * 2,
        compiler_params=_cparams(("arbitrary",)),
        name="mem_kv",
    )(mem, g_mem, wk_b, wv_b)


CONV_HALO = 32
SC_HALO = 8


def _pre_prompt_kernel(x_ref, gmix_ref, w_ref, wab_ref, bglu_ref, wdw_ref, bdw_ref, lng_ref, lnb_ref,
                       wsc_ref, cst_ref, conv_ref, q_ref, k_ref, v_ref, z_ref, gb_ref, cstate_ref,
                       sstate_ref, cbuf, sbuf, *, tm):
    j = pl.program_id(1)

    @pl.when(j == 0)
    def _():
        cbuf[0:CONV_HALO, :] = jnp.zeros((CONV_HALO, CONV_CH), F32)
        sbuf[0:SC_HALO, :] = jnp.zeros((SC_HALO, QKV_CH), F32)

    glu, qkv_pre, z, uab = _project(x_ref[...], gmix_ref[...], w_ref, wab_ref, bglu_ref[...])
    cbuf[CONV_HALO:CONV_HALO + tm, :] = glu
    sbuf[SC_HALO:SC_HALO + tm, :] = qkv_pre
    z_ref[...] = z
    gb_ref[...] = _gate_beta(uab, cst_ref[...])

    base = CONV_HALO - (CONV_WIDTH - 1)
    acc = wdw_ref[0:1, :] * cbuf[pl.ds(base, tm), :]
    for t in range(1, CONV_WIDTH):
        acc = acc + wdw_ref[t:t + 1, :] * cbuf[pl.ds(base + t, tm), :]
    conv_ref[...] = _conv_post(acc, bdw_ref[...], lng_ref[...], lnb_ref[...]).astype(BF16)

    sbase = SC_HALO - (SHORT_CONV - 1)
    cs = wsc_ref[0:1, :] * sbuf[pl.ds(sbase, tm), :]
    for t in range(1, SHORT_CONV):
        cs = cs + wsc_ref[t:t + 1, :] * sbuf[pl.ds(sbase + t, tm), :]
    q, k, v = _qkv_post(cs)
    q_ref[...] = q
    k_ref[...] = k
    v_ref[...] = v

    @pl.when(j == pl.num_programs(1) - 1)
    def _():
        cstate_ref[0] = cbuf[pl.ds(CONV_HALO + tm - (CONV_WIDTH - 1), CONV_WIDTH - 1), :]
        sstate_ref[0] = sbuf[pl.ds(SC_HALO + tm - (SHORT_CONV - 1), SHORT_CONV - 1), :]

    cbuf[0:CONV_HALO, :] = cbuf[tm:tm + CONV_HALO, :]
    sbuf[0:SC_HALO, :] = sbuf[tm:tm + SC_HALO, :]


def _pre_prompt(x2d, batch, seq, wts):
    tm = TOKEN_TILE
    nj = seq // tm
    rows = batch * seq

    def tok(width):
        return pl.BlockSpec((tm, width), lambda b, j: (b * nj + j, 0))

    in_specs = [tok(D_MODEL), _full((1, D_MODEL)), _full((D_MODEL, OFF_A)), _full((D_MODEL, LANES)),
                _full((1, OFF_QKV)), _full((32, CONV_CH)), _full((1, CONV_CH)), _full((1, CONV_CH)),
                _full((1, CONV_CH)), _full((8, QKV_CH)), _full((8, LANES))]
    out_specs = [tok(CONV_CH), tok(GDN_V), tok(GDN_V), tok(GDN_V), tok(GDN_V), tok(LANES),
                 pl.BlockSpec((1, CONV_WIDTH - 1, CONV_CH), lambda b, j: (b, 0, 0)),
                 pl.BlockSpec((1, SHORT_CONV - 1, QKV_CH), lambda b, j: (b, 0, 0))]
    out_shape = [jax.ShapeDtypeStruct((rows, CONV_CH), BF16)] \
        + [jax.ShapeDtypeStruct((rows, GDN_V), F32)] * 4 \
        + [jax.ShapeDtypeStruct((rows, LANES), F32),
           jax.ShapeDtypeStruct((batch, CONV_WIDTH - 1, CONV_CH), F32),
           jax.ShapeDtypeStruct((batch, SHORT_CONV - 1, QKV_CH), F32)]
    return pl.pallas_call(
        functools.partial(_pre_prompt_kernel, tm=tm),
        grid=(batch, nj),
        in_specs=in_specs,
        out_specs=out_specs,
        out_shape=out_shape,
        scratch_shapes=[pltpu.VMEM((CONV_HALO + tm, CONV_CH), F32),
                        pltpu.VMEM((SC_HALO + tm, QKV_CH), F32)],
        compiler_params=_cparams(("arbitrary", "arbitrary")),
        name="pre_prompt",
    )(x2d, wts["g_mix"], wts["w_in_b"], wts["w_ab_b"], wts["b_glu"], wts["w_dw"], wts["b_dw"],
      wts["ln_g"], wts["ln_b"], wts["w_sc"], wts["gdn_cst"])


def _gdn_prompt_kernel(q_ref, k_ref, v_ref, gb_ref, o_ref, sfin_ref, s_scr):
    c = pl.program_id(1)
    n = GDN_BLOCK

    @pl.when(c == 0)
    def _():
        s_scr[...] = jnp.zeros(s_scr.shape, F32)

    gb = gb_ref[...]
    row = lax.broadcasted_iota(I32, (n, n), 0)
    col = lax.broadcasted_iota(I32, (n, n), 1)
    causal = row >= col
    strict = row > col
    tri = jnp.where(causal, 1.0, 0.0).astype(BF16)
    eye = jnp.where(row == col, 1.0, 0.0)
    level_masks = []
    b = 1
    while b < n:
        same_pair = ((row ^ col) & ~(2 * b - 1)) == 0
        level_masks.append(same_pair & ((row & b) != 0) & ((col & b) == 0))
        b *= 2
    g1 = gb.astype(BF16)
    r1 = gb - g1.astype(F32)
    g2 = r1.astype(BF16)
    g3 = (r1 - g2.astype(F32)).astype(BF16)
    gcum = (jnp.dot(tri, g1, preferred_element_type=F32) + jnp.dot(tri, g2, preferred_element_type=F32)
            + jnp.dot(tri, g3, preferred_element_type=F32))
    gcum_t = gcum.T
    egc = jnp.exp(gcum)
    for h in range(GDN_HEADS):
        sl = slice(h * GDN_DK, (h + 1) * GDN_DK)
        qh, kh, vh = q_ref[:, sl], k_ref[:, sl], v_ref[:, sl]
        gcol = gcum[:, h:h + 1]
        grow = gcum_t[h:h + 1, :]
        beta = gb[:, GDN_HEADS + h:GDN_HEADS + h + 1]
        ecol = egc[:, h:h + 1]
        decay = jnp.where(causal, jnp.exp(jnp.where(causal, gcol - grow, 0.0)), 0.0)
        kb = kh * beta
        vb = vh * beta
        lower = jnp.where(strict, _mm_nt(kb, kh) * decay, 0.0)
        x = eye - jnp.where(level_masks[0], lower, 0.0)
        for mask in level_masks[1:]:
            x = x - _mm(_mm(x, jnp.where(mask, lower, 0.0)), x)
        u = _mm(x, vb)
        w = _mm(x, kb * ecol)
        intra = jnp.where(causal, _mm_nt(qh, kh) * decay, 0.0)
        s = s_scr[h]
        v_new = u - _mm(w, s)
        o_ref[:, sl] = _mm(qh * ecol, s) + _mm(intra, v_new)
        glast = gcum[n - 1:n, h:h + 1]
        s_scr[h] = s * jnp.exp(glast) + _mm_tn(kh * jnp.exp(glast - gcol), v_new)

    @pl.when(c == pl.num_programs(1) - 1)
    def _():
        sfin_ref[0] = s_scr[...]


def _gdn_prompt(q, k, v, gb, batch, seq):
    n = GDN_BLOCK
    nc = seq // n

    def tok(width):
        return pl.BlockSpec((n, width), lambda b, c: (b * nc + c, 0))

    return pl.pallas_call(
        _gdn_prompt_kernel,
        grid=(batch, nc),
        in_specs=[tok(GDN_V), tok(GDN_V), tok(GDN_V), tok(LANES)],
        out_specs=[tok(GDN_V), pl.BlockSpec((1, GDN_HEADS, GDN_DK, GDN_DK), lambda b, c: (b, 0, 0, 0))],
        out_shape=[jax.ShapeDtypeStruct((batch * seq, GDN_V), F32),
                   jax.ShapeDtypeStruct((batch, GDN_HEADS, GDN_DK, GDN_DK), F32)],
        scratch_shapes=[pltpu.VMEM((GDN_HEADS, GDN_DK, GDN_DK), F32)],
        compiler_params=_cparams(("arbitrary", "arbitrary")),
        name="gdn_prompt",
    )(q, k, v, gb)


def _post_prompt_kernel(x_ref, conv_ref, o_ref, z_ref, gon_ref, wout_ref, gx_ref, wq_ref, mk_ref, mv_ref,
                        wo_ref, gmoe_ref, wr_ref, br_ref, h3s_ref, x2_ref, h3r_ref, route_ref, cnt_ref,
                        carry, *, n_steps):
    step = pl.program_id(0)

    @pl.when(step == 0)
    def _():
        carry[...] = jnp.zeros(carry.shape, F32)

    @pl.when(step < n_steps)
    def _():
        x1 = _mix_out(conv_ref[...], o_ref[...], z_ref[...], gon_ref[...], wout_ref, x_ref[...])
        qx = jnp.dot(_rms(x1, gx_ref[...]).astype(BF16), wq_ref[...], preferred_element_type=F32)
        outs = []
        for h in range(X_HEADS):
            sl = slice(h * X_HEAD_DIM, (h + 1) * X_HEAD_DIM)
            s = _mm_nt(qx[:, sl], mk_ref[:, sl]) * (X_HEAD_DIM ** -0.5)
            e = jnp.exp(s - jnp.max(s, axis=-1, keepdims=True))
            p = e / jnp.sum(e, axis=-1, keepdims=True)
            outs.append(jnp.dot(p.astype(BF16), mv_ref[:, sl], preferred_element_type=F32))
        att = jnp.concatenate(outs, axis=1)
        x2 = x1 + jnp.dot(att.astype(BF16), wo_ref[...], preferred_element_type=F32)
        x2_ref[...] = x2
        h3, route, new_carry = _route(x2, gmoe_ref[...], wr_ref, br_ref[...], carry[0:1, :])
        _store_rows(h3r_ref, h3)
        route_ref[...] = route
        carry[0:1, :] = new_carry
        cnt_ref[...] = jnp.broadcast_to(new_carry, cnt_ref.shape)

    @pl.when(step == n_steps)
    def _():
        h3r_ref[0:h3s_ref.shape[0]] = h3s_ref[...]


def _post_prompt(x2d, conv, o, z, mk_b, mv_b, h3_sample, batch, seq, wts):
    tm = TOKEN_TILE
    nj = seq // tm
    rows = batch * seq
    n_steps = batch * nj
    n_s = h3_sample.shape[0]
    assert n_s <= tm

    def tok(width):
        return pl.BlockSpec((tm, width), lambda s: (jnp.minimum(s, n_steps - 1), 0))

    mem_spec = pl.BlockSpec((N_MEM, D_MODEL), lambda s: (jnp.minimum(s, n_steps - 1) // nj, 0))
    sq = _full((D_MODEL, D_MODEL))
    in_specs = [tok(D_MODEL), tok(CONV_CH), tok(GDN_V), tok(GDN_V), _full((1, GDN_DK)), sq,
                _full((1, D_MODEL)), sq, mem_spec, mem_spec, sq, _full((1, D_MODEL)),
                _full((D_MODEL, LANES)), _full((1, LANES)), _full(h3_sample.shape)]
    out_specs = [tok(D_MODEL),
                 pl.BlockSpec((tm, ROW_TILES, LANES), lambda s: (s, 0, 0)),
                 tok(LANES), _full((SUBLANES, LANES))]
    out_shape = [jax.ShapeDtypeStruct((rows, D_MODEL), F32),
                 jax.ShapeDtypeStruct((rows + n_s, ROW_TILES, LANES), F32),
                 jax.ShapeDtypeStruct((rows, LANES), F32),
                 jax.ShapeDtypeStruct((SUBLANES, LANES), F32)]
    return pl.pallas_call(
        functools.partial(_post_prompt_kernel, n_steps=n_steps),
        grid=(n_steps + 1,),
        in_specs=in_specs,
        out_specs=out_specs,
        out_shape=out_shape,
        scratch_shapes=[pltpu.VMEM((SUBLANES, LANES), F32)],
        compiler_params=_cparams(("arbitrary",)),
        name="post_prompt",
    )(x2d, conv, o, z, wts["g_onorm"], wts["w_out_b"], wts["g_xattn"], wts["w_xq_b"], mk_b, mv_b,
      wts["w_xo_b"], wts["g_moe"], wts["w_router"], wts["b_router"], h3_sample)


def _pre_sample_kernel(x_ref, gmix_ref, w_ref, wab_ref, bglu_ref, wdw_ref, bdw_ref, lng_ref, lnb_ref,
                       wsc_ref, cst_ref, chist_ref, shist_ref, conv_ref, q_ref, k_ref, v_ref, z_ref,
                       gb_ref, cnew_ref, snew_ref):
    glu, qkv_pre, z, uab = _project(x_ref[...], gmix_ref[...], w_ref, wab_ref, bglu_ref[...])
    z_ref[...] = z
    gb_ref[...] = _gate_beta(uab, cst_ref[...])
    kw = CONV_WIDTH
    acc = wdw_ref[kw - 1:kw, :] * glu
    for t in range(kw - 1):
        row = chist_ref[:, t, :]
        acc = acc + wdw_ref[t:t + 1, :] * row
        if t >= 1:
            cnew_ref[:, t - 1, :] = row
    cnew_ref[:, kw - 2, :] = glu
    conv_ref[...] = _conv_post(acc, bdw_ref[...], lng_ref[...], lnb_ref[...]).astype(BF16)
    ks = SHORT_CONV
    cs = wsc_ref[ks - 1:ks, :] * qkv_pre
    for t in range(ks - 1):
        row = shist_ref[:, t, :]
        cs = cs + wsc_ref[t:t + 1, :] * row
        if t >= 1:
            snew_ref[:, t - 1, :] = row
    snew_ref[:, ks - 2, :] = qkv_pre
    q, k, v = _qkv_post(cs)
    q_ref[...] = q
    k_ref[...] = k
    v_ref[...] = v


PRE_SAMPLE_TOKENS = 32


def _pre_sample(xs, chist, shist, wts):
    n = xs.shape[0]
    tb = min(PRE_SAMPLE_TOKENS, n)

    def tok(width):
        return pl.BlockSpec((tb, width), lambda i: (i, 0))

    def hist(a):
        return pl.BlockSpec((tb,) + a.shape[1:], lambda i: (i, 0, 0))

    consts = (wts["g_mix"], wts["w_in_b"], wts["w_ab_b"], wts["b_glu"], wts["w_dw"], wts["b_dw"],
              wts["ln_g"], wts["ln_b"], wts["w_sc"], wts["gdn_cst"])
    return pl.pallas_call(
        _pre_sample_kernel,
        grid=(n // tb,),
        in_specs=[tok(D_MODEL)] + [_full(a.shape) for a in consts] + [hist(chist), hist(shist)],
        out_specs=[tok(CONV_CH), tok(GDN_V), tok(GDN_V), tok(GDN_V), tok(GDN_V), tok(LANES),
                   hist(chist), hist(shist)],
        out_shape=[jax.ShapeDtypeStruct((n, CONV_CH), BF16)]
        + [jax.ShapeDtypeStruct((n, GDN_V), F32)] * 4
        + [jax.ShapeDtypeStruct((n, LANES), F32), jax.ShapeDtypeStruct(chist.shape, F32),
           jax.ShapeDtypeStruct(shist.shape, F32)],
        compiler_params=_cparams(("arbitrary",)),
        name="pre_sample",
    )(xs, *consts, chist, shist)


GDN_STEP_TOKENS = 8


def _gdn_sample_kernel(q_ref, k_ref, v_ref, gb_ref, s_ref, o_ref, snew_ref):
    n = GDN_DK
    for i in range(GDN_STEP_TOKENS):
        for h in range(GDN_HEADS):
            sl = slice(h * GDN_DK, (h + 1) * GDN_DK)
            qrow = q_ref[i:i + 1, sl]
            krow = k_ref[i:i + 1, sl]
            vrow = v_ref[i:i + 1, sl]
            g = gb_ref[i:i + 1, h:h + 1]
            beta = gb_ref[i:i + 1, GDN_HEADS + h:GDN_HEADS + h + 1]
            kcol = jnp.broadcast_to(krow, (n, n)).T
            qcol = jnp.broadcast_to(qrow, (n, n)).T
            s1 = s_ref[i, h] * jnp.exp(g)
            sk = jnp.sum(s1 * kcol, axis=0, keepdims=True)
            vt = (vrow - sk) * beta
            s2 = s1 + kcol * vt
            snew_ref[i, h] = s2
            o_ref[i:i + 1, sl] = jnp.sum(s2 * qcol, axis=0, keepdims=True)


def _gdn_sample(q, k, v, gb, state):
    n = q.shape[0]
    tb = GDN_STEP_TOKENS

    def tok(width):
        return pl.BlockSpec((tb, width), lambda i: (i, 0))

    st = pl.BlockSpec((tb, GDN_HEADS, GDN_DK, GDN_DK), lambda i: (i, 0, 0, 0))
    return pl.pallas_call(
        _gdn_sample_kernel,
        grid=(n // tb,),
        in_specs=[tok(GDN_V), tok(GDN_V), tok(GDN_V), tok(LANES), st],
        out_specs=[tok(GDN_V), st],
        out_shape=[jax.ShapeDtypeStruct((n, GDN_V), F32), jax.ShapeDtypeStruct(state.shape, F32)],
        compiler_params=_cparams(("arbitrary",)),
        name="gdn_sample",
    )(q, k, v, gb, state)


def _mix_sample_kernel(x_ref, conv_ref, o_ref, z_ref, gon_ref, wout_ref, gx_ref, wq_ref, x1_ref, qx_ref):
    x1 = _mix_out(conv_ref[...], o_ref[...], z_ref[...], gon_ref[...], wout_ref, x_ref[...])
    x1_ref[...] = x1
    qx_ref[...] = jnp.dot(_rms(x1, gx_ref[...]).astype(BF16), wq_ref[...], preferred_element_type=F32)


def _mix_sample(xs, conv, o, z, wts):
    n = xs.shape[0]
    in_arrays = (xs, conv, o, z, wts["g_onorm"], wts["w_out_b"], wts["g_xattn"], wts["w_xq_b"])
    return pl.pallas_call(
        _mix_sample_kernel,
        grid=(1,),
        in_specs=[_full(a.shape) for a in in_arrays],
        out_specs=[_full((n, D_MODEL))] * 2,
        out_shape=[jax.ShapeDtypeStruct((n, D_MODEL), F32)] * 2,
        compiler_params=_cparams(("arbitrary",)),
        name="mix_sample",
    )(*in_arrays)


ATTN_STEP_TOKENS = 4


def _attn_sample_kernel(qx_ref, ck_ref, cv_ref, att_ref):
    for i in range(ATTN_STEP_TOKENS):
        qrow = qx_ref[0, i:i + 1, :]
        prod = ck_ref[i] * qrow
        parts = []
        for h in range(X_HEADS):
            sl = slice(h * X_HEAD_DIM, (h + 1) * X_HEAD_DIM)
            s = jnp.sum(prod[:, sl], axis=-1, keepdims=True) * (X_HEAD_DIM ** -0.5)
            e = jnp.exp(s - jnp.max(s, axis=0, keepdims=True))
            p = e / jnp.sum(e, axis=0, keepdims=True)
            parts.append(jnp.sum(p * cv_ref[i, :, sl], axis=0, keepdims=True))
        att_ref[0, i:i + 1, :] = jnp.concatenate(parts, axis=1)


def _attn_sample(qx, ck, cv):
    n = qx.shape[0]
    tb = ATTN_STEP_TOKENS
    q3 = qx.reshape(n // tb, tb, D_MODEL)
    qspec = pl.BlockSpec((1, tb, D_MODEL), lambda i: (i, 0, 0))
    cspec = pl.BlockSpec((tb, N_MEM, D_MODEL), lambda i: (i, 0, 0))
    out = pl.pallas_call(
        _attn_sample_kernel,
        grid=(n // tb,),
        in_specs=[qspec, cspec, cspec],
        out_specs=qspec,
        out_shape=jax.ShapeDtypeStruct(q3.shape, F32),
        compiler_params=_cparams(("arbitrary",)),
        name="attn_sample",
    )(q3, ck, cv)
    return out.reshape(n, D_MODEL)


def _route_sample_kernel(x1_ref, att_ref, wo_ref, gmoe_ref, wr_ref, br_ref, x2_ref, h3r_ref, route_ref,
                         cnt_ref):
    x2 = x1_ref[...] + jnp.dot(att_ref[...].astype(BF16), wo_ref[...], preferred_element_type=F32)
    x2_ref[...] = x2
    h3, route, counts = _route(x2, gmoe_ref[...], wr_ref, br_ref[...], jnp.zeros((1, LANES), F32))
    _store_rows(h3r_ref, h3)
    route_ref[...] = route
    cnt_ref[...] = jnp.broadcast_to(counts, cnt_ref.shape)


def _route_sample(x1, att, wts):
    n = x1.shape[0]
    in_arrays = (x1, att, wts["w_xo_b"], wts["g_moe"], wts["w_router"], wts["b_router"])
    shapes = [(n, D_MODEL), (n, ROW_TILES, LANES), (n, LANES), (SUBLANES, LANES)]
    return pl.pallas_call(
        _route_sample_kernel,
        grid=(1,),
        in_specs=[_full(a.shape) for a in in_arrays],
        out_specs=[_full(s) for s in shapes],
        out_shape=[jax.ShapeDtypeStruct(s, F32) for s in shapes],
        compiler_params=_cparams(("arbitrary",)),
        name="route_sample",
    )(*in_arrays)


def _moe_kernel(te_ref, tq_ref, nt_ref, src_ref, h3r_ref, wgu_ref, bgu_ref, wdn_ref, bdn_ref,
                ys_ref, xbuf, sems, wgu_b, wdn_b, *, tm):
    i = pl.program_id(0)
    total = nt_ref[0]

    def row_copy(tile, slot, r):
        tok = src_ref[tq_ref[tile] + r]
        return pltpu.make_async_copy(h3r_ref.at[tok], xbuf.at[slot, r], sems.at[slot])

    def issue(tile, slot):
        def body(r, carry):
            row_copy(tile, slot, r).start()
            return carry
        lax.fori_loop(0, tm, body, 0)

    def drain(tile, slot):
        def body(r, carry):
            row_copy(tile, slot, r).wait()
            return carry
        lax.fori_loop(0, tm, body, 0)

    @pl.when(i == 0)
    def _():
        issue(0, 0)

    @pl.when(i + 1 < total)
    def _():
        issue(i + 1, (i + 1) % 2)

    @pl.when(i < total)
    def _():
        prev = te_ref[jnp.maximum(i - 1, 0)]
        fresh = jnp.logical_or(i == 0, te_ref[i] != prev)

        @pl.when(fresh)
        def _():
            wgu_b[...] = wgu_ref[0].astype(BF16)
            wdn_b[...] = wdn_ref[0].astype(BF16)

        slot = i % 2
        drain(i, slot)
        x = _load_rows(xbuf.at[slot]).astype(BF16)
        gu = jnp.dot(x, wgu_b[...], preferred_element_type=F32) + bgu_ref[0]
        x_glu = jnp.minimum(gu[:, :D_EXPERT], SWIGLU_LIMIT)
        x_lin = jnp.clip(gu[:, D_EXPERT:], -SWIGLU_LIMIT, SWIGLU_LIMIT)
        act = x_glu * jax.nn.sigmoid(SWIGLU_ALPHA * x_glu) * (x_lin + 1.0)
        y = jnp.dot(act.astype(BF16), wdn_b[...], preferred_element_type=F32) + bdn_ref[0]
        _store_rows(ys_ref, y)

    @pl.when(i >= total)
    def _():
        ys_ref[...] = jnp.zeros(ys_ref.shape, F32)


def _moe(tile_e, tile_q0, n_tiles, src_tok, h3r, w_gu, b_gu, w_dn, b_dn, max_tiles):
    tm = MOE_TILE
    grid_spec = pltpu.PrefetchScalarGridSpec(
        num_scalar_prefetch=4,
        grid=(max_tiles,),
        in_specs=[
            pl.BlockSpec(memory_space=pl.ANY),
            pl.BlockSpec((1, D_MODEL, 2 * D_EXPERT), lambda i, te, tq, nt, src: (te[i], 0, 0)),
            pl.BlockSpec((1, 1, 2 * D_EXPERT), lambda i, te, tq, nt, src: (te[i], 0, 0)),
            pl.BlockSpec((1, D_EXPERT, D_MODEL), lambda i, te, tq, nt, src: (te[i], 0, 0)),
            pl.BlockSpec((1, 1, D_MODEL), lambda i, te, tq, nt, src: (te[i], 0, 0)),
        ],
        out_specs=pl.BlockSpec((tm, ROW_TILES, LANES), lambda i, te, tq, nt, src: (i, 0, 0)),
        scratch_shapes=[pltpu.VMEM((2, tm, ROW_TILES, LANES), F32),
                        pltpu.SemaphoreType.DMA((2,)),
                        pltpu.VMEM((D_MODEL, 2 * D_EXPERT), BF16),
                        pltpu.VMEM((D_EXPERT, D_MODEL), BF16)],
    )
    return pl.pallas_call(
        functools.partial(_moe_kernel, tm=tm),
        grid_spec=grid_spec,
        out_shape=jax.ShapeDtypeStruct((max_tiles * tm, ROW_TILES, LANES), F32),
        compiler_params=_cparams(("arbitrary",), vmem=56 * 1024 * 1024),
        name="moe",
    )(tile_e, tile_q0, n_tiles, src_tok, h3r, w_gu, b_gu, w_dn, b_dn)


def _combine_kernel(pos_ref, x2_ref, route_ref, gfin_ref, ys_ref, y_ref, buf, sems, *, tc):
    i = pl.program_id(0)
    n = pl.num_programs(0)

    def row_copy(tile, slot, r):
        p = pos_ref[tile * (tc * TOP_K) + r]
        return pltpu.make_async_copy(ys_ref.at[p], buf.at[slot, r], sems.at[slot])

    def issue(tile, slot):
        def body(r, carry):
            row_copy(tile, slot, r).start()
            return carry
        lax.fori_loop(0, tc * TOP_K, body, 0)

    def drain(tile, slot):
        def body(r, carry):
            row_copy(tile, slot, r).wait()
            return carry
        lax.fori_loop(0, tc * TOP_K, body, 0)

    @pl.when(i == 0)
    def _():
        issue(0, 0)

    @pl.when(i + 1 < n)
    def _():
        issue(i + 1, (i + 1) % 2)

    slot = i % 2
    drain(i, slot)
    route = route_ref[...]
    acc = x2_ref[...]
    for j in range(TOP_K):
        rows = jnp.concatenate(
            [buf[slot, pl.ds(j, tc, stride=TOP_K), s, :] for s in range(ROW_TILES)], axis=1)
        acc = acc + route[:, TOP_K + j:TOP_K + j + 1] * rows
    y_ref[...] = _rms(acc, gfin_ref[...])


def _combine(pos_flat, x2, route, g_final, ys):
    tc = min(COMBINE_TILE, x2.shape[0])
    n = x2.shape[0]
    grid_spec = pltpu.PrefetchScalarGridSpec(
        num_scalar_prefetch=1,
        grid=(n // tc,),
        in_specs=[pl.BlockSpec((tc, D_MODEL), lambda i, pos: (i, 0)),
                  pl.BlockSpec((tc, LANES), lambda i, pos: (i, 0)),
                  pl.BlockSpec((1, D_MODEL), lambda i, pos: (0, 0)),
                  pl.BlockSpec(memory_space=pl.ANY)],
        out_specs=pl.BlockSpec((tc, D_MODEL), lambda i, pos: (i, 0)),
        scratch_shapes=[pltpu.VMEM((2, tc * TOP_K, ROW_TILES, LANES), F32),
                        pltpu.SemaphoreType.DMA((2,))],
    )
    return pl.pallas_call(
        functools.partial(_combine_kernel, tc=tc),
        grid_spec=grid_spec,
        out_shape=jax.ShapeDtypeStruct((n, D_MODEL), F32),
        compiler_params=_cparams(("arbitrary",)),
        name="combine",
    )(pos_flat, x2, route, g_final, ys)


def _routing_tables(idx, rank, counts, max_tiles):
    tm = MOE_TILE
    n_assign = idx.shape[0] * TOP_K
    tiles_e = (counts + tm - 1) // tm
    tile_end = jnp.cumsum(tiles_e)
    tile_start = tile_end - tiles_e
    total = tile_end[-1]
    unp_start = jnp.cumsum(counts) - counts
    pos = jnp.take(tile_start * tm, idx, axis=0) + rank
    tid = jnp.minimum(jnp.arange(max_tiles, dtype=I32), total - 1)
    tile_e = jnp.minimum(jnp.searchsorted(tile_end, tid, side="right").astype(I32), N_EXPERTS - 1)
    tile_q0 = jnp.take(unp_start, tile_e) + (tid - jnp.take(tile_start, tile_e)) * tm
    keys = idx.reshape(-1) * (1 << FLAT_BITS) + jnp.arange(n_assign, dtype=I32)
    src_tok = (jnp.sort(keys) & ((1 << FLAT_BITS) - 1)) // TOP_K
    src_tok = jnp.concatenate([src_tok, jnp.zeros((tm,), I32)])
    return (tile_e.astype(I32), tile_q0.astype(I32), total.reshape(1).astype(I32), src_tok.astype(I32),
            pos.reshape(-1).astype(I32))


def _pad_rows(a, rows):
    return jnp.concatenate([a, jnp.zeros((rows - a.shape[0],) + a.shape[1:], a.dtype)], axis=0)


def _pad_lanes(a, lanes=LANES):
    return jnp.concatenate([a, jnp.zeros(a.shape[:-1] + (lanes - a.shape[-1],), a.dtype)], axis=-1)


def kernel(x_prompt, mem_prompt, x_sample, state_conformer_conv, state_gdn_conv, state_gdn, cache_mem_k,
           cache_mem_v, w_in, b_glu, w_dw, b_dw, ln_g, ln_b, w_sc, a_log, dt_bias, g_onorm, w_out, g_mix,
           g_xattn, g_mem, w_xq, w_mk, w_mv, w_xo, g_moe, w_router, b_router, w_gu, b_gu, w_dn, b_dn,
           g_final):
    assert w_in.shape[0] == 1, "single-layer configuration"
    batch, seq, _ = x_prompt.shape
    n_s = x_sample.shape[0]
    n_p = batch * seq
    n_all = n_p + n_s
    assert seq % TOKEN_TILE == 0 and n_p % n_s == 0 and n_all * TOP_K < (1 << FLAT_BITS)

    wts = {
        "g_mix": g_mix[0][None], "g_xattn": g_xattn[0][None], "g_moe": g_moe[0][None],
        "g_onorm": g_onorm[0][None],
        "w_in_b": w_in[0][:, :OFF_A].astype(BF16),
        "w_ab_b": _pad_lanes(w_in[0][:, OFF_A:]).astype(BF16),
        "b_glu": b_glu[0][None],
        "w_dw": _pad_rows(w_dw[0], 32), "b_dw": b_dw[0][None], "ln_g": ln_g[0][None], "ln_b": ln_b[0][None],
        "w_sc": _pad_rows(w_sc[0], 8),
        "gdn_cst": _pad_rows(_pad_lanes(jnp.stack([a_log[0], dt_bias[0]])), 8),
        "w_out_b": w_out[0].astype(BF16), "w_xq_b": w_xq[0].astype(BF16), "w_xo_b": w_xo[0].astype(BF16),
        "w_router": _pad_lanes(w_router[0]), "b_router": _pad_lanes(b_router[0][None]),
    }

    mk, mv, mk_b, mv_b = _mem_kv(mem_prompt.reshape(batch * N_MEM, D_MODEL), g_mem[0][None],
                                 w_mk[0].astype(BF16), w_mv[0].astype(BF16))
    xp = x_prompt.reshape(n_p, D_MODEL)
    conv_p, q_p, k_p, v_p, z_p, gb_p, cstate_p, sstate_p = _pre_prompt(xp, batch, seq, wts)
    o_p, gstate_p = _gdn_prompt(q_p, k_p, v_p, gb_p, batch, seq)

    xs = x_sample.reshape(n_s, D_MODEL)
    conv_s, q_s, k_s, v_s, z_s, gb_s, cstate_s, sstate_s = _pre_sample(
        xs, state_conformer_conv[0], state_gdn_conv[0], wts)
    o_s, gstate_s = _gdn_sample(q_s, k_s, v_s, gb_s, state_gdn[0])
    x1_s, qx_s = _mix_sample(xs, conv_s, o_s, z_s, wts)
    att_s = _attn_sample(qx_s, cache_mem_k[0].reshape(n_s, N_MEM, D_MODEL),
                         cache_mem_v[0].reshape(n_s, N_MEM, D_MODEL))
    x2_s, h3_s, route_s, counts_s = _route_sample(x1_s, att_s, wts)

    x2_p, h3r, route_p, counts_p = _post_prompt(xp, conv_p, o_p, z_p, mk_b, mv_b, h3_s, batch, seq, wts)

    counts_p = counts_p[0, :N_EXPERTS].astype(I32)
    counts_s = counts_s[0, :N_EXPERTS].astype(I32)
    idx_p = route_p[:, 0:TOP_K].astype(I32)
    idx_s = route_s[:, 0:TOP_K].astype(I32)
    rank_p = route_p[:, 2 * TOP_K:3 * TOP_K].astype(I32)
    rank_s = route_s[:, 2 * TOP_K:3 * TOP_K].astype(I32) + jnp.take(counts_p, idx_s, axis=0)
    idx = jnp.concatenate([idx_p, idx_s], axis=0)
    rank = jnp.concatenate([rank_p, rank_s], axis=0)
    max_tiles = (n_all * TOP_K + N_EXPERTS * (MOE_TILE - 1)) // MOE_TILE
    tile_e, tile_q0, n_tiles, src_tok, pos = _routing_tables(idx, rank, counts_p + counts_s, max_tiles)
    ys = _moe(tile_e, tile_q0, n_tiles, src_tok, h3r, w_gu[0], b_gu[0][:, None, :], w_dn[0],
              b_dn[0][:, None, :], max_tiles)
    gfin = g_final[None]
    y_p = _combine(pos[:n_p * TOP_K], x2_p, route_p, gfin, ys)
    y_s = _combine(pos[n_p * TOP_K:], x2_s, route_s, gfin, ys)

    return (y_p.reshape(batch, seq, D_MODEL), y_s.reshape(n_s, 1, D_MODEL),
            cstate_p[None], sstate_p[None], gstate_p[None],
            mk.reshape(1, batch, N_MEM, X_HEADS, X_HEAD_DIM), mv.reshape(1, batch, N_MEM, X_HEADS, X_HEAD_DIM),
            cstate_s[None], sstate_s[None], gstate_s[None])
```

```python
import functools

import jax
import jax.numpy as jnp
from jax import lax
from jax.experimental import pallas as pl
from jax.experimental.pallas import tpu as pltpu
from jax.experimental.pallas import tpu_sc as plsc

F32, BF16, I32 = jnp.float32, jnp.bfloat16, jnp.int32

D_MODEL = 1024
CONV_CH = 512
CONV_WIDTH = 31
GDN_HEADS = 4
GDN_DK = 128
GDN_V = 512
QKV_CH = 1536
SHORT_CONV = 4
N_MEM = 256
X_HEADS = 4
X_HEAD_DIM = 256
N_EXPERTS = 32
TOP_K = 4
D_EXPERT = 1024
SWIGLU_LIMIT = 7.0
SWIGLU_ALPHA = 1.702
NORM_EPS = 1e-6
OFF_QKV = 2 * CONV_CH
OFF_Z = OFF_QKV + QKV_CH
OFF_A = OFF_Z + GDN_V

LANES = 128
SUBLANES = 8
ROW_TILES = D_MODEL // LANES
GDN_BLOCK = 128
TOKEN_TILE = 256
MOE_TILE = 256
COMBINE_TILE = 128
TOKEN_BITS = 15
ROUTE_ROWS = 16
SC_CORES = 2
SC_SUBCORES = 16
VMEM_LIMIT = 48 * 1024 * 1024


def _cparams(sem, vmem=VMEM_LIMIT):
    return pltpu.CompilerParams(dimension_semantics=sem, vmem_limit_bytes=vmem)


def _mm(a, b):
    return jnp.dot(a.astype(BF16), b.astype(BF16), preferred_element_type=F32)


def _mm_nt(a, b):
    return lax.dot_general(a.astype(BF16), b.astype(BF16), (((1,), (1,)), ((), ())),
                           preferred_element_type=F32)


def _mm_tn(a, b):
    return lax.dot_general(a.astype(BF16), b.astype(BF16), (((0,), (0,)), ((), ())),
                           preferred_element_type=F32)


def _rms(x, g):
    return x * lax.rsqrt(jnp.mean(x * x, axis=-1, keepdims=True) + NORM_EPS) * g


def _silu(x):
    return x * jax.nn.sigmoid(x)


def _full(shape):
    return pl.BlockSpec(shape, lambda *_: (0,) * len(shape))


def _project(x, gmix, w_ref, wab_ref, bglu):
    h = _rms(x, gmix).astype(BF16)
    u_glu = jnp.dot(h, w_ref[:, 0:OFF_QKV], preferred_element_type=F32) + bglu
    glu = u_glu[:, :CONV_CH] * jax.nn.sigmoid(u_glu[:, CONV_CH:])
    qkv_pre = jnp.dot(h, w_ref[:, OFF_QKV:OFF_Z], preferred_element_type=F32)
    z = jnp.dot(h, w_ref[:, OFF_Z:OFF_A], preferred_element_type=F32)
    uab = jnp.dot(h, wab_ref[...], preferred_element_type=F32)
    return glu, qkv_pre, z, uab


def _gate_beta(uab, cst):
    lane = lax.broadcasted_iota(I32, uab.shape, 1)
    g = -jnp.exp(cst[0:1, :]) * jax.nn.softplus(uab + cst[1:2, :])
    return jnp.where(lane < GDN_HEADS, g, jax.nn.sigmoid(uab))


def _conv_post(c, b_dw, ln_g, ln_b):
    c = c + b_dw
    mu = jnp.mean(c, axis=-1, keepdims=True)
    d = c - mu
    var = jnp.mean(d * d, axis=-1, keepdims=True)
    return _silu(d * lax.rsqrt(var + NORM_EPS) * ln_g + ln_b)


def _qkv_post(cs):
    a = _silu(cs)
    parts = []
    for h in range(2 * GDN_HEADS):
        seg = a[:, h * GDN_DK:(h + 1) * GDN_DK]
        n = seg * lax.rsqrt(jnp.sum(seg * seg, axis=-1, keepdims=True) + NORM_EPS)
        if h < GDN_HEADS:
            n = n * (GDN_DK ** -0.5)
        parts.append(n)
    q = jnp.concatenate(parts[:GDN_HEADS], axis=1)
    k = jnp.concatenate(parts[GDN_HEADS:], axis=1)
    return q, k, a[:, 2 * GDN_HEADS * GDN_DK:]


def _mix_out(conv_b, o, z, gon, wout_ref, x):
    parts = []
    for h in range(GDN_HEADS):
        oh = o[:, h * 128:(h + 1) * 128]
        parts.append(oh * lax.rsqrt(jnp.mean(oh * oh, axis=-1, keepdims=True) + NORM_EPS) * gon)
    on = jnp.concatenate(parts, axis=1) * _silu(z)
    mixed = (jnp.dot(conv_b, wout_ref[0:CONV_CH, :], preferred_element_type=F32)
             + jnp.dot(on.astype(BF16), wout_ref[CONV_CH:, :], preferred_element_type=F32))
    return x + mixed


def _route(x2, gmoe, wr_ref, br, carry):
    m = x2.shape[0]
    h3 = _rms(x2, gmoe)
    h_hi = h3.astype(BF16)
    r1 = h3 - h_hi.astype(F32)
    h_mid = r1.astype(BF16)
    h_lo = (r1 - h_mid.astype(F32)).astype(BF16)
    w = wr_ref[...]
    w_hi = w.astype(BF16)
    w_lo = (w - w_hi.astype(F32)).astype(BF16)
    logits = (jnp.dot(h_hi, w_hi, preferred_element_type=F32)
              + jnp.dot(h_hi, w_lo, preferred_element_type=F32)
              + jnp.dot(h_mid, w_hi, preferred_element_type=F32)
              + jnp.dot(h_lo, w_hi, preferred_element_type=F32)) + br
    lane = lax.broadcasted_iota(I32, (m, LANES), 1)
    neg = jnp.float32(-jnp.inf)
    work = jnp.where(lane < N_EXPERTS, logits, neg)
    vals, idxs = [], []
    for _ in range(TOP_K):
        mx = jnp.max(work, axis=-1, keepdims=True)
        ix = jnp.min(jnp.where(work == mx, lane, LANES), axis=-1, keepdims=True)
        vals.append(mx)
        idxs.append(ix)
        work = jnp.where(lane == ix, neg, work)
    es = [jnp.exp(v - vals[0]) for v in vals]
    den = es[0] + es[1] + es[2] + es[3]
    gates = [e / den for e in es]
    sel = jnp.zeros((m, LANES), F32)
    for ix in idxs:
        sel = sel + jnp.where(lane == ix, 1.0, 0.0)
    row = lax.broadcasted_iota(I32, (m, m), 0)
    col = lax.broadcasted_iota(I32, (m, m), 1)
    before = jnp.where(row > col, 1.0, 0.0).astype(BF16)
    rank_full = jnp.dot(before, sel.astype(BF16), preferred_element_type=F32) + carry
    route = jnp.zeros((m, LANES), F32)
    for r in range(TOP_K):
        rk = jnp.sum(jnp.where(lane == idxs[r], rank_full, 0.0), axis=-1, keepdims=True)
        route = (route + jnp.where(lane == r, idxs[r].astype(F32), 0.0)
                 + jnp.where(lane == TOP_K + r, gates[r], 0.0)
                 + jnp.where(lane == 2 * TOP_K + r, rk, 0.0))
    new_carry = carry + jnp.sum(sel, axis=0, keepdims=True)
    return h3, route, new_carry


def _store_rows(h3r_ref, h3):
    for s in range(ROW_TILES):
        h3r_ref[:, s, :] = h3[:, s * LANES:(s + 1) * LANES]


def _load_rows(ref):
    return jnp.concatenate([ref[:, s, :] for s in range(ROW_TILES)], axis=1)


def _mem_kv_kernel(mem_ref, g_ref, wk_ref, wv_ref, mk_ref, mv_ref, mkb_ref, mvb_ref):
    m = _rms(mem_ref[...], g_ref[...]).astype(BF16)
    mk = jnp.dot(m, wk_ref[...], preferred_element_type=F32)
    mv = jnp.dot(m, wv_ref[...], preferred_element_type=F32)
    mk_ref[...] = mk
    mv_ref[...] = mv
    mkb_ref[...] = mk.astype(BF16)
    mvb_ref[...] = mv.astype(BF16)


def _mem_kv(mem, g_mem, wk_b, wv_b):
    rows = mem.shape[0]
    tm = TOKEN_TILE
    row_spec = pl.BlockSpec((tm, D_MODEL), lambda i: (i, 0))
    return pl.pallas_call(
        _mem_kv_kernel,
        grid=(rows // tm,),
        in_specs=[row_spec, _full((1, D_MODEL)), _full((D_MODEL, D_MODEL)), _full((D_MODEL, D_MODEL))],
        out_specs=[row_spec] * 4,
        out_shape=[jax.ShapeDtypeStruct((rows, D_MODEL), F32)] * 2
        + [jax.ShapeDtypeStruct((rows, D_MODEL), BF16)] * 2,
        compiler_params=_cparams(("arbitrary",)),
        name="mem_kv",
    )(mem, g_mem, wk_b, wv_b)


CONV_HALO = 32
SC_HALO = 8


def _pre_prompt_kernel(x_ref, gmix_ref, w_ref, wab_ref, bglu_ref, wdw_ref, bdw_ref, lng_ref, lnb_ref,
                       wsc_ref, cst_ref, conv_ref, q_ref, k_ref, v_ref, z_ref, gb_ref, cstate_ref,
                       sstate_ref, cbuf, sbuf, *, tm):
    j = pl.program_id(1)

    @pl.when(j == 0)
    def _():
        cbuf[0:CONV_HALO, :] = jnp.zeros((CONV_HALO, CONV_CH), F32)
        sbuf[0:SC_HALO, :] = jnp.zeros((SC_HALO, QKV_CH), F32)

    glu, qkv_pre, z, uab = _project(x_ref[...], gmix_ref[...], w_ref, wab_ref, bglu_ref[...])
    cbuf[CONV_HALO:CONV_HALO + tm, :] = glu
    sbuf[SC_HALO:SC_HALO + tm, :] = qkv_pre
    z_ref[...] = z
    gb_ref[...] = _gate_beta(uab, cst_ref[...])

    base = CONV_HALO - (CONV_WIDTH - 1)
    acc = wdw_ref[0:1, :] * cbuf[pl.ds(base, tm), :]
    for t in range(1, CONV_WIDTH):
        acc = acc + wdw_ref[t:t + 1, :] * cbuf[pl.ds(base + t, tm), :]
    conv_ref[...] = _conv_post(acc, bdw_ref[...], lng_ref[...], lnb_ref[...]).astype(BF16)

    sbase = SC_HALO - (SHORT_CONV - 1)
    cs = wsc_ref[0:1, :] * sbuf[pl.ds(sbase, tm), :]
    for t in range(1, SHORT_CONV):
        cs = cs + wsc_ref[t:t + 1, :] * sbuf[pl.ds(sbase + t, tm), :]
    q, k, v = _qkv_post(cs)
    q_ref[...] = q
    k_ref[...] = k
    v_ref[...] = v

    @pl.when(j == pl.num_programs(1) - 1)
    def _():
        cstate_ref[0] = cbuf[pl.ds(CONV_HALO + tm - (CONV_WIDTH - 1), CONV_WIDTH - 1), :]
        sstate_ref[0] = sbuf[pl.ds(SC_HALO + tm - (SHORT_CONV - 1), SHORT_CONV - 1), :]

    cbuf[0:CONV_HALO, :] = cbuf[tm:tm + CONV_HALO, :]
    sbuf[0:SC_HALO, :] = sbuf[tm:tm + SC_HALO, :]


def _pre_prompt(x2d, batch, seq, wts):
    tm = TOKEN_TILE
    nj = seq // tm
    rows = batch * seq

    def tok(width):
        return pl.BlockSpec((tm, width), lambda b, j: (b * nj + j, 0))

    in_specs = [tok(D_MODEL), _full((1, D_MODEL)), _full((D_MODEL, OFF_A)), _full((D_MODEL, LANES)),
                _full((1, OFF_QKV)), _full((32, CONV_CH)), _full((1, CONV_CH)), _full((1, CONV_CH)),
                _full((1, CONV_CH)), _full((8, QKV_CH)), _full((8, LANES))]
    out_specs = [tok(CONV_CH), tok(GDN_V), tok(GDN_V), tok(GDN_V), tok(GDN_V), tok(LANES),
                 pl.BlockSpec((1, CONV_WIDTH - 1, CONV_CH), lambda b, j: (b, 0, 0)),
                 pl.BlockSpec((1, SHORT_CONV - 1, QKV_CH), lambda b, j: (b, 0, 0))]
    out_shape = [jax.ShapeDtypeStruct((rows, CONV_CH), BF16)] \
        + [jax.ShapeDtypeStruct((rows, GDN_V), F32)] * 4 \
        + [jax.ShapeDtypeStruct((rows, LANES), F32),
           jax.ShapeDtypeStruct((batch, CONV_WIDTH - 1, CONV_CH), F32),
           jax.ShapeDtypeStruct((batch, SHORT_CONV - 1, QKV_CH), F32)]
    return pl.pallas_call(
        functools.partial(_pre_prompt_kernel, tm=tm),
        grid=(batch, nj),
        in_specs=in_specs,
        out_specs=out_specs,
        out_shape=out_shape,
        scratch_shapes=[pltpu.VMEM((CONV_HALO + tm, CONV_CH), F32),
                        pltpu.VMEM((SC_HALO + tm, QKV_CH), F32)],
        compiler_params=_cparams(("arbitrary", "arbitrary")),
        name="pre_prompt",
    )(x2d, wts["g_mix"], wts["w_in_b"], wts["w_ab_b"], wts["b_glu"], wts["w_dw"], wts["b_dw"],
      wts["ln_g"], wts["ln_b"], wts["w_sc"], wts["gdn_cst"])


def _gdn_prompt_kernel(q_ref, k_ref, v_ref, gb_ref, o_ref, sfin_ref, s_scr):
    c = pl.program_id(1)
    n = GDN_BLOCK

    @pl.when(c == 0)
    def _():
        s_scr[...] = jnp.zeros(s_scr.shape, F32)

    gb = gb_ref[...]
    row = lax.broadcasted_iota(I32, (n, n), 0)
    col = lax.broadcasted_iota(I32, (n, n), 1)
    causal = row >= col
    strict = row > col
    tri = jnp.where(causal, 1.0, 0.0).astype(BF16)
    eye = jnp.where(row == col, 1.0, 0.0)
    level_masks = []
    b = 1
    while b < n:
        same_pair = ((row ^ col) & ~(2 * b - 1)) == 0
        level_masks.append(same_pair & ((row & b) != 0) & ((col & b) == 0))
        b *= 2
    g1 = gb.astype(BF16)
    r1 = gb - g1.astype(F32)
    g2 = r1.astype(BF16)
    g3 = (r1 - g2.astype(F32)).astype(BF16)
    gcum = (jnp.dot(tri, g1, preferred_element_type=F32) + jnp.dot(tri, g2, preferred_element_type=F32)
            + jnp.dot(tri, g3, preferred_element_type=F32))
    gcum_t = gcum.T
    egc = jnp.exp(gcum)
    for h in range(GDN_HEADS):
        sl = slice(h * GDN_DK, (h + 1) * GDN_DK)
        qh, kh, vh = q_ref[:, sl], k_ref[:, sl], v_ref[:, sl]
        gcol = gcum[:, h:h + 1]
        grow = gcum_t[h:h + 1, :]
        beta = gb[:, GDN_HEADS + h:GDN_HEADS + h + 1]
        ecol = egc[:, h:h + 1]
        decay = jnp.where(causal, jnp.exp(jnp.where(causal, gcol - grow, 0.0)), 0.0)
        kb = kh * beta
        vb = vh * beta
        lower = jnp.where(strict, _mm_nt(kb, kh) * decay, 0.0)
        x = eye - jnp.where(level_masks[0], lower, 0.0)
        for mask in level_masks[1:]:
            x = x - _mm(_mm(x, jnp.where(mask, lower, 0.0)), x)
        u = _mm(x, vb)
        w = _mm(x, kb * ecol)
        intra = jnp.where(causal, _mm_nt(qh, kh) * decay, 0.0)
        s = s_scr[h]
        v_new = u - _mm(w, s)
        o_ref[:, sl] = _mm(qh * ecol, s) + _mm(intra, v_new)
        glast = gcum[n - 1:n, h:h + 1]
        s_scr[h] = s * jnp.exp(glast) + _mm_tn(kh * jnp.exp(glast - gcol), v_new)

    @pl.when(c == pl.num_programs(1) - 1)
    def _():
        sfin_ref[0] = s_scr[...]


def _gdn_prompt(q, k, v, gb, batch, seq):
    n = GDN_BLOCK
    nc = seq // n

    def tok(width):
        return pl.BlockSpec((n, width), lambda b, c: (b * nc + c, 0))

    return pl.pallas_call(
        _gdn_prompt_kernel,
        grid=(batch, nc),
        in_specs=[tok(GDN_V), tok(GDN_V), tok(GDN_V), tok(LANES)],
        out_specs=[tok(GDN_V), pl.BlockSpec((1, GDN_HEADS, GDN_DK, GDN_DK), lambda b, c: (b, 0, 0, 0))],
        out_shape=[jax.ShapeDtypeStruct((batch * seq, GDN_V), F32),
                   jax.ShapeDtypeStruct((batch, GDN_HEADS, GDN_DK, GDN_DK), F32)],
        scratch_shapes=[pltpu.VMEM((GDN_HEADS, GDN_DK, GDN_DK), F32)],
        compiler_params=_cparams(("arbitrary", "arbitrary")),
        name="gdn_prompt",
    )(q, k, v, gb)


def _post_prompt_kernel(x_ref, conv_ref, o_ref, z_ref, gon_ref, wout_ref, gx_ref, wq_ref, mk_ref, mv_ref,
                        wo_ref, gmoe_ref, wr_ref, br_ref, h3s_ref, x2_ref, h3r_ref, route_ref, rt_ref,
                        cnt_ref, carry, *, n_steps):
    step = pl.program_id(0)

    @pl.when(step == 0)
    def _():
        carry[...] = jnp.zeros(carry.shape, F32)

    @pl.when(step < n_steps)
    def _():
        x1 = _mix_out(conv_ref[...], o_ref[...], z_ref[...], gon_ref[...], wout_ref, x_ref[...])
        qx = jnp.dot(_rms(x1, gx_ref[...]).astype(BF16), wq_ref[...], preferred_element_type=F32)
        outs = []
        for h in range(X_HEADS):
            sl = slice(h * X_HEAD_DIM, (h + 1) * X_HEAD_DIM)
            s = _mm_nt(qx[:, sl], mk_ref[:, sl]) * (X_HEAD_DIM ** -0.5)
            e = jnp.exp(s - jnp.max(s, axis=-1, keepdims=True))
            p = e / jnp.sum(e, axis=-1, keepdims=True)
            outs.append(jnp.dot(p.astype(BF16), mv_ref[:, sl], preferred_element_type=F32))
        att = jnp.concatenate(outs, axis=1)
        x2 = x1 + jnp.dot(att.astype(BF16), wo_ref[...], preferred_element_type=F32)
        x2_ref[...] = x2
        h3, route, new_carry = _route(x2, gmoe_ref[...], wr_ref, br_ref[...], carry[0:1, :])
        _store_rows(h3r_ref, h3)
        route_ref[...] = route
        rt_ref[...] = route.T[0:ROUTE_ROWS, :]
        carry[0:1, :] = new_carry
        cnt_ref[...] = jnp.broadcast_to(new_carry, cnt_ref.shape)

    @pl.when(step == n_steps)
    def _():
        h3r_ref[0:h3s_ref.shape[0]] = h3s_ref[...]


def _post_prompt(x2d, conv, o, z, mk_b, mv_b, h3_sample, batch, seq, wts):
    tm = TOKEN_TILE
    nj = seq // tm
    rows = batch * seq
    n_steps = batch * nj
    n_s = h3_sample.shape[0]
    assert n_s <= tm

    def tok(width):
        return pl.BlockSpec((tm, width), lambda s: (jnp.minimum(s, n_steps - 1), 0))

    mem_spec = pl.BlockSpec((N_MEM, D_MODEL), lambda s: (jnp.minimum(s, n_steps - 1) // nj, 0))
    sq = _full((D_MODEL, D_MODEL))
    in_specs = [tok(D_MODEL), tok(CONV_CH), tok(GDN_V), tok(GDN_V), _full((1, GDN_DK)), sq,
                _full((1, D_MODEL)), sq, mem_spec, mem_spec, sq, _full((1, D_MODEL)),
                _full((D_MODEL, LANES)), _full((1, LANES)), _full(h3_sample.shape)]
    out_specs = [tok(D_MODEL),
                 pl.BlockSpec((tm, ROW_TILES, LANES), lambda s: (s, 0, 0)),
                 tok(LANES),
                 pl.BlockSpec((ROUTE_ROWS, tm), lambda s: (0, jnp.minimum(s, n_steps - 1))),
                 _full((SUBLANES, LANES))]
    out_shape = [jax.ShapeDtypeStruct((rows, D_MODEL), F32),
                 jax.ShapeDtypeStruct((rows + n_s, ROW_TILES, LANES), F32),
                 jax.ShapeDtypeStruct((rows, LANES), F32),
                 jax.ShapeDtypeStruct((ROUTE_ROWS, rows), F32),
                 jax.ShapeDtypeStruct((SUBLANES, LANES), F32)]
    return pl.pallas_call(
        functools.partial(_post_prompt_kernel, n_steps=n_steps),
        grid=(n_steps + 1,),
        in_specs=in_specs,
        out_specs=out_specs,
        out_shape=out_shape,
        scratch_shapes=[pltpu.VMEM((SUBLANES, LANES), F32)],
        compiler_params=_cparams(("arbitrary",)),
        name="post_prompt",
    )(x2d, conv, o, z, wts["g_onorm"], wts["w_out_b"], wts["g_xattn"], wts["w_xq_b"], mk_b, mv_b,
      wts["w_xo_b"], wts["g_moe"], wts["w_router"], wts["b_router"], h3_sample)


def _pre_sample_kernel(x_ref, gmix_ref, w_ref, wab_ref, bglu_ref, wdw_ref, bdw_ref, lng_ref, lnb_ref,
                       wsc_ref, cst_ref, chist_ref, shist_ref, conv_ref, q_ref, k_ref, v_ref, z_ref,
                       gb_ref, cnew_ref, snew_ref):
    glu, qkv_pre, z, uab = _project(x_ref[...], gmix_ref[...], w_ref, wab_ref, bglu_ref[...])
    z_ref[...] = z
    gb_ref[...] = _gate_beta(uab, cst_ref[...])
    kw = CONV_WIDTH
    acc = wdw_ref[kw - 1:kw, :] * glu
    for t in range(kw - 1):
        row = chist_ref[:, t, :]
        acc = acc + wdw_ref[t:t + 1, :] * row
        if t >= 1:
            cnew_ref[:, t - 1, :] = row
    cnew_ref[:, kw - 2, :] = glu
    conv_ref[...] = _conv_post(acc, bdw_ref[...], lng_ref[...], lnb_ref[...]).astype(BF16)
    ks = SHORT_CONV
    cs = wsc_ref[ks - 1:ks, :] * qkv_pre
    for t in range(ks - 1):
        row = shist_ref[:, t, :]
        cs = cs + wsc_ref[t:t + 1, :] * row
        if t >= 1:
            snew_ref[:, t - 1, :] = row
    snew_ref[:, ks - 2, :] = qkv_pre
    q, k, v = _qkv_post(cs)
    q_ref[...] = q
    k_ref[...] = k
    v_ref[...] = v


PRE_SAMPLE_TOKENS = 32


def _pre_sample(xs, chist, shist, wts):
    n = xs.shape[0]
    tb = min(PRE_SAMPLE_TOKENS, n)

    def tok(width):
        return pl.BlockSpec((tb, width), lambda i: (i, 0))

    def hist(a):
        return pl.BlockSpec((tb,) + a.shape[1:], lambda i: (i, 0, 0))

    consts = (wts["g_mix"], wts["w_in_b"], wts["w_ab_b"], wts["b_glu"], wts["w_dw"], wts["b_dw"],
              wts["ln_g"], wts["ln_b"], wts["w_sc"], wts["gdn_cst"])
    return pl.pallas_call(
        _pre_sample_kernel,
        grid=(n // tb,),
        in_specs=[tok(D_MODEL)] + [_full(a.shape) for a in consts] + [hist(chist), hist(shist)],
        out_specs=[tok(CONV_CH), tok(GDN_V), tok(GDN_V), tok(GDN_V), tok(GDN_V), tok(LANES),
                   hist(chist), hist(shist)],
        out_shape=[jax.ShapeDtypeStruct((n, CONV_CH), BF16)]
        + [jax.ShapeDtypeStruct((n, GDN_V), F32)] * 4
        + [jax.ShapeDtypeStruct((n, LANES), F32), jax.ShapeDtypeStruct(chist.shape, F32),
           jax.ShapeDtypeStruct(shist.shape, F32)],
        compiler_params=_cparams(("arbitrary",)),
        name="pre_sample",
    )(xs, *consts, chist, shist)


GDN_STEP_TOKENS = 8


def _gdn_sample_kernel(q_ref, k_ref, v_ref, gb_ref, s_ref, o_ref, snew_ref):
    n = GDN_DK
    for i in range(GDN_STEP_TOKENS):
        for h in range(GDN_HEADS):
            sl = slice(h * GDN_DK, (h + 1) * GDN_DK)
            qrow = q_ref[i:i + 1, sl]
            krow = k_ref[i:i + 1, sl]
            vrow = v_ref[i:i + 1, sl]
            g = gb_ref[i:i + 1, h:h + 1]
            beta = gb_ref[i:i + 1, GDN_HEADS + h:GDN_HEADS + h + 1]
            kcol = jnp.broadcast_to(krow, (n, n)).T
            qcol = jnp.broadcast_to(qrow, (n, n)).T
            s1 = s_ref[i, h] * jnp.exp(g)
            sk = jnp.sum(s1 * kcol, axis=0, keepdims=True)
            vt = (vrow - sk) * beta
            s2 = s1 + kcol * vt
            snew_ref[i, h] = s2
            o_ref[i:i + 1, sl] = jnp.sum(s2 * qcol, axis=0, keepdims=True)


def _gdn_sample(q, k, v, gb, state):
    n = q.shape[0]
    tb = GDN_STEP_TOKENS

    def tok(width):
        return pl.BlockSpec((tb, width), lambda i: (i, 0))

    st = pl.BlockSpec((tb, GDN_HEADS, GDN_DK, GDN_DK), lambda i: (i, 0, 0, 0))
    return pl.pallas_call(
        _gdn_sample_kernel,
        grid=(n // tb,),
        in_specs=[tok(GDN_V), tok(GDN_V), tok(GDN_V), tok(LANES), st],
        out_specs=[tok(GDN_V), st],
        out_shape=[jax.ShapeDtypeStruct((n, GDN_V), F32), jax.ShapeDtypeStruct(state.shape, F32)],
        compiler_params=_cparams(("arbitrary",)),
        name="gdn_sample",
    )(q, k, v, gb, state)


def _mix_sample_kernel(x_ref, conv_ref, o_ref, z_ref, gon_ref, wout_ref, gx_ref, wq_ref, x1_ref, qx_ref):
    x1 = _mix_out(conv_ref[...], o_ref[...], z_ref[...], gon_ref[...], wout_ref, x_ref[...])
    x1_ref[...] = x1
    qx_ref[...] = jnp.dot(_rms(x1, gx_ref[...]).astype(BF16), wq_ref[...], preferred_element_type=F32)


def _mix_sample(xs, conv, o, z, wts):
    n = xs.shape[0]
    in_arrays = (xs, conv, o, z, wts["g_onorm"], wts["w_out_b"], wts["g_xattn"], wts["w_xq_b"])
    return pl.pallas_call(
        _mix_sample_kernel,
        grid=(1,),
        in_specs=[_full(a.shape) for a in in_arrays],
        out_specs=[_full((n, D_MODEL))] * 2,
        out_shape=[jax.ShapeDtypeStruct((n, D_MODEL), F32)] * 2,
        compiler_params=_cparams(("arbitrary",)),
        name="mix_sample",
    )(*in_arrays)


ATTN_STEP_TOKENS = 4


def _attn_sample_kernel(qx_ref, ck_ref, cv_ref, att_ref):
    for i in range(ATTN_STEP_TOKENS):
        parts = []
        for h in range(X_HEADS):
            sl = slice(h * X_HEAD_DIM, (h + 1) * X_HEAD_DIM)
            prod = ck_ref[i, :, h, :] * qx_ref[0, i:i + 1, sl]
            s = jnp.sum(prod, axis=-1, keepdims=True) * (X_HEAD_DIM ** -0.5)
            e = jnp.exp(s - jnp.max(s, axis=0, keepdims=True))
            p = e / jnp.sum(e, axis=0, keepdims=True)
            parts.append(jnp.sum(p * cv_ref[i, :, h, :], axis=0, keepdims=True))
        att_ref[0, i:i + 1, :] = jnp.concatenate(parts, axis=1)


def _attn_sample(qx, ck, cv):
    n = qx.shape[0]
    tb = ATTN_STEP_TOKENS
    q3 = qx.reshape(n // tb, tb, D_MODEL)
    qspec = pl.BlockSpec((1, tb, D_MODEL), lambda i: (i, 0, 0))
    cspec = pl.BlockSpec((tb, N_MEM, X_HEADS, X_HEAD_DIM), lambda i: (i, 0, 0, 0))
    out = pl.pallas_call(
        _attn_sample_kernel,
        grid=(n // tb,),
        in_specs=[qspec, cspec, cspec],
        out_specs=qspec,
        out_shape=jax.ShapeDtypeStruct(q3.shape, F32),
        compiler_params=_cparams(("arbitrary",)),
        name="attn_sample",
    )(q3, ck, cv)
    return out.reshape(n, D_MODEL)


def _route_sample_kernel(x1_ref, att_ref, wo_ref, gmoe_ref, wr_ref, br_ref, x2_ref, h3r_ref, route_ref,
                         rt_ref, cnt_ref):
    x2 = x1_ref[...] + jnp.dot(att_ref[...].astype(BF16), wo_ref[...], preferred_element_type=F32)
    x2_ref[...] = x2
    h3, route, counts = _route(x2, gmoe_ref[...], wr_ref, br_ref[...], jnp.zeros((1, LANES), F32))
    _store_rows(h3r_ref, h3)
    route_ref[...] = route
    rt_ref[...] = route.T[0:ROUTE_ROWS, :]
    cnt_ref[...] = jnp.broadcast_to(counts, cnt_ref.shape)


def _route_sample(x1, att, wts):
    n = x1.shape[0]
    in_arrays = (x1, att, wts["w_xo_b"], wts["g_moe"], wts["w_router"], wts["b_router"])
    shapes = [(n, D_MODEL), (n, ROW_TILES, LANES), (n, LANES), (ROUTE_ROWS, n), (SUBLANES, LANES)]
    return pl.pallas_call(
        _route_sample_kernel,
        grid=(1,),
        in_specs=[_full(a.shape) for a in in_arrays],
        out_specs=[_full(s) for s in shapes],
        out_shape=[jax.ShapeDtypeStruct(s, F32) for s in shapes],
        compiler_params=_cparams(("arbitrary",)),
        name="route_sample",
    )(*in_arrays)


def _sc_chunk(rows_per_worker):
    for c in range(64, 0, -SUBLANES):
        if rows_per_worker % c == 0:
            return c
    raise ValueError(rows_per_worker)


def _sc_gather_rows(table, idx):
    n_workers = SC_CORES * SC_SUBCORES
    b = idx.shape[0]
    assert b % (n_workers * SUBLANES) == 0
    per_worker = b // n_workers
    chunk = _sc_chunk(per_worker)
    row_shape = table.shape[1:]
    mesh = plsc.VectorSubcoreMesh(core_axis_name="c", subcore_axis_name="s")

    @functools.partial(
        pl.kernel, mesh=mesh,
        out_type=jax.ShapeDtypeStruct((b,) + row_shape, table.dtype),
        scratch_types=[pltpu.VMEM((chunk,), I32), pltpu.VMEM((chunk,) + row_shape, table.dtype),
                       pltpu.SemaphoreType.DMA],
        name="sc_gather_rows",
    )
    def gather(table_hbm, idx_hbm, out_hbm, idx_v, rows_v, sem):
        worker = lax.axis_index("s") * SC_CORES + lax.axis_index("c")
        base = worker * per_worker

        @pl.loop(0, per_worker // chunk)
        def _(c):
            off = pl.multiple_of(base + c * chunk, SUBLANES)
            pltpu.sync_copy(idx_hbm.at[pl.ds(off, chunk)], idx_v)
            pltpu.async_copy(table_hbm.at[idx_v], rows_v, sem).wait()
            pltpu.sync_copy(rows_v, out_hbm.at[pl.ds(off, chunk)])

    return gather(table, idx)


def _moe_kernel(te_ref, nt_ref, xs_ref, wgu_ref, bgu_ref, wdn_ref, bdn_ref, ys_ref, wgu_b, wdn_b):
    i = pl.program_id(0)
    total = nt_ref[0]

    @pl.when(i < total)
    def _():
        prev = te_ref[jnp.maximum(i - 1, 0)]
        fresh = jnp.logical_or(i == 0, te_ref[i] != prev)

        @pl.when(fresh)
        def _():
            wgu_b[...] = wgu_ref[0].astype(BF16)
            wdn_b[...] = wdn_ref[0].astype(BF16)

        x = _load_rows(xs_ref).astype(BF16)
        gu = jnp.dot(x, wgu_b[...], preferred_element_type=F32) + bgu_ref[0]
        x_glu = jnp.minimum(gu[:, :D_EXPERT], SWIGLU_LIMIT)
        x_lin = jnp.clip(gu[:, D_EXPERT:], -SWIGLU_LIMIT, SWIGLU_LIMIT)
        act = x_glu * jax.nn.sigmoid(SWIGLU_ALPHA * x_glu) * (x_lin + 1.0)
        y = jnp.dot(act.astype(BF16), wdn_b[...], preferred_element_type=F32) + bdn_ref[0]
        _store_rows(ys_ref, y)

    @pl.when(i >= total)
    def _():
        ys_ref[...] = jnp.zeros(ys_ref.shape, F32)


def _moe(tile_e, n_tiles, xs, w_gu, b_gu, w_dn, b_dn):
    tm = MOE_TILE
    n_rows = xs.shape[0]
    row_spec = pl.BlockSpec((tm, ROW_TILES, LANES), lambda i, te, nt: (i, 0, 0))
    grid_spec = pltpu.PrefetchScalarGridSpec(
        num_scalar_prefetch=2,
        grid=(n_rows // tm,),
        in_specs=[
            row_spec,
            pl.BlockSpec((1, D_MODEL, 2 * D_EXPERT), lambda i, te, nt: (te[i], 0, 0)),
            pl.BlockSpec((1, 1, 2 * D_EXPERT), lambda i, te, nt: (te[i], 0, 0)),
            pl.BlockSpec((1, D_EXPERT, D_MODEL), lambda i, te, nt: (te[i], 0, 0)),
            pl.BlockSpec((1, 1, D_MODEL), lambda i, te, nt: (te[i], 0, 0)),
        ],
        out_specs=row_spec,
        scratch_shapes=[pltpu.VMEM((D_MODEL, 2 * D_EXPERT), BF16),
                        pltpu.VMEM((D_EXPERT, D_MODEL), BF16)],
    )
    return pl.pallas_call(
        _moe_kernel,
        grid_spec=grid_spec,
        out_shape=jax.ShapeDtypeStruct(xs.shape, F32),
        compiler_params=_cparams(("arbitrary",), vmem=56 * 1024 * 1024),
        name="moe",
    )(tile_e, n_tiles, xs, w_gu, b_gu, w_dn, b_dn)


def _combine_kernel(x2_ref, route_ref, gfin_ref, yt_ref, y_ref):
    route = route_ref[...]
    acc = x2_ref[...]
    for j in range(TOP_K):
        acc = acc + route[:, TOP_K + j:TOP_K + j + 1] * _load_rows(yt_ref.at[j])
    y_ref[...] = _rms(acc, gfin_ref[...])


def _combine(x2, route, g_final, ys_tok, first_block):
    n = x2.shape[0]
    tc = min(COMBINE_TILE, n)
    return pl.pallas_call(
        _combine_kernel,
        grid=(n // tc,),
        in_specs=[pl.BlockSpec((tc, D_MODEL), lambda i: (i, 0)),
                  pl.BlockSpec((tc, LANES), lambda i: (i, 0)),
                  pl.BlockSpec((1, D_MODEL), lambda i: (0, 0)),
                  pl.BlockSpec((TOP_K, tc, ROW_TILES, LANES), lambda i: (0, first_block + i, 0, 0))],
        out_specs=pl.BlockSpec((tc, D_MODEL), lambda i: (i, 0)),
        out_shape=jax.ShapeDtypeStruct((n, D_MODEL), F32),
        compiler_params=_cparams(("arbitrary",)),
        name="combine",
    )(x2, route, g_final, ys_tok)


def _routing_tables(idx_t, rank_t, counts, n_rows):
    tm = MOE_TILE
    n_tok = idx_t.shape[1]
    n_assign = TOP_K * n_tok
    tok_mask = (1 << TOKEN_BITS) - 1
    tiles_e = (counts + tm - 1) // tm
    tile_end = jnp.cumsum(tiles_e)
    row_start = (tile_end - tiles_e) * tm
    total = tile_end[-1]
    expert_ids = jnp.arange(N_EXPERTS, dtype=I32)

    def lookup(table, e):
        return jnp.sum(jnp.where(e[..., None] == expert_ids, table, 0), axis=-1)

    pos = lookup(row_start, idx_t) + rank_t
    keys_real = (idx_t * (1 << TOKEN_BITS) + jnp.arange(n_tok, dtype=I32)[None, :]).reshape(-1)
    k = jnp.arange(n_rows - n_assign, dtype=I32)
    pad_e, pad_s = k // tm, k % tm
    pad_needed = lookup(tiles_e * tm - counts, pad_e)
    pad_key_e = jnp.where((pad_e < N_EXPERTS) & (pad_s < pad_needed), pad_e, N_EXPERTS)
    keys = jnp.sort(jnp.concatenate([keys_real, pad_key_e * (1 << TOKEN_BITS) + tok_mask]))
    src_tok = jnp.minimum(keys & tok_mask, n_tok - 1)
    tid = jnp.minimum(jnp.arange(n_rows // tm, dtype=I32), total - 1)
    tile_e = jnp.minimum(jnp.sum((tid[:, None] >= tile_end[None, :]).astype(I32), axis=1), N_EXPERTS - 1)
    return tile_e.astype(I32), total.reshape(1).astype(I32), src_tok.astype(I32), pos.reshape(-1).astype(I32)


def _pad_rows(a, rows):
    return jnp.concatenate([a, jnp.zeros((rows - a.shape[0],) + a.shape[1:], a.dtype)], axis=0)


def _pad_lanes(a, lanes=LANES):
    return jnp.concatenate([a, jnp.zeros(a.shape[:-1] + (lanes - a.shape[-1],), a.dtype)], axis=-1)


def kernel(x_prompt, mem_prompt, x_sample, state_conformer_conv, state_gdn_conv, state_gdn, cache_mem_k,
           cache_mem_v, w_in, b_glu, w_dw, b_dw, ln_g, ln_b, w_sc, a_log, dt_bias, g_onorm, w_out, g_mix,
           g_xattn, g_mem, w_xq, w_mk, w_mv, w_xo, g_moe, w_router, b_router, w_gu, b_gu, w_dn, b_dn,
           g_final):
    assert w_in.shape[0] == 1, "single-layer configuration"
    batch, seq, _ = x_prompt.shape
    n_s = x_sample.shape[0]
    n_p = batch * seq
    n_all = n_p + n_s
    assert seq % TOKEN_TILE == 0 and n_p % n_s == 0 and n_all < (1 << TOKEN_BITS) - 1
    assert (n_all * TOP_K) % (SC_CORES * SC_SUBCORES * SUBLANES) == 0

    wts = {
        "g_mix": g_mix[0][None], "g_xattn": g_xattn[0][None], "g_moe": g_moe[0][None],
        "g_onorm": g_onorm[0][None],
        "w_in_b": w_in[0][:, :OFF_A].astype(BF16),
        "w_ab_b": _pad_lanes(w_in[0][:, OFF_A:]).astype(BF16),
        "b_glu": b_glu[0][None],
        "w_dw": _pad_rows(w_dw[0], 32), "b_dw": b_dw[0][None], "ln_g": ln_g[0][None], "ln_b": ln_b[0][None],
        "w_sc": _pad_rows(w_sc[0], 8),
        "gdn_cst": _pad_rows(_pad_lanes(jnp.stack([a_log[0], dt_bias[0]])), 8),
        "w_out_b": w_out[0].astype(BF16), "w_xq_b": w_xq[0].astype(BF16), "w_xo_b": w_xo[0].astype(BF16),
        "w_router": _pad_lanes(w_router[0]), "b_router": _pad_lanes(b_router[0][None]),
    }

    mk, mv, mk_b, mv_b = _mem_kv(mem_prompt.reshape(batch * N_MEM, D_MODEL), g_mem[0][None],
                                 w_mk[0].astype(BF16), w_mv[0].astype(BF16))
    xp = x_prompt.reshape(n_p, D_MODEL)
    conv_p, q_p, k_p, v_p, z_p, gb_p, cstate_p, sstate_p = _pre_prompt(xp, batch, seq, wts)
    o_p, gstate_p = _gdn_prompt(q_p, k_p, v_p, gb_p, batch, seq)

    xs = x_sample.reshape(n_s, D_MODEL)
    conv_s, q_s, k_s, v_s, z_s, gb_s, cstate_s, sstate_s = _pre_sample(
        xs, state_conformer_conv[0], state_gdn_conv[0], wts)
    o_s, gstate_s = _gdn_sample(q_s, k_s, v_s, gb_s, state_gdn[0])
    x1_s, qx_s = _mix_sample(xs, conv_s, o_s, z_s, wts)
    att_s = _attn_sample(qx_s, cache_mem_k[0], cache_mem_v[0])
    x2_s, h3_s, route_s, rt_s, counts_s = _route_sample(x1_s, att_s, wts)

    x2_p, h3r, route_p, rt_p, counts_p = _post_prompt(xp, conv_p, o_p, z_p, mk_b, mv_b, h3_s, batch, seq,
                                                      wts)

    counts_p = counts_p[0, :N_EXPERTS].astype(I32)
    counts_s = counts_s[0, :N_EXPERTS].astype(I32)
    idx_s = rt_s[0:TOP_K].astype(I32)
    rank_s = rt_s[2 * TOP_K:3 * TOP_K].astype(I32) + jnp.sum(
        jnp.where(idx_s[..., None] == jnp.arange(N_EXPERTS, dtype=I32), counts_p, 0), axis=-1)
    idx_t = jnp.concatenate([rt_p[0:TOP_K].astype(I32), idx_s], axis=1)
    rank_t = jnp.concatenate([rt_p[2 * TOP_K:3 * TOP_K].astype(I32), rank_s], axis=1)
    n_rows = -(-(n_all * TOP_K + N_EXPERTS * (MOE_TILE - 1)) // MOE_TILE) * MOE_TILE
    tile_e, n_tiles, src_tok, pos = _routing_tables(idx_t, rank_t, counts_p + counts_s, n_rows)
    xs = _sc_gather_rows(h3r, src_tok)
    ys = _moe(tile_e, n_tiles, xs, w_gu[0], b_gu[0][:, None, :], w_dn[0], b_dn[0][:, None, :])
    ys_tok = _sc_gather_rows(ys, pos).reshape(TOP_K, n_all, ROW_TILES, LANES)
    gfin = g_final[None]
    y_p = _combine(x2_p, route_p, gfin, ys_tok, 0)
    y_s = _combine(x2_s, route_s, gfin, ys_tok, n_p // min(COMBINE_TILE, n_s))

    return (y_p.reshape(batch, seq, D_MODEL), y_s.reshape(n_s, 1, D_MODEL),
            cstate_p[None], sstate_p[None], gstate_p[None],
            mk.reshape(1, batch, N_MEM, X_HEADS, X_HEAD_DIM), mv.reshape(1, batch, N_MEM, X_HEADS, X_HEAD_DIM),
            cstate_s[None], sstate_s[None], gstate_s[None])
```

```python
import functools

import jax
import jax.numpy as jnp
from jax import lax
from jax.experimental import pallas as pl
from jax.experimental.pallas import tpu as pltpu
from jax.experimental.pallas import tpu_sc as plsc

F32, BF16, I32 = jnp.float32, jnp.bfloat16, jnp.int32

D_MODEL = 1024
CONV_CH = 512
CONV_WIDTH = 31
GDN_HEADS = 4
GDN_DK = 128
GDN_V = 512
QKV_CH = 1536
SHORT_CONV = 4
N_MEM = 256
X_HEADS = 4
X_HEAD_DIM = 256
N_EXPERTS = 32
TOP_K = 4
D_EXPERT = 1024
SWIGLU_LIMIT = 7.0
SWIGLU_ALPHA = 1.702
NORM_EPS = 1e-6
OFF_QKV = 2 * CONV_CH
OFF_Z = OFF_QKV + QKV_CH
OFF_A = OFF_Z + GDN_V

LANES = 128
SUBLANES = 8
GDN_BLOCK = 128
TOKEN_TILE = 256
MOE_TILE = 256
COMBINE_TILE = 128
TOKEN_BITS = 15
ROUTE_ROWS = 16
SC_CORES = 2
SC_SUBCORES = 16
VMEM_LIMIT = 48 * 1024 * 1024


def _cparams(sem, vmem=VMEM_LIMIT):
    return pltpu.CompilerParams(dimension_semantics=sem, vmem_limit_bytes=vmem)


def _mm(a, b):
    return jnp.dot(a.astype(BF16), b.astype(BF16), preferred_element_type=F32)


def _mm_nt(a, b):
    return lax.dot_general(a.astype(BF16), b.astype(BF16), (((1,), (1,)), ((), ())),
                           preferred_element_type=F32)


def _mm_tn(a, b):
    return lax.dot_general(a.astype(BF16), b.astype(BF16), (((0,), (0,)), ((), ())),
                           preferred_element_type=F32)


def _rms(x, g):
    return x * lax.rsqrt(jnp.mean(x * x, axis=-1, keepdims=True) + NORM_EPS) * g


def _silu(x):
    return x * jax.nn.sigmoid(x)


def _full(shape):
    return pl.BlockSpec(shape, lambda *_: (0,) * len(shape))


def _strict_lower(n):
    return jnp.tril(jnp.ones((n, n), BF16), k=-1)


def _project(x, gmix, w_ref, wab_ref, bglu):
    h = _rms(x, gmix).astype(BF16)
    u_glu = jnp.dot(h, w_ref[:, 0:OFF_QKV], preferred_element_type=F32) + bglu
    glu = u_glu[:, :CONV_CH] * jax.nn.sigmoid(u_glu[:, CONV_CH:])
    qkv_pre = jnp.dot(h, w_ref[:, OFF_QKV:OFF_Z], preferred_element_type=F32)
    z = jnp.dot(h, w_ref[:, OFF_Z:OFF_A], preferred_element_type=F32)
    uab = jnp.dot(h, wab_ref[...], preferred_element_type=F32)
    return glu, qkv_pre, z, uab


def _gate_beta(uab, cst):
    lane = lax.broadcasted_iota(I32, uab.shape, 1)
    g = -jnp.exp(cst[0:1, :]) * jax.nn.softplus(uab + cst[1:2, :])
    return jnp.where(lane < GDN_HEADS, g, jax.nn.sigmoid(uab))


def _conv_post(c, b_dw, ln_g, ln_b):
    c = c + b_dw
    mu = jnp.mean(c, axis=-1, keepdims=True)
    d = c - mu
    var = jnp.mean(d * d, axis=-1, keepdims=True)
    return _silu(d * lax.rsqrt(var + NORM_EPS) * ln_g + ln_b)


def _qkv_post(cs):
    a = _silu(cs)
    parts = []
    for h in range(2 * GDN_HEADS):
        seg = a[:, h * GDN_DK:(h + 1) * GDN_DK]
        n = seg * lax.rsqrt(jnp.sum(seg * seg, axis=-1, keepdims=True) + NORM_EPS)
        if h < GDN_HEADS:
            n = n * (GDN_DK ** -0.5)
        parts.append(n)
    q = jnp.concatenate(parts[:GDN_HEADS], axis=1)
    k = jnp.concatenate(parts[GDN_HEADS:], axis=1)
    return q, k, a[:, 2 * GDN_HEADS * GDN_DK:]


def _mix_out(conv_b, o, z, gon, wout_ref, x):
    parts = []
    for h in range(GDN_HEADS):
        oh = o[:, h * 128:(h + 1) * 128]
        parts.append(oh * lax.rsqrt(jnp.mean(oh * oh, axis=-1, keepdims=True) + NORM_EPS) * gon)
    on = jnp.concatenate(parts, axis=1) * _silu(z)
    mixed = (jnp.dot(conv_b, wout_ref[0:CONV_CH, :], preferred_element_type=F32)
             + jnp.dot(on.astype(BF16), wout_ref[CONV_CH:, :], preferred_element_type=F32))
    return x + mixed


def _route(x2, gmoe, wr_ref, br, carry, before):
    m = x2.shape[0]
    h3 = _rms(x2, gmoe)
    h_hi = h3.astype(BF16)
    r1 = h3 - h_hi.astype(F32)
    h_mid = r1.astype(BF16)
    h_lo = (r1 - h_mid.astype(F32)).astype(BF16)
    w = wr_ref[...]
    w_hi = w.astype(BF16)
    w_lo = (w - w_hi.astype(F32)).astype(BF16)
    logits = (jnp.dot(h_hi, w_hi, preferred_element_type=F32)
              + jnp.dot(h_hi, w_lo, preferred_element_type=F32)
              + jnp.dot(h_mid, w_hi, preferred_element_type=F32)
              + jnp.dot(h_lo, w_hi, preferred_element_type=F32)) + br
    lane = lax.broadcasted_iota(I32, (m, LANES), 1)
    neg = jnp.float32(-jnp.inf)
    work = jnp.where(lane < N_EXPERTS, logits, neg)
    vals, idxs = [], []
    for _ in range(TOP_K):
        mx = jnp.max(work, axis=-1, keepdims=True)
        ix = jnp.min(jnp.where(work == mx, lane, LANES), axis=-1, keepdims=True)
        vals.append(mx)
        idxs.append(ix)
        work = jnp.where(lane == ix, neg, work)
    es = [jnp.exp(v - vals[0]) for v in vals]
    den = es[0] + es[1] + es[2] + es[3]
    gates = [e / den for e in es]
    sel = jnp.zeros((m, LANES), F32)
    for ix in idxs:
        sel = sel + jnp.where(lane == ix, 1.0, 0.0)
    rank_full = jnp.dot(before, sel.astype(BF16), preferred_element_type=F32) + carry
    route = jnp.zeros((m, LANES), F32)
    for r in range(TOP_K):
        rk = jnp.sum(jnp.where(lane == idxs[r], rank_full, 0.0), axis=-1, keepdims=True)
        route = (route + jnp.where(lane == r, idxs[r].astype(F32), 0.0)
                 + jnp.where(lane == TOP_K + r, gates[r], 0.0)
                 + jnp.where(lane == 2 * TOP_K + r, rk, 0.0))
    new_carry = carry + jnp.sum(sel, axis=0, keepdims=True)
    return h3, route, new_carry


def _mem_kv_kernel(mem_ref, g_ref, wk_ref, wv_ref, mk_ref, mv_ref, mkb_ref, mvb_ref):
    m = _rms(mem_ref[...], g_ref[...]).astype(BF16)
    mk = jnp.dot(m, wk_ref[...], preferred_element_type=F32)
    mv = jnp.dot(m, wv_ref[...], preferred_element_type=F32)
    mk_ref[...] = mk
    mv_ref[...] = mv
    mkb_ref[...] = mk.astype(BF16)
    mvb_ref[...] = mv.astype(BF16)


def _mem_kv(mem, g_mem, wk_b, wv_b):
    rows = mem.shape[0]
    tm = TOKEN_TILE
    row_spec = pl.BlockSpec((tm, D_MODEL), lambda i: (i, 0))
    return pl.pallas_call(
        _mem_kv_kernel,
        grid=(rows // tm,),
        in_specs=[row_spec, _full((1, D_MODEL)), _full((D_MODEL, D_MODEL)), _full((D_MODEL, D_MODEL))],
        out_specs=[row_spec] * 4,
        out_shape=[jax.ShapeDtypeStruct((rows, D_MODEL), F32)] * 2
        + [jax.ShapeDtypeStruct((rows, D_MODEL), BF16)] * 2,
        compiler_params=_cparams(("arbitrary",)),
        name="mem_kv",
    )(mem, g_mem, wk_b, wv_b)


CONV_HALO = 32
SC_HALO = 8


def _pre_prompt_kernel(x_ref, gmix_ref, w_ref, wab_ref, bglu_ref, wdw_ref, bdw_ref, lng_ref, lnb_ref,
                       wsc_ref, cst_ref, conv_ref, q_ref, k_ref, v_ref, z_ref, gb_ref, cstate_ref,
                       sstate_ref, cbuf, sbuf, *, tm):
    j = pl.program_id(1)

    @pl.when(j == 0)
    def _():
        cbuf[0:CONV_HALO, :] = jnp.zeros((CONV_HALO, CONV_CH), F32)
        sbuf[0:SC_HALO, :] = jnp.zeros((SC_HALO, QKV_CH), F32)

    glu, qkv_pre, z, uab = _project(x_ref[...], gmix_ref[...], w_ref, wab_ref, bglu_ref[...])
    cbuf[CONV_HALO:CONV_HALO + tm, :] = glu
    sbuf[SC_HALO:SC_HALO + tm, :] = qkv_pre
    z_ref[...] = z
    gb_ref[...] = _gate_beta(uab, cst_ref[...])

    base = CONV_HALO - (CONV_WIDTH - 1)
    rows = CONV_HALO + tm
    accs = []
    for c in range(CONV_CH // LANES):
        lanes = slice(c * LANES, (c + 1) * LANES)
        block = cbuf[:, lanes]
        acc = None
        for r in range(SUBLANES):
            shifted = block if r == 0 else pltpu.roll(block, rows - r, 0)
            for a in range(base, base + CONV_WIDTH):
                if a % SUBLANES == r:
                    t = a - base
                    term = wdw_ref[t:t + 1, lanes] * shifted[a - r:a - r + tm, :]
                    acc = term if acc is None else acc + term
        accs.append(acc)
    acc = jnp.concatenate(accs, axis=1)
    conv_ref[...] = _conv_post(acc, bdw_ref[...], lng_ref[...], lnb_ref[...]).astype(BF16)

    sbase = SC_HALO - (SHORT_CONV - 1)
    cs = wsc_ref[0:1, :] * sbuf[pl.ds(sbase, tm), :]
    for t in range(1, SHORT_CONV):
        cs = cs + wsc_ref[t:t + 1, :] * sbuf[pl.ds(sbase + t, tm), :]
    q, k, v = _qkv_post(cs)
    q_ref[...] = q
    k_ref[...] = k
    v_ref[...] = v

    @pl.when(j == pl.num_programs(1) - 1)
    def _():
        cstate_ref[0] = cbuf[pl.ds(CONV_HALO + tm - (CONV_WIDTH - 1), CONV_WIDTH - 1), :]
        sstate_ref[0] = sbuf[pl.ds(SC_HALO + tm - (SHORT_CONV - 1), SHORT_CONV - 1), :]

    cbuf[0:CONV_HALO, :] = cbuf[tm:tm + CONV_HALO, :]
    sbuf[0:SC_HALO, :] = sbuf[tm:tm + SC_HALO, :]


def _pre_prompt(x2d, batch, seq, wts):
    tm = TOKEN_TILE
    nj = seq // tm
    rows = batch * seq

    def tok(width):
        return pl.BlockSpec((tm, width), lambda b, j: (b * nj + j, 0))

    in_specs = [tok(D_MODEL), _full((1, D_MODEL)), _full((D_MODEL, OFF_A)), _full((D_MODEL, LANES)),
                _full((1, OFF_QKV)), _full((32, CONV_CH)), _full((1, CONV_CH)), _full((1, CONV_CH)),
                _full((1, CONV_CH)), _full((8, QKV_CH)), _full((8, LANES))]
    out_specs = [tok(CONV_CH), tok(GDN_V), tok(GDN_V), tok(GDN_V), tok(GDN_V), tok(LANES),
                 pl.BlockSpec((1, CONV_WIDTH - 1, CONV_CH), lambda b, j: (b, 0, 0)),
                 pl.BlockSpec((1, SHORT_CONV - 1, QKV_CH), lambda b, j: (b, 0, 0))]
    out_shape = [jax.ShapeDtypeStruct((rows, CONV_CH), BF16)] \
        + [jax.ShapeDtypeStruct((rows, GDN_V), F32)] * 4 \
        + [jax.ShapeDtypeStruct((rows, LANES), F32),
           jax.ShapeDtypeStruct((batch, CONV_WIDTH - 1, CONV_CH), F32),
           jax.ShapeDtypeStruct((batch, SHORT_CONV - 1, QKV_CH), F32)]
    return pl.pallas_call(
        functools.partial(_pre_prompt_kernel, tm=tm),
        grid=(batch, nj),
        in_specs=in_specs,
        out_specs=out_specs,
        out_shape=out_shape,
        scratch_shapes=[pltpu.VMEM((CONV_HALO + tm, CONV_CH), F32),
                        pltpu.VMEM((SC_HALO + tm, QKV_CH), F32)],
        compiler_params=_cparams(("arbitrary", "arbitrary")),
        name="pre_prompt",
    )(x2d, wts["g_mix"], wts["w_in_b"], wts["w_ab_b"], wts["b_glu"], wts["w_dw"], wts["b_dw"],
      wts["ln_g"], wts["ln_b"], wts["w_sc"], wts["gdn_cst"])


def _gdn_prompt_kernel(q_ref, k_ref, v_ref, gb_ref, o_ref, sfin_ref, s_scr):
    c = pl.program_id(1)
    n = GDN_BLOCK

    @pl.when(c == 0)
    def _():
        s_scr[...] = jnp.zeros(s_scr.shape, F32)

    gb = gb_ref[...]
    row = lax.broadcasted_iota(I32, (n, n), 0)
    col = lax.broadcasted_iota(I32, (n, n), 1)
    causal = row >= col
    strict = row > col
    tri = jnp.where(causal, 1.0, 0.0).astype(BF16)
    eye = jnp.where(row == col, 1.0, 0.0)
    level_masks = []
    b = 1
    while b < n:
        same_pair = ((row ^ col) & ~(2 * b - 1)) == 0
        level_masks.append(same_pair & ((row & b) != 0) & ((col & b) == 0))
        b *= 2
    g1 = gb.astype(BF16)
    r1 = gb - g1.astype(F32)
    g2 = r1.astype(BF16)
    g3 = (r1 - g2.astype(F32)).astype(BF16)
    gcum = (jnp.dot(tri, g1, preferred_element_type=F32) + jnp.dot(tri, g2, preferred_element_type=F32)
            + jnp.dot(tri, g3, preferred_element_type=F32))
    gcum_t = gcum.T
    egc = jnp.exp(gcum)
    heads = range(GDN_HEADS)
    sls = [slice(h * GDN_DK, (h + 1) * GDN_DK) for h in heads]
    qs = [q_ref[:, sl] for sl in sls]
    ks = [k_ref[:, sl] for sl in sls]
    vs = [v_ref[:, sl] for sl in sls]
    ss = [s_scr[h] for h in heads]
    gcols = [gcum[:, h:h + 1] for h in heads]
    ecols = [egc[:, h:h + 1] for h in heads]
    betas = [gb[:, GDN_HEADS + h:GDN_HEADS + h + 1] for h in heads]
    glasts = [gcum[n - 1:n, h:h + 1] for h in heads]
    decays = [jnp.where(causal, jnp.exp(jnp.where(causal, gcols[h] - gcum_t[h:h + 1, :], 0.0)), 0.0)
              for h in heads]
    kbs = [ks[h] * betas[h] for h in heads]
    lowers = [jnp.where(strict, _mm_nt(kbs[h], ks[h]) * decays[h], 0.0) for h in heads]
    intras = [jnp.where(causal, _mm_nt(qs[h], ks[h]) * decays[h], 0.0) for h in heads]
    xs = [eye - jnp.where(level_masks[0], lowers[h], 0.0) for h in heads]
    for mask in level_masks[1:]:
        ts = [_mm(xs[h], jnp.where(mask, lowers[h], 0.0)) for h in heads]
        xs = [xs[h] - _mm(ts[h], xs[h]) for h in heads]
    us = [_mm(xs[h], vs[h] * betas[h]) for h in heads]
    ws = [_mm(xs[h], kbs[h] * ecols[h]) for h in heads]
    v_news = [us[h] - _mm(ws[h], ss[h]) for h in heads]
    os_ = [_mm(qs[h] * ecols[h], ss[h]) + _mm(intras[h], v_news[h]) for h in heads]
    s_news = [ss[h] * jnp.exp(glasts[h]) + _mm_tn(ks[h] * jnp.exp(glasts[h] - gcols[h]), v_news[h])
              for h in heads]
    for h in heads:
        o_ref[:, sls[h]] = os_[h]
        s_scr[h] = s_news[h]

    @pl.when(c == pl.num_programs(1) - 1)
    def _():
        sfin_ref[0] = s_scr[...]


def _gdn_prompt(q, k, v, gb, batch, seq):
    n = GDN_BLOCK
    nc = seq // n

    def tok(width):
        return pl.BlockSpec((n, width), lambda b, c: (b * nc + c, 0))

    return pl.pallas_call(
        _gdn_prompt_kernel,
        grid=(batch, nc),
        in_specs=[tok(GDN_V), tok(GDN_V), tok(GDN_V), tok(LANES)],
        out_specs=[tok(GDN_V), pl.BlockSpec((1, GDN_HEADS, GDN_DK, GDN_DK), lambda b, c: (b, 0, 0, 0))],
        out_shape=[jax.ShapeDtypeStruct((batch * seq, GDN_V), F32),
                   jax.ShapeDtypeStruct((batch, GDN_HEADS, GDN_DK, GDN_DK), F32)],
        scratch_shapes=[pltpu.VMEM((GDN_HEADS, GDN_DK, GDN_DK), F32)],
        compiler_params=_cparams(("arbitrary", "arbitrary")),
        name="gdn_prompt",
    )(q, k, v, gb)


def _post_prompt_kernel(x_ref, conv_ref, o_ref, z_ref, gon_ref, wout_ref, gx_ref, wq_ref, mk_ref, mv_ref,
                        wo_ref, gmoe_ref, wr_ref, br_ref, tri_ref, h3s_ref, x2_ref, h3r_ref, route_ref, rt_ref,
                        cnt_ref, carry, *, n_steps):
    step = pl.program_id(0)

    @pl.when(step == 0)
    def _():
        carry[...] = jnp.zeros(carry.shape, F32)

    @pl.when(step < n_steps)
    def _():
        x1 = _mix_out(conv_ref[...], o_ref[...], z_ref[...], gon_ref[...], wout_ref, x_ref[...])
        qx = jnp.dot(_rms(x1, gx_ref[...]).astype(BF16), wq_ref[...], preferred_element_type=F32)
        outs = []
        for h in range(X_HEADS):
            sl = slice(h * X_HEAD_DIM, (h + 1) * X_HEAD_DIM)
            s = _mm_nt(qx[:, sl], mk_ref[:, sl]) * (X_HEAD_DIM ** -0.5)
            e = jnp.exp(s - jnp.max(s, axis=-1, keepdims=True))
            p = e / jnp.sum(e, axis=-1, keepdims=True)
            outs.append(jnp.dot(p.astype(BF16), mv_ref[:, sl], preferred_element_type=F32))
        att = jnp.concatenate(outs, axis=1)
        x2 = x1 + jnp.dot(att.astype(BF16), wo_ref[...], preferred_element_type=F32)
        x2_ref[...] = x2
        h3, route, new_carry = _route(x2, gmoe_ref[...], wr_ref, br_ref[...], carry[0:1, :], tri_ref[...])
        h3r_ref[...] = h3
        route_ref[...] = route
        rt_ref[...] = route.T[0:ROUTE_ROWS, :]
        carry[0:1, :] = new_carry
        cnt_ref[...] = jnp.broadcast_to(new_carry, cnt_ref.shape)

    @pl.when(step == n_steps)
    def _():
        h3r_ref[0:h3s_ref.shape[0], :] = h3s_ref[...]


def _post_prompt(x2d, conv, o, z, mk_b, mv_b, h3_sample, batch, seq, wts):
    tm = TOKEN_TILE
    nj = seq // tm
    rows = batch * seq
    n_steps = batch * nj
    n_s = h3_sample.shape[0]
    assert n_s <= tm

    def tok(width):
        return pl.BlockSpec((tm, width), lambda s: (jnp.minimum(s, n_steps - 1), 0))

    mem_spec = pl.BlockSpec((N_MEM, D_MODEL), lambda s: (jnp.minimum(s, n_steps - 1) // nj, 0))
    sq = _full((D_MODEL, D_MODEL))
    in_specs = [tok(D_MODEL), tok(CONV_CH), tok(GDN_V), tok(GDN_V), _full((1, GDN_DK)), sq,
                _full((1, D_MODEL)), sq, mem_spec, mem_spec, sq, _full((1, D_MODEL)),
                _full((D_MODEL, LANES)), _full((1, LANES)), _full((tm, tm)), _full(h3_sample.shape)]
    out_specs = [tok(D_MODEL),
                 pl.BlockSpec((tm, D_MODEL), lambda s: (s, 0)),
                 tok(LANES),
                 pl.BlockSpec((ROUTE_ROWS, tm), lambda s: (0, jnp.minimum(s, n_steps - 1))),
                 _full((SUBLANES, LANES))]
    out_shape = [jax.ShapeDtypeStruct((rows, D_MODEL), F32),
                 jax.ShapeDtypeStruct((rows + n_s, D_MODEL), F32),
                 jax.ShapeDtypeStruct((rows, LANES), F32),
                 jax.ShapeDtypeStruct((ROUTE_ROWS, rows), F32),
                 jax.ShapeDtypeStruct((SUBLANES, LANES), F32)]
    return pl.pallas_call(
        functools.partial(_post_prompt_kernel, n_steps=n_steps),
        grid=(n_steps + 1,),
        in_specs=in_specs,
        out_specs=out_specs,
        out_shape=out_shape,
        scratch_shapes=[pltpu.VMEM((SUBLANES, LANES), F32)],
        compiler_params=_cparams(("arbitrary",)),
        name="post_prompt",
    )(x2d, conv, o, z, wts["g_onorm"], wts["w_out_b"], wts["g_xattn"], wts["w_xq_b"], mk_b, mv_b,
      wts["w_xo_b"], wts["g_moe"], wts["w_router"], wts["b_router"], _strict_lower(tm), h3_sample)


def _pre_sample_kernel(x_ref, gmix_ref, w_ref, wab_ref, bglu_ref, wdw_ref, bdw_ref, lng_ref, lnb_ref,
                       wsc_ref, cst_ref, chist_ref, shist_ref, conv_ref, q_ref, k_ref, v_ref, z_ref,
                       gb_ref, cnew_ref, snew_ref):
    glu, qkv_pre, z, uab = _project(x_ref[...], gmix_ref[...], w_ref, wab_ref, bglu_ref[...])
    z_ref[...] = z
    gb_ref[...] = _gate_beta(uab, cst_ref[...])
    kw = CONV_WIDTH
    acc = wdw_ref[kw - 1:kw, :] * glu
    for t in range(kw - 1):
        row = chist_ref[:, t, :]
        acc = acc + wdw_ref[t:t + 1, :] * row
        if t >= 1:
            cnew_ref[:, t - 1, :] = row
    cnew_ref[:, kw - 2, :] = glu
    conv_ref[...] = _conv_post(acc, bdw_ref[...], lng_ref[...], lnb_ref[...]).astype(BF16)
    ks = SHORT_CONV
    cs = wsc_ref[ks - 1:ks, :] * qkv_pre
    for t in range(ks - 1):
        row = shist_ref[:, t, :]
        cs = cs + wsc_ref[t:t + 1, :] * row
        if t >= 1:
            snew_ref[:, t - 1, :] = row
    snew_ref[:, ks - 2, :] = qkv_pre
    q, k, v = _qkv_post(cs)
    q_ref[...] = q
    k_ref[...] = k
    v_ref[...] = v


PRE_SAMPLE_TOKENS = 32


def _pre_sample(xs, chist, shist, wts):
    n = xs.shape[0]
    tb = min(PRE_SAMPLE_TOKENS, n)

    def tok(width):
        return pl.BlockSpec((tb, width), lambda i: (i, 0))

    def hist(a):
        return pl.BlockSpec((tb,) + a.shape[1:], lambda i: (i, 0, 0))

    consts = (wts["g_mix"], wts["w_in_b"], wts["w_ab_b"], wts["b_glu"], wts["w_dw"], wts["b_dw"],
              wts["ln_g"], wts["ln_b"], wts["w_sc"], wts["gdn_cst"])
    return pl.pallas_call(
        _pre_sample_kernel,
        grid=(n // tb,),
        in_specs=[tok(D_MODEL)] + [_full(a.shape) for a in consts] + [hist(chist), hist(shist)],
        out_specs=[tok(CONV_CH), tok(GDN_V), tok(GDN_V), tok(GDN_V), tok(GDN_V), tok(LANES),
                   hist(chist), hist(shist)],
        out_shape=[jax.ShapeDtypeStruct((n, CONV_CH), BF16)]
        + [jax.ShapeDtypeStruct((n, GDN_V), F32)] * 4
        + [jax.ShapeDtypeStruct((n, LANES), F32), jax.ShapeDtypeStruct(chist.shape, F32),
           jax.ShapeDtypeStruct(shist.shape, F32)],
        compiler_params=_cparams(("arbitrary",)),
        name="pre_sample",
    )(xs, *consts, chist, shist)


GDN_STEP_TOKENS = 8


def _gdn_sample_kernel(q_ref, k_ref, v_ref, gb_ref, s_ref, o_ref, snew_ref):
    n = GDN_DK
    for i in range(GDN_STEP_TOKENS):
        for h in range(GDN_HEADS):
            sl = slice(h * GDN_DK, (h + 1) * GDN_DK)
            qrow = q_ref[i:i + 1, sl]
            krow = k_ref[i:i + 1, sl]
            vrow = v_ref[i:i + 1, sl]
            g = gb_ref[i:i + 1, h:h + 1]
            beta = gb_ref[i:i + 1, GDN_HEADS + h:GDN_HEADS + h + 1]
            kcol = jnp.broadcast_to(krow, (n, n)).T
            qcol = jnp.broadcast_to(qrow, (n, n)).T
            s1 = s_ref[i, h] * jnp.exp(g)
            sk = jnp.sum(s1 * kcol, axis=0, keepdims=True)
            vt = (vrow - sk) * beta
            s2 = s1 + kcol * vt
            snew_ref[i, h] = s2
            o_ref[i:i + 1, sl] = jnp.sum(s2 * qcol, axis=0, keepdims=True)


def _gdn_sample(q, k, v, gb, state):
    n = q.shape[0]
    tb = GDN_STEP_TOKENS

    def tok(width):
        return pl.BlockSpec((tb, width), lambda i: (i, 0))

    st = pl.BlockSpec((tb, GDN_HEADS, GDN_DK, GDN_DK), lambda i: (i, 0, 0, 0))
    return pl.pallas_call(
        _gdn_sample_kernel,
        grid=(n // tb,),
        in_specs=[tok(GDN_V), tok(GDN_V), tok(GDN_V), tok(LANES), st],
        out_specs=[tok(GDN_V), st],
        out_shape=[jax.ShapeDtypeStruct((n, GDN_V), F32), jax.ShapeDtypeStruct(state.shape, F32)],
        compiler_params=_cparams(("arbitrary",)),
        name="gdn_sample",
    )(q, k, v, gb, state)


def _mix_sample_kernel(x_ref, conv_ref, o_ref, z_ref, gon_ref, wout_ref, gx_ref, wq_ref, x1_ref, qx_ref):
    x1 = _mix_out(conv_ref[...], o_ref[...], z_ref[...], gon_ref[...], wout_ref, x_ref[...])
    x1_ref[...] = x1
    qx_ref[...] = jnp.dot(_rms(x1, gx_ref[...]).astype(BF16), wq_ref[...], preferred_element_type=F32)


def _mix_sample(xs, conv, o, z, wts):
    n = xs.shape[0]
    in_arrays = (xs, conv, o, z, wts["g_onorm"], wts["w_out_b"], wts["g_xattn"], wts["w_xq_b"])
    return pl.pallas_call(
        _mix_sample_kernel,
        grid=(1,),
        in_specs=[_full(a.shape) for a in in_arrays],
        out_specs=[_full((n, D_MODEL))] * 2,
        out_shape=[jax.ShapeDtypeStruct((n, D_MODEL), F32)] * 2,
        compiler_params=_cparams(("arbitrary",)),
        name="mix_sample",
    )(*in_arrays)


ATTN_STEP_TOKENS = 4


def _attn_sample_kernel(qx_ref, ck_ref, cv_ref, att_ref):
    for i in range(ATTN_STEP_TOKENS):
        parts = []
        for h in range(X_HEADS):
            sl = slice(h * X_HEAD_DIM, (h + 1) * X_HEAD_DIM)
            prod = ck_ref[i, :, h, :] * qx_ref[0, i:i + 1, sl]
            s = jnp.sum(prod, axis=-1, keepdims=True) * (X_HEAD_DIM ** -0.5)
            e = jnp.exp(s - jnp.max(s, axis=0, keepdims=True))
            p = e / jnp.sum(e, axis=0, keepdims=True)
            parts.append(jnp.sum(p * cv_ref[i, :, h, :], axis=0, keepdims=True))
        att_ref[0, i:i + 1, :] = jnp.concatenate(parts, axis=1)


def _attn_sample(qx, ck, cv):
    n = qx.shape[0]
    tb = ATTN_STEP_TOKENS
    q3 = qx.reshape(n // tb, tb, D_MODEL)
    qspec = pl.BlockSpec((1, tb, D_MODEL), lambda i: (i, 0, 0))
    cspec = pl.BlockSpec((tb, N_MEM, X_HEADS, X_HEAD_DIM), lambda i: (i, 0, 0, 0))
    out = pl.pallas_call(
        _attn_sample_kernel,
        grid=(n // tb,),
        in_specs=[qspec, cspec, cspec],
        out_specs=qspec,
        out_shape=jax.ShapeDtypeStruct(q3.shape, F32),
        compiler_params=_cparams(("arbitrary",)),
        name="attn_sample",
    )(q3, ck, cv)
    return out.reshape(n, D_MODEL)


def _route_sample_kernel(x1_ref, att_ref, wo_ref, gmoe_ref, wr_ref, br_ref, tri_ref, x2_ref, h3r_ref, route_ref,
                         rt_ref, cnt_ref):
    x2 = x1_ref[...] + jnp.dot(att_ref[...].astype(BF16), wo_ref[...], preferred_element_type=F32)
    x2_ref[...] = x2
    h3, route, counts = _route(x2, gmoe_ref[...], wr_ref, br_ref[...], jnp.zeros((1, LANES), F32),
                               tri_ref[...])
    h3r_ref[...] = h3
    route_ref[...] = route
    rt_ref[...] = route.T[0:ROUTE_ROWS, :]
    cnt_ref[...] = jnp.broadcast_to(counts, cnt_ref.shape)


def _route_sample(x1, att, wts):
    n = x1.shape[0]
    in_arrays = (x1, att, wts["w_xo_b"], wts["g_moe"], wts["w_router"], wts["b_router"],
                 _strict_lower(n))
    shapes = [(n, D_MODEL), (n, D_MODEL), (n, LANES), (ROUTE_ROWS, n), (SUBLANES, LANES)]
    return pl.pallas_call(
        _route_sample_kernel,
        grid=(1,),
        in_specs=[_full(a.shape) for a in in_arrays],
        out_specs=[_full(s) for s in shapes],
        out_shape=[jax.ShapeDtypeStruct(s, F32) for s in shapes],
        compiler_params=_cparams(("arbitrary",)),
        name="route_sample",
    )(*in_arrays)


def _sc_chunk(rows_per_worker):
    for c in range(64, 0, -SUBLANES):
        if rows_per_worker % c == 0:
            return c
    raise ValueError(rows_per_worker)


def _sc_gather_rows(table, idx):
    n_workers = SC_CORES * SC_SUBCORES
    b = idx.shape[0]
    assert b % (n_workers * SUBLANES) == 0
    per_worker = b // n_workers
    chunk = _sc_chunk(per_worker)
    row_shape = table.shape[1:]
    mesh = plsc.VectorSubcoreMesh(core_axis_name="c", subcore_axis_name="s")

    @functools.partial(
        pl.kernel, mesh=mesh,
        out_type=jax.ShapeDtypeStruct((b,) + row_shape, table.dtype),
        scratch_types=[pltpu.VMEM((chunk,), I32), pltpu.VMEM((chunk,) + row_shape, table.dtype),
                       pltpu.SemaphoreType.DMA],
        name="sc_gather_rows",
    )
    def gather(table_hbm, idx_hbm, out_hbm, idx_v, rows_v, sem):
        worker = lax.axis_index("s") * SC_CORES + lax.axis_index("c")
        base = worker * per_worker

        @pl.loop(0, per_worker // chunk)
        def _(c):
            off = pl.multiple_of(base + c * chunk, SUBLANES)
            pltpu.sync_copy(idx_hbm.at[pl.ds(off, chunk)], idx_v)
            pltpu.async_copy(table_hbm.at[idx_v], rows_v, sem).wait()
            pltpu.sync_copy(rows_v, out_hbm.at[pl.ds(off, chunk)])

    return gather(table, idx)


def _moe_kernel(te_ref, nt_ref, xs_ref, wgu_ref, bgu_ref, wdn_ref, bdn_ref, ys_ref, wgu_b, wdn_b):
    i = pl.program_id(0)
    total = nt_ref[0]

    @pl.when(i < total)
    def _():
        prev = te_ref[jnp.maximum(i - 1, 0)]
        fresh = jnp.logical_or(i == 0, te_ref[i] != prev)

        @pl.when(fresh)
        def _():
            wgu_b[...] = wgu_ref[0].astype(BF16)
            wdn_b[...] = wdn_ref[0].astype(BF16)

        x = xs_ref[...].astype(BF16)
        gu = jnp.dot(x, wgu_b[...], preferred_element_type=F32) + bgu_ref[0]
        x_glu = jnp.minimum(gu[:, :D_EXPERT], SWIGLU_LIMIT)
        x_lin = jnp.clip(gu[:, D_EXPERT:], -SWIGLU_LIMIT, SWIGLU_LIMIT)
        act = x_glu * jax.nn.sigmoid(SWIGLU_ALPHA * x_glu) * (x_lin + 1.0)
        y = jnp.dot(act.astype(BF16), wdn_b[...], preferred_element_type=F32) + bdn_ref[0]
        ys_ref[...] = y

    @pl.when(i >= total)
    def _():
        ys_ref[...] = jnp.zeros(ys_ref.shape, F32)


def _moe(tile_e, n_tiles, xs, w_gu, b_gu, w_dn, b_dn):
    tm = MOE_TILE
    n_rows = xs.shape[0]
    row_spec = pl.BlockSpec((tm, D_MODEL), lambda i, te, nt: (i, 0))
    grid_spec = pltpu.PrefetchScalarGridSpec(
        num_scalar_prefetch=2,
        grid=(n_rows // tm,),
        in_specs=[
            row_spec,
            pl.BlockSpec((1, D_MODEL, 2 * D_EXPERT), lambda i, te, nt: (te[i], 0, 0)),
            pl.BlockSpec((1, 1, 2 * D_EXPERT), lambda i, te, nt: (te[i], 0, 0)),
            pl.BlockSpec((1, D_EXPERT, D_MODEL), lambda i, te, nt: (te[i], 0, 0)),
            pl.BlockSpec((1, 1, D_MODEL), lambda i, te, nt: (te[i], 0, 0)),
        ],
        out_specs=row_spec,
        scratch_shapes=[pltpu.VMEM((D_MODEL, 2 * D_EXPERT), BF16),
                        pltpu.VMEM((D_EXPERT, D_MODEL), BF16)],
    )
    return pl.pallas_call(
        _moe_kernel,
        grid_spec=grid_spec,
        out_shape=jax.ShapeDtypeStruct(xs.shape, F32),
        compiler_params=_cparams(("arbitrary",), vmem=56 * 1024 * 1024),
        name="moe",
    )(tile_e, n_tiles, xs, w_gu, b_gu, w_dn, b_dn)


def _combine_kernel(x2_ref, route_ref, gfin_ref, yt_ref, y_ref):
    route = route_ref[...]
    acc = x2_ref[...]
    for j in range(TOP_K):
        acc = acc + route[:, TOP_K + j:TOP_K + j + 1] * yt_ref[j]
    y_ref[...] = _rms(acc, gfin_ref[...])


def _combine(x2, route, g_final, ys_tok, first_block):
    n = x2.shape[0]
    tc = min(COMBINE_TILE, n)
    return pl.pallas_call(
        _combine_kernel,
        grid=(n // tc,),
        in_specs=[pl.BlockSpec((tc, D_MODEL), lambda i: (i, 0)),
                  pl.BlockSpec((tc, LANES), lambda i: (i, 0)),
                  pl.BlockSpec((1, D_MODEL), lambda i: (0, 0)),
                  pl.BlockSpec((TOP_K, tc, D_MODEL), lambda i: (0, first_block + i, 0))],
        out_specs=pl.BlockSpec((tc, D_MODEL), lambda i: (i, 0)),
        out_shape=jax.ShapeDtypeStruct((n, D_MODEL), F32),
        compiler_params=_cparams(("arbitrary",)),
        name="combine",
    )(x2, route, g_final, ys_tok)


def _routing_tables(idx_t, rank_t, counts, n_rows):
    tm = MOE_TILE
    n_tok = idx_t.shape[1]
    n_assign = TOP_K * n_tok
    tok_mask = (1 << TOKEN_BITS) - 1
    tiles_e = (counts + tm - 1) // tm
    tile_end = jnp.cumsum(tiles_e)
    row_start = (tile_end - tiles_e) * tm
    total = tile_end[-1]
    expert_ids = jnp.arange(N_EXPERTS, dtype=I32)

    def lookup(table, e):
        return jnp.sum(jnp.where(e[..., None] == expert_ids, table, 0), axis=-1)

    pos = lookup(row_start, idx_t) + rank_t
    keys_real = (idx_t * (1 << TOKEN_BITS) + jnp.arange(n_tok, dtype=I32)[None, :]).reshape(-1)
    k = jnp.arange(n_rows - n_assign, dtype=I32)
    pad_e, pad_s = k // tm, k % tm
    pad_needed = lookup(tiles_e * tm - counts, pad_e)
    pad_key_e = jnp.where((pad_e < N_EXPERTS) & (pad_s < pad_needed), pad_e, N_EXPERTS)
    keys = jnp.sort(jnp.concatenate([keys_real, pad_key_e * (1 << TOKEN_BITS) + tok_mask]))
    src_tok = jnp.where((keys & tok_mask) == tok_mask, jnp.arange(n_rows, dtype=I32) % n_tok,
                        keys & tok_mask)
    tid = jnp.minimum(jnp.arange(n_rows // tm, dtype=I32), total - 1)
    tile_e = jnp.minimum(jnp.sum((tid[:, None] >= tile_end[None, :]).astype(I32), axis=1), N_EXPERTS - 1)
    return tile_e.astype(I32), total.reshape(1).astype(I32), src_tok.astype(I32), pos.reshape(-1).astype(I32)


def _pad_rows(a, rows):
    return jnp.concatenate([a, jnp.zeros((rows - a.shape[0],) + a.shape[1:], a.dtype)], axis=0)


def _pad_lanes(a, lanes=LANES):
    return jnp.concatenate([a, jnp.zeros(a.shape[:-1] + (lanes - a.shape[-1],), a.dtype)], axis=-1)


def kernel(x_prompt, mem_prompt, x_sample, state_conformer_conv, state_gdn_conv, state_gdn, cache_mem_k,
           cache_mem_v, w_in, b_glu, w_dw, b_dw, ln_g, ln_b, w_sc, a_log, dt_bias, g_onorm, w_out, g_mix,
           g_xattn, g_mem, w_xq, w_mk, w_mv, w_xo, g_moe, w_router, b_router, w_gu, b_gu, w_dn, b_dn,
           g_final):
    assert w_in.shape[0] == 1, "single-layer configuration"
    batch, seq, _ = x_prompt.shape
    n_s = x_sample.shape[0]
    n_p = batch * seq
    n_all = n_p + n_s
    assert seq % TOKEN_TILE == 0 and n_p % n_s == 0 and n_all < (1 << TOKEN_BITS) - 1
    assert (n_all * TOP_K) % (SC_CORES * SC_SUBCORES * SUBLANES) == 0

    wts = {
        "g_mix": g_mix[0][None], "g_xattn": g_xattn[0][None], "g_moe": g_moe[0][None],
        "g_onorm": g_onorm[0][None],
        "w_in_b": w_in[0][:, :OFF_A].astype(BF16),
        "w_ab_b": _pad_lanes(w_in[0][:, OFF_A:]).astype(BF16),
        "b_glu": b_glu[0][None],
        "w_dw": _pad_rows(w_dw[0], 32), "b_dw": b_dw[0][None], "ln_g": ln_g[0][None], "ln_b": ln_b[0][None],
        "w_sc": _pad_rows(w_sc[0], 8),
        "gdn_cst": _pad_rows(_pad_lanes(jnp.stack([a_log[0], dt_bias[0]])), 8),
        "w_out_b": w_out[0].astype(BF16), "w_xq_b": w_xq[0].astype(BF16), "w_xo_b": w_xo[0].astype(BF16),
        "w_router": _pad_lanes(w_router[0]), "b_router": _pad_lanes(b_router[0][None]),
    }

    mk, mv, mk_b, mv_b = _mem_kv(mem_prompt.reshape(batch * N_MEM, D_MODEL), g_mem[0][None],
                                 w_mk[0].astype(BF16), w_mv[0].astype(BF16))
    xp = x_prompt.reshape(n_p, D_MODEL)
    conv_p, q_p, k_p, v_p, z_p, gb_p, cstate_p, sstate_p = _pre_prompt(xp, batch, seq, wts)
    o_p, gstate_p = _gdn_prompt(q_p, k_p, v_p, gb_p, batch, seq)

    xs = x_sample.reshape(n_s, D_MODEL)
    conv_s, q_s, k_s, v_s, z_s, gb_s, cstate_s, sstate_s = _pre_sample(
        xs, state_conformer_conv[0], state_gdn_conv[0], wts)
    o_s, gstate_s = _gdn_sample(q_s, k_s, v_s, gb_s, state_gdn[0])
    x1_s, qx_s = _mix_sample(xs, conv_s, o_s, z_s, wts)
    att_s = _attn_sample(qx_s, cache_mem_k[0], cache_mem_v[0])
    x2_s, h3_s, route_s, rt_s, counts_s = _route_sample(x1_s, att_s, wts)

    x2_p, h3r, route_p, rt_p, counts_p = _post_prompt(xp, conv_p, o_p, z_p, mk_b, mv_b, h3_s, batch, seq,
                                                      wts)

    counts_p = counts_p[0, :N_EXPERTS].astype(I32)
    counts_s = counts_s[0, :N_EXPERTS].astype(I32)
    idx_s = rt_s[0:TOP_K].astype(I32)
    rank_s = rt_s[2 * TOP_K:3 * TOP_K].astype(I32) + jnp.sum(
        jnp.where(idx_s[..., None] == jnp.arange(N_EXPERTS, dtype=I32), counts_p, 0), axis=-1)
    idx_t = jnp.concatenate([rt_p[0:TOP_K].astype(I32), idx_s], axis=1)
    rank_t = jnp.concatenate([rt_p[2 * TOP_K:3 * TOP_K].astype(I32), rank_s], axis=1)
    n_rows = -(-(n_all * TOP_K + N_EXPERTS * (MOE_TILE - 1)) // MOE_TILE) * MOE_TILE
    tile_e, n_tiles, src_tok, pos = _routing_tables(idx_t, rank_t, counts_p + counts_s, n_rows)
    xs = _sc_gather_rows(h3r, src_tok)
    ys = _moe(tile_e, n_tiles, xs, w_gu[0], b_gu[0][:, None, :], w_dn[0], b_dn[0][:, None, :])
    ys_tok = _sc_gather_rows(ys, pos).reshape(TOP_K, n_all, D_MODEL)
    gfin = g_final[None]
    y_p = _combine(x2_p, route_p, gfin, ys_tok, 0)
    y_s = _combine(x2_s, route_s, gfin, ys_tok, n_p // min(COMBINE_TILE, n_s))

    return (y_p.reshape(batch, seq, D_MODEL), y_s.reshape(n_s, 1, D_MODEL),
            cstate_p[None], sstate_p[None], gstate_p[None],
            mk.reshape(1, batch, N_MEM, X_HEADS, X_HEAD_DIM), mv.reshape(1, batch, N_MEM, X_HEADS, X_HEAD_DIM),
            cstate_s[None], sstate_s[None], gstate_s[None])
```

```python
import functools

import jax
import jax.numpy as jnp
from jax import lax
from jax.experimental import pallas as pl
from jax.experimental.pallas import tpu as pltpu
from jax.experimental.pallas import tpu_sc as plsc

F32, BF16, I32 = jnp.float32, jnp.bfloat16, jnp.int32

D_MODEL = 1024
CONV_CH = 512
CONV_WIDTH = 31
GDN_HEADS = 4
GDN_DK = 128
GDN_V = 512
QKV_CH = 1536
SHORT_CONV = 4
N_MEM = 256
X_HEADS = 4
X_HEAD_DIM = 256
N_EXPERTS = 32
TOP_K = 4
D_EXPERT = 1024
SWIGLU_LIMIT = 7.0
SWIGLU_ALPHA = 1.702
NORM_EPS = 1e-6
OFF_QKV = 2 * CONV_CH
OFF_Z = OFF_QKV + QKV_CH
OFF_A = OFF_Z + GDN_V

LANES = 128
SUBLANES = 8
GDN_BLOCK = 128
TOKEN_TILE = 256
MOE_TILE = 256
MOE_COLS = 256
MOE_PARTS = 4
COMBINE_TILE = 128
TOKEN_BITS = 15
ROUTE_ROWS = 16
SC_CORES = 2
SC_SUBCORES = 16
VMEM_LIMIT = 48 * 1024 * 1024


def _cparams(sem, vmem=VMEM_LIMIT):
    return pltpu.CompilerParams(dimension_semantics=sem, vmem_limit_bytes=vmem)


def _mm(a, b):
    return jnp.dot(a.astype(BF16), b.astype(BF16), preferred_element_type=F32)


def _mm_nt(a, b):
    return lax.dot_general(a.astype(BF16), b.astype(BF16), (((1,), (1,)), ((), ())),
                           preferred_element_type=F32)


def _mm_tn(a, b):
    return lax.dot_general(a.astype(BF16), b.astype(BF16), (((0,), (0,)), ((), ())),
                           preferred_element_type=F32)


def _rms(x, g):
    return x * lax.rsqrt(jnp.mean(x * x, axis=-1, keepdims=True) + NORM_EPS) * g


def _silu(x):
    return x * jax.nn.sigmoid(x)


def _full(shape):
    return pl.BlockSpec(shape, lambda *_: (0,) * len(shape))


def _strict_lower(n):
    return jnp.tril(jnp.ones((n, n), BF16), k=-1)


def _project(x, gmix, w_ref, wab_ref, bglu):
    h = _rms(x, gmix).astype(BF16)
    u_glu = jnp.dot(h, w_ref[:, 0:OFF_QKV], preferred_element_type=F32) + bglu
    glu = u_glu[:, :CONV_CH] * jax.nn.sigmoid(u_glu[:, CONV_CH:])
    qkv_pre = jnp.dot(h, w_ref[:, OFF_QKV:OFF_Z], preferred_element_type=F32)
    z = jnp.dot(h, w_ref[:, OFF_Z:OFF_A], preferred_element_type=F32)
    uab = jnp.dot(h, wab_ref[...], preferred_element_type=F32)
    return glu, qkv_pre, z, uab


def _gate_beta(uab, cst):
    lane = lax.broadcasted_iota(I32, uab.shape, 1)
    g = -jnp.exp(cst[0:1, :]) * jax.nn.softplus(uab + cst[1:2, :])
    return jnp.where(lane < GDN_HEADS, g, jax.nn.sigmoid(uab))


def _conv_post(c, b_dw, ln_g, ln_b):
    c = c + b_dw
    mu = jnp.mean(c, axis=-1, keepdims=True)
    d = c - mu
    var = jnp.mean(d * d, axis=-1, keepdims=True)
    return _silu(d * lax.rsqrt(var + NORM_EPS) * ln_g + ln_b)


def _qkv_post(cs):
    a = _silu(cs)
    parts = []
    for h in range(2 * GDN_HEADS):
        seg = a[:, h * GDN_DK:(h + 1) * GDN_DK]
        n = seg * lax.rsqrt(jnp.sum(seg * seg, axis=-1, keepdims=True) + NORM_EPS)
        if h < GDN_HEADS:
            n = n * (GDN_DK ** -0.5)
        parts.append(n)
    q = jnp.concatenate(parts[:GDN_HEADS], axis=1)
    k = jnp.concatenate(parts[GDN_HEADS:], axis=1)
    return q, k, a[:, 2 * GDN_HEADS * GDN_DK:]


def _mix_out(conv_b, o, z, gon, wout_ref, x):
    parts = []
    for h in range(GDN_HEADS):
        oh = o[:, h * 128:(h + 1) * 128]
        parts.append(oh * lax.rsqrt(jnp.mean(oh * oh, axis=-1, keepdims=True) + NORM_EPS) * gon)
    on = jnp.concatenate(parts, axis=1) * _silu(z)
    mixed = (jnp.dot(conv_b, wout_ref[0:CONV_CH, :], preferred_element_type=F32)
             + jnp.dot(on.astype(BF16), wout_ref[CONV_CH:, :], preferred_element_type=F32))
    return x + mixed


def _route(x2, gmoe, wr_ref, br, carry, before):
    m = x2.shape[0]
    h3 = _rms(x2, gmoe)
    h_hi = h3.astype(BF16)
    r1 = h3 - h_hi.astype(F32)
    h_mid = r1.astype(BF16)
    h_lo = (r1 - h_mid.astype(F32)).astype(BF16)
    w = wr_ref[...]
    w_hi = w.astype(BF16)
    w_lo = (w - w_hi.astype(F32)).astype(BF16)
    logits = (jnp.dot(h_hi, w_hi, preferred_element_type=F32)
              + jnp.dot(h_hi, w_lo, preferred_element_type=F32)
              + jnp.dot(h_mid, w_hi, preferred_element_type=F32)
              + jnp.dot(h_lo, w_hi, preferred_element_type=F32)) + br
    lane = lax.broadcasted_iota(I32, (m, LANES), 1)
    neg = jnp.float32(-jnp.inf)
    work = jnp.where(lane < N_EXPERTS, logits, neg)
    vals, idxs = [], []
    for _ in range(TOP_K):
        mx = jnp.max(work, axis=-1, keepdims=True)
        ix = jnp.min(jnp.where(work == mx, lane, LANES), axis=-1, keepdims=True)
        vals.append(mx)
        idxs.append(ix)
        work = jnp.where(lane == ix, neg, work)
    es = [jnp.exp(v - vals[0]) for v in vals]
    den = es[0] + es[1] + es[2] + es[3]
    gates = [e / den for e in es]
    sel = jnp.zeros((m, LANES), F32)
    for ix in idxs:
        sel = sel + jnp.where(lane == ix, 1.0, 0.0)
    rank_full = jnp.dot(before, sel.astype(BF16), preferred_element_type=F32) + carry
    route = jnp.zeros((m, LANES), F32)
    for r in range(TOP_K):
        rk = jnp.sum(jnp.where(lane == idxs[r], rank_full, 0.0), axis=-1, keepdims=True)
        route = (route + jnp.where(lane == r, idxs[r].astype(F32), 0.0)
                 + jnp.where(lane == TOP_K + r, gates[r], 0.0)
                 + jnp.where(lane == 2 * TOP_K + r, rk, 0.0))
    new_carry = carry + jnp.sum(sel, axis=0, keepdims=True)
    return h3, route, new_carry


def _mem_kv_kernel(mem_ref, g_ref, wk_ref, wv_ref, mk_ref, mv_ref, mkb_ref, mvb_ref):
    m = _rms(mem_ref[...], g_ref[...]).astype(BF16)
    mk = jnp.dot(m, wk_ref[...], preferred_element_type=F32)
    mv = jnp.dot(m, wv_ref[...], preferred_element_type=F32)
    mk_ref[...] = mk
    mv_ref[...] = mv
    mkb_ref[...] = mk.astype(BF16)
    mvb_ref[...] = mv.astype(BF16)


def _mem_kv(mem, g_mem, wk_b, wv_b):
    rows = mem.shape[0]
    tm = TOKEN_TILE
    row_spec = pl.BlockSpec((tm, D_MODEL), lambda i: (i, 0))
    return pl.pallas_call(
        _mem_kv_kernel,
        grid=(rows // tm,),
        in_specs=[row_spec, _full((1, D_MODEL)), _full((D_MODEL, D_MODEL)), _full((D_MODEL, D_MODEL))],
        out_specs=[row_spec] * 4,
        out_shape=[jax.ShapeDtypeStruct((rows, D_MODEL), F32)] * 2
        + [jax.ShapeDtypeStruct((rows, D_MODEL), BF16)] * 2,
        compiler_params=_cparams(("arbitrary",)),
        name="mem_kv",
    )(mem, g_mem, wk_b, wv_b)


CONV_HALO = 32
SC_HALO = 8


def _pre_prompt_kernel(x_ref, gmix_ref, w_ref, wab_ref, bglu_ref, wdw_ref, bdw_ref, lng_ref, lnb_ref,
                       wsc_ref, cst_ref, conv_ref, q_ref, k_ref, v_ref, z_ref, gb_ref, cstate_ref,
                       sstate_ref, cbuf, sbuf, *, tm):
    j = pl.program_id(1)

    @pl.when(j == 0)
    def _():
        cbuf[0:CONV_HALO, :] = jnp.zeros((CONV_HALO, CONV_CH), F32)
        sbuf[0:SC_HALO, :] = jnp.zeros((SC_HALO, QKV_CH), F32)

    glu, qkv_pre, z, uab = _project(x_ref[...], gmix_ref[...], w_ref, wab_ref, bglu_ref[...])
    cbuf[CONV_HALO:CONV_HALO + tm, :] = glu
    sbuf[SC_HALO:SC_HALO + tm, :] = qkv_pre
    z_ref[...] = z
    gb_ref[...] = _gate_beta(uab, cst_ref[...])

    base = CONV_HALO - (CONV_WIDTH - 1)
    rows = CONV_HALO + tm
    accs = []
    for c in range(CONV_CH // LANES):
        lanes = slice(c * LANES, (c + 1) * LANES)
        block = cbuf[:, lanes]
        acc = None
        for r in range(SUBLANES):
            shifted = block if r == 0 else pltpu.roll(block, rows - r, 0)
            for a in range(base, base + CONV_WIDTH):
                if a % SUBLANES == r:
                    t = a - base
                    term = wdw_ref[t:t + 1, lanes] * shifted[a - r:a - r + tm, :]
                    acc = term if acc is None else acc + term
        accs.append(acc)
    acc = jnp.concatenate(accs, axis=1)
    conv_ref[...] = _conv_post(acc, bdw_ref[...], lng_ref[...], lnb_ref[...]).astype(BF16)

    sbase = SC_HALO - (SHORT_CONV - 1)
    cs = wsc_ref[0:1, :] * sbuf[pl.ds(sbase, tm), :]
    for t in range(1, SHORT_CONV):
        cs = cs + wsc_ref[t:t + 1, :] * sbuf[pl.ds(sbase + t, tm), :]
    q, k, v = _qkv_post(cs)
    q_ref[...] = q
    k_ref[...] = k
    v_ref[...] = v

    @pl.when(j == pl.num_programs(1) - 1)
    def _():
        cstate_ref[0] = cbuf[pl.ds(CONV_HALO + tm - (CONV_WIDTH - 1), CONV_WIDTH - 1), :]
        sstate_ref[0] = sbuf[pl.ds(SC_HALO + tm - (SHORT_CONV - 1), SHORT_CONV - 1), :]

    cbuf[0:CONV_HALO, :] = cbuf[tm:tm + CONV_HALO, :]
    sbuf[0:SC_HALO, :] = sbuf[tm:tm + SC_HALO, :]


def _pre_prompt(x2d, batch, seq, wts):
    tm = TOKEN_TILE
    nj = seq // tm
    rows = batch * seq

    def tok(width):
        return pl.BlockSpec((tm, width), lambda b, j: (b * nj + j, 0))

    in_specs = [tok(D_MODEL), _full((1, D_MODEL)), _full((D_MODEL, OFF_A)), _full((D_MODEL, LANES)),
                _full((1, OFF_QKV)), _full((32, CONV_CH)), _full((1, CONV_CH)), _full((1, CONV_CH)),
                _full((1, CONV_CH)), _full((8, QKV_CH)), _full((8, LANES))]
    out_specs = [tok(CONV_CH), tok(GDN_V), tok(GDN_V), tok(GDN_V), tok(GDN_V), tok(LANES),
                 pl.BlockSpec((1, CONV_WIDTH - 1, CONV_CH), lambda b, j: (b, 0, 0)),
                 pl.BlockSpec((1, SHORT_CONV - 1, QKV_CH), lambda b, j: (b, 0, 0))]
    out_shape = [jax.ShapeDtypeStruct((rows, CONV_CH), BF16)] \
        + [jax.ShapeDtypeStruct((rows, GDN_V), F32)] * 4 \
        + [jax.ShapeDtypeStruct((rows, LANES), F32),
           jax.ShapeDtypeStruct((batch, CONV_WIDTH - 1, CONV_CH), F32),
           jax.ShapeDtypeStruct((batch, SHORT_CONV - 1, QKV_CH), F32)]
    return pl.pallas_call(
        functools.partial(_pre_prompt_kernel, tm=tm),
        grid=(batch, nj),
        in_specs=in_specs,
        out_specs=out_specs,
        out_shape=out_shape,
        scratch_shapes=[pltpu.VMEM((CONV_HALO + tm, CONV_CH), F32),
                        pltpu.VMEM((SC_HALO + tm, QKV_CH), F32)],
        compiler_params=_cparams(("arbitrary", "arbitrary")),
        name="pre_prompt",
    )(x2d, wts["g_mix"], wts["w_in_b"], wts["w_ab_b"], wts["b_glu"], wts["w_dw"], wts["b_dw"],
      wts["ln_g"], wts["ln_b"], wts["w_sc"], wts["gdn_cst"])


def _gdn_prompt_kernel(q_ref, k_ref, v_ref, gb_ref, o_ref, sfin_ref, s_scr):
    c = pl.program_id(1)
    n = GDN_BLOCK

    @pl.when(c == 0)
    def _():
        s_scr[...] = jnp.zeros(s_scr.shape, F32)

    gb = gb_ref[...]
    row = lax.broadcasted_iota(I32, (n, n), 0)
    col = lax.broadcasted_iota(I32, (n, n), 1)
    causal = row >= col
    strict = row > col
    tri = jnp.where(causal, 1.0, 0.0).astype(BF16)
    eye = jnp.where(row == col, 1.0, 0.0)
    level_masks = []
    b = 1
    while b < n:
        same_pair = ((row ^ col) & ~(2 * b - 1)) == 0
        level_masks.append(same_pair & ((row & b) != 0) & ((col & b) == 0))
        b *= 2
    g1 = gb.astype(BF16)
    r1 = gb - g1.astype(F32)
    g2 = r1.astype(BF16)
    g3 = (r1 - g2.astype(F32)).astype(BF16)
    gcum = (jnp.dot(tri, g1, preferred_element_type=F32) + jnp.dot(tri, g2, preferred_element_type=F32)
            + jnp.dot(tri, g3, preferred_element_type=F32))
    gcum_t = gcum.T
    egc = jnp.exp(gcum)
    heads = range(GDN_HEADS)
    sls = [slice(h * GDN_DK, (h + 1) * GDN_DK) for h in heads]
    qs = [q_ref[:, sl] for sl in sls]
    ks = [k_ref[:, sl] for sl in sls]
    vs = [v_ref[:, sl] for sl in sls]
    ss = [s_scr[h] for h in heads]
    gcols = [gcum[:, h:h + 1] for h in heads]
    ecols = [egc[:, h:h + 1] for h in heads]
    betas = [gb[:, GDN_HEADS + h:GDN_HEADS + h + 1] for h in heads]
    glasts = [gcum[n - 1:n, h:h + 1] for h in heads]
    decays = [jnp.where(causal, jnp.exp(jnp.where(causal, gcols[h] - gcum_t[h:h + 1, :], 0.0)), 0.0)
              for h in heads]
    kbs = [ks[h] * betas[h] for h in heads]
    lowers = [jnp.where(strict, _mm_nt(kbs[h], ks[h]) * decays[h], 0.0) for h in heads]
    intras = [jnp.where(causal, _mm_nt(qs[h], ks[h]) * decays[h], 0.0) for h in heads]
    xs = [eye - jnp.where(level_masks[0], lowers[h], 0.0) for h in heads]
    for mask in level_masks[1:]:
        ts = [_mm(xs[h], jnp.where(mask, lowers[h], 0.0)) for h in heads]
        xs = [xs[h] - _mm(ts[h], xs[h]) for h in heads]
    us = [_mm(xs[h], vs[h] * betas[h]) for h in heads]
    ws = [_mm(xs[h], kbs[h] * ecols[h]) for h in heads]
    v_news = [us[h] - _mm(ws[h], ss[h]) for h in heads]
    os_ = [_mm(qs[h] * ecols[h], ss[h]) + _mm(intras[h], v_news[h]) for h in heads]
    s_news = [ss[h] * jnp.exp(glasts[h]) + _mm_tn(ks[h] * jnp.exp(glasts[h] - gcols[h]), v_news[h])
              for h in heads]
    for h in heads:
        o_ref[:, sls[h]] = os_[h]
        s_scr[h] = s_news[h]

    @pl.when(c == pl.num_programs(1) - 1)
    def _():
        sfin_ref[0] = s_scr[...]


def _gdn_prompt(q, k, v, gb, batch, seq):
    n = GDN_BLOCK
    nc = seq // n

    def tok(width):
        return pl.BlockSpec((n, width), lambda b, c: (b * nc + c, 0))

    return pl.pallas_call(
        _gdn_prompt_kernel,
        grid=(batch, nc),
        in_specs=[tok(GDN_V), tok(GDN_V), tok(GDN_V), tok(LANES)],
        out_specs=[tok(GDN_V), pl.BlockSpec((1, GDN_HEADS, GDN_DK, GDN_DK), lambda b, c: (b, 0, 0, 0))],
        out_shape=[jax.ShapeDtypeStruct((batch * seq, GDN_V), F32),
                   jax.ShapeDtypeStruct((batch, GDN_HEADS, GDN_DK, GDN_DK), F32)],
        scratch_shapes=[pltpu.VMEM((GDN_HEADS, GDN_DK, GDN_DK), F32)],
        compiler_params=_cparams(("arbitrary", "arbitrary")),
        name="gdn_prompt",
    )(q, k, v, gb)


def _post_prompt_kernel(x_ref, conv_ref, o_ref, z_ref, gon_ref, wout_ref, gx_ref, wq_ref, mk_ref, mv_ref,
                        wo_ref, gmoe_ref, wr_ref, br_ref, tri_ref, h3s_ref, x2_ref, h3r_ref, route_ref, rt_ref,
                        cnt_ref, carry, *, n_steps):
    step = pl.program_id(0)

    @pl.when(step == 0)
    def _():
        carry[...] = jnp.zeros(carry.shape, F32)

    @pl.when(step < n_steps)
    def _():
        x1 = _mix_out(conv_ref[...], o_ref[...], z_ref[...], gon_ref[...], wout_ref, x_ref[...])
        qx = jnp.dot(_rms(x1, gx_ref[...]).astype(BF16), wq_ref[...], preferred_element_type=F32)
        sls = [slice(h * X_HEAD_DIM, (h + 1) * X_HEAD_DIM) for h in range(X_HEADS)]
        qb = qx.astype(BF16)
        ss = [lax.dot_general(qb[:, sl], mk_ref[:, sl], (((1,), (1,)), ((), ())),
                              preferred_element_type=F32) * (X_HEAD_DIM ** -0.5) for sl in sls]
        es = [jnp.exp(s - jnp.max(s, axis=-1, keepdims=True)) for s in ss]
        ps = [(e / jnp.sum(e, axis=-1, keepdims=True)).astype(BF16) for e in es]
        att = jnp.concatenate(
            [jnp.dot(p, mv_ref[:, sl], preferred_element_type=F32) for p, sl in zip(ps, sls)], axis=1)
        x2 = x1 + jnp.dot(att.astype(BF16), wo_ref[...], preferred_element_type=F32)
        x2_ref[...] = x2
        h3, route, new_carry = _route(x2, gmoe_ref[...], wr_ref, br_ref[...], carry[0:1, :], tri_ref[...])
        h3r_ref[...] = h3
        route_ref[...] = route
        rt_ref[...] = route.T[0:ROUTE_ROWS, :]
        carry[0:1, :] = new_carry
        cnt_ref[...] = jnp.broadcast_to(new_carry, cnt_ref.shape)

    @pl.when(step == n_steps)
    def _():
        h3r_ref[0:h3s_ref.shape[0], :] = h3s_ref[...]


def _post_prompt(x2d, conv, o, z, mk_b, mv_b, h3_sample, batch, seq, wts):
    tm = TOKEN_TILE
    nj = seq // tm
    rows = batch * seq
    n_steps = batch * nj
    n_s = h3_sample.shape[0]
    assert n_s <= tm

    def tok(width):
        return pl.BlockSpec((tm, width), lambda s: (jnp.minimum(s, n_steps - 1), 0))

    mem_spec = pl.BlockSpec((N_MEM, D_MODEL), lambda s: (jnp.minimum(s, n_steps - 1) // nj, 0))
    sq = _full((D_MODEL, D_MODEL))
    in_specs = [tok(D_MODEL), tok(CONV_CH), tok(GDN_V), tok(GDN_V), _full((1, GDN_DK)), sq,
                _full((1, D_MODEL)), sq, mem_spec, mem_spec, sq, _full((1, D_MODEL)),
                _full((D_MODEL, LANES)), _full((1, LANES)), _full((tm, tm)), _full(h3_sample.shape)]
    out_specs = [tok(D_MODEL),
                 pl.BlockSpec((tm, D_MODEL), lambda s: (s, 0)),
                 tok(LANES),
                 pl.BlockSpec((ROUTE_ROWS, tm), lambda s: (0, jnp.minimum(s, n_steps - 1))),
                 _full((SUBLANES, LANES))]
    out_shape = [jax.ShapeDtypeStruct((rows, D_MODEL), F32),
                 jax.ShapeDtypeStruct((rows + n_s, D_MODEL), F32),
                 jax.ShapeDtypeStruct((rows, LANES), F32),
                 jax.ShapeDtypeStruct((ROUTE_ROWS, rows), F32),
                 jax.ShapeDtypeStruct((SUBLANES, LANES), F32)]
    return pl.pallas_call(
        functools.partial(_post_prompt_kernel, n_steps=n_steps),
        grid=(n_steps + 1,),
        in_specs=in_specs,
        out_specs=out_specs,
        out_shape=out_shape,
        scratch_shapes=[pltpu.VMEM((SUBLANES, LANES), F32)],
        compiler_params=_cparams(("arbitrary",)),
        name="post_prompt",
    )(x2d, conv, o, z, wts["g_onorm"], wts["w_out_b"], wts["g_xattn"], wts["w_xq_b"], mk_b, mv_b,
      wts["w_xo_b"], wts["g_moe"], wts["w_router"], wts["b_router"], _strict_lower(tm), h3_sample)


def _pre_sample_kernel(x_ref, gmix_ref, w_ref, wab_ref, bglu_ref, wdw_ref, bdw_ref, lng_ref, lnb_ref,
                       wsc_ref, cst_ref, chist_ref, shist_ref, conv_ref, q_ref, k_ref, v_ref, z_ref,
                       gb_ref, cnew_ref, snew_ref):
    glu, qkv_pre, z, uab = _project(x_ref[...], gmix_ref[...], w_ref, wab_ref, bglu_ref[...])
    z_ref[...] = z
    gb_ref[...] = _gate_beta(uab, cst_ref[...])
    kw = CONV_WIDTH
    acc = wdw_ref[kw - 1:kw, :] * glu
    for t in range(kw - 1):
        row = chist_ref[:, t, :]
        acc = acc + wdw_ref[t:t + 1, :] * row
        if t >= 1:
            cnew_ref[:, t - 1, :] = row
    cnew_ref[:, kw - 2, :] = glu
    conv_ref[...] = _conv_post(acc, bdw_ref[...], lng_ref[...], lnb_ref[...]).astype(BF16)
    ks = SHORT_CONV
    cs = wsc_ref[ks - 1:ks, :] * qkv_pre
    for t in range(ks - 1):
        row = shist_ref[:, t, :]
        cs = cs + wsc_ref[t:t + 1, :] * row
        if t >= 1:
            snew_ref[:, t - 1, :] = row
    snew_ref[:, ks - 2, :] = qkv_pre
    q, k, v = _qkv_post(cs)
    q_ref[...] = q
    k_ref[...] = k
    v_ref[...] = v


PRE_SAMPLE_TOKENS = 32


def _pre_sample(xs, chist, shist, wts):
    n = xs.shape[0]
    tb = min(PRE_SAMPLE_TOKENS, n)

    def tok(width):
        return pl.BlockSpec((tb, width), lambda i: (i, 0))

    def hist(a):
        return pl.BlockSpec((tb,) + a.shape[1:], lambda i: (i, 0, 0))

    consts = (wts["g_mix"], wts["w_in_b"], wts["w_ab_b"], wts["b_glu"], wts["w_dw"], wts["b_dw"],
              wts["ln_g"], wts["ln_b"], wts["w_sc"], wts["gdn_cst"])
    return pl.pallas_call(
        _pre_sample_kernel,
        grid=(n // tb,),
        in_specs=[tok(D_MODEL)] + [_full(a.shape) for a in consts] + [hist(chist), hist(shist)],
        out_specs=[tok(CONV_CH), tok(GDN_V), tok(GDN_V), tok(GDN_V), tok(GDN_V), tok(LANES),
                   hist(chist), hist(shist)],
        out_shape=[jax.ShapeDtypeStruct((n, CONV_CH), BF16)]
        + [jax.ShapeDtypeStruct((n, GDN_V), F32)] * 4
        + [jax.ShapeDtypeStruct((n, LANES), F32), jax.ShapeDtypeStruct(chist.shape, F32),
           jax.ShapeDtypeStruct(shist.shape, F32)],
        compiler_params=_cparams(("arbitrary",)),
        name="pre_sample",
    )(xs, *consts, chist, shist)


GDN_STEP_TOKENS = 8


def _gdn_sample_kernel(q_ref, k_ref, v_ref, gb_ref, s_ref, o_ref, snew_ref):
    n = GDN_DK
    for i in range(GDN_STEP_TOKENS):
        for h in range(GDN_HEADS):
            sl = slice(h * GDN_DK, (h + 1) * GDN_DK)
            qrow = q_ref[i:i + 1, sl]
            krow = k_ref[i:i + 1, sl]
            vrow = v_ref[i:i + 1, sl]
            g = gb_ref[i:i + 1, h:h + 1]
            beta = gb_ref[i:i + 1, GDN_HEADS + h:GDN_HEADS + h + 1]
            kcol = jnp.broadcast_to(krow, (n, n)).T
            qcol = jnp.broadcast_to(qrow, (n, n)).T
            s1 = s_ref[i, h] * jnp.exp(g)
            sk = jnp.sum(s1 * kcol, axis=0, keepdims=True)
            vt = (vrow - sk) * beta
            s2 = s1 + kcol * vt
            snew_ref[i, h] = s2
            o_ref[i:i + 1, sl] = jnp.sum(s2 * qcol, axis=0, keepdims=True)


def _gdn_sample(q, k, v, gb, state):
    n = q.shape[0]
    tb = GDN_STEP_TOKENS

    def tok(width):
        return pl.BlockSpec((tb, width), lambda i: (i, 0))

    st = pl.BlockSpec((tb, GDN_HEADS, GDN_DK, GDN_DK), lambda i: (i, 0, 0, 0))
    return pl.pallas_call(
        _gdn_sample_kernel,
        grid=(n // tb,),
        in_specs=[tok(GDN_V), tok(GDN_V), tok(GDN_V), tok(LANES), st],
        out_specs=[tok(GDN_V), st],
        out_shape=[jax.ShapeDtypeStruct((n, GDN_V), F32), jax.ShapeDtypeStruct(state.shape, F32)],
        compiler_params=_cparams(("arbitrary",)),
        name="gdn_sample",
    )(q, k, v, gb, state)


def _mix_sample_kernel(x_ref, conv_ref, o_ref, z_ref, gon_ref, wout_ref, gx_ref, wq_ref, x1_ref, qx_ref):
    x1 = _mix_out(conv_ref[...], o_ref[...], z_ref[...], gon_ref[...], wout_ref, x_ref[...])
    x1_ref[...] = x1
    qx_ref[...] = jnp.dot(_rms(x1, gx_ref[...]).astype(BF16), wq_ref[...], preferred_element_type=F32)


def _mix_sample(xs, conv, o, z, wts):
    n = xs.shape[0]
    in_arrays = (xs, conv, o, z, wts["g_onorm"], wts["w_out_b"], wts["g_xattn"], wts["w_xq_b"])
    return pl.pallas_call(
        _mix_sample_kernel,
        grid=(1,),
        in_specs=[_full(a.shape) for a in in_arrays],
        out_specs=[_full((n, D_MODEL))] * 2,
        out_shape=[jax.ShapeDtypeStruct((n, D_MODEL), F32)] * 2,
        compiler_params=_cparams(("arbitrary",)),
        name="mix_sample",
    )(*in_arrays)


ATTN_STEP_TOKENS = 4


def _attn_sample_kernel(qx_ref, ck_ref, cv_ref, att_ref):
    for i in range(ATTN_STEP_TOKENS):
        parts = []
        for h in range(X_HEADS):
            sl = slice(h * X_HEAD_DIM, (h + 1) * X_HEAD_DIM)
            prod = ck_ref[i, :, h, :] * qx_ref[0, i:i + 1, sl]
            s = jnp.sum(prod, axis=-1, keepdims=True) * (X_HEAD_DIM ** -0.5)
            e = jnp.exp(s - jnp.max(s, axis=0, keepdims=True))
            p = e / jnp.sum(e, axis=0, keepdims=True)
            parts.append(jnp.sum(p * cv_ref[i, :, h, :], axis=0, keepdims=True))
        att_ref[0, i:i + 1, :] = jnp.concatenate(parts, axis=1)


def _attn_sample(qx, ck, cv):
    n = qx.shape[0]
    tb = ATTN_STEP_TOKENS
    q3 = qx.reshape(n // tb, tb, D_MODEL)
    qspec = pl.BlockSpec((1, tb, D_MODEL), lambda i: (i, 0, 0))
    cspec = pl.BlockSpec((tb, N_MEM, X_HEADS, X_HEAD_DIM), lambda i: (i, 0, 0, 0))
    out = pl.pallas_call(
        _attn_sample_kernel,
        grid=(n // tb,),
        in_specs=[qspec, cspec, cspec],
        out_specs=qspec,
        out_shape=jax.ShapeDtypeStruct(q3.shape, F32),
        compiler_params=_cparams(("arbitrary",)),
        name="attn_sample",
    )(q3, ck, cv)
    return out.reshape(n, D_MODEL)


def _route_sample_kernel(x1_ref, att_ref, wo_ref, gmoe_ref, wr_ref, br_ref, tri_ref, x2_ref, h3r_ref, route_ref,
                         rt_ref, cnt_ref):
    x2 = x1_ref[...] + jnp.dot(att_ref[...].astype(BF16), wo_ref[...], preferred_element_type=F32)
    x2_ref[...] = x2
    h3, route, counts = _route(x2, gmoe_ref[...], wr_ref, br_ref[...], jnp.zeros((1, LANES), F32),
                               tri_ref[...])
    h3r_ref[...] = h3
    route_ref[...] = route
    rt_ref[...] = route.T[0:ROUTE_ROWS, :]
    cnt_ref[...] = jnp.broadcast_to(counts, cnt_ref.shape)


def _route_sample(x1, att, wts):
    n = x1.shape[0]
    in_arrays = (x1, att, wts["w_xo_b"], wts["g_moe"], wts["w_router"], wts["b_router"],
                 _strict_lower(n))
    shapes = [(n, D_MODEL), (n, D_MODEL), (n, LANES), (ROUTE_ROWS, n), (SUBLANES, LANES)]
    return pl.pallas_call(
        _route_sample_kernel,
        grid=(1,),
        in_specs=[_full(a.shape) for a in in_arrays],
        out_specs=[_full(s) for s in shapes],
        out_shape=[jax.ShapeDtypeStruct(s, F32) for s in shapes],
        compiler_params=_cparams(("arbitrary",)),
        name="route_sample",
    )(*in_arrays)


def _part_tiles(tiles):
    while not any(tiles % d == 0 for d in (8, 7, 6, 5, 4)):
        tiles += 1
    return tiles


def _sc_chunk(rows_per_worker):
    for c in range(64, 0, -SUBLANES):
        if rows_per_worker % c == 0:
            return c
    raise ValueError(rows_per_worker)


def _sc_gather_rows(table, idx):
    n_workers = SC_CORES * SC_SUBCORES
    b = idx.shape[0]
    assert b % (n_workers * SUBLANES) == 0
    per_worker = b // n_workers
    chunk = _sc_chunk(per_worker)
    row_shape = table.shape[1:]
    mesh = plsc.VectorSubcoreMesh(core_axis_name="c", subcore_axis_name="s")

    @functools.partial(
        pl.kernel, mesh=mesh,
        out_type=jax.ShapeDtypeStruct((b,) + row_shape, table.dtype),
        scratch_types=[pltpu.VMEM((chunk,), I32), pltpu.VMEM((chunk,) + row_shape, table.dtype),
                       pltpu.SemaphoreType.DMA],
        name="sc_gather_rows",
    )
    def gather(table_hbm, idx_hbm, out_hbm, idx_v, rows_v, sem):
        worker = lax.axis_index("s") * SC_CORES + lax.axis_index("c")
        base = worker * per_worker

        @pl.loop(0, per_worker // chunk)
        def _(c):
            off = pl.multiple_of(base + c * chunk, SUBLANES)
            pltpu.sync_copy(idx_hbm.at[pl.ds(off, chunk)], idx_v)
            pltpu.async_copy(table_hbm.at[idx_v], rows_v, sem).wait()
            pltpu.sync_copy(rows_v, out_hbm.at[pl.ds(off, chunk)])

    return gather(table, idx)


def _moe_kernel(te_ref, nt_ref, xs_ref, wgu_ref, bgu_ref, wdn_ref, bdn_ref, *rest, first_tile):
    ys_ref, wgu_b, wdn_b = rest[-3:]
    step = pl.program_id(0)
    i = first_tile + step
    total = nt_ref[0]

    @pl.when(i < total)
    def _():
        prev = te_ref[jnp.maximum(i - 1, 0)]
        fresh = jnp.logical_or(step == 0, te_ref[i] != prev)

        @pl.when(fresh)
        def _():
            wgu_b[...] = wgu_ref[0].astype(BF16)
            wdn_b[...] = wdn_ref[0].astype(BF16)

        x = xs_ref[...].astype(BF16)

        def up(c):
            glu_cols = slice(c * MOE_COLS, (c + 1) * MOE_COLS)
            lin_cols = slice(D_EXPERT + c * MOE_COLS, D_EXPERT + (c + 1) * MOE_COLS)
            return (jnp.dot(x, wgu_b[:, glu_cols], preferred_element_type=F32) + bgu_ref[0, :, glu_cols],
                    jnp.dot(x, wgu_b[:, lin_cols], preferred_element_type=F32) + bgu_ref[0, :, lin_cols])

        n_chunks = D_EXPERT // MOE_COLS
        nxt = up(0)
        y = None
        for c in range(n_chunks):
            g, lin = nxt
            if c + 1 < n_chunks:
                nxt = up(c + 1)
            x_glu = jnp.minimum(g, SWIGLU_LIMIT)
            x_lin = jnp.clip(lin, -SWIGLU_LIMIT, SWIGLU_LIMIT)
            act = x_glu * jax.nn.sigmoid(SWIGLU_ALPHA * x_glu) * (x_lin + 1.0)
            part = jnp.dot(act.astype(BF16), wdn_b[c * MOE_COLS:(c + 1) * MOE_COLS, :],
                           preferred_element_type=F32)
            y = part if y is None else y + part
        ys_ref[...] = y + bdn_ref[0]

    @pl.when(i >= total)
    def _():
        ys_ref[...] = jnp.zeros(ys_ref.shape, F32)


def _moe(tile_e, n_tiles, xs_part, ys_prev, first_tile, n_rows, w_gu, b_gu, w_dn, b_dn):
    tm = MOE_TILE

    def weights(shape):
        return pl.BlockSpec(shape, lambda i, te, nt: (te[first_tile + i], 0, 0))

    in_specs = [pl.BlockSpec((tm, D_MODEL), lambda i, te, nt: (i, 0)),
                weights((1, D_MODEL, 2 * D_EXPERT)), weights((1, 1, 2 * D_EXPERT)),
                weights((1, D_EXPERT, D_MODEL)), weights((1, 1, D_MODEL))]
    operands = [tile_e, n_tiles, xs_part, w_gu, b_gu, w_dn, b_dn]
    aliases = {}
    if ys_prev is not None:
        in_specs.append(pl.BlockSpec(memory_space=pl.ANY))
        aliases = {len(operands): 0}
        operands.append(ys_prev)
    grid_spec = pltpu.PrefetchScalarGridSpec(
        num_scalar_prefetch=2,
        grid=(xs_part.shape[0] // tm,),
        in_specs=in_specs,
        out_specs=pl.BlockSpec((tm, D_MODEL), lambda i, te, nt: (first_tile + i, 0)),
        scratch_shapes=[pltpu.VMEM((D_MODEL, 2 * D_EXPERT), BF16),
                        pltpu.VMEM((D_EXPERT, D_MODEL), BF16)],
    )
    return pl.pallas_call(
        functools.partial(_moe_kernel, first_tile=first_tile),
        grid_spec=grid_spec,
        out_shape=jax.ShapeDtypeStruct((n_rows, D_MODEL), F32),
        input_output_aliases=aliases,
        compiler_params=_cparams(("arbitrary",), vmem=56 * 1024 * 1024),
        name="moe",
    )(*operands)


def _combine_kernel(x2_ref, route_ref, gfin_ref, yt_ref, *rest):
    y_ref = rest[-1]
    route = route_ref[...]
    acc = x2_ref[...]
    for j in range(TOP_K):
        acc = acc + route[:, TOP_K + j:TOP_K + j + 1] * yt_ref[j]
    y_ref[...] = _rms(acc, gfin_ref[...])


def _combine(x2, route, g_final, ys_tok, y_prev, tok0, n_tok, ys_block0):
    tc = min(COMBINE_TILE, n_tok)
    b0 = tok0 // tc

    def tok(width):
        return pl.BlockSpec((tc, width), lambda i: (b0 + i, 0))

    in_specs = [tok(D_MODEL), tok(LANES), pl.BlockSpec((1, D_MODEL), lambda i: (0, 0)),
                pl.BlockSpec((TOP_K, tc, D_MODEL), lambda i: (0, ys_block0 + i, 0))]
    operands = [x2, route, g_final, ys_tok]
    aliases = {}
    if y_prev is not None:
        in_specs.append(pl.BlockSpec(memory_space=pl.ANY))
        aliases = {len(operands): 0}
        operands.append(y_prev)
    return pl.pallas_call(
        _combine_kernel,
        grid=(n_tok // tc,),
        in_specs=in_specs,
        out_specs=tok(D_MODEL),
        out_shape=jax.ShapeDtypeStruct(x2.shape, F32),
        input_output_aliases=aliases,
        compiler_params=_cparams(("arbitrary",)),
        name="combine",
    )(*operands)


def _routing_tables(idx_t, rank_t, counts, n_rows):
    tm = MOE_TILE
    n_tok = idx_t.shape[1]
    n_assign = TOP_K * n_tok
    tok_mask = (1 << TOKEN_BITS) - 1
    tiles_e = (counts + tm - 1) // tm
    tile_end = jnp.cumsum(tiles_e)
    row_start = (tile_end - tiles_e) * tm
    total = tile_end[-1]
    expert_ids = jnp.arange(N_EXPERTS, dtype=I32)

    def lookup(table, e):
        return jnp.sum(jnp.where(e[..., None] == expert_ids, table, 0), axis=-1)

    pos = lookup(row_start, idx_t) + rank_t
    keys_real = (idx_t * (1 << TOKEN_BITS) + jnp.arange(n_tok, dtype=I32)[None, :]).reshape(-1)
    k = jnp.arange(n_rows - n_assign, dtype=I32)
    pad_e, pad_s = k // tm, k % tm
    pad_needed = lookup(tiles_e * tm - counts, pad_e)
    pad_key_e = jnp.where((pad_e < N_EXPERTS) & (pad_s < pad_needed), pad_e, N_EXPERTS)
    keys = jnp.sort(jnp.concatenate([keys_real, pad_key_e * (1 << TOKEN_BITS) + tok_mask]))
    src_tok = jnp.where((keys & tok_mask) == tok_mask, jnp.arange(n_rows, dtype=I32) % n_tok,
                        keys & tok_mask)
    tid = jnp.minimum(jnp.arange(n_rows // tm, dtype=I32), total - 1)
    tile_e = jnp.minimum(jnp.sum((tid[:, None] >= tile_end[None, :]).astype(I32), axis=1), N_EXPERTS - 1)
    return tile_e.astype(I32), total.reshape(1).astype(I32), src_tok.astype(I32), pos.reshape(-1).astype(I32)


def _pad_rows(a, rows):
    return jnp.concatenate([a, jnp.zeros((rows - a.shape[0],) + a.shape[1:], a.dtype)], axis=0)


def _pad_lanes(a, lanes=LANES):
    return jnp.concatenate([a, jnp.zeros(a.shape[:-1] + (lanes - a.shape[-1],), a.dtype)], axis=-1)


def kernel(x_prompt, mem_prompt, x_sample, state_conformer_conv, state_gdn_conv, state_gdn, cache_mem_k,
           cache_mem_v, w_in, b_glu, w_dw, b_dw, ln_g, ln_b, w_sc, a_log, dt_bias, g_onorm, w_out, g_mix,
           g_xattn, g_mem, w_xq, w_mk, w_mv, w_xo, g_moe, w_router, b_router, w_gu, b_gu, w_dn, b_dn,
           g_final):
    assert w_in.shape[0] == 1, "single-layer configuration"
    batch, seq, _ = x_prompt.shape
    n_s = x_sample.shape[0]
    n_p = batch * seq
    n_all = n_p + n_s
    assert seq % TOKEN_TILE == 0 and n_p % n_s == 0 and n_all < (1 << TOKEN_BITS) - 1
    assert (n_all * TOP_K) % (SC_CORES * SC_SUBCORES * SUBLANES) == 0

    wts = {
        "g_mix": g_mix[0][None], "g_xattn": g_xattn[0][None], "g_moe": g_moe[0][None],
        "g_onorm": g_onorm[0][None],
        "w_in_b": w_in[0][:, :OFF_A].astype(BF16),
        "w_ab_b": _pad_lanes(w_in[0][:, OFF_A:]).astype(BF16),
        "b_glu": b_glu[0][None],
        "w_dw": _pad_rows(w_dw[0], 32), "b_dw": b_dw[0][None], "ln_g": ln_g[0][None], "ln_b": ln_b[0][None],
        "w_sc": _pad_rows(w_sc[0], 8),
        "gdn_cst": _pad_rows(_pad_lanes(jnp.stack([a_log[0], dt_bias[0]])), 8),
        "w_out_b": w_out[0].astype(BF16), "w_xq_b": w_xq[0].astype(BF16), "w_xo_b": w_xo[0].astype(BF16),
        "w_router": _pad_lanes(w_router[0]), "b_router": _pad_lanes(b_router[0][None]),
    }

    mk, mv, mk_b, mv_b = _mem_kv(mem_prompt.reshape(batch * N_MEM, D_MODEL), g_mem[0][None],
                                 w_mk[0].astype(BF16), w_mv[0].astype(BF16))
    xp = x_prompt.reshape(n_p, D_MODEL)
    conv_p, q_p, k_p, v_p, z_p, gb_p, cstate_p, sstate_p = _pre_prompt(xp, batch, seq, wts)
    o_p, gstate_p = _gdn_prompt(q_p, k_p, v_p, gb_p, batch, seq)

    xs = x_sample.reshape(n_s, D_MODEL)
    conv_s, q_s, k_s, v_s, z_s, gb_s, cstate_s, sstate_s = _pre_sample(
        xs, state_conformer_conv[0], state_gdn_conv[0], wts)
    o_s, gstate_s = _gdn_sample(q_s, k_s, v_s, gb_s, state_gdn[0])
    x1_s, qx_s = _mix_sample(xs, conv_s, o_s, z_s, wts)
    att_s = _attn_sample(qx_s, cache_mem_k[0], cache_mem_v[0])
    x2_s, h3_s, route_s, rt_s, counts_s = _route_sample(x1_s, att_s, wts)

    x2_p, h3r, route_p, rt_p, counts_p = _post_prompt(xp, conv_p, o_p, z_p, mk_b, mv_b, h3_s, batch, seq,
                                                      wts)

    counts_p = counts_p[0, :N_EXPERTS].astype(I32)
    counts_s = counts_s[0, :N_EXPERTS].astype(I32)
    idx_s = rt_s[0:TOP_K].astype(I32)
    rank_s = rt_s[2 * TOP_K:3 * TOP_K].astype(I32) + jnp.sum(
        jnp.where(idx_s[..., None] == jnp.arange(N_EXPERTS, dtype=I32), counts_p, 0), axis=-1)
    idx_t = jnp.concatenate([rt_p[0:TOP_K].astype(I32), idx_s], axis=1)
    rank_t = jnp.concatenate([rt_p[2 * TOP_K:3 * TOP_K].astype(I32), rank_s], axis=1)
    min_tiles = -(-(n_all * TOP_K + N_EXPERTS * (MOE_TILE - 1)) // MOE_TILE)
    tiles_part = _part_tiles(-(-min_tiles // MOE_PARTS))
    rows_part = tiles_part * MOE_TILE
    n_rows = rows_part * MOE_PARTS
    tile_e, n_tiles, src_tok, pos = _routing_tables(idx_t, rank_t, counts_p + counts_s, n_rows)
    pos_t = pos.reshape(TOP_K, n_all)
    xs_parts = [_sc_gather_rows(h3r, src_tok[k * rows_part:(k + 1) * rows_part]) for k in range(MOE_PARTS)]
    ys = None
    for k in range(MOE_PARTS):
        ys = _moe(tile_e, n_tiles, xs_parts[k], ys, k * tiles_part, n_rows, w_gu[0], b_gu[0][:, None, :],
                  w_dn[0], b_dn[0][:, None, :])
    gfin = g_final[None]
    tok_part = n_p // MOE_PARTS
    assert tok_part % COMBINE_TILE == 0
    y_p = None
    for k in range(MOE_PARTS):
        last = k == MOE_PARTS - 1
        pos_k = pos_t[:, k * tok_part:(k + 1) * tok_part]
        if last:
            pos_k = jnp.concatenate([pos_k, pos_t[:, n_p:]], axis=1)
        ys_tok = _sc_gather_rows(ys, pos_k.reshape(-1)).reshape(TOP_K, pos_k.shape[1], D_MODEL)
        y_p = _combine(x2_p, route_p, gfin, ys_tok, y_p, k * tok_part, tok_part, 0)
        if last:
            y_s = _combine(x2_s, route_s, gfin, ys_tok, None, 0, n_s, tok_part // min(COMBINE_TILE, n_s))

    return (y_p.reshape(batch, seq, D_MODEL), y_s.reshape(n_s, 1, D_MODEL),
            cstate_p[None], sstate_p[None], gstate_p[None],
            mk.reshape(1, batch, N_MEM, X_HEADS, X_HEAD_DIM), mv.reshape(1, batch, N_MEM, X_HEADS, X_HEAD_DIM),
            cstate_s[None], sstate_s[None], gstate_s[None])
```

```python
import functools

import jax
import jax.numpy as jnp
from jax import lax
from jax.experimental import pallas as pl
from jax.experimental.pallas import tpu as pltpu
from jax.experimental.pallas import tpu_sc as plsc

F32, BF16, I32 = jnp.float32, jnp.bfloat16, jnp.int32

D_MODEL = 1024
CONV_CH = 512
CONV_WIDTH = 31
GDN_HEADS = 4
GDN_DK = 128
GDN_V = 512
QKV_CH = 1536
SHORT_CONV = 4
N_MEM = 256
X_HEADS = 4
X_HEAD_DIM = 256
N_EXPERTS = 32
TOP_K = 4
D_EXPERT = 1024
SWIGLU_LIMIT = 7.0
SWIGLU_ALPHA = 1.702
NORM_EPS = 1e-6
OFF_QKV = 2 * CONV_CH
OFF_Z = OFF_QKV + QKV_CH
OFF_A = OFF_Z + GDN_V

LANES = 128
SUBLANES = 8
GDN_BLOCK = 128
TOKEN_TILE = 256
MOE_TILE = 512
MOE_COLS = 256
MOE_PARTS = 4
COMBINE_TILE = 128
TOKEN_BITS = 15
ROUTE_ROWS = 16
SC_CORES = 2
SC_SUBCORES = 16
VMEM_LIMIT = 48 * 1024 * 1024


def _cparams(sem, vmem=VMEM_LIMIT):
    return pltpu.CompilerParams(dimension_semantics=sem, vmem_limit_bytes=vmem)


def _mm(a, b):
    return jnp.dot(a.astype(BF16), b.astype(BF16), preferred_element_type=F32)


def _mm_nt(a, b):
    return lax.dot_general(a.astype(BF16), b.astype(BF16), (((1,), (1,)), ((), ())),
                           preferred_element_type=F32)


def _mm_tn(a, b):
    return lax.dot_general(a.astype(BF16), b.astype(BF16), (((0,), (0,)), ((), ())),
                           preferred_element_type=F32)


def _rms(x, g):
    return x * lax.rsqrt(jnp.mean(x * x, axis=-1, keepdims=True) + NORM_EPS) * g


def _silu(x):
    return x * jax.nn.sigmoid(x)


def _full(shape):
    return pl.BlockSpec(shape, lambda *_: (0,) * len(shape))


def _strict_lower(n):
    return jnp.tril(jnp.ones((n, n), BF16), k=-1)


def _project(x, gmix, w_ref, wab_ref, bglu):
    h = _rms(x, gmix).astype(BF16)
    u_glu = jnp.dot(h, w_ref[:, 0:OFF_QKV], preferred_element_type=F32) + bglu
    glu = u_glu[:, :CONV_CH] * jax.nn.sigmoid(u_glu[:, CONV_CH:])
    qkv_pre = jnp.dot(h, w_ref[:, OFF_QKV:OFF_Z], preferred_element_type=F32)
    z = jnp.dot(h, w_ref[:, OFF_Z:OFF_A], preferred_element_type=F32)
    uab = jnp.dot(h, wab_ref[...], preferred_element_type=F32)
    return glu, qkv_pre, z, uab


def _gate_beta(uab, cst):
    lane = lax.broadcasted_iota(I32, uab.shape, 1)
    g = -jnp.exp(cst[0:1, :]) * jax.nn.softplus(uab + cst[1:2, :])
    return jnp.where(lane < GDN_HEADS, g, jax.nn.sigmoid(uab))


def _conv_post(c, b_dw, ln_g, ln_b):
    c = c + b_dw
    mu = jnp.mean(c, axis=-1, keepdims=True)
    d = c - mu
    var = jnp.mean(d * d, axis=-1, keepdims=True)
    return _silu(d * lax.rsqrt(var + NORM_EPS) * ln_g + ln_b)


def _qkv_post(cs):
    a = _silu(cs)
    parts = []
    for h in range(2 * GDN_HEADS):
        seg = a[:, h * GDN_DK:(h + 1) * GDN_DK]
        n = seg * lax.rsqrt(jnp.sum(seg * seg, axis=-1, keepdims=True) + NORM_EPS)
        if h < GDN_HEADS:
            n = n * (GDN_DK ** -0.5)
        parts.append(n)
    q = jnp.concatenate(parts[:GDN_HEADS], axis=1)
    k = jnp.concatenate(parts[GDN_HEADS:], axis=1)
    return q, k, a[:, 2 * GDN_HEADS * GDN_DK:]


def _mix_out(conv_b, o, z, gon, wout_ref, x):
    parts = []
    for h in range(GDN_HEADS):
        oh = o[:, h * 128:(h + 1) * 128]
        parts.append(oh * lax.rsqrt(jnp.mean(oh * oh, axis=-1, keepdims=True) + NORM_EPS) * gon)
    on = jnp.concatenate(parts, axis=1) * _silu(z)
    mixed = (jnp.dot(conv_b, wout_ref[0:CONV_CH, :], preferred_element_type=F32)
             + jnp.dot(on.astype(BF16), wout_ref[CONV_CH:, :], preferred_element_type=F32))
    return x + mixed


def _route(x2, gmoe, wr_ref, br, carry, before):
    m = x2.shape[0]
    h3 = _rms(x2, gmoe)
    h_hi = h3.astype(BF16)
    r1 = h3 - h_hi.astype(F32)
    h_mid = r1.astype(BF16)
    h_lo = (r1 - h_mid.astype(F32)).astype(BF16)
    w = wr_ref[...]
    w_hi = w.astype(BF16)
    w_lo = (w - w_hi.astype(F32)).astype(BF16)
    logits = (jnp.dot(h_hi, w_hi, preferred_element_type=F32)
              + jnp.dot(h_hi, w_lo, preferred_element_type=F32)
              + jnp.dot(h_mid, w_hi, preferred_element_type=F32)
              + jnp.dot(h_lo, w_hi, preferred_element_type=F32)) + br
    lane = lax.broadcasted_iota(I32, (m, LANES), 1)
    neg = jnp.float32(-jnp.inf)
    work = jnp.where(lane < N_EXPERTS, logits, neg)
    vals, idxs = [], []
    for _ in range(TOP_K):
        mx = jnp.max(work, axis=-1, keepdims=True)
        ix = jnp.min(jnp.where(work == mx, lane, LANES), axis=-1, keepdims=True)
        vals.append(mx)
        idxs.append(ix)
        work = jnp.where(lane == ix, neg, work)
    es = [jnp.exp(v - vals[0]) for v in vals]
    den = es[0] + es[1] + es[2] + es[3]
    gates = [e / den for e in es]
    sel = jnp.zeros((m, LANES), F32)
    for ix in idxs:
        sel = sel + jnp.where(lane == ix, 1.0, 0.0)
    rank_full = jnp.dot(before, sel.astype(BF16), preferred_element_type=F32) + carry
    route = jnp.zeros((m, LANES), F32)
    for r in range(TOP_K):
        rk = jnp.sum(jnp.where(lane == idxs[r], rank_full, 0.0), axis=-1, keepdims=True)
        route = (route + jnp.where(lane == r, idxs[r].astype(F32), 0.0)
                 + jnp.where(lane == TOP_K + r, gates[r], 0.0)
                 + jnp.where(lane == 2 * TOP_K + r, rk, 0.0))
    new_carry = carry + jnp.sum(sel, axis=0, keepdims=True)
    return h3, route, new_carry


def _mem_kv_kernel(mem_ref, g_ref, wk_ref, wv_ref, mk_ref, mv_ref, mkb_ref, mvb_ref):
    m = _rms(mem_ref[...], g_ref[...]).astype(BF16)
    mk = jnp.dot(m, wk_ref[...], preferred_element_type=F32)
    mv = jnp.dot(m, wv_ref[...], preferred_element_type=F32)
    mk_ref[...] = mk
    mv_ref[...] = mv
    mkb_ref[...] = mk.astype(BF16)
    mvb_ref[...] = mv.astype(BF16)


def _mem_kv(mem, g_mem, wk_b, wv_b):
    rows = mem.shape[0]
    tm = TOKEN_TILE
    row_spec = pl.BlockSpec((tm, D_MODEL), lambda i: (i, 0))
    return pl.pallas_call(
        _mem_kv_kernel,
        grid=(rows // tm,),
        in_specs=[row_spec, _full((1, D_MODEL)), _full((D_MODEL, D_MODEL)), _full((D_MODEL, D_MODEL))],
        out_specs=[row_spec] * 4,
        out_shape=[jax.ShapeDtypeStruct((rows, D_MODEL), F32)] * 2
        + [jax.ShapeDtypeStruct((rows, D_MODEL), BF16)] * 2,
        compiler_params=_cparams(("arbitrary",)),
        name="mem_kv",
    )(mem, g_mem, wk_b, wv_b)


CONV_HALO = 32
SC_HALO = 8


def _pre_prompt_kernel(x_ref, gmix_ref, w_ref, wab_ref, bglu_ref, wdw_ref, bdw_ref, lng_ref, lnb_ref,
                       cst_ref, conv_ref, qkv_ref, z_ref, gb_ref, cstate_ref, sstate_ref, cbuf, *, tm):
    j = pl.program_id(1)

    @pl.when(j == 0)
    def _():
        cbuf[0:CONV_HALO, :] = jnp.zeros((CONV_HALO, CONV_CH), F32)

    glu, qkv_pre, z, uab = _project(x_ref[...], gmix_ref[...], w_ref, wab_ref, bglu_ref[...])
    cbuf[CONV_HALO:CONV_HALO + tm, :] = glu
    qkv_ref[...] = qkv_pre
    z_ref[...] = z
    gb_ref[...] = _gate_beta(uab, cst_ref[...])

    base = CONV_HALO - (CONV_WIDTH - 1)
    rows = CONV_HALO + tm
    accs = []
    for c in range(CONV_CH // LANES):
        lanes = slice(c * LANES, (c + 1) * LANES)
        block = cbuf[:, lanes]
        acc = None
        for r in range(SUBLANES):
            shifted = block if r == 0 else pltpu.roll(block, rows - r, 0)
            for a in range(base, base + CONV_WIDTH):
                if a % SUBLANES == r:
                    t = a - base
                    term = wdw_ref[t:t + 1, lanes] * shifted[a - r:a - r + tm, :]
                    acc = term if acc is None else acc + term
        accs.append(acc)
    acc = jnp.concatenate(accs, axis=1)
    conv_ref[...] = _conv_post(acc, bdw_ref[...], lng_ref[...], lnb_ref[...]).astype(BF16)

    @pl.when(j == pl.num_programs(1) - 1)
    def _():
        cstate_ref[0] = cbuf[pl.ds(CONV_HALO + tm - (CONV_WIDTH - 1), CONV_WIDTH - 1), :]
        sstate_ref[0] = qkv_pre[tm - (SHORT_CONV - 1):, :]

    cbuf[0:CONV_HALO, :] = cbuf[tm:tm + CONV_HALO, :]


def _pre_prompt(x2d, batch, seq, wts):
    tm = TOKEN_TILE
    nj = seq // tm
    rows = batch * seq

    def tok(width):
        return pl.BlockSpec((tm, width), lambda b, j: (b * nj + j, 0))

    in_specs = [tok(D_MODEL), _full((1, D_MODEL)), _full((D_MODEL, OFF_A)), _full((D_MODEL, LANES)),
                _full((1, OFF_QKV)), _full((32, CONV_CH)), _full((1, CONV_CH)), _full((1, CONV_CH)),
                _full((1, CONV_CH)), _full((8, LANES))]
    out_specs = [tok(CONV_CH), tok(QKV_CH), tok(GDN_V), tok(LANES),
                 pl.BlockSpec((1, CONV_WIDTH - 1, CONV_CH), lambda b, j: (b, 0, 0)),
                 pl.BlockSpec((1, SHORT_CONV - 1, QKV_CH), lambda b, j: (b, 0, 0))]
    out_shape = [jax.ShapeDtypeStruct((rows, CONV_CH), BF16),
                 jax.ShapeDtypeStruct((rows, QKV_CH), F32),
                 jax.ShapeDtypeStruct((rows, GDN_V), F32),
                 jax.ShapeDtypeStruct((rows, LANES), F32),
                 jax.ShapeDtypeStruct((batch, CONV_WIDTH - 1, CONV_CH), F32),
                 jax.ShapeDtypeStruct((batch, SHORT_CONV - 1, QKV_CH), F32)]
    return pl.pallas_call(
        functools.partial(_pre_prompt_kernel, tm=tm),
        grid=(batch, nj),
        in_specs=in_specs,
        out_specs=out_specs,
        out_shape=out_shape,
        scratch_shapes=[pltpu.VMEM((CONV_HALO + tm, CONV_CH), F32)],
        compiler_params=_cparams(("arbitrary", "arbitrary")),
        name="pre_prompt",
    )(x2d, wts["g_mix"], wts["w_in_b"], wts["w_ab_b"], wts["b_glu"], wts["w_dw"], wts["b_dw"],
      wts["ln_g"], wts["ln_b"], wts["gdn_cst"])


GDN_SEQS = 2


def _gdn_prompt_kernel(qkv_ref, gb_ref, wsc_ref, o_ref, sfin_ref, s_scr, sbuf):
    c = pl.program_id(1)
    n = GDN_BLOCK
    seqs = range(GDN_SEQS)

    @pl.when(c == 0)
    def _():
        s_scr[...] = jnp.zeros(s_scr.shape, F32)
        sbuf[:, 0:SC_HALO, :] = jnp.zeros((GDN_SEQS, SC_HALO, QKV_CH), F32)

    sbase = SC_HALO - (SHORT_CONV - 1)
    qkvs = []
    for b in seqs:
        sbuf[b, SC_HALO:SC_HALO + n, :] = qkv_ref[b]
        cs = wsc_ref[0:1, :] * sbuf[b, pl.ds(sbase, n), :]
        for t in range(1, SHORT_CONV):
            cs = cs + wsc_ref[t:t + 1, :] * sbuf[b, pl.ds(sbase + t, n), :]
        qkvs.append(_qkv_post(cs))
        sbuf[b, 0:SC_HALO, :] = sbuf[b, n:n + SC_HALO, :]

    row = lax.broadcasted_iota(I32, (n, n), 0)
    col = lax.broadcasted_iota(I32, (n, n), 1)
    causal = row >= col
    strict = row > col
    tri = jnp.where(causal, 1.0, 0.0).astype(BF16)
    eye = jnp.where(row == col, 1.0, 0.0)
    level_masks = []
    b = 1
    while b < n:
        same_pair = ((row ^ col) & ~(2 * b - 1)) == 0
        level_masks.append(same_pair & ((row & b) != 0) & ((col & b) == 0))
        b *= 2
    gbs, gcums, gcum_ts, egcs = [], [], [], []
    for b in seqs:
        gb = gb_ref[b]
        g1 = gb.astype(BF16)
        r1 = gb - g1.astype(F32)
        g2 = r1.astype(BF16)
        g3 = (r1 - g2.astype(F32)).astype(BF16)
        gcum = (jnp.dot(tri, g1, preferred_element_type=F32) + jnp.dot(tri, g2, preferred_element_type=F32)
                + jnp.dot(tri, g3, preferred_element_type=F32))
        gbs.append(gb)
        gcums.append(gcum)
        gcum_ts.append(gcum.T)
        egcs.append(jnp.exp(gcum))
    units = [(b, h) for b in seqs for h in range(GDN_HEADS)]
    idx = range(len(units))
    sls = [slice(h * GDN_DK, (h + 1) * GDN_DK) for h in range(GDN_HEADS)]
    qs = [qkvs[b][0][:, sls[h]] for b, h in units]
    ks = [qkvs[b][1][:, sls[h]] for b, h in units]
    vs = [qkvs[b][2][:, sls[h]] for b, h in units]
    ss = [s_scr[b, h] for b, h in units]
    gcols = [gcums[b][:, h:h + 1] for b, h in units]
    ecols = [egcs[b][:, h:h + 1] for b, h in units]
    betas = [gbs[b][:, GDN_HEADS + h:GDN_HEADS + h + 1] for b, h in units]
    glasts = [gcums[b][n - 1:n, h:h + 1] for b, h in units]
    decays = [jnp.where(causal, jnp.exp(jnp.where(causal, gcols[u] - gcum_ts[b][h:h + 1, :], 0.0)), 0.0)
              for u, (b, h) in enumerate(units)]
    kbs = [ks[u] * betas[u] for u in idx]
    lowers = [jnp.where(strict, _mm_nt(kbs[u], ks[u]) * decays[u], 0.0) for u in idx]
    intras = [jnp.where(causal, _mm_nt(qs[u], ks[u]) * decays[u], 0.0) for u in idx]
    xs = [eye - jnp.where(level_masks[0], lowers[u], 0.0) for u in idx]
    for mask in level_masks[1:]:
        ts = [_mm(xs[u], jnp.where(mask, lowers[u], 0.0)) for u in idx]
        xs = [xs[u] - _mm(ts[u], xs[u]) for u in idx]
    us = [_mm(xs[u], vs[u] * betas[u]) for u in idx]
    ws = [_mm(xs[u], kbs[u] * ecols[u]) for u in idx]
    v_news = [us[u] - _mm(ws[u], ss[u]) for u in idx]
    os_ = [_mm(qs[u] * ecols[u], ss[u]) + _mm(intras[u], v_news[u]) for u in idx]
    s_news = [ss[u] * jnp.exp(glasts[u]) + _mm_tn(ks[u] * jnp.exp(glasts[u] - gcols[u]), v_news[u])
              for u in idx]
    for u, (b, h) in enumerate(units):
        o_ref[b, :, sls[h]] = os_[u]
        s_scr[b, h] = s_news[u]

    @pl.when(c == pl.num_programs(1) - 1)
    def _():
        sfin_ref[...] = s_scr[...]


def _gdn_prompt(qkv, gb, w_sc, batch, seq):
    n = GDN_BLOCK
    nc = seq // n
    g = GDN_SEQS
    assert batch % g == 0

    def tok(width):
        return pl.BlockSpec((g, n, width), lambda b, c: (b, c, 0))

    state_shape = (g, GDN_HEADS, GDN_DK, GDN_DK)
    o, s_fin = pl.pallas_call(
        _gdn_prompt_kernel,
        grid=(batch // g, nc),
        in_specs=[tok(QKV_CH), tok(LANES), _full(w_sc.shape)],
        out_specs=[tok(GDN_V), pl.BlockSpec(state_shape, lambda b, c: (b, 0, 0, 0))],
        out_shape=[jax.ShapeDtypeStruct((batch, seq, GDN_V), F32),
                   jax.ShapeDtypeStruct((batch, GDN_HEADS, GDN_DK, GDN_DK), F32)],
        scratch_shapes=[pltpu.VMEM(state_shape, F32), pltpu.VMEM((g, SC_HALO + n, QKV_CH), F32)],
        compiler_params=_cparams(("arbitrary", "arbitrary")),
        name="gdn_prompt",
    )(qkv.reshape(batch, seq, QKV_CH), gb.reshape(batch, seq, LANES), w_sc)
    return o.reshape(batch * seq, GDN_V), s_fin


def _post_prompt_kernel(x_ref, conv_ref, o_ref, z_ref, gon_ref, wout_ref, gx_ref, wq_ref, mk_ref, mv_ref,
                        wo_ref, gmoe_ref, wr_ref, br_ref, tri_ref, h3s_ref, x2_ref, h3r_ref, route_ref, rt_ref,
                        cnt_ref, carry, *, n_steps):
    step = pl.program_id(0)

    @pl.when(step == 0)
    def _():
        carry[...] = jnp.zeros(carry.shape, F32)

    @pl.when(step < n_steps)
    def _():
        x1 = _mix_out(conv_ref[...], o_ref[...], z_ref[...], gon_ref[...], wout_ref, x_ref[...])
        qx = jnp.dot(_rms(x1, gx_ref[...]).astype(BF16), wq_ref[...], preferred_element_type=F32)
        sls = [slice(h * X_HEAD_DIM, (h + 1) * X_HEAD_DIM) for h in range(X_HEADS)]
        qb = qx.astype(BF16)
        ss = [lax.dot_general(qb[:, sl], mk_ref[:, sl], (((1,), (1,)), ((), ())),
                              preferred_element_type=F32) * (X_HEAD_DIM ** -0.5) for sl in sls]
        es = [jnp.exp(s - jnp.max(s, axis=-1, keepdims=True)) for s in ss]
        ps = [(e / jnp.sum(e, axis=-1, keepdims=True)).astype(BF16) for e in es]
        att = jnp.concatenate(
            [jnp.dot(p, mv_ref[:, sl], preferred_element_type=F32) for p, sl in zip(ps, sls)], axis=1)
        x2 = x1 + jnp.dot(att.astype(BF16), wo_ref[...], preferred_element_type=F32)
        x2_ref[...] = x2
        h3, route, new_carry = _route(x2, gmoe_ref[...], wr_ref, br_ref[...], carry[0:1, :], tri_ref[...])
        h3r_ref[...] = h3
        route_ref[...] = route
        rt_ref[...] = route.T[0:ROUTE_ROWS, :]
        carry[0:1, :] = new_carry
        cnt_ref[...] = jnp.broadcast_to(new_carry, cnt_ref.shape)

    @pl.when(step == n_steps)
    def _():
        h3r_ref[0:h3s_ref.shape[0], :] = h3s_ref[...]


def _post_prompt(x2d, conv, o, z, mk_b, mv_b, h3_sample, batch, seq, wts):
    tm = TOKEN_TILE
    nj = seq // tm
    rows = batch * seq
    n_steps = batch * nj
    n_s = h3_sample.shape[0]
    assert n_s <= tm

    def tok(width):
        return pl.BlockSpec((tm, width), lambda s: (jnp.minimum(s, n_steps - 1), 0))

    mem_spec = pl.BlockSpec((N_MEM, D_MODEL), lambda s: (jnp.minimum(s, n_steps - 1) // nj, 0))
    sq = _full((D_MODEL, D_MODEL))
    in_specs = [tok(D_MODEL), tok(CONV_CH), tok(GDN_V), tok(GDN_V), _full((1, GDN_DK)), sq,
                _full((1, D_MODEL)), sq, mem_spec, mem_spec, sq, _full((1, D_MODEL)),
                _full((D_MODEL, LANES)), _full((1, LANES)), _full((tm, tm)), _full(h3_sample.shape)]
    out_specs = [tok(D_MODEL),
                 pl.BlockSpec((tm, D_MODEL), lambda s: (s, 0)),
                 tok(LANES),
                 pl.BlockSpec((ROUTE_ROWS, tm), lambda s: (0, jnp.minimum(s, n_steps - 1))),
                 _full((SUBLANES, LANES))]
    out_shape = [jax.ShapeDtypeStruct((rows, D_MODEL), F32),
                 jax.ShapeDtypeStruct((rows + n_s, D_MODEL), F32),
                 jax.ShapeDtypeStruct((rows, LANES), F32),
                 jax.ShapeDtypeStruct((ROUTE_ROWS, rows), F32),
                 jax.ShapeDtypeStruct((SUBLANES, LANES), F32)]
    return pl.pallas_call(
        functools.partial(_post_prompt_kernel, n_steps=n_steps),
        grid=(n_steps + 1,),
        in_specs=in_specs,
        out_specs=out_specs,
        out_shape=out_shape,
        scratch_shapes=[pltpu.VMEM((SUBLANES, LANES), F32)],
        compiler_params=_cparams(("arbitrary",)),
        name="post_prompt",
    )(x2d, conv, o, z, wts["g_onorm"], wts["w_out_b"], wts["g_xattn"], wts["w_xq_b"], mk_b, mv_b,
      wts["w_xo_b"], wts["g_moe"], wts["w_router"], wts["b_router"], _strict_lower(tm), h3_sample)


def _pre_sample_kernel(x_ref, gmix_ref, w_ref, wab_ref, bglu_ref, wdw_ref, bdw_ref, lng_ref, lnb_ref,
                       wsc_ref, cst_ref, chist_ref, shist_ref, conv_ref, q_ref, k_ref, v_ref, z_ref,
                       gb_ref, cnew_ref, snew_ref):
    glu, qkv_pre, z, uab = _project(x_ref[...], gmix_ref[...], w_ref, wab_ref, bglu_ref[...])
    z_ref[...] = z
    gb_ref[...] = _gate_beta(uab, cst_ref[...])
    kw = CONV_WIDTH
    acc = wdw_ref[kw - 1:kw, :] * glu
    for t in range(kw - 1):
        row = chist_ref[:, t, :]
        acc = acc + wdw_ref[t:t + 1, :] * row
        if t >= 1:
            cnew_ref[:, t - 1, :] = row
    cnew_ref[:, kw - 2, :] = glu
    conv_ref[...] = _conv_post(acc, bdw_ref[...], lng_ref[...], lnb_ref[...]).astype(BF16)
    ks = SHORT_CONV
    cs = wsc_ref[ks - 1:ks, :] * qkv_pre
    for t in range(ks - 1):
        row = shist_ref[:, t, :]
        cs = cs + wsc_ref[t:t + 1, :] * row
        if t >= 1:
            snew_ref[:, t - 1, :] = row
    snew_ref[:, ks - 2, :] = qkv_pre
    q, k, v = _qkv_post(cs)
    q_ref[...] = q
    k_ref[...] = k
    v_ref[...] = v


PRE_SAMPLE_TOKENS = 32


def _pre_sample(xs, chist, shist, wts):
    n = xs.shape[0]
    tb = min(PRE_SAMPLE_TOKENS, n)

    def tok(width):
        return pl.BlockSpec((tb, width), lambda i: (i, 0))

    def hist(a):
        return pl.BlockSpec((tb,) + a.shape[1:], lambda i: (i, 0, 0))

    consts = (wts["g_mix"], wts["w_in_b"], wts["w_ab_b"], wts["b_glu"], wts["w_dw"], wts["b_dw"],
              wts["ln_g"], wts["ln_b"], wts["w_sc"], wts["gdn_cst"])
    return pl.pallas_call(
        _pre_sample_kernel,
        grid=(n // tb,),
        in_specs=[tok(D_MODEL)] + [_full(a.shape) for a in consts] + [hist(chist), hist(shist)],
        out_specs=[tok(CONV_CH), tok(GDN_V), tok(GDN_V), tok(GDN_V), tok(GDN_V), tok(LANES),
                   hist(chist), hist(shist)],
        out_shape=[jax.ShapeDtypeStruct((n, CONV_CH), BF16)]
        + [jax.ShapeDtypeStruct((n, GDN_V), F32)] * 4
        + [jax.ShapeDtypeStruct((n, LANES), F32), jax.ShapeDtypeStruct(chist.shape, F32),
           jax.ShapeDtypeStruct(shist.shape, F32)],
        compiler_params=_cparams(("arbitrary",)),
        name="pre_sample",
    )(xs, *consts, chist, shist)


GDN_STEP_TOKENS = 8


def _gdn_sample_kernel(q_ref, k_ref, v_ref, gb_ref, s_ref, o_ref, snew_ref):
    n = GDN_DK
    for i in range(GDN_STEP_TOKENS):
        for h in range(GDN_HEADS):
            sl = slice(h * GDN_DK, (h + 1) * GDN_DK)
            qrow = q_ref[i:i + 1, sl]
            krow = k_ref[i:i + 1, sl]
            vrow = v_ref[i:i + 1, sl]
            g = gb_ref[i:i + 1, h:h + 1]
            beta = gb_ref[i:i + 1, GDN_HEADS + h:GDN_HEADS + h + 1]
            kcol = jnp.broadcast_to(krow, (n, n)).T
            qcol = jnp.broadcast_to(qrow, (n, n)).T
            s1 = s_ref[i, h] * jnp.exp(g)
            sk = jnp.sum(s1 * kcol, axis=0, keepdims=True)
            vt = (vrow - sk) * beta
            s2 = s1 + kcol * vt
            snew_ref[i, h] = s2
            o_ref[i:i + 1, sl] = jnp.sum(s2 * qcol, axis=0, keepdims=True)


def _gdn_sample(q, k, v, gb, state):
    n = q.shape[0]
    tb = GDN_STEP_TOKENS

    def tok(width):
        return pl.BlockSpec((tb, width), lambda i: (i, 0))

    st = pl.BlockSpec((tb, GDN_HEADS, GDN_DK, GDN_DK), lambda i: (i, 0, 0, 0))
    return pl.pallas_call(
        _gdn_sample_kernel,
        grid=(n // tb,),
        in_specs=[tok(GDN_V), tok(GDN_V), tok(GDN_V), tok(LANES), st],
        out_specs=[tok(GDN_V), st],
        out_shape=[jax.ShapeDtypeStruct((n, GDN_V), F32), jax.ShapeDtypeStruct(state.shape, F32)],
        compiler_params=_cparams(("arbitrary",)),
        name="gdn_sample",
    )(q, k, v, gb, state)


def _mix_sample_kernel(x_ref, conv_ref, o_ref, z_ref, gon_ref, wout_ref, gx_ref, wq_ref, x1_ref, qx_ref):
    x1 = _mix_out(conv_ref[...], o_ref[...], z_ref[...], gon_ref[...], wout_ref, x_ref[...])
    x1_ref[...] = x1
    qx_ref[...] = jnp.dot(_rms(x1, gx_ref[...]).astype(BF16), wq_ref[...], preferred_element_type=F32)


def _mix_sample(xs, conv, o, z, wts):
    n = xs.shape[0]
    in_arrays = (xs, conv, o, z, wts["g_onorm"], wts["w_out_b"], wts["g_xattn"], wts["w_xq_b"])
    return pl.pallas_call(
        _mix_sample_kernel,
        grid=(1,),
        in_specs=[_full(a.shape) for a in in_arrays],
        out_specs=[_full((n, D_MODEL))] * 2,
        out_shape=[jax.ShapeDtypeStruct((n, D_MODEL), F32)] * 2,
        compiler_params=_cparams(("arbitrary",)),
        name="mix_sample",
    )(*in_arrays)


ATTN_STEP_TOKENS = 4


def _attn_sample_kernel(qx_ref, ck_ref, cv_ref, att_ref):
    for i in range(ATTN_STEP_TOKENS):
        parts = []
        for h in range(X_HEADS):
            sl = slice(h * X_HEAD_DIM, (h + 1) * X_HEAD_DIM)
            prod = ck_ref[i, :, h, :] * qx_ref[0, i:i + 1, sl]
            s = jnp.sum(prod, axis=-1, keepdims=True) * (X_HEAD_DIM ** -0.5)
            e = jnp.exp(s - jnp.max(s, axis=0, keepdims=True))
            p = e / jnp.sum(e, axis=0, keepdims=True)
            parts.append(jnp.sum(p * cv_ref[i, :, h, :], axis=0, keepdims=True))
        att_ref[0, i:i + 1, :] = jnp.concatenate(parts, axis=1)


def _attn_sample(qx, ck, cv):
    n = qx.shape[0]
    tb = ATTN_STEP_TOKENS
    q3 = qx.reshape(n // tb, tb, D_MODEL)
    qspec = pl.BlockSpec((1, tb, D_MODEL), lambda i: (i, 0, 0))
    cspec = pl.BlockSpec((tb, N_MEM, X_HEADS, X_HEAD_DIM), lambda i: (i, 0, 0, 0))
    out = pl.pallas_call(
        _attn_sample_kernel,
        grid=(n // tb,),
        in_specs=[qspec, cspec, cspec],
        out_specs=qspec,
        out_shape=jax.ShapeDtypeStruct(q3.shape, F32),
        compiler_params=_cparams(("arbitrary",)),
        name="attn_sample",
    )(q3, ck, cv)
    return out.reshape(n, D_MODEL)


def _route_sample_kernel(x1_ref, att_ref, wo_ref, gmoe_ref, wr_ref, br_ref, tri_ref, x2_ref, h3r_ref, route_ref,
                         rt_ref, cnt_ref):
    x2 = x1_ref[...] + jnp.dot(att_ref[...].astype(BF16), wo_ref[...], preferred_element_type=F32)
    x2_ref[...] = x2
    h3, route, counts = _route(x2, gmoe_ref[...], wr_ref, br_ref[...], jnp.zeros((1, LANES), F32),
                               tri_ref[...])
    h3r_ref[...] = h3
    route_ref[...] = route
    rt_ref[...] = route.T[0:ROUTE_ROWS, :]
    cnt_ref[...] = jnp.broadcast_to(counts, cnt_ref.shape)


def _route_sample(x1, att, wts):
    n = x1.shape[0]
    in_arrays = (x1, att, wts["w_xo_b"], wts["g_moe"], wts["w_router"], wts["b_router"],
                 _strict_lower(n))
    shapes = [(n, D_MODEL), (n, D_MODEL), (n, LANES), (ROUTE_ROWS, n), (SUBLANES, LANES)]
    return pl.pallas_call(
        _route_sample_kernel,
        grid=(1,),
        in_specs=[_full(a.shape) for a in in_arrays],
        out_specs=[_full(s) for s in shapes],
        out_shape=[jax.ShapeDtypeStruct(s, F32) for s in shapes],
        compiler_params=_cparams(("arbitrary",)),
        name="route_sample",
    )(*in_arrays)


def _part_tiles(tiles):
    while not any(tiles % d == 0 for d in (8, 7, 6, 5, 4)):
        tiles += 1
    return tiles


def _sc_chunk(rows_per_worker):
    for c in range(64, 0, -SUBLANES):
        if rows_per_worker % c == 0:
            return c
    raise ValueError(rows_per_worker)


def _sc_gather_rows(table, idx):
    n_workers = SC_CORES * SC_SUBCORES
    b = idx.shape[0]
    assert b % (n_workers * SUBLANES) == 0
    per_worker = b // n_workers
    chunk = _sc_chunk(per_worker)
    row_shape = table.shape[1:]
    mesh = plsc.VectorSubcoreMesh(core_axis_name="c", subcore_axis_name="s")

    @functools.partial(
        pl.kernel, mesh=mesh,
        out_type=jax.ShapeDtypeStruct((b,) + row_shape, table.dtype),
        scratch_types=[pltpu.VMEM((chunk,), I32), pltpu.VMEM((chunk,) + row_shape, table.dtype),
                       pltpu.SemaphoreType.DMA],
        name="sc_gather_rows",
    )
    def gather(table_hbm, idx_hbm, out_hbm, idx_v, rows_v, sem):
        worker = lax.axis_index("s") * SC_CORES + lax.axis_index("c")
        base = worker * per_worker

        @pl.loop(0, per_worker // chunk)
        def _(c):
            off = pl.multiple_of(base + c * chunk, SUBLANES)
            pltpu.sync_copy(idx_hbm.at[pl.ds(off, chunk)], idx_v)
            pltpu.async_copy(table_hbm.at[idx_v], rows_v, sem).wait()
            pltpu.sync_copy(rows_v, out_hbm.at[pl.ds(off, chunk)])

    return gather(table, idx)


def _moe_kernel(te_ref, nt_ref, xs_ref, wgu_ref, bgu_ref, wdn_ref, bdn_ref, *rest, first_tile):
    ys_ref, wgu_b, wdn_b = rest[-3:]
    step = pl.program_id(0)
    i = first_tile + step
    total = nt_ref[0]

    @pl.when(i < total)
    def _():
        prev = te_ref[jnp.maximum(i - 1, 0)]
        fresh = jnp.logical_or(step == 0, te_ref[i] != prev)

        @pl.when(fresh)
        def _():
            wgu_b[...] = wgu_ref[0].astype(BF16)
            wdn_b[...] = wdn_ref[0].astype(BF16)

        x = xs_ref[...].astype(BF16)

        def up(c):
            glu_cols = slice(c * MOE_COLS, (c + 1) * MOE_COLS)
            lin_cols = slice(D_EXPERT + c * MOE_COLS, D_EXPERT + (c + 1) * MOE_COLS)
            return (jnp.dot(x, wgu_b[:, glu_cols], preferred_element_type=F32) + bgu_ref[0, :, glu_cols],
                    jnp.dot(x, wgu_b[:, lin_cols], preferred_element_type=F32) + bgu_ref[0, :, lin_cols])

        n_chunks = D_EXPERT // MOE_COLS
        nxt = up(0)
        y = None
        for c in range(n_chunks):
            g, lin = nxt
            if c + 1 < n_chunks:
                nxt = up(c + 1)
            x_glu = jnp.minimum(g, SWIGLU_LIMIT)
            x_lin = jnp.clip(lin, -SWIGLU_LIMIT, SWIGLU_LIMIT)
            act = x_glu * jax.nn.sigmoid(SWIGLU_ALPHA * x_glu) * (x_lin + 1.0)
            part = jnp.dot(act.astype(BF16), wdn_b[c * MOE_COLS:(c + 1) * MOE_COLS, :],
                           preferred_element_type=F32)
            y = part if y is None else y + part
        ys_ref[...] = y + bdn_ref[0]

    @pl.when(i >= total)
    def _():
        ys_ref[...] = jnp.zeros(ys_ref.shape, F32)


def _moe(tile_e, n_tiles, xs_part, ys_prev, first_tile, n_rows, w_gu, b_gu, w_dn, b_dn):
    tm = MOE_TILE

    def weights(shape):
        return pl.BlockSpec(shape, lambda i, te, nt: (te[first_tile + i], 0, 0))

    in_specs = [pl.BlockSpec((tm, D_MODEL), lambda i, te, nt: (i, 0)),
                weights((1, D_MODEL, 2 * D_EXPERT)), weights((1, 1, 2 * D_EXPERT)),
                weights((1, D_EXPERT, D_MODEL)), weights((1, 1, D_MODEL))]
    operands = [tile_e, n_tiles, xs_part, w_gu, b_gu, w_dn, b_dn]
    aliases = {}
    if ys_prev is not None:
        in_specs.append(pl.BlockSpec(memory_space=pl.ANY))
        aliases = {len(operands): 0}
        operands.append(ys_prev)
    grid_spec = pltpu.PrefetchScalarGridSpec(
        num_scalar_prefetch=2,
        grid=(xs_part.shape[0] // tm,),
        in_specs=in_specs,
        out_specs=pl.BlockSpec((tm, D_MODEL), lambda i, te, nt: (first_tile + i, 0)),
        scratch_shapes=[pltpu.VMEM((D_MODEL, 2 * D_EXPERT), BF16),
                        pltpu.VMEM((D_EXPERT, D_MODEL), BF16)],
    )
    return pl.pallas_call(
        functools.partial(_moe_kernel, first_tile=first_tile),
        grid_spec=grid_spec,
        out_shape=jax.ShapeDtypeStruct((n_rows, D_MODEL), F32),
        input_output_aliases=aliases,
        compiler_params=_cparams(("arbitrary",), vmem=56 * 1024 * 1024),
        name="moe",
    )(*operands)


def _combine_kernel(x2_ref, route_ref, gfin_ref, yt_ref, *rest):
    y_ref = rest[-1]
    route = route_ref[...]
    acc = x2_ref[...]
    for j in range(TOP_K):
        acc = acc + route[:, TOP_K + j:TOP_K + j + 1] * yt_ref[j]
    y_ref[...] = _rms(acc, gfin_ref[...])


def _combine(x2, route, g_final, ys_tok, y_prev, tok0, n_tok, ys_block0):
    tc = min(COMBINE_TILE, n_tok)
    b0 = tok0 // tc

    def tok(width):
        return pl.BlockSpec((tc, width), lambda i: (b0 + i, 0))

    in_specs = [tok(D_MODEL), tok(LANES), pl.BlockSpec((1, D_MODEL), lambda i: (0, 0)),
                pl.BlockSpec((TOP_K, tc, D_MODEL), lambda i: (0, ys_block0 + i, 0))]
    operands = [x2, route, g_final, ys_tok]
    aliases = {}
    if y_prev is not None:
        in_specs.append(pl.BlockSpec(memory_space=pl.ANY))
        aliases = {len(operands): 0}
        operands.append(y_prev)
    return pl.pallas_call(
        _combine_kernel,
        grid=(n_tok // tc,),
        in_specs=in_specs,
        out_specs=tok(D_MODEL),
        out_shape=jax.ShapeDtypeStruct(x2.shape, F32),
        input_output_aliases=aliases,
        compiler_params=_cparams(("arbitrary",)),
        name="combine",
    )(*operands)


def _routing_tables(idx_t, rank_t, counts, n_rows):
    tm = MOE_TILE
    n_tok = idx_t.shape[1]
    n_assign = TOP_K * n_tok
    tok_mask = (1 << TOKEN_BITS) - 1
    tiles_e = (counts + tm - 1) // tm
    tile_end = jnp.cumsum(tiles_e)
    row_start = (tile_end - tiles_e) * tm
    total = tile_end[-1]
    expert_ids = jnp.arange(N_EXPERTS, dtype=I32)

    def lookup(table, e):
        return jnp.sum(jnp.where(e[..., None] == expert_ids, table, 0), axis=-1)

    pos = lookup(row_start, idx_t) + rank_t
    keys_real = (idx_t * (1 << TOKEN_BITS) + jnp.arange(n_tok, dtype=I32)[None, :]).reshape(-1)
    k = jnp.arange(n_rows - n_assign, dtype=I32)
    pad_e, pad_s = k // tm, k % tm
    pad_needed = lookup(tiles_e * tm - counts, pad_e)
    pad_key_e = jnp.where((pad_e < N_EXPERTS) & (pad_s < pad_needed), pad_e, N_EXPERTS)
    keys = jnp.sort(jnp.concatenate([keys_real, pad_key_e * (1 << TOKEN_BITS) + tok_mask]))
    src_tok = jnp.where((keys & tok_mask) == tok_mask, jnp.arange(n_rows, dtype=I32) % n_tok,
                        keys & tok_mask)
    tid = jnp.minimum(jnp.arange(n_rows // tm, dtype=I32), total - 1)
    tile_e = jnp.minimum(jnp.sum((tid[:, None] >= tile_end[None, :]).astype(I32), axis=1), N_EXPERTS - 1)
    return tile_e.astype(I32), total.reshape(1).astype(I32), src_tok.astype(I32), pos.reshape(-1).astype(I32)


def _pad_rows(a, rows):
    return jnp.concatenate([a, jnp.zeros((rows - a.shape[0],) + a.shape[1:], a.dtype)], axis=0)


def _pad_lanes(a, lanes=LANES):
    return jnp.concatenate([a, jnp.zeros(a.shape[:-1] + (lanes - a.shape[-1],), a.dtype)], axis=-1)


def kernel(x_prompt, mem_prompt, x_sample, state_conformer_conv, state_gdn_conv, state_gdn, cache_mem_k,
           cache_mem_v, w_in, b_glu, w_dw, b_dw, ln_g, ln_b, w_sc, a_log, dt_bias, g_onorm, w_out, g_mix,
           g_xattn, g_mem, w_xq, w_mk, w_mv, w_xo, g_moe, w_router, b_router, w_gu, b_gu, w_dn, b_dn,
           g_final):
    assert w_in.shape[0] == 1, "single-layer configuration"
    batch, seq, _ = x_prompt.shape
    n_s = x_sample.shape[0]
    n_p = batch * seq
    n_all = n_p + n_s
    assert seq % TOKEN_TILE == 0 and n_p % n_s == 0 and n_all < (1 << TOKEN_BITS) - 1
    assert (n_all * TOP_K) % (SC_CORES * SC_SUBCORES * SUBLANES) == 0

    wts = {
        "g_mix": g_mix[0][None], "g_xattn": g_xattn[0][None], "g_moe": g_moe[0][None],
        "g_onorm": g_onorm[0][None],
        "w_in_b": w_in[0][:, :OFF_A].astype(BF16),
        "w_ab_b": _pad_lanes(w_in[0][:, OFF_A:]).astype(BF16),
        "b_glu": b_glu[0][None],
        "w_dw": _pad_rows(w_dw[0], 32), "b_dw": b_dw[0][None], "ln_g": ln_g[0][None], "ln_b": ln_b[0][None],
        "w_sc": _pad_rows(w_sc[0], 8),
        "gdn_cst": _pad_rows(_pad_lanes(jnp.stack([a_log[0], dt_bias[0]])), 8),
        "w_out_b": w_out[0].astype(BF16), "w_xq_b": w_xq[0].astype(BF16), "w_xo_b": w_xo[0].astype(BF16),
        "w_router": _pad_lanes(w_router[0]), "b_router": _pad_lanes(b_router[0][None]),
    }

    mk, mv, mk_b, mv_b = _mem_kv(mem_prompt.reshape(batch * N_MEM, D_MODEL), g_mem[0][None],
                                 w_mk[0].astype(BF16), w_mv[0].astype(BF16))
    xp = x_prompt.reshape(n_p, D_MODEL)
    conv_p, qkv_p, z_p, gb_p, cstate_p, sstate_p = _pre_prompt(xp, batch, seq, wts)
    o_p, gstate_p = _gdn_prompt(qkv_p, gb_p, wts["w_sc"], batch, seq)

    xs = x_sample.reshape(n_s, D_MODEL)
    conv_s, q_s, k_s, v_s, z_s, gb_s, cstate_s, sstate_s = _pre_sample(
        xs, state_conformer_conv[0], state_gdn_conv[0], wts)
    o_s, gstate_s = _gdn_sample(q_s, k_s, v_s, gb_s, state_gdn[0])
    x1_s, qx_s = _mix_sample(xs, conv_s, o_s, z_s, wts)
    att_s = _attn_sample(qx_s, cache_mem_k[0], cache_mem_v[0])
    x2_s, h3_s, route_s, rt_s, counts_s = _route_sample(x1_s, att_s, wts)

    x2_p, h3r, route_p, rt_p, counts_p = _post_prompt(xp, conv_p, o_p, z_p, mk_b, mv_b, h3_s, batch, seq,
                                                      wts)

    counts_p = counts_p[0, :N_EXPERTS].astype(I32)
    counts_s = counts_s[0, :N_EXPERTS].astype(I32)
    idx_s = rt_s[0:TOP_K].astype(I32)
    rank_s = rt_s[2 * TOP_K:3 * TOP_K].astype(I32) + jnp.sum(
        jnp.where(idx_s[..., None] == jnp.arange(N_EXPERTS, dtype=I32), counts_p, 0), axis=-1)
    idx_t = jnp.concatenate([rt_p[0:TOP_K].astype(I32), idx_s], axis=1)
    rank_t = jnp.concatenate([rt_p[2 * TOP_K:3 * TOP_K].astype(I32), rank_s], axis=1)
    min_tiles = -(-(n_all * TOP_K + N_EXPERTS * (MOE_TILE - 1)) // MOE_TILE)
    tiles_part = _part_tiles(-(-min_tiles // MOE_PARTS))
    rows_part = tiles_part * MOE_TILE
    n_rows = rows_part * MOE_PARTS
    tile_e, n_tiles, src_tok, pos = _routing_tables(idx_t, rank_t, counts_p + counts_s, n_rows)
    pos_t = pos.reshape(TOP_K, n_all)
    xs_parts = [_sc_gather_rows(h3r, src_tok[k * rows_part:(k + 1) * rows_part]) for k in range(MOE_PARTS)]
    ys = None
    for k in range(MOE_PARTS):
        ys = _moe(tile_e, n_tiles, xs_parts[k], ys, k * tiles_part, n_rows, w_gu[0], b_gu[0][:, None, :],
                  w_dn[0], b_dn[0][:, None, :])
    gfin = g_final[None]
    tok_part = n_p // MOE_PARTS
    assert tok_part % COMBINE_TILE == 0
    y_p = None
    for k in range(MOE_PARTS):
        last = k == MOE_PARTS - 1
        pos_k = pos_t[:, k * tok_part:(k + 1) * tok_part]
        if last:
            pos_k = jnp.concatenate([pos_k, pos_t[:, n_p:]], axis=1)
        ys_tok = _sc_gather_rows(ys, pos_k.reshape(-1)).reshape(TOP_K, pos_k.shape[1], D_MODEL)
        y_p = _combine(x2_p, route_p, gfin, ys_tok, y_p, k * tok_part, tok_part, 0)
        if last:
            y_s = _combine(x2_s, route_s, gfin, ys_tok, None, 0, n_s, tok_part // min(COMBINE_TILE, n_s))

    return (y_p.reshape(batch, seq, D_MODEL), y_s.reshape(n_s, 1, D_MODEL),
            cstate_p[None], sstate_p[None], gstate_p[None],
            mk.reshape(1, batch, N_MEM, X_HEADS, X_HEAD_DIM), mv.reshape(1, batch, N_MEM, X_HEADS, X_HEAD_DIM),
            cstate_s[None], sstate_s[None], gstate_s[None])
```

```python
import functools

import jax
import jax.numpy as jnp
from jax import lax
from jax.experimental import pallas as pl
from jax.experimental.pallas import tpu as pltpu
from jax.experimental.pallas import tpu_sc as plsc

F32, BF16, I32 = jnp.float32, jnp.bfloat16, jnp.int32

D_MODEL = 1024
CONV_CH = 512
CONV_WIDTH = 31
GDN_HEADS = 4
GDN_DK = 128
GDN_V = 512
QKV_CH = 1536
SHORT_CONV = 4
N_MEM = 256
X_HEADS = 4
X_HEAD_DIM = 256
N_EXPERTS = 32
TOP_K = 4
D_EXPERT = 1024
SWIGLU_LIMIT = 7.0
SWIGLU_ALPHA = 1.702
NORM_EPS = 1e-6
OFF_QKV = 2 * CONV_CH
OFF_Z = OFF_QKV + QKV_CH
OFF_A = OFF_Z + GDN_V

LANES = 128
SUBLANES = 8
GDN_BLOCK = 128
TOKEN_TILE = 256
MOE_TILE = 512
MOE_COLS = 256
MOE_PARTS = 4
COMBINE_TILE = 128
TOKEN_BITS = 15
ROUTE_ROWS = 16
SC_CORES = 2
SC_SUBCORES = 16
VMEM_LIMIT = 48 * 1024 * 1024


def _cparams(sem, vmem=VMEM_LIMIT):
    return pltpu.CompilerParams(dimension_semantics=sem, vmem_limit_bytes=vmem)


def _mm(a, b):
    return jnp.dot(a.astype(BF16), b.astype(BF16), preferred_element_type=F32)


def _mm_nt(a, b):
    return lax.dot_general(a.astype(BF16), b.astype(BF16), (((1,), (1,)), ((), ())),
                           preferred_element_type=F32)


def _mm_tn(a, b):
    return lax.dot_general(a.astype(BF16), b.astype(BF16), (((0,), (0,)), ((), ())),
                           preferred_element_type=F32)


def _rms(x, g):
    return x * lax.rsqrt(jnp.mean(x * x, axis=-1, keepdims=True) + NORM_EPS) * g


def _silu(x):
    return x * jax.nn.sigmoid(x)


def _full(shape):
    return pl.BlockSpec(shape, lambda *_: (0,) * len(shape))


def _strict_lower(n):
    return jnp.tril(jnp.ones((n, n), BF16), k=-1)


def _project(x, gmix, w_ref, wab_ref, bglu):
    h = _rms(x, gmix).astype(BF16)
    u_glu = jnp.dot(h, w_ref[:, 0:OFF_QKV], preferred_element_type=F32) + bglu
    glu = u_glu[:, :CONV_CH] * jax.nn.sigmoid(u_glu[:, CONV_CH:])
    qkv_pre = jnp.dot(h, w_ref[:, OFF_QKV:OFF_Z], preferred_element_type=F32)
    z = jnp.dot(h, w_ref[:, OFF_Z:OFF_A], preferred_element_type=F32)
    uab = jnp.dot(h, wab_ref[...], preferred_element_type=F32)
    return glu, qkv_pre, z, uab


def _gate_beta(uab, cst):
    lane = lax.broadcasted_iota(I32, uab.shape, 1)
    g = -jnp.exp(cst[0:1, :]) * jax.nn.softplus(uab + cst[1:2, :])
    return jnp.where(lane < GDN_HEADS, g, jax.nn.sigmoid(uab))


def _conv_post(c, b_dw, ln_g, ln_b):
    c = c + b_dw
    mu = jnp.mean(c, axis=-1, keepdims=True)
    d = c - mu
    var = jnp.mean(d * d, axis=-1, keepdims=True)
    return _silu(d * lax.rsqrt(var + NORM_EPS) * ln_g + ln_b)


def _qkv_post(cs):
    a = _silu(cs)
    parts = []
    for h in range(2 * GDN_HEADS):
        seg = a[:, h * GDN_DK:(h + 1) * GDN_DK]
        n = seg * lax.rsqrt(jnp.sum(seg * seg, axis=-1, keepdims=True) + NORM_EPS)
        if h < GDN_HEADS:
            n = n * (GDN_DK ** -0.5)
        parts.append(n)
    q = jnp.concatenate(parts[:GDN_HEADS], axis=1)
    k = jnp.concatenate(parts[GDN_HEADS:], axis=1)
    return q, k, a[:, 2 * GDN_HEADS * GDN_DK:]


def _mix_out(conv_b, o, z, gon, wout_ref, x):
    parts = []
    for h in range(GDN_HEADS):
        oh = o[:, h * 128:(h + 1) * 128]
        parts.append(oh * lax.rsqrt(jnp.mean(oh * oh, axis=-1, keepdims=True) + NORM_EPS) * gon)
    on = jnp.concatenate(parts, axis=1) * _silu(z)
    mixed = (jnp.dot(conv_b, wout_ref[0:CONV_CH, :], preferred_element_type=F32)
             + jnp.dot(on.astype(BF16), wout_ref[CONV_CH:, :], preferred_element_type=F32))
    return x + mixed


def _route(x2, gmoe, wr_ref, br, carry, before):
    m = x2.shape[0]
    h3 = _rms(x2, gmoe)
    h_hi = h3.astype(BF16)
    r1 = h3 - h_hi.astype(F32)
    h_mid = r1.astype(BF16)
    h_lo = (r1 - h_mid.astype(F32)).astype(BF16)
    w = wr_ref[...]
    w_hi = w.astype(BF16)
    w_lo = (w - w_hi.astype(F32)).astype(BF16)
    logits = (jnp.dot(h_hi, w_hi, preferred_element_type=F32)
              + jnp.dot(h_hi, w_lo, preferred_element_type=F32)
              + jnp.dot(h_mid, w_hi, preferred_element_type=F32)
              + jnp.dot(h_lo, w_hi, preferred_element_type=F32)) + br
    neg = jnp.float32(-jnp.inf)
    n_groups = 4 if m % 32 == 0 else 1
    rows = m // n_groups
    lanes_g = lax.broadcasted_iota(I32, (rows, LANES), 1).astype(F32)
    works = [jnp.where(lanes_g < N_EXPERTS, logits[g * rows:(g + 1) * rows], neg) for g in range(n_groups)]
    vals, idxs = [], []
    for _ in range(TOP_K):
        mxs = [jnp.max(w, axis=-1, keepdims=True) for w in works]
        ixs = [jnp.min(jnp.where(w == mx, lanes_g, float(LANES)), axis=-1, keepdims=True)
               for w, mx in zip(works, mxs)]
        works = [jnp.where(lanes_g == ix, neg, w) for w, ix in zip(works, ixs)]
        vals.append(mxs)
        idxs.append(ixs)
    sels, gates = [], []
    for g in range(n_groups):
        es = [jnp.exp(vals[r][g] - vals[0][g]) for r in range(TOP_K)]
        den = es[0] + es[1] + es[2] + es[3]
        gates.append([e / den for e in es])
        sel_g = jnp.zeros((rows, LANES), F32)
        for r in range(TOP_K):
            sel_g = sel_g + jnp.where(lanes_g == idxs[r][g], 1.0, 0.0)
        sels.append(sel_g)
    sel = jnp.concatenate(sels, axis=0)
    rank_full = jnp.dot(before, sel.astype(BF16), preferred_element_type=F32) + carry
    routes = []
    for g in range(n_groups):
        rank_g = rank_full[g * rows:(g + 1) * rows]
        route_g = jnp.zeros((rows, LANES), F32)
        for r in range(TOP_K):
            rk = jnp.sum(jnp.where(lanes_g == idxs[r][g], rank_g, 0.0), axis=-1, keepdims=True)
            route_g = (route_g + jnp.where(lanes_g == r, idxs[r][g], 0.0)
                       + jnp.where(lanes_g == TOP_K + r, gates[g][r], 0.0)
                       + jnp.where(lanes_g == 2 * TOP_K + r, rk, 0.0))
        routes.append(route_g)
    new_carry = carry + jnp.sum(sel, axis=0, keepdims=True)
    return h3, jnp.concatenate(routes, axis=0), new_carry


def _mem_kv_kernel(mem_ref, g_ref, wk_ref, wv_ref, mk_ref, mv_ref, mkb_ref, mvb_ref):
    m = _rms(mem_ref[...], g_ref[...]).astype(BF16)
    mk = jnp.dot(m, wk_ref[...], preferred_element_type=F32)
    mv = jnp.dot(m, wv_ref[...], preferred_element_type=F32)
    mk_ref[...] = mk
    mv_ref[...] = mv
    mkb_ref[...] = mk.astype(BF16)
    mvb_ref[...] = mv.astype(BF16)


def _mem_kv(mem, g_mem, wk_b, wv_b):
    rows = mem.shape[0]
    tm = TOKEN_TILE
    row_spec = pl.BlockSpec((tm, D_MODEL), lambda i: (i, 0))
    return pl.pallas_call(
        _mem_kv_kernel,
        grid=(rows // tm,),
        in_specs=[row_spec, _full((1, D_MODEL)), _full((D_MODEL, D_MODEL)), _full((D_MODEL, D_MODEL))],
        out_specs=[row_spec] * 4,
        out_shape=[jax.ShapeDtypeStruct((rows, D_MODEL), F32)] * 2
        + [jax.ShapeDtypeStruct((rows, D_MODEL), BF16)] * 2,
        compiler_params=_cparams(("arbitrary",)),
        name="mem_kv",
    )(mem, g_mem, wk_b, wv_b)


CONV_HALO = 32
SC_HALO = 8


def _pre_prompt_kernel(x_ref, gmix_ref, w_ref, wab_ref, bglu_ref, wdw_ref, bdw_ref, lng_ref, lnb_ref,
                       cst_ref, conv_ref, qkv_ref, z_ref, gb_ref, cstate_ref, sstate_ref, cbuf, *, tm):
    j = pl.program_id(1)

    @pl.when(j == 0)
    def _():
        cbuf[0:CONV_HALO, :] = jnp.zeros((CONV_HALO, CONV_CH), F32)

    glu, qkv_pre, z, uab = _project(x_ref[...], gmix_ref[...], w_ref, wab_ref, bglu_ref[...])
    cbuf[CONV_HALO:CONV_HALO + tm, :] = glu
    qkv_ref[...] = qkv_pre
    z_ref[...] = z
    gb_ref[...] = _gate_beta(uab, cst_ref[...])

    base = CONV_HALO - (CONV_WIDTH - 1)
    rows = CONV_HALO + tm
    accs = []
    for c in range(CONV_CH // LANES):
        lanes = slice(c * LANES, (c + 1) * LANES)
        block = cbuf[:, lanes]
        acc = None
        for r in range(SUBLANES):
            shifted = block if r == 0 else pltpu.roll(block, rows - r, 0)
            for a in range(base, base + CONV_WIDTH):
                if a % SUBLANES == r:
                    t = a - base
                    term = wdw_ref[t:t + 1, lanes] * shifted[a - r:a - r + tm, :]
                    acc = term if acc is None else acc + term
        accs.append(acc)
    acc = jnp.concatenate(accs, axis=1)
    conv_ref[...] = _conv_post(acc, bdw_ref[...], lng_ref[...], lnb_ref[...]).astype(BF16)

    @pl.when(j == pl.num_programs(1) - 1)
    def _():
        cstate_ref[0] = cbuf[pl.ds(CONV_HALO + tm - (CONV_WIDTH - 1), CONV_WIDTH - 1), :]
        sstate_ref[0] = qkv_pre[tm - (SHORT_CONV - 1):, :]

    cbuf[0:CONV_HALO, :] = cbuf[tm:tm + CONV_HALO, :]


def _pre_prompt(x2d, batch, seq, wts):
    tm = TOKEN_TILE
    nj = seq // tm
    rows = batch * seq

    def tok(width):
        return pl.BlockSpec((tm, width), lambda b, j: (b * nj + j, 0))

    in_specs = [tok(D_MODEL), _full((1, D_MODEL)), _full((D_MODEL, OFF_A)), _full((D_MODEL, LANES)),
                _full((1, OFF_QKV)), _full((32, CONV_CH)), _full((1, CONV_CH)), _full((1, CONV_CH)),
                _full((1, CONV_CH)), _full((8, LANES))]
    out_specs = [tok(CONV_CH), tok(QKV_CH), tok(GDN_V), tok(LANES),
                 pl.BlockSpec((1, CONV_WIDTH - 1, CONV_CH), lambda b, j: (b, 0, 0)),
                 pl.BlockSpec((1, SHORT_CONV - 1, QKV_CH), lambda b, j: (b, 0, 0))]
    out_shape = [jax.ShapeDtypeStruct((rows, CONV_CH), BF16),
                 jax.ShapeDtypeStruct((rows, QKV_CH), F32),
                 jax.ShapeDtypeStruct((rows, GDN_V), F32),
                 jax.ShapeDtypeStruct((rows, LANES), F32),
                 jax.ShapeDtypeStruct((batch, CONV_WIDTH - 1, CONV_CH), F32),
                 jax.ShapeDtypeStruct((batch, SHORT_CONV - 1, QKV_CH), F32)]
    return pl.pallas_call(
        functools.partial(_pre_prompt_kernel, tm=tm),
        grid=(batch, nj),
        in_specs=in_specs,
        out_specs=out_specs,
        out_shape=out_shape,
        scratch_shapes=[pltpu.VMEM((CONV_HALO + tm, CONV_CH), F32)],
        compiler_params=_cparams(("arbitrary", "arbitrary")),
        name="pre_prompt",
    )(x2d, wts["g_mix"], wts["w_in_b"], wts["w_ab_b"], wts["b_glu"], wts["w_dw"], wts["b_dw"],
      wts["ln_g"], wts["ln_b"], wts["gdn_cst"])


GDN_SEQS = 2


def _gdn_prompt_kernel(qkv_ref, gb_ref, wsc_ref, o_ref, sfin_ref, s_scr, sbuf):
    c = pl.program_id(1)
    n = GDN_BLOCK
    seqs = range(GDN_SEQS)

    @pl.when(c == 0)
    def _():
        s_scr[...] = jnp.zeros(s_scr.shape, F32)
        sbuf[:, 0:SC_HALO, :] = jnp.zeros((GDN_SEQS, SC_HALO, QKV_CH), F32)

    sbase = SC_HALO - (SHORT_CONV - 1)
    qkvs = []
    for b in seqs:
        sbuf[b, SC_HALO:SC_HALO + n, :] = qkv_ref[b]
        cs = wsc_ref[0:1, :] * sbuf[b, pl.ds(sbase, n), :]
        for t in range(1, SHORT_CONV):
            cs = cs + wsc_ref[t:t + 1, :] * sbuf[b, pl.ds(sbase + t, n), :]
        qkvs.append(_qkv_post(cs))
        sbuf[b, 0:SC_HALO, :] = sbuf[b, n:n + SC_HALO, :]

    row = lax.broadcasted_iota(I32, (n, n), 0)
    col = lax.broadcasted_iota(I32, (n, n), 1)
    causal = row >= col
    strict = row > col
    tri = jnp.where(causal, 1.0, 0.0).astype(BF16)
    eye = jnp.where(row == col, 1.0, 0.0)
    level_masks = []
    b = 1
    while b < n:
        same_pair = ((row ^ col) & ~(2 * b - 1)) == 0
        level_masks.append(same_pair & ((row & b) != 0) & ((col & b) == 0))
        b *= 2
    gbs, gcums, gcum_ts, egcs = [], [], [], []
    for b in seqs:
        gb = gb_ref[b]
        g1 = gb.astype(BF16)
        r1 = gb - g1.astype(F32)
        g2 = r1.astype(BF16)
        g3 = (r1 - g2.astype(F32)).astype(BF16)
        gcum = (jnp.dot(tri, g1, preferred_element_type=F32) + jnp.dot(tri, g2, preferred_element_type=F32)
                + jnp.dot(tri, g3, preferred_element_type=F32))
        gbs.append(gb)
        gcums.append(gcum)
        gcum_ts.append(gcum.T)
        egcs.append(jnp.exp(gcum))
    units = [(b, h) for b in seqs for h in range(GDN_HEADS)]
    idx = range(len(units))
    sls = [slice(h * GDN_DK, (h + 1) * GDN_DK) for h in range(GDN_HEADS)]
    qs = [qkvs[b][0][:, sls[h]] for b, h in units]
    ks = [qkvs[b][1][:, sls[h]] for b, h in units]
    vs = [qkvs[b][2][:, sls[h]] for b, h in units]
    ss = [s_scr[b, h] for b, h in units]
    gcols = [gcums[b][:, h:h + 1] for b, h in units]
    ecols = [egcs[b][:, h:h + 1] for b, h in units]
    betas = [gbs[b][:, GDN_HEADS + h:GDN_HEADS + h + 1] for b, h in units]
    glasts = [gcums[b][n - 1:n, h:h + 1] for b, h in units]
    decays = [jnp.where(causal, jnp.exp(jnp.where(causal, gcols[u] - gcum_ts[b][h:h + 1, :], 0.0)), 0.0)
              for u, (b, h) in enumerate(units)]
    kbs = [ks[u] * betas[u] for u in idx]
    lowers = [jnp.where(strict, _mm_nt(kbs[u], ks[u]) * decays[u], 0.0) for u in idx]
    intras = [jnp.where(causal, _mm_nt(qs[u], ks[u]) * decays[u], 0.0) for u in idx]
    xs = [eye - jnp.where(level_masks[0], lowers[u], 0.0) for u in idx]
    for mask in level_masks[1:]:
        ts = [_mm(xs[u], jnp.where(mask, lowers[u], 0.0)) for u in idx]
        xs = [xs[u] - _mm(ts[u], xs[u]) for u in idx]
    us = [_mm(xs[u], vs[u] * betas[u]) for u in idx]
    ws = [_mm(xs[u], kbs[u] * ecols[u]) for u in idx]
    v_news = [us[u] - _mm(ws[u], ss[u]) for u in idx]
    os_ = [_mm(qs[u] * ecols[u], ss[u]) + _mm(intras[u], v_news[u]) for u in idx]
    s_news = [ss[u] * jnp.exp(glasts[u]) + _mm_tn(ks[u] * jnp.exp(glasts[u] - gcols[u]), v_news[u])
              for u in idx]
    for u, (b, h) in enumerate(units):
        o_ref[b, :, sls[h]] = os_[u]
        s_scr[b, h] = s_news[u]

    @pl.when(c == pl.num_programs(1) - 1)
    def _():
        sfin_ref[...] = s_scr[...]


def _gdn_prompt(qkv, gb, w_sc, batch, seq):
    n = GDN_BLOCK
    nc = seq // n
    g = GDN_SEQS
    assert batch % g == 0

    def tok(width):
        return pl.BlockSpec((g, n, width), lambda b, c: (b, c, 0))

    state_shape = (g, GDN_HEADS, GDN_DK, GDN_DK)
    o, s_fin = pl.pallas_call(
        _gdn_prompt_kernel,
        grid=(batch // g, nc),
        in_specs=[tok(QKV_CH), tok(LANES), _full(w_sc.shape)],
        out_specs=[tok(GDN_V), pl.BlockSpec(state_shape, lambda b, c: (b, 0, 0, 0))],
        out_shape=[jax.ShapeDtypeStruct((batch, seq, GDN_V), F32),
                   jax.ShapeDtypeStruct((batch, GDN_HEADS, GDN_DK, GDN_DK), F32)],
        scratch_shapes=[pltpu.VMEM(state_shape, F32), pltpu.VMEM((g, SC_HALO + n, QKV_CH), F32)],
        compiler_params=_cparams(("arbitrary", "arbitrary")),
        name="gdn_prompt",
    )(qkv.reshape(batch, seq, QKV_CH), gb.reshape(batch, seq, LANES), w_sc)
    return o.reshape(batch * seq, GDN_V), s_fin


def _post_prompt_kernel(x_ref, conv_ref, o_ref, z_ref, gon_ref, wout_ref, gx_ref, wq_ref, mk_ref, mv_ref,
                        wo_ref, gmoe_ref, wr_ref, br_ref, tri_ref, h3s_ref, x2_ref, h3r_ref, route_ref, rt_ref,
                        cnt_ref, carry, *, n_steps):
    step = pl.program_id(0)

    @pl.when(step == 0)
    def _():
        carry[...] = jnp.zeros(carry.shape, F32)

    @pl.when(step < n_steps)
    def _():
        x1 = _mix_out(conv_ref[...], o_ref[...], z_ref[...], gon_ref[...], wout_ref, x_ref[...])
        qx = jnp.dot(_rms(x1, gx_ref[...]).astype(BF16), wq_ref[...], preferred_element_type=F32)
        sls = [slice(h * X_HEAD_DIM, (h + 1) * X_HEAD_DIM) for h in range(X_HEADS)]
        qb = qx.astype(BF16)
        ss = [lax.dot_general(qb[:, sl], mk_ref[:, sl], (((1,), (1,)), ((), ())),
                              preferred_element_type=F32) * (X_HEAD_DIM ** -0.5) for sl in sls]
        es = [jnp.exp(s - jnp.max(s, axis=-1, keepdims=True)) for s in ss]
        ps = [(e / jnp.sum(e, axis=-1, keepdims=True)).astype(BF16) for e in es]
        att = jnp.concatenate(
            [jnp.dot(p, mv_ref[:, sl], preferred_element_type=F32) for p, sl in zip(ps, sls)], axis=1)
        x2 = x1 + jnp.dot(att.astype(BF16), wo_ref[...], preferred_element_type=F32)
        x2_ref[...] = x2
        h3, route, new_carry = _route(x2, gmoe_ref[...], wr_ref, br_ref[...], carry[0:1, :], tri_ref[...])
        h3r_ref[...] = h3
        route_ref[...] = route
        rt_ref[...] = route.T[0:ROUTE_ROWS, :]
        carry[0:1, :] = new_carry
        cnt_ref[...] = jnp.broadcast_to(new_carry, cnt_ref.shape)

    @pl.when(step == n_steps)
    def _():
        h3r_ref[0:h3s_ref.shape[0], :] = h3s_ref[...]


def _post_prompt(x2d, conv, o, z, mk_b, mv_b, h3_sample, batch, seq, wts):
    tm = TOKEN_TILE
    nj = seq // tm
    rows = batch * seq
    n_steps = batch * nj
    n_s = h3_sample.shape[0]
    assert n_s <= tm

    def tok(width):
        return pl.BlockSpec((tm, width), lambda s: (jnp.minimum(s, n_steps - 1), 0))

    mem_spec = pl.BlockSpec((N_MEM, D_MODEL), lambda s: (jnp.minimum(s, n_steps - 1) // nj, 0))
    sq = _full((D_MODEL, D_MODEL))
    in_specs = [tok(D_MODEL), tok(CONV_CH), tok(GDN_V), tok(GDN_V), _full((1, GDN_DK)), sq,
                _full((1, D_MODEL)), sq, mem_spec, mem_spec, sq, _full((1, D_MODEL)),
                _full((D_MODEL, LANES)), _full((1, LANES)), _full((tm, tm)), _full(h3_sample.shape)]
    out_specs = [tok(D_MODEL),
                 pl.BlockSpec((tm, D_MODEL), lambda s: (s, 0)),
                 tok(LANES),
                 pl.BlockSpec((ROUTE_ROWS, tm), lambda s: (0, jnp.minimum(s, n_steps - 1))),
                 _full((SUBLANES, LANES))]
    out_shape = [jax.ShapeDtypeStruct((rows, D_MODEL), F32),
                 jax.ShapeDtypeStruct((rows + n_s, D_MODEL), F32),
                 jax.ShapeDtypeStruct((rows, LANES), F32),
                 jax.ShapeDtypeStruct((ROUTE_ROWS, rows), F32),
                 jax.ShapeDtypeStruct((SUBLANES, LANES), F32)]
    return pl.pallas_call(
        functools.partial(_post_prompt_kernel, n_steps=n_steps),
        grid=(n_steps + 1,),
        in_specs=in_specs,
        out_specs=out_specs,
        out_shape=out_shape,
        scratch_shapes=[pltpu.VMEM((SUBLANES, LANES), F32)],
        compiler_params=_cparams(("arbitrary",)),
        name="post_prompt",
    )(x2d, conv, o, z, wts["g_onorm"], wts["w_out_b"], wts["g_xattn"], wts["w_xq_b"], mk_b, mv_b,
      wts["w_xo_b"], wts["g_moe"], wts["w_router"], wts["b_router"], _strict_lower(tm), h3_sample)


def _pre_sample_kernel(x_ref, gmix_ref, w_ref, wab_ref, bglu_ref, wdw_ref, bdw_ref, lng_ref, lnb_ref,
                       wsc_ref, cst_ref, chist_ref, shist_ref, conv_ref, q_ref, k_ref, v_ref, z_ref,
                       gb_ref, cnew_ref, snew_ref):
    glu, qkv_pre, z, uab = _project(x_ref[...], gmix_ref[...], w_ref, wab_ref, bglu_ref[...])
    z_ref[...] = z
    gb_ref[...] = _gate_beta(uab, cst_ref[...])
    kw = CONV_WIDTH
    acc = wdw_ref[kw - 1:kw, :] * glu
    for t in range(kw - 1):
        row = chist_ref[:, t, :]
        acc = acc + wdw_ref[t:t + 1, :] * row
        if t >= 1:
            cnew_ref[:, t - 1, :] = row
    cnew_ref[:, kw - 2, :] = glu
    conv_ref[...] = _conv_post(acc, bdw_ref[...], lng_ref[...], lnb_ref[...]).astype(BF16)
    ks = SHORT_CONV
    cs = wsc_ref[ks - 1:ks, :] * qkv_pre
    for t in range(ks - 1):
        row = shist_ref[:, t, :]
        cs = cs + wsc_ref[t:t + 1, :] * row
        if t >= 1:
            snew_ref[:, t - 1, :] = row
    snew_ref[:, ks - 2, :] = qkv_pre
    q, k, v = _qkv_post(cs)
    q_ref[...] = q
    k_ref[...] = k
    v_ref[...] = v


PRE_SAMPLE_TOKENS = 32


def _pre_sample(xs, chist, shist, wts):
    n = xs.shape[0]
    tb = min(PRE_SAMPLE_TOKENS, n)

    def tok(width):
        return pl.BlockSpec((tb, width), lambda i: (i, 0))

    def hist(a):
        return pl.BlockSpec((tb,) + a.shape[1:], lambda i: (i, 0, 0))

    consts = (wts["g_mix"], wts["w_in_b"], wts["w_ab_b"], wts["b_glu"], wts["w_dw"], wts["b_dw"],
              wts["ln_g"], wts["ln_b"], wts["w_sc"], wts["gdn_cst"])
    return pl.pallas_call(
        _pre_sample_kernel,
        grid=(n // tb,),
        in_specs=[tok(D_MODEL)] + [_full(a.shape) for a in consts] + [hist(chist), hist(shist)],
        out_specs=[tok(CONV_CH), tok(GDN_V), tok(GDN_V), tok(GDN_V), tok(GDN_V), tok(LANES),
                   hist(chist), hist(shist)],
        out_shape=[jax.ShapeDtypeStruct((n, CONV_CH), BF16)]
        + [jax.ShapeDtypeStruct((n, GDN_V), F32)] * 4
        + [jax.ShapeDtypeStruct((n, LANES), F32), jax.ShapeDtypeStruct(chist.shape, F32),
           jax.ShapeDtypeStruct(shist.shape, F32)],
        compiler_params=_cparams(("arbitrary",)),
        name="pre_sample",
    )(xs, *consts, chist, shist)


GDN_STEP_TOKENS = 8


def _gdn_sample_kernel(q_ref, k_ref, v_ref, gb_ref, s_ref, o_ref, snew_ref):
    n = GDN_DK
    for i in range(GDN_STEP_TOKENS):
        for h in range(GDN_HEADS):
            sl = slice(h * GDN_DK, (h + 1) * GDN_DK)
            qrow = q_ref[i:i + 1, sl]
            krow = k_ref[i:i + 1, sl]
            vrow = v_ref[i:i + 1, sl]
            g = gb_ref[i:i + 1, h:h + 1]
            beta = gb_ref[i:i + 1, GDN_HEADS + h:GDN_HEADS + h + 1]
            kcol = jnp.broadcast_to(krow, (n, n)).T
            qcol = jnp.broadcast_to(qrow, (n, n)).T
            s1 = s_ref[i, h] * jnp.exp(g)
            sk = jnp.sum(s1 * kcol, axis=0, keepdims=True)
            vt = (vrow - sk) * beta
            s2 = s1 + kcol * vt
            snew_ref[i, h] = s2
            o_ref[i:i + 1, sl] = jnp.sum(s2 * qcol, axis=0, keepdims=True)


def _gdn_sample(q, k, v, gb, state):
    n = q.shape[0]
    tb = GDN_STEP_TOKENS

    def tok(width):
        return pl.BlockSpec((tb, width), lambda i: (i, 0))

    st = pl.BlockSpec((tb, GDN_HEADS, GDN_DK, GDN_DK), lambda i: (i, 0, 0, 0))
    return pl.pallas_call(
        _gdn_sample_kernel,
        grid=(n // tb,),
        in_specs=[tok(GDN_V), tok(GDN_V), tok(GDN_V), tok(LANES), st],
        out_specs=[tok(GDN_V), st],
        out_shape=[jax.ShapeDtypeStruct((n, GDN_V), F32), jax.ShapeDtypeStruct(state.shape, F32)],
        compiler_params=_cparams(("arbitrary",)),
        name="gdn_sample",
    )(q, k, v, gb, state)


def _mix_sample_kernel(x_ref, conv_ref, o_ref, z_ref, gon_ref, wout_ref, gx_ref, wq_ref, x1_ref, qx_ref):
    x1 = _mix_out(conv_ref[...], o_ref[...], z_ref[...], gon_ref[...], wout_ref, x_ref[...])
    x1_ref[...] = x1
    qx_ref[...] = jnp.dot(_rms(x1, gx_ref[...]).astype(BF16), wq_ref[...], preferred_element_type=F32)


def _mix_sample(xs, conv, o, z, wts):
    n = xs.shape[0]
    in_arrays = (xs, conv, o, z, wts["g_onorm"], wts["w_out_b"], wts["g_xattn"], wts["w_xq_b"])
    return pl.pallas_call(
        _mix_sample_kernel,
        grid=(1,),
        in_specs=[_full(a.shape) for a in in_arrays],
        out_specs=[_full((n, D_MODEL))] * 2,
        out_shape=[jax.ShapeDtypeStruct((n, D_MODEL), F32)] * 2,
        compiler_params=_cparams(("arbitrary",)),
        name="mix_sample",
    )(*in_arrays)


ATTN_STEP_TOKENS = 4


def _attn_sample_kernel(qx_ref, ck_ref, cv_ref, att_ref):
    for i in range(ATTN_STEP_TOKENS):
        parts = []
        for h in range(X_HEADS):
            sl = slice(h * X_HEAD_DIM, (h + 1) * X_HEAD_DIM)
            prod = ck_ref[i, :, h, :] * qx_ref[0, i:i + 1, sl]
            s = jnp.sum(prod, axis=-1, keepdims=True) * (X_HEAD_DIM ** -0.5)
            e = jnp.exp(s - jnp.max(s, axis=0, keepdims=True))
            p = e / jnp.sum(e, axis=0, keepdims=True)
            parts.append(jnp.sum(p * cv_ref[i, :, h, :], axis=0, keepdims=True))
        att_ref[0, i:i + 1, :] = jnp.concatenate(parts, axis=1)


def _attn_sample(qx, ck, cv):
    n = qx.shape[0]
    tb = ATTN_STEP_TOKENS
    q3 = qx.reshape(n // tb, tb, D_MODEL)
    qspec = pl.BlockSpec((1, tb, D_MODEL), lambda i: (i, 0, 0))
    cspec = pl.BlockSpec((tb, N_MEM, X_HEADS, X_HEAD_DIM), lambda i: (i, 0, 0, 0))
    out = pl.pallas_call(
        _attn_sample_kernel,
        grid=(n // tb,),
        in_specs=[qspec, cspec, cspec],
        out_specs=qspec,
        out_shape=jax.ShapeDtypeStruct(q3.shape, F32),
        compiler_params=_cparams(("arbitrary",)),
        name="attn_sample",
    )(q3, ck, cv)
    return out.reshape(n, D_MODEL)


def _route_sample_kernel(x1_ref, att_ref, wo_ref, gmoe_ref, wr_ref, br_ref, tri_ref, x2_ref, h3r_ref, route_ref,
                         rt_ref, cnt_ref):
    x2 = x1_ref[...] + jnp.dot(att_ref[...].astype(BF16), wo_ref[...], preferred_element_type=F32)
    x2_ref[...] = x2
    h3, route, counts = _route(x2, gmoe_ref[...], wr_ref, br_ref[...], jnp.zeros((1, LANES), F32),
                               tri_ref[...])
    h3r_ref[...] = h3
    route_ref[...] = route
    rt_ref[...] = route.T[0:ROUTE_ROWS, :]
    cnt_ref[...] = jnp.broadcast_to(counts, cnt_ref.shape)


def _route_sample(x1, att, wts):
    n = x1.shape[0]
    in_arrays = (x1, att, wts["w_xo_b"], wts["g_moe"], wts["w_router"], wts["b_router"],
                 _strict_lower(n))
    shapes = [(n, D_MODEL), (n, D_MODEL), (n, LANES), (ROUTE_ROWS, n), (SUBLANES, LANES)]
    return pl.pallas_call(
        _route_sample_kernel,
        grid=(1,),
        in_specs=[_full(a.shape) for a in in_arrays],
        out_specs=[_full(s) for s in shapes],
        out_shape=[jax.ShapeDtypeStruct(s, F32) for s in shapes],
        compiler_params=_cparams(("arbitrary",)),
        name="route_sample",
    )(*in_arrays)


def _part_tiles(tiles):
    while not any(tiles % d == 0 for d in (8, 7, 6, 5, 4)):
        tiles += 1
    return tiles


def _sc_chunk(rows_per_worker):
    for c in range(64, 0, -SUBLANES):
        if rows_per_worker % c == 0:
            return c
    raise ValueError(rows_per_worker)


def _sc_gather_rows(table, idx):
    n_workers = SC_CORES * SC_SUBCORES
    b = idx.shape[0]
    assert b % (n_workers * SUBLANES) == 0
    per_worker = b // n_workers
    chunk = _sc_chunk(per_worker)
    row_shape = table.shape[1:]
    mesh = plsc.VectorSubcoreMesh(core_axis_name="c", subcore_axis_name="s")

    @functools.partial(
        pl.kernel, mesh=mesh,
        out_type=jax.ShapeDtypeStruct((b,) + row_shape, table.dtype),
        scratch_types=[pltpu.VMEM((chunk,), I32), pltpu.VMEM((chunk,) + row_shape, table.dtype),
                       pltpu.SemaphoreType.DMA],
        name="sc_gather_rows",
    )
    def gather(table_hbm, idx_hbm, out_hbm, idx_v, rows_v, sem):
        worker = lax.axis_index("s") * SC_CORES + lax.axis_index("c")
        base = worker * per_worker

        @pl.loop(0, per_worker // chunk)
        def _(c):
            off = pl.multiple_of(base + c * chunk, SUBLANES)
            pltpu.sync_copy(idx_hbm.at[pl.ds(off, chunk)], idx_v)
            pltpu.async_copy(table_hbm.at[idx_v], rows_v, sem).wait()
            pltpu.sync_copy(rows_v, out_hbm.at[pl.ds(off, chunk)])

    return gather(table, idx)


def _moe_kernel(te_ref, tn_ref, nt_ref, xs_ref, wgu_hbm, bgu_ref, wdn_hbm, bdn_ref, *rest, first_tile):
    ys_ref, wgu_f, wdn_f, wgu_b, wdn_b, sems = rest[-6:]
    step = pl.program_id(0)
    i = first_tile + step
    total = nt_ref[0]

    def weight_copies(e):
        return (pltpu.make_async_copy(wgu_hbm.at[e], wgu_f, sems.at[0]),
                pltpu.make_async_copy(wdn_hbm.at[e], wdn_f, sems.at[1]))

    def start(e):
        for cp in weight_copies(e):
            cp.start()

    @pl.when(i < total)
    def _():
        expert = te_ref[i]
        prev = te_ref[jnp.maximum(i - 1, 0)]
        fresh = jnp.logical_or(step == 0, expert != prev)

        @pl.when(step == 0)
        def _():
            start(expert)

        @pl.when(fresh)
        def _():
            for cp in weight_copies(expert):
                cp.wait()
            wgu_b[...] = wgu_f[...].astype(BF16)
            wdn_b[...] = wdn_f[...].astype(BF16)
            nxt = tn_ref[i]

            @pl.when(nxt >= 0)
            def _():
                start(nxt)

        x = xs_ref[...].astype(BF16)

        def up(c):
            glu_cols = slice(c * MOE_COLS, (c + 1) * MOE_COLS)
            lin_cols = slice(D_EXPERT + c * MOE_COLS, D_EXPERT + (c + 1) * MOE_COLS)
            return (jnp.dot(x, wgu_b[:, glu_cols], preferred_element_type=F32) + bgu_ref[0, :, glu_cols],
                    jnp.dot(x, wgu_b[:, lin_cols], preferred_element_type=F32) + bgu_ref[0, :, lin_cols])

        n_chunks = D_EXPERT // MOE_COLS
        nxt = up(0)
        y = None
        for c in range(n_chunks):
            g, lin = nxt
            if c + 1 < n_chunks:
                nxt = up(c + 1)
            x_glu = jnp.minimum(g, SWIGLU_LIMIT)
            x_lin = jnp.clip(lin, -SWIGLU_LIMIT, SWIGLU_LIMIT)
            act = x_glu * jax.nn.sigmoid(SWIGLU_ALPHA * x_glu) * (x_lin + 1.0)
            part = jnp.dot(act.astype(BF16), wdn_b[c * MOE_COLS:(c + 1) * MOE_COLS, :],
                           preferred_element_type=F32)
            y = part if y is None else y + part
        ys_ref[...] = y + bdn_ref[0]

    @pl.when(i >= total)
    def _():
        ys_ref[...] = jnp.zeros(ys_ref.shape, F32)


def _moe(tile_e, tile_next, n_tiles, xs_part, ys_prev, first_tile, n_rows, w_gu, b_gu, w_dn, b_dn):
    tm = MOE_TILE

    def bias(shape):
        return pl.BlockSpec(shape, lambda i, te, tn, nt: (te[first_tile + i], 0, 0))

    hbm = pl.BlockSpec(memory_space=pl.ANY)
    in_specs = [pl.BlockSpec((tm, D_MODEL), lambda i, te, tn, nt: (i, 0)),
                hbm, bias((1, 1, 2 * D_EXPERT)), hbm, bias((1, 1, D_MODEL))]
    operands = [tile_e, tile_next, n_tiles, xs_part, w_gu, b_gu, w_dn, b_dn]
    aliases = {}
    if ys_prev is not None:
        in_specs.append(hbm)
        aliases = {len(operands): 0}
        operands.append(ys_prev)
    grid_spec = pltpu.PrefetchScalarGridSpec(
        num_scalar_prefetch=3,
        grid=(xs_part.shape[0] // tm,),
        in_specs=in_specs,
        out_specs=pl.BlockSpec((tm, D_MODEL), lambda i, te, tn, nt: (first_tile + i, 0)),
        scratch_shapes=[pltpu.VMEM((D_MODEL, 2 * D_EXPERT), F32),
                        pltpu.VMEM((D_EXPERT, D_MODEL), F32),
                        pltpu.VMEM((D_MODEL, 2 * D_EXPERT), BF16),
                        pltpu.VMEM((D_EXPERT, D_MODEL), BF16),
                        pltpu.SemaphoreType.DMA((2,))],
    )
    return pl.pallas_call(
        functools.partial(_moe_kernel, first_tile=first_tile),
        grid_spec=grid_spec,
        out_shape=jax.ShapeDtypeStruct((n_rows, D_MODEL), F32),
        input_output_aliases=aliases,
        compiler_params=_cparams(("arbitrary",), vmem=56 * 1024 * 1024),
        name="moe",
    )(*operands)


def _combine_kernel(x2_ref, route_ref, gfin_ref, yt_ref, *rest):
    y_ref = rest[-1]
    route = route_ref[...]
    acc = x2_ref[...]
    for j in range(TOP_K):
        acc = acc + route[:, TOP_K + j:TOP_K + j + 1] * yt_ref[j]
    y_ref[...] = _rms(acc, gfin_ref[...])


def _combine(x2, route, g_final, ys_tok, y_prev, tok0, n_tok, ys_block0):
    tc = min(COMBINE_TILE, n_tok)
    b0 = tok0 // tc

    def tok(width):
        return pl.BlockSpec((tc, width), lambda i: (b0 + i, 0))

    in_specs = [tok(D_MODEL), tok(LANES), pl.BlockSpec((1, D_MODEL), lambda i: (0, 0)),
                pl.BlockSpec((TOP_K, tc, D_MODEL), lambda i: (0, ys_block0 + i, 0))]
    operands = [x2, route, g_final, ys_tok]
    aliases = {}
    if y_prev is not None:
        in_specs.append(pl.BlockSpec(memory_space=pl.ANY))
        aliases = {len(operands): 0}
        operands.append(y_prev)
    return pl.pallas_call(
        _combine_kernel,
        grid=(n_tok // tc,),
        in_specs=in_specs,
        out_specs=tok(D_MODEL),
        out_shape=jax.ShapeDtypeStruct(x2.shape, F32),
        input_output_aliases=aliases,
        compiler_params=_cparams(("arbitrary",)),
        name="combine",
    )(*operands)


def _routing_tables(idx_t, rank_t, counts, n_rows, tiles_part):
    tm = MOE_TILE
    n_tok = idx_t.shape[1]
    n_assign = TOP_K * n_tok
    tok_mask = (1 << TOKEN_BITS) - 1
    tiles_e = (counts + tm - 1) // tm
    tile_end = jnp.cumsum(tiles_e)
    row_start = (tile_end - tiles_e) * tm
    total = tile_end[-1]
    expert_ids = jnp.arange(N_EXPERTS, dtype=I32)

    def lookup(table, e):
        return jnp.sum(jnp.where(e[..., None] == expert_ids, table, 0), axis=-1)

    pos = lookup(row_start, idx_t) + rank_t
    keys_real = (idx_t * (1 << TOKEN_BITS) + jnp.arange(n_tok, dtype=I32)[None, :]).reshape(-1)
    k = jnp.arange(n_rows - n_assign, dtype=I32)
    pad_e, pad_s = k // tm, k % tm
    pad_needed = lookup(tiles_e * tm - counts, pad_e)
    pad_key_e = jnp.where((pad_e < N_EXPERTS) & (pad_s < pad_needed), pad_e, N_EXPERTS)
    keys = jnp.sort(jnp.concatenate([keys_real, pad_key_e * (1 << TOKEN_BITS) + tok_mask]))
    src_tok = jnp.where((keys & tok_mask) == tok_mask, jnp.arange(n_rows, dtype=I32) % n_tok,
                        keys & tok_mask)
    tid = jnp.minimum(jnp.arange(n_rows // tm, dtype=I32), total - 1)
    tile_e = jnp.minimum(jnp.sum((tid[:, None] >= tile_end[None, :]).astype(I32), axis=1), N_EXPERTS - 1)
    later = (expert_ids[None, :] > expert_ids[:, None]) & (tiles_e[None, :] > 0)
    next_e = jnp.min(jnp.where(later, expert_ids[None, :], N_EXPERTS), axis=1)
    nxt = lookup(next_e, tile_e)
    nxt_first_tile = lookup(tile_end - tiles_e, jnp.minimum(nxt, N_EXPERTS - 1))
    part_end = (tid // tiles_part + 1) * tiles_part
    tile_next = jnp.where((nxt < N_EXPERTS) & (nxt_first_tile < part_end), nxt, -1)
    return (tile_e.astype(I32), tile_next.astype(I32), total.reshape(1).astype(I32), src_tok.astype(I32),
            pos.reshape(-1).astype(I32))


def _pad_rows(a, rows):
    return jnp.concatenate([a, jnp.zeros((rows - a.shape[0],) + a.shape[1:], a.dtype)], axis=0)


def _pad_lanes(a, lanes=LANES):
    return jnp.concatenate([a, jnp.zeros(a.shape[:-1] + (lanes - a.shape[-1],), a.dtype)], axis=-1)


def kernel(x_prompt, mem_prompt, x_sample, state_conformer_conv, state_gdn_conv, state_gdn, cache_mem_k,
           cache_mem_v, w_in, b_glu, w_dw, b_dw, ln_g, ln_b, w_sc, a_log, dt_bias, g_onorm, w_out, g_mix,
           g_xattn, g_mem, w_xq, w_mk, w_mv, w_xo, g_moe, w_router, b_router, w_gu, b_gu, w_dn, b_dn,
           g_final):
    assert w_in.shape[0] == 1, "single-layer configuration"
    batch, seq, _ = x_prompt.shape
    n_s = x_sample.shape[0]
    n_p = batch * seq
    n_all = n_p + n_s
    assert seq % TOKEN_TILE == 0 and n_p % n_s == 0 and n_all < (1 << TOKEN_BITS) - 1
    assert (n_all * TOP_K) % (SC_CORES * SC_SUBCORES * SUBLANES) == 0

    wts = {
        "g_mix": g_mix[0][None], "g_xattn": g_xattn[0][None], "g_moe": g_moe[0][None],
        "g_onorm": g_onorm[0][None],
        "w_in_b": w_in[0][:, :OFF_A].astype(BF16),
        "w_ab_b": _pad_lanes(w_in[0][:, OFF_A:]).astype(BF16),
        "b_glu": b_glu[0][None],
        "w_dw": _pad_rows(w_dw[0], 32), "b_dw": b_dw[0][None], "ln_g": ln_g[0][None], "ln_b": ln_b[0][None],
        "w_sc": _pad_rows(w_sc[0], 8),
        "gdn_cst": _pad_rows(_pad_lanes(jnp.stack([a_log[0], dt_bias[0]])), 8),
        "w_out_b": w_out[0].astype(BF16), "w_xq_b": w_xq[0].astype(BF16), "w_xo_b": w_xo[0].astype(BF16),
        "w_router": _pad_lanes(w_router[0]), "b_router": _pad_lanes(b_router[0][None]),
    }

    mk, mv, mk_b, mv_b = _mem_kv(mem_prompt.reshape(batch * N_MEM, D_MODEL), g_mem[0][None],
                                 w_mk[0].astype(BF16), w_mv[0].astype(BF16))
    xp = x_prompt.reshape(n_p, D_MODEL)
    conv_p, qkv_p, z_p, gb_p, cstate_p, sstate_p = _pre_prompt(xp, batch, seq, wts)
    o_p, gstate_p = _gdn_prompt(qkv_p, gb_p, wts["w_sc"], batch, seq)

    xs = x_sample.reshape(n_s, D_MODEL)
    conv_s, q_s, k_s, v_s, z_s, gb_s, cstate_s, sstate_s = _pre_sample(
        xs, state_conformer_conv[0], state_gdn_conv[0], wts)
    o_s, gstate_s = _gdn_sample(q_s, k_s, v_s, gb_s, state_gdn[0])
    x1_s, qx_s = _mix_sample(xs, conv_s, o_s, z_s, wts)
    att_s = _attn_sample(qx_s, cache_mem_k[0], cache_mem_v[0])
    x2_s, h3_s, route_s, rt_s, counts_s = _route_sample(x1_s, att_s, wts)

    x2_p, h3r, route_p, rt_p, counts_p = _post_prompt(xp, conv_p, o_p, z_p, mk_b, mv_b, h3_s, batch, seq,
                                                      wts)

    counts_p = counts_p[0, :N_EXPERTS].astype(I32)
    counts_s = counts_s[0, :N_EXPERTS].astype(I32)
    idx_s = rt_s[0:TOP_K].astype(I32)
    rank_s = rt_s[2 * TOP_K:3 * TOP_K].astype(I32) + jnp.sum(
        jnp.where(idx_s[..., None] == jnp.arange(N_EXPERTS, dtype=I32), counts_p, 0), axis=-1)
    idx_t = jnp.concatenate([rt_p[0:TOP_K].astype(I32), idx_s], axis=1)
    rank_t = jnp.concatenate([rt_p[2 * TOP_K:3 * TOP_K].astype(I32), rank_s], axis=1)
    min_tiles = -(-(n_all * TOP_K + N_EXPERTS * (MOE_TILE - 1)) // MOE_TILE)
    tiles_part = _part_tiles(-(-min_tiles // MOE_PARTS))
    rows_part = tiles_part * MOE_TILE
    n_rows = rows_part * MOE_PARTS
    tile_e, tile_next, n_tiles, src_tok, pos = _routing_tables(idx_t, rank_t, counts_p + counts_s, n_rows,
                                                               tiles_part)
    pos_t = pos.reshape(TOP_K, n_all)
    xs_parts = [_sc_gather_rows(h3r, src_tok[k * rows_part:(k + 1) * rows_part]) for k in range(MOE_PARTS)]
    ys = None
    for k in range(MOE_PARTS):
        ys = _moe(tile_e, tile_next, n_tiles, xs_parts[k], ys, k * tiles_part, n_rows, w_gu[0],
                  b_gu[0][:, None, :], w_dn[0], b_dn[0][:, None, :])
    gfin = g_final[None]
    tok_part = n_p // MOE_PARTS
    assert tok_part % COMBINE_TILE == 0
    y_p = None
    for k in range(MOE_PARTS):
        last = k == MOE_PARTS - 1
        pos_k = pos_t[:, k * tok_part:(k + 1) * tok_part]
        if last:
            pos_k = jnp.concatenate([pos_k, pos_t[:, n_p:]], axis=1)
        ys_tok = _sc_gather_rows(ys, pos_k.reshape(-1)).reshape(TOP_K, pos_k.shape[1], D_MODEL)
        y_p = _combine(x2_p, route_p, gfin, ys_tok, y_p, k * tok_part, tok_part, 0)
        if last:
            y_s = _combine(x2_s, route_s, gfin, ys_tok, None, 0, n_s, tok_part // min(COMBINE_TILE, n_s))

    return (y_p.reshape(batch, seq, D_MODEL), y_s.reshape(n_s, 1, D_MODEL),
            cstate_p[None], sstate_p[None], gstate_p[None],
            mk.reshape(1, batch, N_MEM, X_HEADS, X_HEAD_DIM), mv.reshape(1, batch, N_MEM, X_HEADS, X_HEAD_DIM),
            cstate_s[None], sstate_s[None], gstate_s[None])
```

```python
import functools

import jax
import jax.numpy as jnp
from jax import lax
from jax.experimental import pallas as pl
from jax.experimental.pallas import tpu as pltpu
from jax.experimental.pallas import tpu_sc as plsc

F32, BF16, I32 = jnp.float32, jnp.bfloat16, jnp.int32

D_MODEL = 1024
CONV_CH = 512
CONV_WIDTH = 31
GDN_HEADS = 4
GDN_DK = 128
GDN_V = 512
QKV_CH = 1536
SHORT_CONV = 4
N_MEM = 256
X_HEADS = 4
X_HEAD_DIM = 256
N_EXPERTS = 32
TOP_K = 4
D_EXPERT = 1024
SWIGLU_LIMIT = 7.0
SWIGLU_ALPHA = 1.702
NORM_EPS = 1e-6
OFF_QKV = 2 * CONV_CH
OFF_Z = OFF_QKV + QKV_CH
OFF_A = OFF_Z + GDN_V

LANES = 128
SUBLANES = 8
GDN_BLOCK = 128
TOKEN_TILE = 256
MOE_TILE = 384
MOE_COLS = 256
MOE_PARTS = 4
COMBINE_TILE = 128
TOKEN_BITS = 15
ROUTE_ROWS = 16
SC_CORES = 2
SC_SUBCORES = 16
VMEM_LIMIT = 48 * 1024 * 1024


def _cparams(sem, vmem=VMEM_LIMIT):
    return pltpu.CompilerParams(dimension_semantics=sem, vmem_limit_bytes=vmem)


def _mm(a, b):
    return jnp.dot(a.astype(BF16), b.astype(BF16), preferred_element_type=F32)


def _mm_nt(a, b):
    return lax.dot_general(a.astype(BF16), b.astype(BF16), (((1,), (1,)), ((), ())),
                           preferred_element_type=F32)


def _mm_tn(a, b):
    return lax.dot_general(a.astype(BF16), b.astype(BF16), (((0,), (0,)), ((), ())),
                           preferred_element_type=F32)


def _rms(x, g):
    return x * lax.rsqrt(jnp.mean(x * x, axis=-1, keepdims=True) + NORM_EPS) * g


def _silu(x):
    return x * jax.nn.sigmoid(x)


def _full(shape):
    return pl.BlockSpec(shape, lambda *_: (0,) * len(shape))


def _strict_lower(n):
    return jnp.tril(jnp.ones((n, n), BF16), k=-1)


def _project(x, gmix, w_ref, wab_ref, bglu):
    h = _rms(x, gmix).astype(BF16)
    u_glu = jnp.dot(h, w_ref[:, 0:OFF_QKV], preferred_element_type=F32) + bglu
    glu = u_glu[:, :CONV_CH] * jax.nn.sigmoid(u_glu[:, CONV_CH:])
    qkv_pre = jnp.dot(h, w_ref[:, OFF_QKV:OFF_Z], preferred_element_type=F32)
    z = jnp.dot(h, w_ref[:, OFF_Z:OFF_A], preferred_element_type=F32)
    uab = jnp.dot(h, wab_ref[...], preferred_element_type=F32)
    return glu, qkv_pre, z, uab


def _gate_beta(uab, cst):
    lane = lax.broadcasted_iota(I32, uab.shape, 1)
    g = -jnp.exp(cst[0:1, :]) * jax.nn.softplus(uab + cst[1:2, :])
    return jnp.where(lane < GDN_HEADS, g, jax.nn.sigmoid(uab))


def _conv_post(c, b_dw, ln_g, ln_b):
    c = c + b_dw
    mu = jnp.mean(c, axis=-1, keepdims=True)
    d = c - mu
    var = jnp.mean(d * d, axis=-1, keepdims=True)
    return _silu(d * lax.rsqrt(var + NORM_EPS) * ln_g + ln_b)


def _qkv_post(cs):
    a = _silu(cs)
    parts = []
    for h in range(2 * GDN_HEADS):
        seg = a[:, h * GDN_DK:(h + 1) * GDN_DK]
        n = seg * lax.rsqrt(jnp.sum(seg * seg, axis=-1, keepdims=True) + NORM_EPS)
        if h < GDN_HEADS:
            n = n * (GDN_DK ** -0.5)
        parts.append(n)
    q = jnp.concatenate(parts[:GDN_HEADS], axis=1)
    k = jnp.concatenate(parts[GDN_HEADS:], axis=1)
    return q, k, a[:, 2 * GDN_HEADS * GDN_DK:]


def _mix_out(conv_b, o, z, gon, wout_ref, x):
    parts = []
    for h in range(GDN_HEADS):
        oh = o[:, h * 128:(h + 1) * 128]
        parts.append(oh * lax.rsqrt(jnp.mean(oh * oh, axis=-1, keepdims=True) + NORM_EPS) * gon)
    on = jnp.concatenate(parts, axis=1) * _silu(z)
    mixed = (jnp.dot(conv_b, wout_ref[0:CONV_CH, :], preferred_element_type=F32)
             + jnp.dot(on.astype(BF16), wout_ref[CONV_CH:, :], preferred_element_type=F32))
    return x + mixed


def _route(x2, gmoe, wr_ref, br, carry, before):
    m = x2.shape[0]
    h3 = _rms(x2, gmoe)
    h_hi = h3.astype(BF16)
    r1 = h3 - h_hi.astype(F32)
    h_mid = r1.astype(BF16)
    h_lo = (r1 - h_mid.astype(F32)).astype(BF16)
    w = wr_ref[...]
    w_hi = w.astype(BF16)
    w_lo = (w - w_hi.astype(F32)).astype(BF16)
    logits = (jnp.dot(h_hi, w_hi, preferred_element_type=F32)
              + jnp.dot(h_hi, w_lo, preferred_element_type=F32)
              + jnp.dot(h_mid, w_hi, preferred_element_type=F32)
              + jnp.dot(h_lo, w_hi, preferred_element_type=F32)) + br
    neg = jnp.float32(-jnp.inf)
    n_groups = 4 if m % 32 == 0 else 1
    rows = m // n_groups
    lanes_g = lax.broadcasted_iota(I32, (rows, LANES), 1).astype(F32)
    works = [jnp.where(lanes_g < N_EXPERTS, logits[g * rows:(g + 1) * rows], neg) for g in range(n_groups)]
    vals, idxs = [], []
    for _ in range(TOP_K):
        mxs = [jnp.max(w, axis=-1, keepdims=True) for w in works]
        ixs = [jnp.min(jnp.where(w == mx, lanes_g, float(LANES)), axis=-1, keepdims=True)
               for w, mx in zip(works, mxs)]
        works = [jnp.where(lanes_g == ix, neg, w) for w, ix in zip(works, ixs)]
        vals.append(mxs)
        idxs.append(ixs)
    sels, gates = [], []
    for g in range(n_groups):
        es = [jnp.exp(vals[r][g] - vals[0][g]) for r in range(TOP_K)]
        den = es[0] + es[1] + es[2] + es[3]
        gates.append([e / den for e in es])
        sel_g = jnp.zeros((rows, LANES), F32)
        for r in range(TOP_K):
            sel_g = sel_g + jnp.where(lanes_g == idxs[r][g], 1.0, 0.0)
        sels.append(sel_g)
    sel = jnp.concatenate(sels, axis=0)
    rank_full = jnp.dot(before, sel.astype(BF16), preferred_element_type=F32) + carry
    routes = []
    for g in range(n_groups):
        rank_g = rank_full[g * rows:(g + 1) * rows]
        route_g = jnp.zeros((rows, LANES), F32)
        for r in range(TOP_K):
            rk = jnp.sum(jnp.where(lanes_g == idxs[r][g], rank_g, 0.0), axis=-1, keepdims=True)
            route_g = (route_g + jnp.where(lanes_g == r, idxs[r][g], 0.0)
                       + jnp.where(lanes_g == TOP_K + r, gates[g][r], 0.0)
                       + jnp.where(lanes_g == 2 * TOP_K + r, rk, 0.0))
        routes.append(route_g)
    new_carry = carry + jnp.sum(sel, axis=0, keepdims=True)
    return h3, jnp.concatenate(routes, axis=0), new_carry


def _mem_kv_kernel(mem_ref, g_ref, wk_ref, wv_ref, mk_ref, mv_ref, mkb_ref, mvb_ref):
    m = _rms(mem_ref[...], g_ref[...]).astype(BF16)
    mk = jnp.dot(m, wk_ref[...], preferred_element_type=F32)
    mv = jnp.dot(m, wv_ref[...], preferred_element_type=F32)
    mk_ref[...] = mk
    mv_ref[...] = mv
    mkb_ref[...] = mk.astype(BF16)
    mvb_ref[...] = mv.astype(BF16)


def _mem_kv(mem, g_mem, wk_b, wv_b):
    rows = mem.shape[0]
    tm = TOKEN_TILE
    row_spec = pl.BlockSpec((tm, D_MODEL), lambda i: (i, 0))
    return pl.pallas_call(
        _mem_kv_kernel,
        grid=(rows // tm,),
        in_specs=[row_spec, _full((1, D_MODEL)), _full((D_MODEL, D_MODEL)), _full((D_MODEL, D_MODEL))],
        out_specs=[row_spec] * 4,
        out_shape=[jax.ShapeDtypeStruct((rows, D_MODEL), F32)] * 2
        + [jax.ShapeDtypeStruct((rows, D_MODEL), BF16)] * 2,
        compiler_params=_cparams(("arbitrary",)),
        name="mem_kv",
    )(mem, g_mem, wk_b, wv_b)


CONV_HALO = 32
SC_HALO = 8


def _pre_prompt_kernel(x_ref, gmix_ref, w_ref, wab_ref, bglu_ref, wdw_ref, bdw_ref, lng_ref, lnb_ref,
                       cst_ref, conv_ref, qkv_ref, z_ref, gb_ref, cstate_ref, sstate_ref, cbuf, *, tm):
    j = pl.program_id(1)

    @pl.when(j == 0)
    def _():
        cbuf[0:CONV_HALO, :] = jnp.zeros((CONV_HALO, CONV_CH), F32)

    glu, qkv_pre, z, uab = _project(x_ref[...], gmix_ref[...], w_ref, wab_ref, bglu_ref[...])
    cbuf[CONV_HALO:CONV_HALO + tm, :] = glu
    qkv_ref[...] = qkv_pre
    z_ref[...] = z
    gb_ref[...] = _gate_beta(uab, cst_ref[...])

    base = CONV_HALO - (CONV_WIDTH - 1)
    rows = CONV_HALO + tm
    accs = []
    for c in range(CONV_CH // LANES):
        lanes = slice(c * LANES, (c + 1) * LANES)
        block = cbuf[:, lanes]
        acc = None
        for r in range(SUBLANES):
            shifted = block if r == 0 else pltpu.roll(block, rows - r, 0)
            for a in range(base, base + CONV_WIDTH):
                if a % SUBLANES == r:
                    t = a - base
                    term = wdw_ref[t:t + 1, lanes] * shifted[a - r:a - r + tm, :]
                    acc = term if acc is None else acc + term
        accs.append(acc)
    acc = jnp.concatenate(accs, axis=1)
    conv_ref[...] = _conv_post(acc, bdw_ref[...], lng_ref[...], lnb_ref[...]).astype(BF16)

    @pl.when(j == pl.num_programs(1) - 1)
    def _():
        cstate_ref[0] = cbuf[pl.ds(CONV_HALO + tm - (CONV_WIDTH - 1), CONV_WIDTH - 1), :]
        sstate_ref[0] = qkv_pre[tm - (SHORT_CONV - 1):, :]

    cbuf[0:CONV_HALO, :] = cbuf[tm:tm + CONV_HALO, :]


def _pre_prompt(x2d, batch, seq, wts):
    tm = TOKEN_TILE
    nj = seq // tm
    rows = batch * seq

    def tok(width):
        return pl.BlockSpec((tm, width), lambda b, j: (b * nj + j, 0))

    in_specs = [tok(D_MODEL), _full((1, D_MODEL)), _full((D_MODEL, OFF_A)), _full((D_MODEL, LANES)),
                _full((1, OFF_QKV)), _full((32, CONV_CH)), _full((1, CONV_CH)), _full((1, CONV_CH)),
                _full((1, CONV_CH)), _full((8, LANES))]
    out_specs = [tok(CONV_CH), tok(QKV_CH), tok(GDN_V), tok(LANES),
                 pl.BlockSpec((1, CONV_WIDTH - 1, CONV_CH), lambda b, j: (b, 0, 0)),
                 pl.BlockSpec((1, SHORT_CONV - 1, QKV_CH), lambda b, j: (b, 0, 0))]
    out_shape = [jax.ShapeDtypeStruct((rows, CONV_CH), BF16),
                 jax.ShapeDtypeStruct((rows, QKV_CH), F32),
                 jax.ShapeDtypeStruct((rows, GDN_V), F32),
                 jax.ShapeDtypeStruct((rows, LANES), F32),
                 jax.ShapeDtypeStruct((batch, CONV_WIDTH - 1, CONV_CH), F32),
                 jax.ShapeDtypeStruct((batch, SHORT_CONV - 1, QKV_CH), F32)]
    return pl.pallas_call(
        functools.partial(_pre_prompt_kernel, tm=tm),
        grid=(batch, nj),
        in_specs=in_specs,
        out_specs=out_specs,
        out_shape=out_shape,
        scratch_shapes=[pltpu.VMEM((CONV_HALO + tm, CONV_CH), F32)],
        compiler_params=_cparams(("arbitrary", "arbitrary")),
        name="pre_prompt",
    )(x2d, wts["g_mix"], wts["w_in_b"], wts["w_ab_b"], wts["b_glu"], wts["w_dw"], wts["b_dw"],
      wts["ln_g"], wts["ln_b"], wts["gdn_cst"])


GDN_SEQS = 4


def _gdn_prompt_kernel(qkv_ref, gb_ref, wsc_ref, o_ref, sfin_ref, s_scr, sbuf):
    c = pl.program_id(1)
    n = GDN_BLOCK
    seqs = range(GDN_SEQS)

    @pl.when(c == 0)
    def _():
        s_scr[...] = jnp.zeros(s_scr.shape, F32)
        sbuf[:, 0:SC_HALO, :] = jnp.zeros((GDN_SEQS, SC_HALO, QKV_CH), F32)

    sbase = SC_HALO - (SHORT_CONV - 1)
    qkvs = []
    for b in seqs:
        sbuf[b, SC_HALO:SC_HALO + n, :] = qkv_ref[b]
        cs = wsc_ref[0:1, :] * sbuf[b, pl.ds(sbase, n), :]
        for t in range(1, SHORT_CONV):
            cs = cs + wsc_ref[t:t + 1, :] * sbuf[b, pl.ds(sbase + t, n), :]
        qkvs.append(_qkv_post(cs))
        sbuf[b, 0:SC_HALO, :] = sbuf[b, n:n + SC_HALO, :]

    row = lax.broadcasted_iota(I32, (n, n), 0)
    col = lax.broadcasted_iota(I32, (n, n), 1)
    causal = row >= col
    strict = row > col
    tri = jnp.where(causal, 1.0, 0.0).astype(BF16)
    eye = jnp.where(row == col, 1.0, 0.0)
    level_masks = []
    b = 1
    while b < n:
        same_pair = ((row ^ col) & ~(2 * b - 1)) == 0
        level_masks.append(same_pair & ((row & b) != 0) & ((col & b) == 0))
        b *= 2
    gbs, gcums, gcum_ts, egcs = [], [], [], []
    for b in seqs:
        gb = gb_ref[b]
        g1 = gb.astype(BF16)
        r1 = gb - g1.astype(F32)
        g2 = r1.astype(BF16)
        g3 = (r1 - g2.astype(F32)).astype(BF16)
        gcum = (jnp.dot(tri, g1, preferred_element_type=F32) + jnp.dot(tri, g2, preferred_element_type=F32)
                + jnp.dot(tri, g3, preferred_element_type=F32))
        gbs.append(gb)
        gcums.append(gcum)
        gcum_ts.append(gcum.T)
        egcs.append(jnp.exp(gcum))
    units = [(b, h) for b in seqs for h in range(GDN_HEADS)]
    idx = range(len(units))
    sls = [slice(h * GDN_DK, (h + 1) * GDN_DK) for h in range(GDN_HEADS)]
    qs = [qkvs[b][0][:, sls[h]] for b, h in units]
    ks = [qkvs[b][1][:, sls[h]] for b, h in units]
    vs = [qkvs[b][2][:, sls[h]] for b, h in units]
    ss = [s_scr[b, h] for b, h in units]
    gcols = [gcums[b][:, h:h + 1] for b, h in units]
    ecols = [egcs[b][:, h:h + 1] for b, h in units]
    betas = [gbs[b][:, GDN_HEADS + h:GDN_HEADS + h + 1] for b, h in units]
    glasts = [gcums[b][n - 1:n, h:h + 1] for b, h in units]
    decays = [jnp.where(causal, jnp.exp(jnp.where(causal, gcols[u] - gcum_ts[b][h:h + 1, :], 0.0)), 0.0)
              for u, (b, h) in enumerate(units)]
    kbs = [ks[u] * betas[u] for u in idx]
    lowers = [jnp.where(strict, _mm_nt(kbs[u], ks[u]) * decays[u], 0.0) for u in idx]
    intras = [jnp.where(causal, _mm_nt(qs[u], ks[u]) * decays[u], 0.0) for u in idx]
    xs = [eye - jnp.where(level_masks[0], lowers[u], 0.0) for u in idx]
    for mask in level_masks[1:]:
        ts = [_mm(xs[u], jnp.where(mask, lowers[u], 0.0)) for u in idx]
        xs = [xs[u] - _mm(ts[u], xs[u]) for u in idx]
    us = [_mm(xs[u], vs[u] * betas[u]) for u in idx]
    ws = [_mm(xs[u], kbs[u] * ecols[u]) for u in idx]
    v_news = [us[u] - _mm(ws[u], ss[u]) for u in idx]
    os_ = [_mm(qs[u] * ecols[u], ss[u]) + _mm(intras[u], v_news[u]) for u in idx]
    s_news = [ss[u] * jnp.exp(glasts[u]) + _mm_tn(ks[u] * jnp.exp(glasts[u] - gcols[u]), v_news[u])
              for u in idx]
    for u, (b, h) in enumerate(units):
        o_ref[b, :, sls[h]] = os_[u]
        s_scr[b, h] = s_news[u]

    @pl.when(c == pl.num_programs(1) - 1)
    def _():
        sfin_ref[...] = s_scr[...]


def _gdn_prompt(qkv, gb, w_sc, batch, seq):
    n = GDN_BLOCK
    nc = seq // n
    g = GDN_SEQS
    assert batch % g == 0

    def tok(width):
        return pl.BlockSpec((g, n, width), lambda b, c: (b, c, 0))

    state_shape = (g, GDN_HEADS, GDN_DK, GDN_DK)
    o, s_fin = pl.pallas_call(
        _gdn_prompt_kernel,
        grid=(batch // g, nc),
        in_specs=[tok(QKV_CH), tok(LANES), _full(w_sc.shape)],
        out_specs=[tok(GDN_V), pl.BlockSpec(state_shape, lambda b, c: (b, 0, 0, 0))],
        out_shape=[jax.ShapeDtypeStruct((batch, seq, GDN_V), F32),
                   jax.ShapeDtypeStruct((batch, GDN_HEADS, GDN_DK, GDN_DK), F32)],
        scratch_shapes=[pltpu.VMEM(state_shape, F32), pltpu.VMEM((g, SC_HALO + n, QKV_CH), F32)],
        compiler_params=_cparams(("arbitrary", "arbitrary")),
        name="gdn_prompt",
    )(qkv.reshape(batch, seq, QKV_CH), gb.reshape(batch, seq, LANES), w_sc)
    return o.reshape(batch * seq, GDN_V), s_fin


def _post_prompt_kernel(x_ref, conv_ref, o_ref, z_ref, gon_ref, wout_ref, gx_ref, wq_ref, mk_ref, mv_ref,
                        wo_ref, gmoe_ref, wr_ref, br_ref, tri_ref, h3s_ref, x2_ref, h3r_ref, route_ref, rt_ref,
                        cnt_ref, carry, *, n_steps):
    step = pl.program_id(0)

    @pl.when(step == 0)
    def _():
        carry[...] = jnp.zeros(carry.shape, F32)

    @pl.when(step < n_steps)
    def _():
        x1 = _mix_out(conv_ref[...], o_ref[...], z_ref[...], gon_ref[...], wout_ref, x_ref[...])
        qx = jnp.dot(_rms(x1, gx_ref[...]).astype(BF16), wq_ref[...], preferred_element_type=F32)
        sls = [slice(h * X_HEAD_DIM, (h + 1) * X_HEAD_DIM) for h in range(X_HEADS)]
        qb = qx.astype(BF16)
        ss = [lax.dot_general(qb[:, sl], mk_ref[:, sl], (((1,), (1,)), ((), ())),
                              preferred_element_type=F32) * (X_HEAD_DIM ** -0.5) for sl in sls]
        es = [jnp.exp(s - jnp.max(s, axis=-1, keepdims=True)) for s in ss]
        ps = [(e / jnp.sum(e, axis=-1, keepdims=True)).astype(BF16) for e in es]
        att = jnp.concatenate(
            [jnp.dot(p, mv_ref[:, sl], preferred_element_type=F32) for p, sl in zip(ps, sls)], axis=1)
        x2 = x1 + jnp.dot(att.astype(BF16), wo_ref[...], preferred_element_type=F32)
        x2_ref[...] = x2
        h3, route, new_carry = _route(x2, gmoe_ref[...], wr_ref, br_ref[...], carry[0:1, :], tri_ref[...])
        h3r_ref[...] = h3
        route_ref[...] = route
        rt_ref[...] = route.T[0:ROUTE_ROWS, :]
        carry[0:1, :] = new_carry
        cnt_ref[...] = jnp.broadcast_to(new_carry, cnt_ref.shape)

    @pl.when(step == n_steps)
    def _():
        h3r_ref[0:h3s_ref.shape[0], :] = h3s_ref[...]


def _post_prompt(x2d, conv, o, z, mk_b, mv_b, h3_sample, batch, seq, wts):
    tm = TOKEN_TILE
    nj = seq // tm
    rows = batch * seq
    n_steps = batch * nj
    n_s = h3_sample.shape[0]
    assert n_s <= tm

    def tok(width):
        return pl.BlockSpec((tm, width), lambda s: (jnp.minimum(s, n_steps - 1), 0))

    mem_spec = pl.BlockSpec((N_MEM, D_MODEL), lambda s: (jnp.minimum(s, n_steps - 1) // nj, 0))
    sq = _full((D_MODEL, D_MODEL))
    in_specs = [tok(D_MODEL), tok(CONV_CH), tok(GDN_V), tok(GDN_V), _full((1, GDN_DK)), sq,
                _full((1, D_MODEL)), sq, mem_spec, mem_spec, sq, _full((1, D_MODEL)),
                _full((D_MODEL, LANES)), _full((1, LANES)), _full((tm, tm)), _full(h3_sample.shape)]
    out_specs = [tok(D_MODEL),
                 pl.BlockSpec((tm, D_MODEL), lambda s: (s, 0)),
                 tok(LANES),
                 pl.BlockSpec((ROUTE_ROWS, tm), lambda s: (0, jnp.minimum(s, n_steps - 1))),
                 _full((SUBLANES, LANES))]
    out_shape = [jax.ShapeDtypeStruct((rows, D_MODEL), F32),
                 jax.ShapeDtypeStruct((rows + n_s, D_MODEL), F32),
                 jax.ShapeDtypeStruct((rows, LANES), F32),
                 jax.ShapeDtypeStruct((ROUTE_ROWS, rows), F32),
                 jax.ShapeDtypeStruct((SUBLANES, LANES), F32)]
    return pl.pallas_call(
        functools.partial(_post_prompt_kernel, n_steps=n_steps),
        grid=(n_steps + 1,),
        in_specs=in_specs,
        out_specs=out_specs,
        out_shape=out_shape,
        scratch_shapes=[pltpu.VMEM((SUBLANES, LANES), F32)],
        compiler_params=_cparams(("arbitrary",)),
        name="post_prompt",
    )(x2d, conv, o, z, wts["g_onorm"], wts["w_out_b"], wts["g_xattn"], wts["w_xq_b"], mk_b, mv_b,
      wts["w_xo_b"], wts["g_moe"], wts["w_router"], wts["b_router"], _strict_lower(tm), h3_sample)


def _pre_sample_kernel(x_ref, gmix_ref, w_ref, wab_ref, bglu_ref, wdw_ref, bdw_ref, lng_ref, lnb_ref,
                       wsc_ref, cst_ref, chist_ref, shist_ref, conv_ref, q_ref, k_ref, v_ref, z_ref,
                       gb_ref, cnew_ref, snew_ref):
    glu, qkv_pre, z, uab = _project(x_ref[...], gmix_ref[...], w_ref, wab_ref, bglu_ref[...])
    z_ref[...] = z
    gb_ref[...] = _gate_beta(uab, cst_ref[...])
    kw = CONV_WIDTH
    acc = wdw_ref[kw - 1:kw, :] * glu
    for t in range(kw - 1):
        row = chist_ref[:, t, :]
        acc = acc + wdw_ref[t:t + 1, :] * row
        if t >= 1:
            cnew_ref[:, t - 1, :] = row
    cnew_ref[:, kw - 2, :] = glu
    conv_ref[...] = _conv_post(acc, bdw_ref[...], lng_ref[...], lnb_ref[...]).astype(BF16)
    ks = SHORT_CONV
    cs = wsc_ref[ks - 1:ks, :] * qkv_pre
    for t in range(ks - 1):
        row = shist_ref[:, t, :]
        cs = cs + wsc_ref[t:t + 1, :] * row
        if t >= 1:
            snew_ref[:, t - 1, :] = row
    snew_ref[:, ks - 2, :] = qkv_pre
    q, k, v = _qkv_post(cs)
    q_ref[...] = q
    k_ref[...] = k
    v_ref[...] = v


PRE_SAMPLE_TOKENS = 32


def _pre_sample(xs, chist, shist, wts):
    n = xs.shape[0]
    tb = min(PRE_SAMPLE_TOKENS, n)

    def tok(width):
        return pl.BlockSpec((tb, width), lambda i: (i, 0))

    def hist(a):
        return pl.BlockSpec((tb,) + a.shape[1:], lambda i: (i, 0, 0))

    consts = (wts["g_mix"], wts["w_in_b"], wts["w_ab_b"], wts["b_glu"], wts["w_dw"], wts["b_dw"],
              wts["ln_g"], wts["ln_b"], wts["w_sc"], wts["gdn_cst"])
    return pl.pallas_call(
        _pre_sample_kernel,
        grid=(n // tb,),
        in_specs=[tok(D_MODEL)] + [_full(a.shape) for a in consts] + [hist(chist), hist(shist)],
        out_specs=[tok(CONV_CH), tok(GDN_V), tok(GDN_V), tok(GDN_V), tok(GDN_V), tok(LANES),
                   hist(chist), hist(shist)],
        out_shape=[jax.ShapeDtypeStruct((n, CONV_CH), BF16)]
        + [jax.ShapeDtypeStruct((n, GDN_V), F32)] * 4
        + [jax.ShapeDtypeStruct((n, LANES), F32), jax.ShapeDtypeStruct(chist.shape, F32),
           jax.ShapeDtypeStruct(shist.shape, F32)],
        compiler_params=_cparams(("arbitrary",)),
        name="pre_sample",
    )(xs, *consts, chist, shist)


GDN_STEP_TOKENS = 8


def _gdn_sample_kernel(q_ref, k_ref, v_ref, gb_ref, s_ref, o_ref, snew_ref):
    n = GDN_DK
    for i in range(GDN_STEP_TOKENS):
        for h in range(GDN_HEADS):
            sl = slice(h * GDN_DK, (h + 1) * GDN_DK)
            qrow = q_ref[i:i + 1, sl]
            krow = k_ref[i:i + 1, sl]
            vrow = v_ref[i:i + 1, sl]
            g = gb_ref[i:i + 1, h:h + 1]
            beta = gb_ref[i:i + 1, GDN_HEADS + h:GDN_HEADS + h + 1]
            kcol = jnp.broadcast_to(krow, (n, n)).T
            qcol = jnp.broadcast_to(qrow, (n, n)).T
            s1 = s_ref[i, h] * jnp.exp(g)
            sk = jnp.sum(s1 * kcol, axis=0, keepdims=True)
            vt = (vrow - sk) * beta
            s2 = s1 + kcol * vt
            snew_ref[i, h] = s2
            o_ref[i:i + 1, sl] = jnp.sum(s2 * qcol, axis=0, keepdims=True)


def _gdn_sample(q, k, v, gb, state):
    n = q.shape[0]
    tb = GDN_STEP_TOKENS

    def tok(width):
        return pl.BlockSpec((tb, width), lambda i: (i, 0))

    st = pl.BlockSpec((tb, GDN_HEADS, GDN_DK, GDN_DK), lambda i: (i, 0, 0, 0))
    return pl.pallas_call(
        _gdn_sample_kernel,
        grid=(n // tb,),
        in_specs=[tok(GDN_V), tok(GDN_V), tok(GDN_V), tok(LANES), st],
        out_specs=[tok(GDN_V), st],
        out_shape=[jax.ShapeDtypeStruct((n, GDN_V), F32), jax.ShapeDtypeStruct(state.shape, F32)],
        compiler_params=_cparams(("arbitrary",)),
        name="gdn_sample",
    )(q, k, v, gb, state)


def _mix_sample_kernel(x_ref, conv_ref, o_ref, z_ref, gon_ref, wout_ref, gx_ref, wq_ref, x1_ref, qx_ref):
    x1 = _mix_out(conv_ref[...], o_ref[...], z_ref[...], gon_ref[...], wout_ref, x_ref[...])
    x1_ref[...] = x1
    qx_ref[...] = jnp.dot(_rms(x1, gx_ref[...]).astype(BF16), wq_ref[...], preferred_element_type=F32)


def _mix_sample(xs, conv, o, z, wts):
    n = xs.shape[0]
    in_arrays = (xs, conv, o, z, wts["g_onorm"], wts["w_out_b"], wts["g_xattn"], wts["w_xq_b"])
    return pl.pallas_call(
        _mix_sample_kernel,
        grid=(1,),
        in_specs=[_full(a.shape) for a in in_arrays],
        out_specs=[_full((n, D_MODEL))] * 2,
        out_shape=[jax.ShapeDtypeStruct((n, D_MODEL), F32)] * 2,
        compiler_params=_cparams(("arbitrary",)),
        name="mix_sample",
    )(*in_arrays)


ATTN_STEP_TOKENS = 4


def _attn_sample_kernel(qx_ref, ck_ref, cv_ref, att_ref):
    for i in range(ATTN_STEP_TOKENS):
        parts = []
        for h in range(X_HEADS):
            sl = slice(h * X_HEAD_DIM, (h + 1) * X_HEAD_DIM)
            prod = ck_ref[i, :, h, :] * qx_ref[0, i:i + 1, sl]
            s = jnp.sum(prod, axis=-1, keepdims=True) * (X_HEAD_DIM ** -0.5)
            e = jnp.exp(s - jnp.max(s, axis=0, keepdims=True))
            p = e / jnp.sum(e, axis=0, keepdims=True)
            parts.append(jnp.sum(p * cv_ref[i, :, h, :], axis=0, keepdims=True))
        att_ref[0, i:i + 1, :] = jnp.concatenate(parts, axis=1)


def _attn_sample(qx, ck, cv):
    n = qx.shape[0]
    tb = ATTN_STEP_TOKENS
    q3 = qx.reshape(n // tb, tb, D_MODEL)
    qspec = pl.BlockSpec((1, tb, D_MODEL), lambda i: (i, 0, 0))
    cspec = pl.BlockSpec((tb, N_MEM, X_HEADS, X_HEAD_DIM), lambda i: (i, 0, 0, 0))
    out = pl.pallas_call(
        _attn_sample_kernel,
        grid=(n // tb,),
        in_specs=[qspec, cspec, cspec],
        out_specs=qspec,
        out_shape=jax.ShapeDtypeStruct(q3.shape, F32),
        compiler_params=_cparams(("arbitrary",)),
        name="attn_sample",
    )(q3, ck, cv)
    return out.reshape(n, D_MODEL)


def _route_sample_kernel(x1_ref, att_ref, wo_ref, gmoe_ref, wr_ref, br_ref, tri_ref, x2_ref, h3r_ref, route_ref,
                         rt_ref, cnt_ref):
    x2 = x1_ref[...] + jnp.dot(att_ref[...].astype(BF16), wo_ref[...], preferred_element_type=F32)
    x2_ref[...] = x2
    h3, route, counts = _route(x2, gmoe_ref[...], wr_ref, br_ref[...], jnp.zeros((1, LANES), F32),
                               tri_ref[...])
    h3r_ref[...] = h3
    route_ref[...] = route
    rt_ref[...] = route.T[0:ROUTE_ROWS, :]
    cnt_ref[...] = jnp.broadcast_to(counts, cnt_ref.shape)


def _route_sample(x1, att, wts):
    n = x1.shape[0]
    in_arrays = (x1, att, wts["w_xo_b"], wts["g_moe"], wts["w_router"], wts["b_router"],
                 _strict_lower(n))
    shapes = [(n, D_MODEL), (n, D_MODEL), (n, LANES), (ROUTE_ROWS, n), (SUBLANES, LANES)]
    return pl.pallas_call(
        _route_sample_kernel,
        grid=(1,),
        in_specs=[_full(a.shape) for a in in_arrays],
        out_specs=[_full(s) for s in shapes],
        out_shape=[jax.ShapeDtypeStruct(s, F32) for s in shapes],
        compiler_params=_cparams(("arbitrary",)),
        name="route_sample",
    )(*in_arrays)


def _part_tiles(tiles):
    while not any(tiles % d == 0 for d in (8, 7, 6, 5, 4)):
        tiles += 1
    return tiles


def _sc_chunk(rows_per_worker):
    for c in range(64, 0, -SUBLANES):
        if rows_per_worker % c == 0:
            return c
    raise ValueError(rows_per_worker)


def _sc_gather_rows(table, idx):
    n_workers = SC_CORES * SC_SUBCORES
    b = idx.shape[0]
    assert b % (n_workers * SUBLANES) == 0
    per_worker = b // n_workers
    chunk = _sc_chunk(per_worker)
    row_shape = table.shape[1:]
    mesh = plsc.VectorSubcoreMesh(core_axis_name="c", subcore_axis_name="s")

    @functools.partial(
        pl.kernel, mesh=mesh,
        out_type=jax.ShapeDtypeStruct((b,) + row_shape, table.dtype),
        scratch_types=[pltpu.VMEM((chunk,), I32), pltpu.VMEM((chunk,) + row_shape, table.dtype),
                       pltpu.SemaphoreType.DMA],
        name="sc_gather_rows",
    )
    def gather(table_hbm, idx_hbm, out_hbm, idx_v, rows_v, sem):
        worker = lax.axis_index("s") * SC_CORES + lax.axis_index("c")
        base = worker * per_worker

        @pl.loop(0, per_worker // chunk)
        def _(c):
            off = pl.multiple_of(base + c * chunk, SUBLANES)
            pltpu.sync_copy(idx_hbm.at[pl.ds(off, chunk)], idx_v)
            pltpu.async_copy(table_hbm.at[idx_v], rows_v, sem).wait()
            pltpu.sync_copy(rows_v, out_hbm.at[pl.ds(off, chunk)])

    return gather(table, idx)


def _moe_kernel(te_ref, tn_ref, nt_ref, xs_ref, wgu_hbm, bgu_ref, wdn_hbm, bdn_ref, *rest, first_tile):
    ys_ref, wgu_f, wdn_f, wgu_b, wdn_b, sems = rest[-6:]
    step = pl.program_id(0)
    i = first_tile + step
    total = nt_ref[0]

    def weight_copies(e):
        return (pltpu.make_async_copy(wgu_hbm.at[e], wgu_f, sems.at[0]),
                pltpu.make_async_copy(wdn_hbm.at[e], wdn_f, sems.at[1]))

    def start(e):
        for cp in weight_copies(e):
            cp.start()

    @pl.when(i < total)
    def _():
        expert = te_ref[i]
        prev = te_ref[jnp.maximum(i - 1, 0)]
        fresh = jnp.logical_or(step == 0, expert != prev)

        @pl.when(step == 0)
        def _():
            start(expert)

        @pl.when(fresh)
        def _():
            for cp in weight_copies(expert):
                cp.wait()
            wgu_b[...] = wgu_f[...].astype(BF16)
            wdn_b[...] = wdn_f[...].astype(BF16)
            nxt = tn_ref[i]

            @pl.when(nxt >= 0)
            def _():
                start(nxt)

        x = xs_ref[...].astype(BF16)

        def up(c):
            glu_cols = slice(c * MOE_COLS, (c + 1) * MOE_COLS)
            lin_cols = slice(D_EXPERT + c * MOE_COLS, D_EXPERT + (c + 1) * MOE_COLS)
            return (jnp.dot(x, wgu_b[:, glu_cols], preferred_element_type=F32) + bgu_ref[0, :, glu_cols],
                    jnp.dot(x, wgu_b[:, lin_cols], preferred_element_type=F32) + bgu_ref[0, :, lin_cols])

        n_chunks = D_EXPERT // MOE_COLS
        nxt = up(0)
        y = None
        for c in range(n_chunks):
            g, lin = nxt
            if c + 1 < n_chunks:
                nxt = up(c + 1)
            x_glu = jnp.minimum(g, SWIGLU_LIMIT)
            x_lin = jnp.clip(lin, -SWIGLU_LIMIT, SWIGLU_LIMIT)
            act = x_glu * jax.nn.sigmoid(SWIGLU_ALPHA * x_glu) * (x_lin + 1.0)
            part = jnp.dot(act.astype(BF16), wdn_b[c * MOE_COLS:(c + 1) * MOE_COLS, :],
                           preferred_element_type=F32)
            y = part if y is None else y + part
        ys_ref[...] = y + bdn_ref[0]

    @pl.when(i >= total)
    def _():
        ys_ref[...] = jnp.zeros(ys_ref.shape, F32)


def _moe(tile_e, tile_next, n_tiles, xs_part, ys_prev, first_tile, n_rows, w_gu, b_gu, w_dn, b_dn):
    tm = MOE_TILE

    def bias(shape):
        return pl.BlockSpec(shape, lambda i, te, tn, nt: (te[first_tile + i], 0, 0))

    hbm = pl.BlockSpec(memory_space=pl.ANY)
    in_specs = [pl.BlockSpec((tm, D_MODEL), lambda i, te, tn, nt: (i, 0)),
                hbm, bias((1, 1, 2 * D_EXPERT)), hbm, bias((1, 1, D_MODEL))]
    operands = [tile_e, tile_next, n_tiles, xs_part, w_gu, b_gu, w_dn, b_dn]
    aliases = {}
    if ys_prev is not None:
        in_specs.append(hbm)
        aliases = {len(operands): 0}
        operands.append(ys_prev)
    grid_spec = pltpu.PrefetchScalarGridSpec(
        num_scalar_prefetch=3,
        grid=(xs_part.shape[0] // tm,),
        in_specs=in_specs,
        out_specs=pl.BlockSpec((tm, D_MODEL), lambda i, te, tn, nt: (first_tile + i, 0)),
        scratch_shapes=[pltpu.VMEM((D_MODEL, 2 * D_EXPERT), F32),
                        pltpu.VMEM((D_EXPERT, D_MODEL), F32),
                        pltpu.VMEM((D_MODEL, 2 * D_EXPERT), BF16),
                        pltpu.VMEM((D_EXPERT, D_MODEL), BF16),
                        pltpu.SemaphoreType.DMA((2,))],
    )
    return pl.pallas_call(
        functools.partial(_moe_kernel, first_tile=first_tile),
        grid_spec=grid_spec,
        out_shape=jax.ShapeDtypeStruct((n_rows, D_MODEL), F32),
        input_output_aliases=aliases,
        compiler_params=_cparams(("arbitrary",), vmem=56 * 1024 * 1024),
        name="moe",
    )(*operands)


def _combine_kernel(x2_ref, route_ref, gfin_ref, yt_ref, *rest):
    y_ref = rest[-1]
    route = route_ref[...]
    acc = x2_ref[...]
    for j in range(TOP_K):
        acc = acc + route[:, TOP_K + j:TOP_K + j + 1] * yt_ref[j]
    y_ref[...] = _rms(acc, gfin_ref[...])


def _combine(x2, route, g_final, ys_tok, y_prev, tok0, n_tok, ys_block0):
    tc = min(COMBINE_TILE, n_tok)
    b0 = tok0 // tc

    def tok(width):
        return pl.BlockSpec((tc, width), lambda i: (b0 + i, 0))

    in_specs = [tok(D_MODEL), tok(LANES), pl.BlockSpec((1, D_MODEL), lambda i: (0, 0)),
                pl.BlockSpec((TOP_K, tc, D_MODEL), lambda i: (0, ys_block0 + i, 0))]
    operands = [x2, route, g_final, ys_tok]
    aliases = {}
    if y_prev is not None:
        in_specs.append(pl.BlockSpec(memory_space=pl.ANY))
        aliases = {len(operands): 0}
        operands.append(y_prev)
    return pl.pallas_call(
        _combine_kernel,
        grid=(n_tok // tc,),
        in_specs=in_specs,
        out_specs=tok(D_MODEL),
        out_shape=jax.ShapeDtypeStruct(x2.shape, F32),
        input_output_aliases=aliases,
        compiler_params=_cparams(("arbitrary",)),
        name="combine",
    )(*operands)


def _routing_tables(idx_t, rank_t, counts, n_rows, tiles_part):
    tm = MOE_TILE
    n_tok = idx_t.shape[1]
    n_assign = TOP_K * n_tok
    tok_mask = (1 << TOKEN_BITS) - 1
    tiles_e = (counts + tm - 1) // tm
    tile_end = jnp.cumsum(tiles_e)
    row_start = (tile_end - tiles_e) * tm
    total = tile_end[-1]
    expert_ids = jnp.arange(N_EXPERTS, dtype=I32)

    def lookup(table, e):
        return jnp.sum(jnp.where(e[..., None] == expert_ids, table, 0), axis=-1)

    pos = lookup(row_start, idx_t) + rank_t
    keys_real = (idx_t * (1 << TOKEN_BITS) + jnp.arange(n_tok, dtype=I32)[None, :]).reshape(-1)
    k = jnp.arange(n_rows - n_assign, dtype=I32)
    pad_e, pad_s = k // tm, k % tm
    pad_needed = lookup(tiles_e * tm - counts, pad_e)
    pad_key_e = jnp.where((pad_e < N_EXPERTS) & (pad_s < pad_needed), pad_e, N_EXPERTS)
    keys = lax.sort(jnp.concatenate([keys_real, pad_key_e * (1 << TOKEN_BITS) + tok_mask]),
                    is_stable=False)
    src_tok = jnp.where((keys & tok_mask) == tok_mask, jnp.arange(n_rows, dtype=I32) % n_tok,
                        keys & tok_mask)
    tid = jnp.minimum(jnp.arange(n_rows // tm, dtype=I32), total - 1)
    tile_e = jnp.minimum(jnp.sum((tid[:, None] >= tile_end[None, :]).astype(I32), axis=1), N_EXPERTS - 1)
    later = (expert_ids[None, :] > expert_ids[:, None]) & (tiles_e[None, :] > 0)
    next_e = jnp.min(jnp.where(later, expert_ids[None, :], N_EXPERTS), axis=1)
    nxt = lookup(next_e, tile_e)
    nxt_first_tile = lookup(tile_end - tiles_e, jnp.minimum(nxt, N_EXPERTS - 1))
    part_end = (tid // tiles_part + 1) * tiles_part
    tile_next = jnp.where((nxt < N_EXPERTS) & (nxt_first_tile < part_end), nxt, -1)
    return (tile_e.astype(I32), tile_next.astype(I32), total.reshape(1).astype(I32), src_tok.astype(I32),
            pos.reshape(-1).astype(I32))


def _pad_rows(a, rows):
    return jnp.concatenate([a, jnp.zeros((rows - a.shape[0],) + a.shape[1:], a.dtype)], axis=0)


def _pad_lanes(a, lanes=LANES):
    return jnp.concatenate([a, jnp.zeros(a.shape[:-1] + (lanes - a.shape[-1],), a.dtype)], axis=-1)


def kernel(x_prompt, mem_prompt, x_sample, state_conformer_conv, state_gdn_conv, state_gdn, cache_mem_k,
           cache_mem_v, w_in, b_glu, w_dw, b_dw, ln_g, ln_b, w_sc, a_log, dt_bias, g_onorm, w_out, g_mix,
           g_xattn, g_mem, w_xq, w_mk, w_mv, w_xo, g_moe, w_router, b_router, w_gu, b_gu, w_dn, b_dn,
           g_final):
    assert w_in.shape[0] == 1, "single-layer configuration"
    batch, seq, _ = x_prompt.shape
    n_s = x_sample.shape[0]
    n_p = batch * seq
    n_all = n_p + n_s
    assert seq % TOKEN_TILE == 0 and n_p % n_s == 0 and n_all < (1 << TOKEN_BITS) - 1
    assert (n_all * TOP_K) % (SC_CORES * SC_SUBCORES * SUBLANES) == 0

    wts = {
        "g_mix": g_mix[0][None], "g_xattn": g_xattn[0][None], "g_moe": g_moe[0][None],
        "g_onorm": g_onorm[0][None],
        "w_in_b": w_in[0][:, :OFF_A].astype(BF16),
        "w_ab_b": _pad_lanes(w_in[0][:, OFF_A:]).astype(BF16),
        "b_glu": b_glu[0][None],
        "w_dw": _pad_rows(w_dw[0], 32), "b_dw": b_dw[0][None], "ln_g": ln_g[0][None], "ln_b": ln_b[0][None],
        "w_sc": _pad_rows(w_sc[0], 8),
        "gdn_cst": _pad_rows(_pad_lanes(jnp.stack([a_log[0], dt_bias[0]])), 8),
        "w_out_b": w_out[0].astype(BF16), "w_xq_b": w_xq[0].astype(BF16), "w_xo_b": w_xo[0].astype(BF16),
        "w_router": _pad_lanes(w_router[0]), "b_router": _pad_lanes(b_router[0][None]),
    }

    mk, mv, mk_b, mv_b = _mem_kv(mem_prompt.reshape(batch * N_MEM, D_MODEL), g_mem[0][None],
                                 w_mk[0].astype(BF16), w_mv[0].astype(BF16))
    xp = x_prompt.reshape(n_p, D_MODEL)
    conv_p, qkv_p, z_p, gb_p, cstate_p, sstate_p = _pre_prompt(xp, batch, seq, wts)
    o_p, gstate_p = _gdn_prompt(qkv_p, gb_p, wts["w_sc"], batch, seq)

    xs = x_sample.reshape(n_s, D_MODEL)
    conv_s, q_s, k_s, v_s, z_s, gb_s, cstate_s, sstate_s = _pre_sample(
        xs, state_conformer_conv[0], state_gdn_conv[0], wts)
    o_s, gstate_s = _gdn_sample(q_s, k_s, v_s, gb_s, state_gdn[0])
    x1_s, qx_s = _mix_sample(xs, conv_s, o_s, z_s, wts)
    att_s = _attn_sample(qx_s, cache_mem_k[0], cache_mem_v[0])
    x2_s, h3_s, route_s, rt_s, counts_s = _route_sample(x1_s, att_s, wts)

    x2_p, h3r, route_p, rt_p, counts_p = _post_prompt(xp, conv_p, o_p, z_p, mk_b, mv_b, h3_s, batch, seq,
                                                      wts)

    counts_p = counts_p[0, :N_EXPERTS].astype(I32)
    counts_s = counts_s[0, :N_EXPERTS].astype(I32)
    idx_s = rt_s[0:TOP_K].astype(I32)
    rank_s = rt_s[2 * TOP_K:3 * TOP_K].astype(I32) + jnp.sum(
        jnp.where(idx_s[..., None] == jnp.arange(N_EXPERTS, dtype=I32), counts_p, 0), axis=-1)
    idx_t = jnp.concatenate([rt_p[0:TOP_K].astype(I32), idx_s], axis=1)
    rank_t = jnp.concatenate([rt_p[2 * TOP_K:3 * TOP_K].astype(I32), rank_s], axis=1)
    min_tiles = -(-(n_all * TOP_K + N_EXPERTS * (MOE_TILE - 1)) // MOE_TILE)
    tiles_part = _part_tiles(-(-min_tiles // MOE_PARTS))
    rows_part = tiles_part * MOE_TILE
    n_rows = rows_part * MOE_PARTS
    tile_e, tile_next, n_tiles, src_tok, pos = _routing_tables(idx_t, rank_t, counts_p + counts_s, n_rows,
                                                               tiles_part)
    pos_t = pos.reshape(TOP_K, n_all)
    xs_parts = [_sc_gather_rows(h3r, src_tok[k * rows_part:(k + 1) * rows_part]) for k in range(MOE_PARTS)]
    ys = None
    for k in range(MOE_PARTS):
        ys = _moe(tile_e, tile_next, n_tiles, xs_parts[k], ys, k * tiles_part, n_rows, w_gu[0],
                  b_gu[0][:, None, :], w_dn[0], b_dn[0][:, None, :])
    gfin = g_final[None]
    tok_part = n_p // MOE_PARTS
    assert tok_part % COMBINE_TILE == 0
    y_p = None
    for k in range(MOE_PARTS):
        last = k == MOE_PARTS - 1
        pos_k = pos_t[:, k * tok_part:(k + 1) * tok_part]
        if last:
            pos_k = jnp.concatenate([pos_k, pos_t[:, n_p:]], axis=1)
        ys_tok = _sc_gather_rows(ys, pos_k.reshape(-1)).reshape(TOP_K, pos_k.shape[1], D_MODEL)
        y_p = _combine(x2_p, route_p, gfin, ys_tok, y_p, k * tok_part, tok_part, 0)
        if last:
            y_s = _combine(x2_s, route_s, gfin, ys_tok, None, 0, n_s, tok_part // min(COMBINE_TILE, n_s))

    return (y_p.reshape(batch, seq, D_MODEL), y_s.reshape(n_s, 1, D_MODEL),
            cstate_p[None], sstate_p[None], gstate_p[None],
            mk.reshape(1, batch, N_MEM, X_HEADS, X_HEAD_DIM), mv.reshape(1, batch, N_MEM, X_HEADS, X_HEAD_DIM),
            cstate_s[None], sstate_s[None], gstate_s[None])
```

```python
import functools

import jax
import jax.numpy as jnp
from jax import lax
from jax.experimental import pallas as pl
from jax.experimental.pallas import tpu as pltpu
from jax.experimental.pallas import tpu_sc as plsc

F32, BF16, I32 = jnp.float32, jnp.bfloat16, jnp.int32

D_MODEL = 1024
CONV_CH = 512
CONV_WIDTH = 31
GDN_HEADS = 4
GDN_DK = 128
GDN_V = 512
QKV_CH = 1536
SHORT_CONV = 4
N_MEM = 256
X_HEADS = 4
X_HEAD_DIM = 256
N_EXPERTS = 32
TOP_K = 4
D_EXPERT = 1024
SWIGLU_LIMIT = 7.0
SWIGLU_ALPHA = 1.702
NORM_EPS = 1e-6
OFF_QKV = 2 * CONV_CH
OFF_Z = OFF_QKV + QKV_CH
OFF_A = OFF_Z + GDN_V

LANES = 128
SUBLANES = 8
GDN_BLOCK = 128
TOKEN_TILE = 256
MOE_TILE = 384
MOE_COLS = 256
MOE_PART_WEIGHTS = (3, 6, 8, 9)
TOKEN_PART_WEIGHTS = (1, 2, 2, 3)
COMBINE_TILE = 128
TOKEN_BITS = 15
ROUTE_ROWS = 16
SC_CORES = 2
SC_SUBCORES = 16
VMEM_LIMIT = 48 * 1024 * 1024


def _cparams(sem, vmem=VMEM_LIMIT):
    return pltpu.CompilerParams(dimension_semantics=sem, vmem_limit_bytes=vmem)


def _mm(a, b):
    return jnp.dot(a.astype(BF16), b.astype(BF16), preferred_element_type=F32)


def _mm_nt(a, b):
    return lax.dot_general(a.astype(BF16), b.astype(BF16), (((1,), (1,)), ((), ())),
                           preferred_element_type=F32)


def _mm_tn(a, b):
    return lax.dot_general(a.astype(BF16), b.astype(BF16), (((0,), (0,)), ((), ())),
                           preferred_element_type=F32)


def _rms(x, g):
    return x * lax.rsqrt(jnp.mean(x * x, axis=-1, keepdims=True) + NORM_EPS) * g


def _silu(x):
    return x * jax.nn.sigmoid(x)


def _full(shape):
    return pl.BlockSpec(shape, lambda *_: (0,) * len(shape))


def _strict_lower(n):
    return jnp.tril(jnp.ones((n, n), BF16), k=-1)


def _project(x, gmix, w_ref, wab_ref, bglu):
    h = _rms(x, gmix).astype(BF16)
    u_glu = jnp.dot(h, w_ref[:, 0:OFF_QKV], preferred_element_type=F32) + bglu
    glu = u_glu[:, :CONV_CH] * jax.nn.sigmoid(u_glu[:, CONV_CH:])
    qkv_pre = jnp.dot(h, w_ref[:, OFF_QKV:OFF_Z], preferred_element_type=F32)
    z = jnp.dot(h, w_ref[:, OFF_Z:OFF_A], preferred_element_type=F32)
    uab = jnp.dot(h, wab_ref[...], preferred_element_type=F32)
    return glu, qkv_pre, z, uab


def _gate_beta(uab, cst):
    lane = lax.broadcasted_iota(I32, uab.shape, 1)
    g = -jnp.exp(cst[0:1, :]) * jax.nn.softplus(uab + cst[1:2, :])
    return jnp.where(lane < GDN_HEADS, g, jax.nn.sigmoid(uab))


def _conv_post(c, b_dw, ln_g, ln_b):
    c = c + b_dw
    mu = jnp.mean(c, axis=-1, keepdims=True)
    d = c - mu
    var = jnp.mean(d * d, axis=-1, keepdims=True)
    return _silu(d * lax.rsqrt(var + NORM_EPS) * ln_g + ln_b)


def _qkv_post(cs):
    a = _silu(cs)
    parts = []
    for h in range(2 * GDN_HEADS):
        seg = a[:, h * GDN_DK:(h + 1) * GDN_DK]
        n = seg * lax.rsqrt(jnp.sum(seg * seg, axis=-1, keepdims=True) + NORM_EPS)
        if h < GDN_HEADS:
            n = n * (GDN_DK ** -0.5)
        parts.append(n)
    q = jnp.concatenate(parts[:GDN_HEADS], axis=1)
    k = jnp.concatenate(parts[GDN_HEADS:], axis=1)
    return q, k, a[:, 2 * GDN_HEADS * GDN_DK:]


def _mix_out(conv_b, o, z, gon, wout_ref, x):
    parts = []
    for h in range(GDN_HEADS):
        oh = o[:, h * 128:(h + 1) * 128]
        parts.append(oh * lax.rsqrt(jnp.mean(oh * oh, axis=-1, keepdims=True) + NORM_EPS) * gon)
    on = jnp.concatenate(parts, axis=1) * _silu(z)
    mixed = (jnp.dot(conv_b, wout_ref[0:CONV_CH, :], preferred_element_type=F32)
             + jnp.dot(on.astype(BF16), wout_ref[CONV_CH:, :], preferred_element_type=F32))
    return x + mixed


def _router_logits(x2, gmoe, wr_ref, br):
    h3 = _rms(x2, gmoe)
    h_hi = h3.astype(BF16)
    r1 = h3 - h_hi.astype(F32)
    h_mid = r1.astype(BF16)
    h_lo = (r1 - h_mid.astype(F32)).astype(BF16)
    w = wr_ref[...]
    w_hi = w.astype(BF16)
    w_lo = (w - w_hi.astype(F32)).astype(BF16)
    logits = (jnp.dot(h_hi, w_hi, preferred_element_type=F32)
              + jnp.dot(h_hi, w_lo, preferred_element_type=F32)
              + jnp.dot(h_mid, w_hi, preferred_element_type=F32)
              + jnp.dot(h_lo, w_hi, preferred_element_type=F32)) + br
    return h3, logits


def _route_finish(logits, carry, before):
    m = logits.shape[0]
    neg = jnp.float32(-jnp.inf)
    n_groups = 4 if m % 32 == 0 else 1
    rows = m // n_groups
    lanes_g = lax.broadcasted_iota(I32, (rows, LANES), 1).astype(F32)
    works = [jnp.where(lanes_g < N_EXPERTS, logits[g * rows:(g + 1) * rows], neg) for g in range(n_groups)]
    vals, idxs = [], []
    for _ in range(TOP_K):
        mxs = [jnp.max(w, axis=-1, keepdims=True) for w in works]
        ixs = [jnp.min(jnp.where(w == mx, lanes_g, float(LANES)), axis=-1, keepdims=True)
               for w, mx in zip(works, mxs)]
        works = [jnp.where(lanes_g == ix, neg, w) for w, ix in zip(works, ixs)]
        vals.append(mxs)
        idxs.append(ixs)
    sels, gates = [], []
    for g in range(n_groups):
        es = [jnp.exp(vals[r][g] - vals[0][g]) for r in range(TOP_K)]
        den = es[0] + es[1] + es[2] + es[3]
        gates.append([e / den for e in es])
        sel_g = jnp.zeros((rows, LANES), F32)
        for r in range(TOP_K):
            sel_g = sel_g + jnp.where(lanes_g == idxs[r][g], 1.0, 0.0)
        sels.append(sel_g)
    sel = jnp.concatenate(sels, axis=0)
    rank_full = jnp.dot(before, sel.astype(BF16), preferred_element_type=F32) + carry
    routes = []
    for g in range(n_groups):
        rank_g = rank_full[g * rows:(g + 1) * rows]
        route_g = jnp.zeros((rows, LANES), F32)
        for r in range(TOP_K):
            rk = jnp.sum(jnp.where(lanes_g == idxs[r][g], rank_g, 0.0), axis=-1, keepdims=True)
            route_g = (route_g + jnp.where(lanes_g == r, idxs[r][g], 0.0)
                       + jnp.where(lanes_g == TOP_K + r, gates[g][r], 0.0)
                       + jnp.where(lanes_g == 2 * TOP_K + r, rk, 0.0))
        routes.append(route_g)
    new_carry = carry + jnp.sum(sel, axis=0, keepdims=True)
    return jnp.concatenate(routes, axis=0), new_carry


def _mem_kv_kernel(mem_ref, g_ref, wk_ref, wv_ref, mk_ref, mv_ref, mkb_ref, mvb_ref):
    m = _rms(mem_ref[...], g_ref[...]).astype(BF16)
    mk = jnp.dot(m, wk_ref[...], preferred_element_type=F32)
    mv = jnp.dot(m, wv_ref[...], preferred_element_type=F32)
    for h in range(X_HEADS):
        sl = slice(h * X_HEAD_DIM, (h + 1) * X_HEAD_DIM)
        mk_ref[0, :, h, :] = mk[:, sl]
        mv_ref[0, :, h, :] = mv[:, sl]
    mkb_ref[...] = mk.astype(BF16)
    mvb_ref[...] = mv.astype(BF16)


def _mem_kv(mem, g_mem, wk_b, wv_b):
    rows = mem.shape[0]
    tm = N_MEM
    row_spec = pl.BlockSpec((tm, D_MODEL), lambda i: (i, 0))
    head_spec = pl.BlockSpec((1, tm, X_HEADS, X_HEAD_DIM), lambda i: (i, 0, 0, 0))
    head_shape = jax.ShapeDtypeStruct((rows // tm, tm, X_HEADS, X_HEAD_DIM), F32)
    return pl.pallas_call(
        _mem_kv_kernel,
        grid=(rows // tm,),
        in_specs=[row_spec, _full((1, D_MODEL)), _full((D_MODEL, D_MODEL)), _full((D_MODEL, D_MODEL))],
        out_specs=[head_spec, head_spec, row_spec, row_spec],
        out_shape=[head_shape, head_shape] + [jax.ShapeDtypeStruct((rows, D_MODEL), BF16)] * 2,
        compiler_params=_cparams(("arbitrary",)),
        name="mem_kv",
    )(mem, g_mem, wk_b, wv_b)


CONV_HALO = 32
SC_HALO = 8


def _pre_prompt_kernel(x_ref, gmix_ref, w_ref, wab_ref, bglu_ref, wdw_ref, bdw_ref, lng_ref, lnb_ref,
                       cst_ref, conv_ref, qkv_ref, z_ref, gb_ref, cstate_ref, sstate_ref, cbuf, *, tm):
    j = pl.program_id(1)

    @pl.when(j == 0)
    def _():
        cbuf[0:CONV_HALO, :] = jnp.zeros((CONV_HALO, CONV_CH), F32)

    glu, qkv_pre, z, uab = _project(x_ref[...], gmix_ref[...], w_ref, wab_ref, bglu_ref[...])
    cbuf[CONV_HALO:CONV_HALO + tm, :] = glu
    qkv_ref[...] = qkv_pre
    z_ref[...] = z
    gb_ref[...] = _gate_beta(uab, cst_ref[...])

    base = CONV_HALO - (CONV_WIDTH - 1)
    rows = CONV_HALO + tm
    accs = []
    for c in range(CONV_CH // LANES):
        lanes = slice(c * LANES, (c + 1) * LANES)
        block = cbuf[:, lanes]
        acc = None
        for r in range(SUBLANES):
            shifted = block if r == 0 else pltpu.roll(block, rows - r, 0)
            for a in range(base, base + CONV_WIDTH):
                if a % SUBLANES == r:
                    t = a - base
                    term = wdw_ref[t:t + 1, lanes] * shifted[a - r:a - r + tm, :]
                    acc = term if acc is None else acc + term
        accs.append(acc)
    acc = jnp.concatenate(accs, axis=1)
    conv_ref[...] = _conv_post(acc, bdw_ref[...], lng_ref[...], lnb_ref[...]).astype(BF16)

    @pl.when(j == pl.num_programs(1) - 1)
    def _():
        cstate_ref[0] = cbuf[pl.ds(CONV_HALO + tm - (CONV_WIDTH - 1), CONV_WIDTH - 1), :]
        sstate_ref[0] = qkv_pre[tm - (SHORT_CONV - 1):, :]

    cbuf[0:CONV_HALO, :] = cbuf[tm:tm + CONV_HALO, :]


def _pre_prompt(x2d, batch, seq, wts):
    tm = TOKEN_TILE
    nj = seq // tm
    rows = batch * seq

    def tok(width):
        return pl.BlockSpec((tm, width), lambda b, j: (b * nj + j, 0))

    in_specs = [tok(D_MODEL), _full((1, D_MODEL)), _full((D_MODEL, OFF_A)), _full((D_MODEL, LANES)),
                _full((1, OFF_QKV)), _full((32, CONV_CH)), _full((1, CONV_CH)), _full((1, CONV_CH)),
                _full((1, CONV_CH)), _full((8, LANES))]
    out_specs = [tok(CONV_CH), tok(QKV_CH), tok(GDN_V), tok(LANES),
                 pl.BlockSpec((1, CONV_WIDTH - 1, CONV_CH), lambda b, j: (b, 0, 0)),
                 pl.BlockSpec((1, SHORT_CONV - 1, QKV_CH), lambda b, j: (b, 0, 0))]
    out_shape = [jax.ShapeDtypeStruct((rows, CONV_CH), BF16),
                 jax.ShapeDtypeStruct((rows, QKV_CH), F32),
                 jax.ShapeDtypeStruct((rows, GDN_V), F32),
                 jax.ShapeDtypeStruct((rows, LANES), F32),
                 jax.ShapeDtypeStruct((batch, CONV_WIDTH - 1, CONV_CH), F32),
                 jax.ShapeDtypeStruct((batch, SHORT_CONV - 1, QKV_CH), F32)]
    return pl.pallas_call(
        functools.partial(_pre_prompt_kernel, tm=tm),
        grid=(batch, nj),
        in_specs=in_specs,
        out_specs=out_specs,
        out_shape=out_shape,
        scratch_shapes=[pltpu.VMEM((CONV_HALO + tm, CONV_CH), F32)],
        compiler_params=_cparams(("arbitrary", "arbitrary")),
        name="pre_prompt",
    )(x2d, wts["g_mix"], wts["w_in_b"], wts["w_ab_b"], wts["b_glu"], wts["w_dw"], wts["b_dw"],
      wts["ln_g"], wts["ln_b"], wts["gdn_cst"])


GDN_SEQS = 4


def _gdn_prompt_kernel(qkv_ref, gb_ref, wsc_ref, o_ref, sfin_ref, s_scr, sbuf):
    c = pl.program_id(1)
    n = GDN_BLOCK
    seqs = range(GDN_SEQS)

    @pl.when(c == 0)
    def _():
        s_scr[...] = jnp.zeros(s_scr.shape, F32)
        sbuf[:, 0:SC_HALO, :] = jnp.zeros((GDN_SEQS, SC_HALO, QKV_CH), F32)

    sbase = SC_HALO - (SHORT_CONV - 1)
    qkvs = []
    for b in seqs:
        sbuf[b, SC_HALO:SC_HALO + n, :] = qkv_ref[b]
        cs = wsc_ref[0:1, :] * sbuf[b, pl.ds(sbase, n), :]
        for t in range(1, SHORT_CONV):
            cs = cs + wsc_ref[t:t + 1, :] * sbuf[b, pl.ds(sbase + t, n), :]
        qkvs.append(_qkv_post(cs))
        sbuf[b, 0:SC_HALO, :] = sbuf[b, n:n + SC_HALO, :]

    row = lax.broadcasted_iota(I32, (n, n), 0)
    col = lax.broadcasted_iota(I32, (n, n), 1)
    causal = row >= col
    strict = row > col
    tri = jnp.where(causal, 1.0, 0.0).astype(BF16)
    eye = jnp.where(row == col, 1.0, 0.0)
    level_masks = []
    b = 1
    while b < n:
        same_pair = ((row ^ col) & ~(2 * b - 1)) == 0
        level_masks.append(same_pair & ((row & b) != 0) & ((col & b) == 0))
        b *= 2
    gbs, gcums, gcum_ts, egcs = [], [], [], []
    for b in seqs:
        gb = gb_ref[b]
        g1 = gb.astype(BF16)
        r1 = gb - g1.astype(F32)
        g2 = r1.astype(BF16)
        g3 = (r1 - g2.astype(F32)).astype(BF16)
        gcum = (jnp.dot(tri, g1, preferred_element_type=F32) + jnp.dot(tri, g2, preferred_element_type=F32)
                + jnp.dot(tri, g3, preferred_element_type=F32))
        gbs.append(gb)
        gcums.append(gcum)
        gcum_ts.append(gcum.T)
        egcs.append(jnp.exp(gcum))
    units = [(b, h) for b in seqs for h in range(GDN_HEADS)]
    idx = range(len(units))
    sls = [slice(h * GDN_DK, (h + 1) * GDN_DK) for h in range(GDN_HEADS)]
    qs = [qkvs[b][0][:, sls[h]] for b, h in units]
    ks = [qkvs[b][1][:, sls[h]] for b, h in units]
    vs = [qkvs[b][2][:, sls[h]] for b, h in units]
    ss = [s_scr[b, h] for b, h in units]
    gcols = [gcums[b][:, h:h + 1] for b, h in units]
    ecols = [egcs[b][:, h:h + 1] for b, h in units]
    betas = [gbs[b][:, GDN_HEADS + h:GDN_HEADS + h + 1] for b, h in units]
    glasts = [gcums[b][n - 1:n, h:h + 1] for b, h in units]
    decays = [jnp.where(causal, jnp.exp(jnp.where(causal, gcols[u] - gcum_ts[b][h:h + 1, :], 0.0)), 0.0)
              for u, (b, h) in enumerate(units)]
    kbs = [ks[u] * betas[u] for u in idx]
    lowers = [jnp.where(strict, _mm_nt(kbs[u], ks[u]) * decays[u], 0.0) for u in idx]
    intras = [jnp.where(causal, _mm_nt(qs[u], ks[u]) * decays[u], 0.0) for u in idx]
    xs = [eye - jnp.where(level_masks[0], lowers[u], 0.0) for u in idx]
    for mask in level_masks[1:]:
        ts = [_mm(xs[u], jnp.where(mask, lowers[u], 0.0)) for u in idx]
        xs = [xs[u] - _mm(ts[u], xs[u]) for u in idx]
    us = [_mm(xs[u], vs[u] * betas[u]) for u in idx]
    ws = [_mm(xs[u], kbs[u] * ecols[u]) for u in idx]
    v_news = [us[u] - _mm(ws[u], ss[u]) for u in idx]
    os_ = [_mm(qs[u] * ecols[u], ss[u]) + _mm(intras[u], v_news[u]) for u in idx]
    s_news = [ss[u] * jnp.exp(glasts[u]) + _mm_tn(ks[u] * jnp.exp(glasts[u] - gcols[u]), v_news[u])
              for u in idx]
    for u, (b, h) in enumerate(units):
        o_ref[b, :, sls[h]] = os_[u]
        s_scr[b, h] = s_news[u]

    @pl.when(c == pl.num_programs(1) - 1)
    def _():
        sfin_ref[...] = s_scr[...]


def _gdn_prompt(qkv, gb, w_sc, batch, seq):
    n = GDN_BLOCK
    nc = seq // n
    g = GDN_SEQS
    assert batch % g == 0

    def tok(width):
        return pl.BlockSpec((g, n, width), lambda b, c: (b, c, 0))

    state_shape = (g, GDN_HEADS, GDN_DK, GDN_DK)
    o, s_fin = pl.pallas_call(
        _gdn_prompt_kernel,
        grid=(batch // g, nc),
        in_specs=[tok(QKV_CH), tok(LANES), _full(w_sc.shape)],
        out_specs=[tok(GDN_V), pl.BlockSpec(state_shape, lambda b, c: (b, 0, 0, 0))],
        out_shape=[jax.ShapeDtypeStruct((batch, seq, GDN_V), F32),
                   jax.ShapeDtypeStruct((batch, GDN_HEADS, GDN_DK, GDN_DK), F32)],
        scratch_shapes=[pltpu.VMEM(state_shape, F32), pltpu.VMEM((g, SC_HALO + n, QKV_CH), F32)],
        compiler_params=_cparams(("arbitrary", "arbitrary")),
        name="gdn_prompt",
    )(qkv.reshape(batch, seq, QKV_CH), gb.reshape(batch, seq, LANES), w_sc)
    return o.reshape(batch * seq, GDN_V), s_fin


def _post_prompt_kernel(x_ref, conv_ref, o_ref, z_ref, gon_ref, wout_ref, gx_ref, wq_ref, mk_ref, mv_ref,
                        wo_ref, gmoe_ref, wr_ref, br_ref, tri_ref, h3s_ref, x2_ref, h3r_ref, route_ref, rt_ref,
                        cnt_ref, carry, logit_buf, *, n_steps):
    step = pl.program_id(0)

    @pl.when(step == 0)
    def _():
        carry[...] = jnp.zeros(carry.shape, F32)
        logit_buf[...] = jnp.zeros(logit_buf.shape, F32)

    x1 = _mix_out(conv_ref[...], o_ref[...], z_ref[...], gon_ref[...], wout_ref, x_ref[...])
    qx = jnp.dot(_rms(x1, gx_ref[...]).astype(BF16), wq_ref[...], preferred_element_type=F32)

    route, new_carry = _route_finish(logit_buf[...], carry[0:1, :], tri_ref[...])
    route_ref[...] = route
    rt_ref[...] = route.T[0:ROUTE_ROWS, :]
    kept = jnp.where(step >= 1, new_carry, carry[0:1, :])
    carry[0:1, :] = kept
    cnt_ref[...] = jnp.broadcast_to(kept, cnt_ref.shape)

    sls = [slice(h * X_HEAD_DIM, (h + 1) * X_HEAD_DIM) for h in range(X_HEADS)]
    qb = qx.astype(BF16)
    ss = [lax.dot_general(qb[:, sl], mk_ref[:, sl], (((1,), (1,)), ((), ())),
                          preferred_element_type=F32) * (X_HEAD_DIM ** -0.5) for sl in sls]
    es = [jnp.exp(s - jnp.max(s, axis=-1, keepdims=True)) for s in ss]
    ps = [(e / jnp.sum(e, axis=-1, keepdims=True)).astype(BF16) for e in es]
    att = jnp.concatenate(
        [jnp.dot(p, mv_ref[:, sl], preferred_element_type=F32) for p, sl in zip(ps, sls)], axis=1)
    x2 = x1 + jnp.dot(att.astype(BF16), wo_ref[...], preferred_element_type=F32)
    h3, logits = _router_logits(x2, gmoe_ref[...], wr_ref, br_ref[...])
    logit_buf[...] = logits

    @pl.when(step < n_steps)
    def _():
        x2_ref[...] = x2
        h3r_ref[...] = h3

    @pl.when(step == n_steps)
    def _():
        h3r_ref[0:h3s_ref.shape[0], :] = h3s_ref[...]


def _post_prompt(x2d, conv, o, z, mk_b, mv_b, h3_sample, batch, seq, wts):
    tm = TOKEN_TILE
    nj = seq // tm
    rows = batch * seq
    n_steps = batch * nj
    n_s = h3_sample.shape[0]
    assert n_s <= tm

    def tok(width):
        return pl.BlockSpec((tm, width), lambda s: (jnp.minimum(s, n_steps - 1), 0))

    mem_spec = pl.BlockSpec((N_MEM, D_MODEL), lambda s: (jnp.minimum(s, n_steps - 1) // nj, 0))
    sq = _full((D_MODEL, D_MODEL))
    in_specs = [tok(D_MODEL), tok(CONV_CH), tok(GDN_V), tok(GDN_V), _full((1, GDN_DK)), sq,
                _full((1, D_MODEL)), sq, mem_spec, mem_spec, sq, _full((1, D_MODEL)),
                _full((D_MODEL, LANES)), _full((1, LANES)), _full((tm, tm)), _full(h3_sample.shape)]
    out_specs = [tok(D_MODEL),
                 pl.BlockSpec((tm, D_MODEL), lambda s: (s, 0)),
                 pl.BlockSpec((tm, LANES), lambda s: (jnp.maximum(s - 1, 0), 0)),
                 pl.BlockSpec((ROUTE_ROWS, tm), lambda s: (0, jnp.maximum(s - 1, 0))),
                 _full((SUBLANES, LANES))]
    out_shape = [jax.ShapeDtypeStruct((rows, D_MODEL), F32),
                 jax.ShapeDtypeStruct((rows + n_s, D_MODEL), F32),
                 jax.ShapeDtypeStruct((rows, LANES), F32),
                 jax.ShapeDtypeStruct((ROUTE_ROWS, rows), F32),
                 jax.ShapeDtypeStruct((SUBLANES, LANES), F32)]
    return pl.pallas_call(
        functools.partial(_post_prompt_kernel, n_steps=n_steps),
        grid=(n_steps + 1,),
        in_specs=in_specs,
        out_specs=out_specs,
        out_shape=out_shape,
        scratch_shapes=[pltpu.VMEM((SUBLANES, LANES), F32), pltpu.VMEM((tm, LANES), F32)],
        compiler_params=_cparams(("arbitrary",)),
        name="post_prompt",
    )(x2d, conv, o, z, wts["g_onorm"], wts["w_out_b"], wts["g_xattn"], wts["w_xq_b"], mk_b, mv_b,
      wts["w_xo_b"], wts["g_moe"], wts["w_router"], wts["b_router"], _strict_lower(tm), h3_sample)


def _pre_sample_kernel(x_ref, gmix_ref, w_ref, wab_ref, bglu_ref, wdw_ref, bdw_ref, lng_ref, lnb_ref,
                       wsc_ref, cst_ref, chist_ref, shist_ref, conv_ref, q_ref, k_ref, v_ref, z_ref,
                       gb_ref, cnew_ref, snew_ref):
    glu, qkv_pre, z, uab = _project(x_ref[...], gmix_ref[...], w_ref, wab_ref, bglu_ref[...])
    z_ref[...] = z
    gb_ref[...] = _gate_beta(uab, cst_ref[...])
    kw = CONV_WIDTH
    acc = wdw_ref[kw - 1:kw, :] * glu
    for t in range(kw - 1):
        row = chist_ref[:, t, :]
        acc = acc + wdw_ref[t:t + 1, :] * row
        if t >= 1:
            cnew_ref[:, t - 1, :] = row
    cnew_ref[:, kw - 2, :] = glu
    conv_ref[...] = _conv_post(acc, bdw_ref[...], lng_ref[...], lnb_ref[...]).astype(BF16)
    ks = SHORT_CONV
    cs = wsc_ref[ks - 1:ks, :] * qkv_pre
    for t in range(ks - 1):
        row = shist_ref[:, t, :]
        cs = cs + wsc_ref[t:t + 1, :] * row
        if t >= 1:
            snew_ref[:, t - 1, :] = row
    snew_ref[:, ks - 2, :] = qkv_pre
    q, k, v = _qkv_post(cs)
    q_ref[...] = q
    k_ref[...] = k
    v_ref[...] = v


PRE_SAMPLE_TOKENS = 32


def _pre_sample(xs, chist, shist, wts):
    n = xs.shape[0]
    tb = min(PRE_SAMPLE_TOKENS, n)

    def tok(width):
        return pl.BlockSpec((tb, width), lambda i: (i, 0))

    def hist(a):
        return pl.BlockSpec((tb,) + a.shape[1:], lambda i: (i, 0, 0))

    consts = (wts["g_mix"], wts["w_in_b"], wts["w_ab_b"], wts["b_glu"], wts["w_dw"], wts["b_dw"],
              wts["ln_g"], wts["ln_b"], wts["w_sc"], wts["gdn_cst"])
    return pl.pallas_call(
        _pre_sample_kernel,
        grid=(n // tb,),
        in_specs=[tok(D_MODEL)] + [_full(a.shape) for a in consts] + [hist(chist), hist(shist)],
        out_specs=[tok(CONV_CH), tok(GDN_V), tok(GDN_V), tok(GDN_V), tok(GDN_V), tok(LANES),
                   hist(chist), hist(shist)],
        out_shape=[jax.ShapeDtypeStruct((n, CONV_CH), BF16)]
        + [jax.ShapeDtypeStruct((n, GDN_V), F32)] * 4
        + [jax.ShapeDtypeStruct((n, LANES), F32), jax.ShapeDtypeStruct(chist.shape, F32),
           jax.ShapeDtypeStruct(shist.shape, F32)],
        compiler_params=_cparams(("arbitrary",)),
        name="pre_sample",
    )(xs, *consts, chist, shist)


GDN_STEP_TOKENS = 8


def _gdn_sample_kernel(q_ref, k_ref, v_ref, gb_ref, s_ref, o_ref, snew_ref):
    n = GDN_DK
    for i in range(GDN_STEP_TOKENS):
        for h in range(GDN_HEADS):
            sl = slice(h * GDN_DK, (h + 1) * GDN_DK)
            qrow = q_ref[i:i + 1, sl]
            krow = k_ref[i:i + 1, sl]
            vrow = v_ref[i:i + 1, sl]
            g = gb_ref[i:i + 1, h:h + 1]
            beta = gb_ref[i:i + 1, GDN_HEADS + h:GDN_HEADS + h + 1]
            kcol = jnp.broadcast_to(krow, (n, n)).T
            qcol = jnp.broadcast_to(qrow, (n, n)).T
            s1 = s_ref[i, h] * jnp.exp(g)
            sk = jnp.sum(s1 * kcol, axis=0, keepdims=True)
            vt = (vrow - sk) * beta
            s2 = s1 + kcol * vt
            snew_ref[i, h] = s2
            o_ref[i:i + 1, sl] = jnp.sum(s2 * qcol, axis=0, keepdims=True)


def _gdn_sample(q, k, v, gb, state):
    n = q.shape[0]
    tb = GDN_STEP_TOKENS

    def tok(width):
        return pl.BlockSpec((tb, width), lambda i: (i, 0))

    st = pl.BlockSpec((tb, GDN_HEADS, GDN_DK, GDN_DK), lambda i: (i, 0, 0, 0))
    return pl.pallas_call(
        _gdn_sample_kernel,
        grid=(n // tb,),
        in_specs=[tok(GDN_V), tok(GDN_V), tok(GDN_V), tok(LANES), st],
        out_specs=[tok(GDN_V), st],
        out_shape=[jax.ShapeDtypeStruct((n, GDN_V), F32), jax.ShapeDtypeStruct(state.shape, F32)],
        compiler_params=_cparams(("arbitrary",)),
        name="gdn_sample",
    )(q, k, v, gb, state)


def _mix_sample_kernel(x_ref, conv_ref, o_ref, z_ref, gon_ref, wout_ref, gx_ref, wq_ref, x1_ref, qx_ref):
    x1 = _mix_out(conv_ref[...], o_ref[...], z_ref[...], gon_ref[...], wout_ref, x_ref[...])
    x1_ref[...] = x1
    qx_ref[...] = jnp.dot(_rms(x1, gx_ref[...]).astype(BF16), wq_ref[...], preferred_element_type=F32)


def _mix_sample(xs, conv, o, z, wts):
    n = xs.shape[0]
    in_arrays = (xs, conv, o, z, wts["g_onorm"], wts["w_out_b"], wts["g_xattn"], wts["w_xq_b"])
    return pl.pallas_call(
        _mix_sample_kernel,
        grid=(1,),
        in_specs=[_full(a.shape) for a in in_arrays],
        out_specs=[_full((n, D_MODEL))] * 2,
        out_shape=[jax.ShapeDtypeStruct((n, D_MODEL), F32)] * 2,
        compiler_params=_cparams(("arbitrary",)),
        name="mix_sample",
    )(*in_arrays)


ATTN_STEP_TOKENS = 4


def _attn_sample_kernel(qx_ref, ck_ref, cv_ref, att_ref):
    for i in range(ATTN_STEP_TOKENS):
        parts = []
        for h in range(X_HEADS):
            sl = slice(h * X_HEAD_DIM, (h + 1) * X_HEAD_DIM)
            prod = ck_ref[i, :, h, :] * qx_ref[0, i:i + 1, sl]
            s = jnp.sum(prod, axis=-1, keepdims=True) * (X_HEAD_DIM ** -0.5)
            e = jnp.exp(s - jnp.max(s, axis=0, keepdims=True))
            p = e / jnp.sum(e, axis=0, keepdims=True)
            parts.append(jnp.sum(p * cv_ref[i, :, h, :], axis=0, keepdims=True))
        att_ref[0, i:i + 1, :] = jnp.concatenate(parts, axis=1)


def _attn_sample(qx, ck, cv):
    n = qx.shape[0]
    tb = ATTN_STEP_TOKENS
    q3 = qx.reshape(n // tb, tb, D_MODEL)
    qspec = pl.BlockSpec((1, tb, D_MODEL), lambda i: (i, 0, 0))
    cspec = pl.BlockSpec((tb, N_MEM, X_HEADS, X_HEAD_DIM), lambda i: (i, 0, 0, 0))
    out = pl.pallas_call(
        _attn_sample_kernel,
        grid=(n // tb,),
        in_specs=[qspec, cspec, cspec],
        out_specs=qspec,
        out_shape=jax.ShapeDtypeStruct(q3.shape, F32),
        compiler_params=_cparams(("arbitrary",)),
        name="attn_sample",
    )(q3, ck, cv)
    return out.reshape(n, D_MODEL)


def _route_sample_kernel(x1_ref, att_ref, wo_ref, gmoe_ref, wr_ref, br_ref, tri_ref, x2_ref, h3r_ref, route_ref,
                         rt_ref, cnt_ref):
    x2 = x1_ref[...] + jnp.dot(att_ref[...].astype(BF16), wo_ref[...], preferred_element_type=F32)
    x2_ref[...] = x2
    h3, logits = _router_logits(x2, gmoe_ref[...], wr_ref, br_ref[...])
    route, counts = _route_finish(logits, jnp.zeros((1, LANES), F32), tri_ref[...])
    h3r_ref[...] = h3
    route_ref[...] = route
    rt_ref[...] = route.T[0:ROUTE_ROWS, :]
    cnt_ref[...] = jnp.broadcast_to(counts, cnt_ref.shape)


def _route_sample(x1, att, wts):
    n = x1.shape[0]
    in_arrays = (x1, att, wts["w_xo_b"], wts["g_moe"], wts["w_router"], wts["b_router"],
                 _strict_lower(n))
    shapes = [(n, D_MODEL), (n, D_MODEL), (n, LANES), (ROUTE_ROWS, n), (SUBLANES, LANES)]
    return pl.pallas_call(
        _route_sample_kernel,
        grid=(1,),
        in_specs=[_full(a.shape) for a in in_arrays],
        out_specs=[_full(s) for s in shapes],
        out_shape=[jax.ShapeDtypeStruct(s, F32) for s in shapes],
        compiler_params=_cparams(("arbitrary",)),
        name="route_sample",
    )(*in_arrays)


def _part_tiles(tiles):
    n_workers = SC_CORES * SC_SUBCORES
    while True:
        rows = tiles * MOE_TILE
        if rows % (n_workers * SUBLANES) == 0 and any(
                (rows // n_workers) % c == 0 for c in range(64, 24, -SUBLANES)):
            return tiles
        tiles += 1


def _sc_chunk(rows_per_worker):
    for c in range(64, 0, -SUBLANES):
        if rows_per_worker % c == 0:
            return c
    raise ValueError(rows_per_worker)


def _sc_gather_rows(table, idx):
    n_workers = SC_CORES * SC_SUBCORES
    b = idx.shape[0]
    assert b % (n_workers * SUBLANES) == 0
    per_worker = b // n_workers
    chunk = _sc_chunk(per_worker)
    row_shape = table.shape[1:]
    mesh = plsc.VectorSubcoreMesh(core_axis_name="c", subcore_axis_name="s")

    @functools.partial(
        pl.kernel, mesh=mesh,
        out_type=jax.ShapeDtypeStruct((b,) + row_shape, table.dtype),
        scratch_types=[pltpu.VMEM((chunk,), I32), pltpu.VMEM((chunk,) + row_shape, table.dtype),
                       pltpu.SemaphoreType.DMA],
        name="sc_gather_rows",
    )
    def gather(table_hbm, idx_hbm, out_hbm, idx_v, rows_v, sem):
        worker = lax.axis_index("s") * SC_CORES + lax.axis_index("c")
        base = worker * per_worker

        @pl.loop(0, per_worker // chunk)
        def _(c):
            off = pl.multiple_of(base + c * chunk, SUBLANES)
            pltpu.sync_copy(idx_hbm.at[pl.ds(off, chunk)], idx_v)
            pltpu.async_copy(table_hbm.at[idx_v], rows_v, sem).wait()
            pltpu.sync_copy(rows_v, out_hbm.at[pl.ds(off, chunk)])

    return gather(table, idx)


def _moe_kernel(te_ref, tn_ref, nt_ref, xs_ref, wgu_hbm, bgu_ref, wdn_hbm, bdn_ref, *rest, first_tile):
    ys_ref, wgu_f, wdn_f, wgu_b, wdn_b, sems = rest[-6:]
    step = pl.program_id(0)
    i = first_tile + step
    total = nt_ref[0]

    def weight_copies(e):
        return (pltpu.make_async_copy(wgu_hbm.at[e], wgu_f, sems.at[0]),
                pltpu.make_async_copy(wdn_hbm.at[e], wdn_f, sems.at[1]))

    def start(e):
        for cp in weight_copies(e):
            cp.start()

    @pl.when(i < total)
    def _():
        expert = te_ref[i]
        prev = te_ref[jnp.maximum(i - 1, 0)]
        fresh = jnp.logical_or(step == 0, expert != prev)

        @pl.when(step == 0)
        def _():
            start(expert)

        @pl.when(fresh)
        def _():
            for cp in weight_copies(expert):
                cp.wait()
            wgu_b[...] = wgu_f[...].astype(BF16)
            wdn_b[...] = wdn_f[...].astype(BF16)
            nxt = tn_ref[i]

            @pl.when(nxt >= 0)
            def _():
                start(nxt)

        x = xs_ref[...].astype(BF16)

        def up(c):
            glu_cols = slice(c * MOE_COLS, (c + 1) * MOE_COLS)
            lin_cols = slice(D_EXPERT + c * MOE_COLS, D_EXPERT + (c + 1) * MOE_COLS)
            return (jnp.dot(x, wgu_b[:, glu_cols], preferred_element_type=F32) + bgu_ref[0, :, glu_cols],
                    jnp.dot(x, wgu_b[:, lin_cols], preferred_element_type=F32) + bgu_ref[0, :, lin_cols])

        n_chunks = D_EXPERT // MOE_COLS
        nxt = up(0)
        y = None
        for c in range(n_chunks):
            g, lin = nxt
            if c + 1 < n_chunks:
                nxt = up(c + 1)
            x_glu = jnp.minimum(g, SWIGLU_LIMIT)
            x_lin = jnp.clip(lin, -SWIGLU_LIMIT, SWIGLU_LIMIT)
            act = x_glu * jax.nn.sigmoid(SWIGLU_ALPHA * x_glu) * (x_lin + 1.0)
            part = jnp.dot(act.astype(BF16), wdn_b[c * MOE_COLS:(c + 1) * MOE_COLS, :],
                           preferred_element_type=F32)
            y = part if y is None else y + part
        ys_ref[...] = y + bdn_ref[0]

    @pl.when(i >= total)
    def _():
        ys_ref[...] = jnp.zeros(ys_ref.shape, F32)


def _moe(tile_e, tile_next, n_tiles, xs_part, ys_prev, first_tile, n_rows, w_gu, b_gu, w_dn, b_dn):
    tm = MOE_TILE

    def bias(shape):
        return pl.BlockSpec(shape, lambda i, te, tn, nt: (te[first_tile + i], 0, 0))

    hbm = pl.BlockSpec(memory_space=pl.ANY)
    in_specs = [pl.BlockSpec((tm, D_MODEL), lambda i, te, tn, nt: (i, 0)),
                hbm, bias((1, 1, 2 * D_EXPERT)), hbm, bias((1, 1, D_MODEL))]
    operands = [tile_e, tile_next, n_tiles, xs_part, w_gu, b_gu, w_dn, b_dn]
    aliases = {}
    if ys_prev is not None:
        in_specs.append(hbm)
        aliases = {len(operands): 0}
        operands.append(ys_prev)
    grid_spec = pltpu.PrefetchScalarGridSpec(
        num_scalar_prefetch=3,
        grid=(xs_part.shape[0] // tm,),
        in_specs=in_specs,
        out_specs=pl.BlockSpec((tm, D_MODEL), lambda i, te, tn, nt: (first_tile + i, 0)),
        scratch_shapes=[pltpu.VMEM((D_MODEL, 2 * D_EXPERT), F32),
                        pltpu.VMEM((D_EXPERT, D_MODEL), F32),
                        pltpu.VMEM((D_MODEL, 2 * D_EXPERT), BF16),
                        pltpu.VMEM((D_EXPERT, D_MODEL), BF16),
                        pltpu.SemaphoreType.DMA((2,))],
    )
    return pl.pallas_call(
        functools.partial(_moe_kernel, first_tile=first_tile),
        grid_spec=grid_spec,
        out_shape=jax.ShapeDtypeStruct((n_rows, D_MODEL), F32),
        input_output_aliases=aliases,
        compiler_params=_cparams(("arbitrary",), vmem=56 * 1024 * 1024),
        name="moe",
    )(*operands)


def _combine_kernel(x2_ref, route_ref, gfin_ref, yt_ref, *rest):
    y_ref = rest[-1]
    route = route_ref[...]
    acc = x2_ref[...]
    for j in range(TOP_K):
        acc = acc + route[:, TOP_K + j:TOP_K + j + 1] * yt_ref[j]
    y_ref[...] = _rms(acc, gfin_ref[...])


def _combine(x2, route, g_final, ys_tok, y_prev, tok0, n_tok, ys_block0):
    tc = min(COMBINE_TILE, n_tok)
    b0 = tok0 // tc

    def tok(width):
        return pl.BlockSpec((tc, width), lambda i: (b0 + i, 0))

    in_specs = [tok(D_MODEL), tok(LANES), pl.BlockSpec((1, D_MODEL), lambda i: (0, 0)),
                pl.BlockSpec((TOP_K, tc, D_MODEL), lambda i: (0, ys_block0 + i, 0))]
    operands = [x2, route, g_final, ys_tok]
    aliases = {}
    if y_prev is not None:
        in_specs.append(pl.BlockSpec(memory_space=pl.ANY))
        aliases = {len(operands): 0}
        operands.append(y_prev)
    return pl.pallas_call(
        _combine_kernel,
        grid=(n_tok // tc,),
        in_specs=in_specs,
        out_specs=tok(D_MODEL),
        out_shape=jax.ShapeDtypeStruct(x2.shape, F32),
        input_output_aliases=aliases,
        compiler_params=_cparams(("arbitrary",)),
        name="combine",
    )(*operands)


def _routing_tables(idx_t, rank_t, counts, n_rows, part_tiles):
    tm = MOE_TILE
    n_tok = idx_t.shape[1]
    n_assign = TOP_K * n_tok
    tok_mask = (1 << TOKEN_BITS) - 1
    tiles_e = (counts + tm - 1) // tm
    tile_end = jnp.cumsum(tiles_e)
    row_start = (tile_end - tiles_e) * tm
    total = tile_end[-1]
    expert_ids = jnp.arange(N_EXPERTS, dtype=I32)

    def lookup(table, e):
        return jnp.sum(jnp.where(e[..., None] == expert_ids, table, 0), axis=-1)

    pos = lookup(row_start, idx_t) + rank_t
    keys_real = (idx_t * (1 << TOKEN_BITS) + jnp.arange(n_tok, dtype=I32)[None, :]).reshape(-1)
    k = jnp.arange(n_rows - n_assign, dtype=I32)
    pad_e, pad_s = k // tm, k % tm
    pad_needed = lookup(tiles_e * tm - counts, pad_e)
    pad_key_e = jnp.where((pad_e < N_EXPERTS) & (pad_s < pad_needed), pad_e, N_EXPERTS)
    keys = lax.sort(jnp.concatenate([keys_real, pad_key_e * (1 << TOKEN_BITS) + tok_mask]),
                    is_stable=False)
    src_tok = jnp.where((keys & tok_mask) == tok_mask, jnp.arange(n_rows, dtype=I32) % n_tok,
                        keys & tok_mask)
    tid = jnp.minimum(jnp.arange(n_rows // tm, dtype=I32), total - 1)
    tile_e = jnp.minimum(jnp.sum((tid[:, None] >= tile_end[None, :]).astype(I32), axis=1), N_EXPERTS - 1)
    later = (expert_ids[None, :] > expert_ids[:, None]) & (tiles_e[None, :] > 0)
    next_e = jnp.min(jnp.where(later, expert_ids[None, :], N_EXPERTS), axis=1)
    nxt = lookup(next_e, tile_e)
    nxt_first_tile = lookup(tile_end - tiles_e, jnp.minimum(nxt, N_EXPERTS - 1))
    bounds = jnp.cumsum(jnp.asarray(part_tiles, I32))
    part_end = jnp.min(jnp.where(bounds[None, :] > tid[:, None], bounds[None, :], n_rows), axis=1)
    tile_next = jnp.where((nxt < N_EXPERTS) & (nxt_first_tile < part_end), nxt, -1)
    return (tile_e.astype(I32), tile_next.astype(I32), total.reshape(1).astype(I32), src_tok.astype(I32),
            pos.reshape(-1).astype(I32))


def _pad_rows(a, rows):
    return jnp.concatenate([a, jnp.zeros((rows - a.shape[0],) + a.shape[1:], a.dtype)], axis=0)


def _pad_lanes(a, lanes=LANES):
    return jnp.concatenate([a, jnp.zeros(a.shape[:-1] + (lanes - a.shape[-1],), a.dtype)], axis=-1)


def kernel(x_prompt, mem_prompt, x_sample, state_conformer_conv, state_gdn_conv, state_gdn, cache_mem_k,
           cache_mem_v, w_in, b_glu, w_dw, b_dw, ln_g, ln_b, w_sc, a_log, dt_bias, g_onorm, w_out, g_mix,
           g_xattn, g_mem, w_xq, w_mk, w_mv, w_xo, g_moe, w_router, b_router, w_gu, b_gu, w_dn, b_dn,
           g_final):
    assert w_in.shape[0] == 1, "single-layer configuration"
    batch, seq, _ = x_prompt.shape
    n_s = x_sample.shape[0]
    n_p = batch * seq
    n_all = n_p + n_s
    assert seq % TOKEN_TILE == 0 and n_p % n_s == 0 and n_all < (1 << TOKEN_BITS) - 1
    assert (n_all * TOP_K) % (SC_CORES * SC_SUBCORES * SUBLANES) == 0

    wts = {
        "g_mix": g_mix[0][None], "g_xattn": g_xattn[0][None], "g_moe": g_moe[0][None],
        "g_onorm": g_onorm[0][None],
        "w_in_b": w_in[0][:, :OFF_A].astype(BF16),
        "w_ab_b": _pad_lanes(w_in[0][:, OFF_A:]).astype(BF16),
        "b_glu": b_glu[0][None],
        "w_dw": _pad_rows(w_dw[0], 32), "b_dw": b_dw[0][None], "ln_g": ln_g[0][None], "ln_b": ln_b[0][None],
        "w_sc": _pad_rows(w_sc[0], 8),
        "gdn_cst": _pad_rows(_pad_lanes(jnp.stack([a_log[0], dt_bias[0]])), 8),
        "w_out_b": w_out[0].astype(BF16), "w_xq_b": w_xq[0].astype(BF16), "w_xo_b": w_xo[0].astype(BF16),
        "w_router": _pad_lanes(w_router[0]), "b_router": _pad_lanes(b_router[0][None]),
    }

    mk, mv, mk_b, mv_b = _mem_kv(mem_prompt.reshape(batch * N_MEM, D_MODEL), g_mem[0][None],
                                 w_mk[0].astype(BF16), w_mv[0].astype(BF16))
    xp = x_prompt.reshape(n_p, D_MODEL)
    conv_p, qkv_p, z_p, gb_p, cstate_p, sstate_p = _pre_prompt(xp, batch, seq, wts)
    o_p, gstate_p = _gdn_prompt(qkv_p, gb_p, wts["w_sc"], batch, seq)

    xs = x_sample.reshape(n_s, D_MODEL)
    conv_s, q_s, k_s, v_s, z_s, gb_s, cstate_s, sstate_s = _pre_sample(
        xs, state_conformer_conv[0], state_gdn_conv[0], wts)
    o_s, gstate_s = _gdn_sample(q_s, k_s, v_s, gb_s, state_gdn[0])
    x1_s, qx_s = _mix_sample(xs, conv_s, o_s, z_s, wts)
    att_s = _attn_sample(qx_s, cache_mem_k[0], cache_mem_v[0])
    x2_s, h3_s, route_s, rt_s, counts_s = _route_sample(x1_s, att_s, wts)

    x2_p, h3r, route_p, rt_p, counts_p = _post_prompt(xp, conv_p, o_p, z_p, mk_b, mv_b, h3_s, batch, seq,
                                                      wts)

    counts_p = counts_p[0, :N_EXPERTS].astype(I32)
    counts_s = counts_s[0, :N_EXPERTS].astype(I32)
    idx_s = rt_s[0:TOP_K].astype(I32)
    rank_s = rt_s[2 * TOP_K:3 * TOP_K].astype(I32) + jnp.sum(
        jnp.where(idx_s[..., None] == jnp.arange(N_EXPERTS, dtype=I32), counts_p, 0), axis=-1)
    idx_t = jnp.concatenate([rt_p[0:TOP_K].astype(I32), idx_s], axis=1)
    rank_t = jnp.concatenate([rt_p[2 * TOP_K:3 * TOP_K].astype(I32), rank_s], axis=1)
    min_tiles = -(-(n_all * TOP_K + N_EXPERTS * (MOE_TILE - 1)) // MOE_TILE)
    part_tiles = [_part_tiles(-(-min_tiles * w // sum(MOE_PART_WEIGHTS))) for w in MOE_PART_WEIGHTS]
    first_tiles = [sum(part_tiles[:k]) for k in range(len(part_tiles))]
    n_rows = sum(part_tiles) * MOE_TILE
    tile_e, tile_next, n_tiles, src_tok, pos = _routing_tables(idx_t, rank_t, counts_p + counts_s, n_rows,
                                                               part_tiles)
    pos_t = pos.reshape(TOP_K, n_all)
    xs_parts = [_sc_gather_rows(h3r, src_tok[f * MOE_TILE:(f + t) * MOE_TILE])
                for f, t in zip(first_tiles, part_tiles)]
    ys = None
    for k, f in enumerate(first_tiles):
        ys = _moe(tile_e, tile_next, n_tiles, xs_parts[k], ys, f, n_rows, w_gu[0],
                  b_gu[0][:, None, :], w_dn[0], b_dn[0][:, None, :])
    gfin = g_final[None]
    assert n_p % (sum(TOKEN_PART_WEIGHTS) * COMBINE_TILE) == 0
    tok_parts = [n_p * w // sum(TOKEN_PART_WEIGHTS) for w in TOKEN_PART_WEIGHTS]
    y_p = None
    for k, n_tok in enumerate(tok_parts):
        tok0 = sum(tok_parts[:k])
        last = k == len(tok_parts) - 1
        pos_k = pos_t[:, tok0:tok0 + n_tok]
        if last:
            pos_k = jnp.concatenate([pos_k, pos_t[:, n_p:]], axis=1)
        ys_tok = _sc_gather_rows(ys, pos_k.reshape(-1)).reshape(TOP_K, pos_k.shape[1], D_MODEL)
        y_p = _combine(x2_p, route_p, gfin, ys_tok, y_p, tok0, n_tok, 0)
        if last:
            y_s = _combine(x2_s, route_s, gfin, ys_tok, None, 0, n_s, n_tok // min(COMBINE_TILE, n_s))

    return (y_p.reshape(batch, seq, D_MODEL), y_s.reshape(n_s, 1, D_MODEL),
            cstate_p[None], sstate_p[None], gstate_p[None],
            mk[None], mv[None],
            cstate_s[None], sstate_s[None], gstate_s[None])
```

```python
import functools

import jax
import jax.numpy as jnp
from jax import lax
from jax.experimental import pallas as pl
from jax.experimental.pallas import tpu as pltpu
from jax.experimental.pallas import tpu_sc as plsc

F32, BF16, I32 = jnp.float32, jnp.bfloat16, jnp.int32

D_MODEL = 1024
CONV_CH = 512
CONV_WIDTH = 31
GDN_HEADS = 4
GDN_DK = 128
GDN_V = 512
QKV_CH = 1536
SHORT_CONV = 4
N_MEM = 256
X_HEADS = 4
X_HEAD_DIM = 256
N_EXPERTS = 32
TOP_K = 4
D_EXPERT = 1024
SWIGLU_LIMIT = 7.0
SWIGLU_ALPHA = 1.702
NORM_EPS = 1e-6
OFF_QKV = 2 * CONV_CH
OFF_Z = OFF_QKV + QKV_CH
OFF_A = OFF_Z + GDN_V

LANES = 128
SUBLANES = 8
GDN_BLOCK = 128
TOKEN_TILE = 256
MOE_TILE = 384
MOE_COLS = 256
MOE_PART_WEIGHTS = (3, 6, 8, 9)
TOKEN_PART_WEIGHTS = (1, 2, 2, 3)
COMBINE_TILE = 128
TOKEN_BITS = 15
ROUTE_ROWS = 16
SC_CORES = 2
SC_SUBCORES = 16
VMEM_LIMIT = 48 * 1024 * 1024


def _cparams(sem, vmem=VMEM_LIMIT):
    return pltpu.CompilerParams(dimension_semantics=sem, vmem_limit_bytes=vmem)


def _mm(a, b):
    return jnp.dot(a.astype(BF16), b.astype(BF16), preferred_element_type=F32)


def _mm_nt(a, b):
    return lax.dot_general(a.astype(BF16), b.astype(BF16), (((1,), (1,)), ((), ())),
                           preferred_element_type=F32)


def _mm_tn(a, b):
    return lax.dot_general(a.astype(BF16), b.astype(BF16), (((0,), (0,)), ((), ())),
                           preferred_element_type=F32)


def _rms(x, g):
    return x * lax.rsqrt(jnp.mean(x * x, axis=-1, keepdims=True) + NORM_EPS) * g


def _silu(x):
    return x * jax.nn.sigmoid(x)


PACKED = D_MODEL // 2


def _pack_rows(h):
    lo = pltpu.bitcast(h[:, :PACKED].astype(BF16).astype(F32), jnp.uint32)
    hi = pltpu.bitcast(h[:, PACKED:].astype(BF16).astype(F32), jnp.uint32)
    return (lo >> 16) | (hi & jnp.uint32(0xFFFF0000))


def _unpack_rows(w):
    lo = pltpu.bitcast(w << 16, F32)
    hi = pltpu.bitcast(w & jnp.uint32(0xFFFF0000), F32)
    return jnp.concatenate([lo, hi], axis=1).astype(BF16)


def _full(shape):
    return pl.BlockSpec(shape, lambda *_: (0,) * len(shape))


def _strict_lower(n):
    return jnp.tril(jnp.ones((n, n), BF16), k=-1)


def _project(x, gmix, w_ref, wab_ref, bglu):
    h = _rms(x, gmix).astype(BF16)
    u_glu = jnp.dot(h, w_ref[:, 0:OFF_QKV], preferred_element_type=F32) + bglu
    glu = u_glu[:, :CONV_CH] * jax.nn.sigmoid(u_glu[:, CONV_CH:])
    qkv_pre = jnp.dot(h, w_ref[:, OFF_QKV:OFF_Z], preferred_element_type=F32)
    z = jnp.dot(h, w_ref[:, OFF_Z:OFF_A], preferred_element_type=F32)
    uab = jnp.dot(h, wab_ref[...], preferred_element_type=F32)
    return glu, qkv_pre, z, uab


def _gate_beta(uab, cst):
    lane = lax.broadcasted_iota(I32, uab.shape, 1)
    g = -jnp.exp(cst[0:1, :]) * jax.nn.softplus(uab + cst[1:2, :])
    return jnp.where(lane < GDN_HEADS, g, jax.nn.sigmoid(uab))


def _conv_post(c, b_dw, ln_g, ln_b):
    c = c + b_dw
    mu = jnp.mean(c, axis=-1, keepdims=True)
    d = c - mu
    var = jnp.mean(d * d, axis=-1, keepdims=True)
    return _silu(d * lax.rsqrt(var + NORM_EPS) * ln_g + ln_b)


def _qkv_post(cs):
    a = _silu(cs)
    parts = []
    for h in range(2 * GDN_HEADS):
        seg = a[:, h * GDN_DK:(h + 1) * GDN_DK]
        n = seg * lax.rsqrt(jnp.sum(seg * seg, axis=-1, keepdims=True) + NORM_EPS)
        if h < GDN_HEADS:
            n = n * (GDN_DK ** -0.5)
        parts.append(n)
    q = jnp.concatenate(parts[:GDN_HEADS], axis=1)
    k = jnp.concatenate(parts[GDN_HEADS:], axis=1)
    return q, k, a[:, 2 * GDN_HEADS * GDN_DK:]


def _mix_out(conv_b, o, z, gon, wout_ref, x):
    parts = []
    for h in range(GDN_HEADS):
        oh = o[:, h * 128:(h + 1) * 128]
        parts.append(oh * lax.rsqrt(jnp.mean(oh * oh, axis=-1, keepdims=True) + NORM_EPS) * gon)
    on = jnp.concatenate(parts, axis=1) * _silu(z)
    mixed = (jnp.dot(conv_b, wout_ref[0:CONV_CH, :], preferred_element_type=F32)
             + jnp.dot(on.astype(BF16), wout_ref[CONV_CH:, :], preferred_element_type=F32))
    return x + mixed


def _router_logits(x2, gmoe, wr_ref, br):
    h3 = _rms(x2, gmoe)
    h_hi = h3.astype(BF16)
    r1 = h3 - h_hi.astype(F32)
    h_mid = r1.astype(BF16)
    h_lo = (r1 - h_mid.astype(F32)).astype(BF16)
    w = wr_ref[...]
    w_hi = w.astype(BF16)
    w_lo = (w - w_hi.astype(F32)).astype(BF16)
    logits = (jnp.dot(h_hi, w_hi, preferred_element_type=F32)
              + jnp.dot(h_hi, w_lo, preferred_element_type=F32)
              + jnp.dot(h_mid, w_hi, preferred_element_type=F32)
              + jnp.dot(h_lo, w_hi, preferred_element_type=F32)) + br
    return h3, logits


def _route_finish(logits, carry, before):
    m = logits.shape[0]
    neg = jnp.float32(-jnp.inf)
    n_groups = 4 if m % 32 == 0 else 1
    rows = m // n_groups
    lanes_g = lax.broadcasted_iota(I32, (rows, LANES), 1).astype(F32)
    works = [jnp.where(lanes_g < N_EXPERTS, logits[g * rows:(g + 1) * rows], neg) for g in range(n_groups)]
    vals, idxs = [], []
    for _ in range(TOP_K):
        mxs = [jnp.max(w, axis=-1, keepdims=True) for w in works]
        ixs = [jnp.min(jnp.where(w == mx, lanes_g, float(LANES)), axis=-1, keepdims=True)
               for w, mx in zip(works, mxs)]
        works = [jnp.where(lanes_g == ix, neg, w) for w, ix in zip(works, ixs)]
        vals.append(mxs)
        idxs.append(ixs)
    sels, gates = [], []
    for g in range(n_groups):
        es = [jnp.exp(vals[r][g] - vals[0][g]) for r in range(TOP_K)]
        den = es[0] + es[1] + es[2] + es[3]
        gates.append([e / den for e in es])
        sel_g = jnp.zeros((rows, LANES), F32)
        for r in range(TOP_K):
            sel_g = sel_g + jnp.where(lanes_g == idxs[r][g], 1.0, 0.0)
        sels.append(sel_g)
    sel = jnp.concatenate(sels, axis=0)
    rank_full = jnp.dot(before, sel.astype(BF16), preferred_element_type=F32) + carry
    routes = []
    for g in range(n_groups):
        rank_g = rank_full[g * rows:(g + 1) * rows]
        route_g = jnp.zeros((rows, LANES), F32)
        for r in range(TOP_K):
            rk = jnp.sum(jnp.where(lanes_g == idxs[r][g], rank_g, 0.0), axis=-1, keepdims=True)
            route_g = (route_g + jnp.where(lanes_g == r, idxs[r][g], 0.0)
                       + jnp.where(lanes_g == TOP_K + r, gates[g][r], 0.0)
                       + jnp.where(lanes_g == 2 * TOP_K + r, rk, 0.0))
        routes.append(route_g)
    new_carry = carry + jnp.sum(sel, axis=0, keepdims=True)
    return jnp.concatenate(routes, axis=0), new_carry


def _mem_kv_kernel(mem_ref, g_ref, wk_ref, wv_ref, mk_ref, mv_ref, mkb_ref, mvb_ref):
    m = _rms(mem_ref[...], g_ref[...]).astype(BF16)
    mk = jnp.dot(m, wk_ref[...], preferred_element_type=F32)
    mv = jnp.dot(m, wv_ref[...], preferred_element_type=F32)
    for h in range(X_HEADS):
        sl = slice(h * X_HEAD_DIM, (h + 1) * X_HEAD_DIM)
        mk_ref[0, :, h, :] = mk[:, sl]
        mv_ref[0, :, h, :] = mv[:, sl]
    mkb_ref[...] = mk.astype(BF16)
    mvb_ref[...] = mv.astype(BF16)


def _mem_kv(mem, g_mem, wk_b, wv_b):
    rows = mem.shape[0]
    tm = N_MEM
    row_spec = pl.BlockSpec((tm, D_MODEL), lambda i: (i, 0))
    head_spec = pl.BlockSpec((1, tm, X_HEADS, X_HEAD_DIM), lambda i: (i, 0, 0, 0))
    head_shape = jax.ShapeDtypeStruct((rows // tm, tm, X_HEADS, X_HEAD_DIM), F32)
    return pl.pallas_call(
        _mem_kv_kernel,
        grid=(rows // tm,),
        in_specs=[row_spec, _full((1, D_MODEL)), _full((D_MODEL, D_MODEL)), _full((D_MODEL, D_MODEL))],
        out_specs=[head_spec, head_spec, row_spec, row_spec],
        out_shape=[head_shape, head_shape] + [jax.ShapeDtypeStruct((rows, D_MODEL), BF16)] * 2,
        compiler_params=_cparams(("arbitrary",)),
        name="mem_kv",
    )(mem, g_mem, wk_b, wv_b)


CONV_HALO = 32
SC_HALO = 8


def _pre_prompt_kernel(x_ref, gmix_ref, w_ref, wab_ref, bglu_ref, wdw_ref, bdw_ref, lng_ref, lnb_ref,
                       cst_ref, conv_ref, qkv_ref, z_ref, gb_ref, cstate_ref, sstate_ref, cbuf, *, tm):
    j = pl.program_id(1)

    @pl.when(j == 0)
    def _():
        cbuf[0:CONV_HALO, :] = jnp.zeros((CONV_HALO, CONV_CH), F32)

    glu, qkv_pre, z, uab = _project(x_ref[...], gmix_ref[...], w_ref, wab_ref, bglu_ref[...])
    cbuf[CONV_HALO:CONV_HALO + tm, :] = glu
    qkv_ref[...] = qkv_pre
    z_ref[...] = z
    gb_ref[...] = _gate_beta(uab, cst_ref[...])

    base = CONV_HALO - (CONV_WIDTH - 1)
    rows = CONV_HALO + tm
    accs = []
    for c in range(CONV_CH // LANES):
        lanes = slice(c * LANES, (c + 1) * LANES)
        block = cbuf[:, lanes]
        acc = None
        for r in range(SUBLANES):
            shifted = block if r == 0 else pltpu.roll(block, rows - r, 0)
            for a in range(base, base + CONV_WIDTH):
                if a % SUBLANES == r:
                    t = a - base
                    term = wdw_ref[t:t + 1, lanes] * shifted[a - r:a - r + tm, :]
                    acc = term if acc is None else acc + term
        accs.append(acc)
    acc = jnp.concatenate(accs, axis=1)
    conv_ref[...] = _conv_post(acc, bdw_ref[...], lng_ref[...], lnb_ref[...]).astype(BF16)

    @pl.when(j == pl.num_programs(1) - 1)
    def _():
        cstate_ref[0] = cbuf[pl.ds(CONV_HALO + tm - (CONV_WIDTH - 1), CONV_WIDTH - 1), :]
        sstate_ref[0] = qkv_pre[tm - (SHORT_CONV - 1):, :]

    cbuf[0:CONV_HALO, :] = cbuf[tm:tm + CONV_HALO, :]


def _pre_prompt(x2d, batch, seq, wts):
    tm = TOKEN_TILE
    nj = seq // tm
    rows = batch * seq

    def tok(width):
        return pl.BlockSpec((tm, width), lambda b, j: (b * nj + j, 0))

    in_specs = [tok(D_MODEL), _full((1, D_MODEL)), _full((D_MODEL, OFF_A)), _full((D_MODEL, LANES)),
                _full((1, OFF_QKV)), _full((32, CONV_CH)), _full((1, CONV_CH)), _full((1, CONV_CH)),
                _full((1, CONV_CH)), _full((8, LANES))]
    out_specs = [tok(CONV_CH), tok(QKV_CH), tok(GDN_V), tok(LANES),
                 pl.BlockSpec((1, CONV_WIDTH - 1, CONV_CH), lambda b, j: (b, 0, 0)),
                 pl.BlockSpec((1, SHORT_CONV - 1, QKV_CH), lambda b, j: (b, 0, 0))]
    out_shape = [jax.ShapeDtypeStruct((rows, CONV_CH), BF16),
                 jax.ShapeDtypeStruct((rows, QKV_CH), F32),
                 jax.ShapeDtypeStruct((rows, GDN_V), F32),
                 jax.ShapeDtypeStruct((rows, LANES), F32),
                 jax.ShapeDtypeStruct((batch, CONV_WIDTH - 1, CONV_CH), F32),
                 jax.ShapeDtypeStruct((batch, SHORT_CONV - 1, QKV_CH), F32)]
    return pl.pallas_call(
        functools.partial(_pre_prompt_kernel, tm=tm),
        grid=(batch, nj),
        in_specs=in_specs,
        out_specs=out_specs,
        out_shape=out_shape,
        scratch_shapes=[pltpu.VMEM((CONV_HALO + tm, CONV_CH), F32)],
        compiler_params=_cparams(("arbitrary", "arbitrary")),
        name="pre_prompt",
    )(x2d, wts["g_mix"], wts["w_in_b"], wts["w_ab_b"], wts["b_glu"], wts["w_dw"], wts["b_dw"],
      wts["ln_g"], wts["ln_b"], wts["gdn_cst"])


GDN_SEQS = 4


def _gdn_prompt_kernel(qkv_ref, gb_ref, wsc_ref, o_ref, sfin_ref, s_scr, sbuf):
    c = pl.program_id(1)
    n = GDN_BLOCK
    seqs = range(GDN_SEQS)

    @pl.when(c == 0)
    def _():
        s_scr[...] = jnp.zeros(s_scr.shape, F32)
        sbuf[:, 0:SC_HALO, :] = jnp.zeros((GDN_SEQS, SC_HALO, QKV_CH), F32)

    sbase = SC_HALO - (SHORT_CONV - 1)
    qkvs = []
    for b in seqs:
        sbuf[b, SC_HALO:SC_HALO + n, :] = qkv_ref[b]
        cs = wsc_ref[0:1, :] * sbuf[b, pl.ds(sbase, n), :]
        for t in range(1, SHORT_CONV):
            cs = cs + wsc_ref[t:t + 1, :] * sbuf[b, pl.ds(sbase + t, n), :]
        qkvs.append(_qkv_post(cs))
        sbuf[b, 0:SC_HALO, :] = sbuf[b, n:n + SC_HALO, :]

    row = lax.broadcasted_iota(I32, (n, n), 0)
    col = lax.broadcasted_iota(I32, (n, n), 1)
    causal = row >= col
    strict = row > col
    tri = jnp.where(causal, 1.0, 0.0).astype(BF16)
    eye = jnp.where(row == col, 1.0, 0.0)
    level_masks = []
    b = 1
    while b < n:
        same_pair = ((row ^ col) & ~(2 * b - 1)) == 0
        level_masks.append(same_pair & ((row & b) != 0) & ((col & b) == 0))
        b *= 2
    gbs, gcums, gcum_ts, egcs = [], [], [], []
    for b in seqs:
        gb = gb_ref[b]
        g1 = gb.astype(BF16)
        r1 = gb - g1.astype(F32)
        g2 = r1.astype(BF16)
        g3 = (r1 - g2.astype(F32)).astype(BF16)
        gcum = (jnp.dot(tri, g1, preferred_element_type=F32) + jnp.dot(tri, g2, preferred_element_type=F32)
                + jnp.dot(tri, g3, preferred_element_type=F32))
        gbs.append(gb)
        gcums.append(gcum)
        gcum_ts.append(gcum.T)
        egcs.append(jnp.exp(gcum))
    units = [(b, h) for b in seqs for h in range(GDN_HEADS)]
    idx = range(len(units))
    sls = [slice(h * GDN_DK, (h + 1) * GDN_DK) for h in range(GDN_HEADS)]
    qs = [qkvs[b][0][:, sls[h]] for b, h in units]
    ks = [qkvs[b][1][:, sls[h]] for b, h in units]
    vs = [qkvs[b][2][:, sls[h]] for b, h in units]
    ss = [s_scr[b, h] for b, h in units]
    gcols = [gcums[b][:, h:h + 1] for b, h in units]
    ecols = [egcs[b][:, h:h + 1] for b, h in units]
    betas = [gbs[b][:, GDN_HEADS + h:GDN_HEADS + h + 1] for b, h in units]
    glasts = [gcums[b][n - 1:n, h:h + 1] for b, h in units]
    decays = [jnp.where(causal, jnp.exp(jnp.where(causal, gcols[u] - gcum_ts[b][h:h + 1, :], 0.0)), 0.0)
              for u, (b, h) in enumerate(units)]
    kbs = [ks[u] * betas[u] for u in idx]
    lowers = [jnp.where(strict, _mm_nt(kbs[u], ks[u]) * decays[u], 0.0) for u in idx]
    intras = [jnp.where(causal, _mm_nt(qs[u], ks[u]) * decays[u], 0.0) for u in idx]
    xs = [eye - jnp.where(level_masks[0], lowers[u], 0.0) for u in idx]
    for mask in level_masks[1:]:
        ts = [_mm(xs[u], jnp.where(mask, lowers[u], 0.0)) for u in idx]
        xs = [xs[u] - _mm(ts[u], xs[u]) for u in idx]
    us = [_mm(xs[u], vs[u] * betas[u]) for u in idx]
    ws = [_mm(xs[u], kbs[u] * ecols[u]) for u in idx]
    v_news = [us[u] - _mm(ws[u], ss[u]) for u in idx]
    os_ = [_mm(qs[u] * ecols[u], ss[u]) + _mm(intras[u], v_news[u]) for u in idx]
    s_news = [ss[u] * jnp.exp(glasts[u]) + _mm_tn(ks[u] * jnp.exp(glasts[u] - gcols[u]), v_news[u])
              for u in idx]
    for u, (b, h) in enumerate(units):
        o_ref[b, :, sls[h]] = os_[u]
        s_scr[b, h] = s_news[u]

    @pl.when(c == pl.num_programs(1) - 1)
    def _():
        sfin_ref[...] = s_scr[...]


def _gdn_prompt(qkv, gb, w_sc, batch, seq):
    n = GDN_BLOCK
    nc = seq // n
    g = GDN_SEQS
    assert batch % g == 0

    def tok(width):
        return pl.BlockSpec((g, n, width), lambda b, c: (b, c, 0))

    state_shape = (g, GDN_HEADS, GDN_DK, GDN_DK)
    o, s_fin = pl.pallas_call(
        _gdn_prompt_kernel,
        grid=(batch // g, nc),
        in_specs=[tok(QKV_CH), tok(LANES), _full(w_sc.shape)],
        out_specs=[tok(GDN_V), pl.BlockSpec(state_shape, lambda b, c: (b, 0, 0, 0))],
        out_shape=[jax.ShapeDtypeStruct((batch, seq, GDN_V), F32),
                   jax.ShapeDtypeStruct((batch, GDN_HEADS, GDN_DK, GDN_DK), F32)],
        scratch_shapes=[pltpu.VMEM(state_shape, F32), pltpu.VMEM((g, SC_HALO + n, QKV_CH), F32)],
        compiler_params=_cparams(("arbitrary", "arbitrary")),
        name="gdn_prompt",
    )(qkv.reshape(batch, seq, QKV_CH), gb.reshape(batch, seq, LANES), w_sc)
    return o.reshape(batch * seq, GDN_V), s_fin


def _post_prompt_kernel(x_ref, conv_ref, o_ref, z_ref, gon_ref, wout_ref, gx_ref, wq_ref, mk_ref, mv_ref,
                        wo_ref, gmoe_ref, wr_ref, br_ref, tri_ref, h3s_ref, x2_ref, h3r_ref, route_ref, rt_ref,
                        cnt_ref, carry, logit_buf, *, n_steps):
    step = pl.program_id(0)

    @pl.when(step == 0)
    def _():
        carry[...] = jnp.zeros(carry.shape, F32)
        logit_buf[...] = jnp.zeros(logit_buf.shape, F32)

    x1 = _mix_out(conv_ref[...], o_ref[...], z_ref[...], gon_ref[...], wout_ref, x_ref[...])
    qx = jnp.dot(_rms(x1, gx_ref[...]).astype(BF16), wq_ref[...], preferred_element_type=F32)

    route, new_carry = _route_finish(logit_buf[...], carry[0:1, :], tri_ref[...])
    route_ref[...] = route
    rt_ref[...] = route.T[0:ROUTE_ROWS, :]
    kept = jnp.where(step >= 1, new_carry, carry[0:1, :])
    carry[0:1, :] = kept
    cnt_ref[...] = jnp.broadcast_to(kept, cnt_ref.shape)

    sls = [slice(h * X_HEAD_DIM, (h + 1) * X_HEAD_DIM) for h in range(X_HEADS)]
    qb = qx.astype(BF16)
    ss = [lax.dot_general(qb[:, sl], mk_ref[:, sl], (((1,), (1,)), ((), ())),
                          preferred_element_type=F32) * (X_HEAD_DIM ** -0.5) for sl in sls]
    es = [jnp.exp(s - jnp.max(s, axis=-1, keepdims=True)) for s in ss]
    ps = [(e / jnp.sum(e, axis=-1, keepdims=True)).astype(BF16) for e in es]
    att = jnp.concatenate(
        [jnp.dot(p, mv_ref[:, sl], preferred_element_type=F32) for p, sl in zip(ps, sls)], axis=1)
    x2 = x1 + jnp.dot(att.astype(BF16), wo_ref[...], preferred_element_type=F32)
    h3, logits = _router_logits(x2, gmoe_ref[...], wr_ref, br_ref[...])
    logit_buf[...] = logits

    @pl.when(step < n_steps)
    def _():
        x2_ref[...] = x2
        h3r_ref[...] = _pack_rows(h3)

    @pl.when(step == n_steps)
    def _():
        h3r_ref[0:h3s_ref.shape[0], :] = h3s_ref[...]


def _post_prompt(x2d, conv, o, z, mk_b, mv_b, h3_sample, batch, seq, wts):
    tm = TOKEN_TILE
    nj = seq // tm
    rows = batch * seq
    n_steps = batch * nj
    n_s = h3_sample.shape[0]
    assert n_s <= tm

    def tok(width):
        return pl.BlockSpec((tm, width), lambda s: (jnp.minimum(s, n_steps - 1), 0))

    mem_spec = pl.BlockSpec((N_MEM, D_MODEL), lambda s: (jnp.minimum(s, n_steps - 1) // nj, 0))
    sq = _full((D_MODEL, D_MODEL))
    in_specs = [tok(D_MODEL), tok(CONV_CH), tok(GDN_V), tok(GDN_V), _full((1, GDN_DK)), sq,
                _full((1, D_MODEL)), sq, mem_spec, mem_spec, sq, _full((1, D_MODEL)),
                _full((D_MODEL, LANES)), _full((1, LANES)), _full((tm, tm)), _full(h3_sample.shape)]
    out_specs = [tok(D_MODEL),
                 pl.BlockSpec((tm, PACKED), lambda s: (s, 0)),
                 pl.BlockSpec((tm, LANES), lambda s: (jnp.maximum(s - 1, 0), 0)),
                 pl.BlockSpec((ROUTE_ROWS, tm), lambda s: (0, jnp.maximum(s - 1, 0))),
                 _full((SUBLANES, LANES))]
    out_shape = [jax.ShapeDtypeStruct((rows, D_MODEL), F32),
                 jax.ShapeDtypeStruct((rows + n_s, PACKED), jnp.uint32),
                 jax.ShapeDtypeStruct((rows, LANES), F32),
                 jax.ShapeDtypeStruct((ROUTE_ROWS, rows), F32),
                 jax.ShapeDtypeStruct((SUBLANES, LANES), F32)]
    return pl.pallas_call(
        functools.partial(_post_prompt_kernel, n_steps=n_steps),
        grid=(n_steps + 1,),
        in_specs=in_specs,
        out_specs=out_specs,
        out_shape=out_shape,
        scratch_shapes=[pltpu.VMEM((SUBLANES, LANES), F32), pltpu.VMEM((tm, LANES), F32)],
        compiler_params=_cparams(("arbitrary",)),
        name="post_prompt",
    )(x2d, conv, o, z, wts["g_onorm"], wts["w_out_b"], wts["g_xattn"], wts["w_xq_b"], mk_b, mv_b,
      wts["w_xo_b"], wts["g_moe"], wts["w_router"], wts["b_router"], _strict_lower(tm), h3_sample)


def _pre_sample_kernel(x_ref, gmix_ref, w_ref, wab_ref, bglu_ref, wdw_ref, bdw_ref, lng_ref, lnb_ref,
                       wsc_ref, cst_ref, chist_ref, shist_ref, conv_ref, q_ref, k_ref, v_ref, z_ref,
                       gb_ref, cnew_ref, snew_ref):
    glu, qkv_pre, z, uab = _project(x_ref[...], gmix_ref[...], w_ref, wab_ref, bglu_ref[...])
    z_ref[...] = z
    gb_ref[...] = _gate_beta(uab, cst_ref[...])
    kw = CONV_WIDTH
    acc = wdw_ref[kw - 1:kw, :] * glu
    for t in range(kw - 1):
        row = chist_ref[:, t, :]
        acc = acc + wdw_ref[t:t + 1, :] * row
        if t >= 1:
            cnew_ref[:, t - 1, :] = row
    cnew_ref[:, kw - 2, :] = glu
    conv_ref[...] = _conv_post(acc, bdw_ref[...], lng_ref[...], lnb_ref[...]).astype(BF16)
    ks = SHORT_CONV
    cs = wsc_ref[ks - 1:ks, :] * qkv_pre
    for t in range(ks - 1):
        row = shist_ref[:, t, :]
        cs = cs + wsc_ref[t:t + 1, :] * row
        if t >= 1:
            snew_ref[:, t - 1, :] = row
    snew_ref[:, ks - 2, :] = qkv_pre
    q, k, v = _qkv_post(cs)
    q_ref[...] = q
    k_ref[...] = k
    v_ref[...] = v


PRE_SAMPLE_TOKENS = 32


def _pre_sample(xs, chist, shist, wts):
    n = xs.shape[0]
    tb = min(PRE_SAMPLE_TOKENS, n)

    def tok(width):
        return pl.BlockSpec((tb, width), lambda i: (i, 0))

    def hist(a):
        return pl.BlockSpec((tb,) + a.shape[1:], lambda i: (i, 0, 0))

    consts = (wts["g_mix"], wts["w_in_b"], wts["w_ab_b"], wts["b_glu"], wts["w_dw"], wts["b_dw"],
              wts["ln_g"], wts["ln_b"], wts["w_sc"], wts["gdn_cst"])
    return pl.pallas_call(
        _pre_sample_kernel,
        grid=(n // tb,),
        in_specs=[tok(D_MODEL)] + [_full(a.shape) for a in consts] + [hist(chist), hist(shist)],
        out_specs=[tok(CONV_CH), tok(GDN_V), tok(GDN_V), tok(GDN_V), tok(GDN_V), tok(LANES),
                   hist(chist), hist(shist)],
        out_shape=[jax.ShapeDtypeStruct((n, CONV_CH), BF16)]
        + [jax.ShapeDtypeStruct((n, GDN_V), F32)] * 4
        + [jax.ShapeDtypeStruct((n, LANES), F32), jax.ShapeDtypeStruct(chist.shape, F32),
           jax.ShapeDtypeStruct(shist.shape, F32)],
        compiler_params=_cparams(("arbitrary",)),
        name="pre_sample",
    )(xs, *consts, chist, shist)


GDN_STEP_TOKENS = 8


def _gdn_sample_kernel(q_ref, k_ref, v_ref, gb_ref, s_ref, o_ref, snew_ref):
    n = GDN_DK
    for i in range(GDN_STEP_TOKENS):
        for h in range(GDN_HEADS):
            sl = slice(h * GDN_DK, (h + 1) * GDN_DK)
            qrow = q_ref[i:i + 1, sl]
            krow = k_ref[i:i + 1, sl]
            vrow = v_ref[i:i + 1, sl]
            g = gb_ref[i:i + 1, h:h + 1]
            beta = gb_ref[i:i + 1, GDN_HEADS + h:GDN_HEADS + h + 1]
            kcol = jnp.broadcast_to(krow, (n, n)).T
            qcol = jnp.broadcast_to(qrow, (n, n)).T
            s1 = s_ref[i, h] * jnp.exp(g)
            sk = jnp.sum(s1 * kcol, axis=0, keepdims=True)
            vt = (vrow - sk) * beta
            s2 = s1 + kcol * vt
            snew_ref[i, h] = s2
            o_ref[i:i + 1, sl] = jnp.sum(s2 * qcol, axis=0, keepdims=True)


def _gdn_sample(q, k, v, gb, state):
    n = q.shape[0]
    tb = GDN_STEP_TOKENS

    def tok(width):
        return pl.BlockSpec((tb, width), lambda i: (i, 0))

    st = pl.BlockSpec((tb, GDN_HEADS, GDN_DK, GDN_DK), lambda i: (i, 0, 0, 0))
    return pl.pallas_call(
        _gdn_sample_kernel,
        grid=(n // tb,),
        in_specs=[tok(GDN_V), tok(GDN_V), tok(GDN_V), tok(LANES), st],
        out_specs=[tok(GDN_V), st],
        out_shape=[jax.ShapeDtypeStruct((n, GDN_V), F32), jax.ShapeDtypeStruct(state.shape, F32)],
        compiler_params=_cparams(("arbitrary",)),
        name="gdn_sample",
    )(q, k, v, gb, state)


def _mix_sample_kernel(x_ref, conv_ref, o_ref, z_ref, gon_ref, wout_ref, gx_ref, wq_ref, x1_ref, qx_ref):
    x1 = _mix_out(conv_ref[...], o_ref[...], z_ref[...], gon_ref[...], wout_ref, x_ref[...])
    x1_ref[...] = x1
    qx_ref[...] = jnp.dot(_rms(x1, gx_ref[...]).astype(BF16), wq_ref[...], preferred_element_type=F32)


def _mix_sample(xs, conv, o, z, wts):
    n = xs.shape[0]
    in_arrays = (xs, conv, o, z, wts["g_onorm"], wts["w_out_b"], wts["g_xattn"], wts["w_xq_b"])
    return pl.pallas_call(
        _mix_sample_kernel,
        grid=(1,),
        in_specs=[_full(a.shape) for a in in_arrays],
        out_specs=[_full((n, D_MODEL))] * 2,
        out_shape=[jax.ShapeDtypeStruct((n, D_MODEL), F32)] * 2,
        compiler_params=_cparams(("arbitrary",)),
        name="mix_sample",
    )(*in_arrays)


ATTN_STEP_TOKENS = 4


def _attn_sample_kernel(qx_ref, ck_ref, cv_ref, att_ref):
    for i in range(ATTN_STEP_TOKENS):
        parts = []
        for h in range(X_HEADS):
            sl = slice(h * X_HEAD_DIM, (h + 1) * X_HEAD_DIM)
            prod = ck_ref[i, :, h, :] * qx_ref[0, i:i + 1, sl]
            s = jnp.sum(prod, axis=-1, keepdims=True) * (X_HEAD_DIM ** -0.5)
            e = jnp.exp(s - jnp.max(s, axis=0, keepdims=True))
            p = e / jnp.sum(e, axis=0, keepdims=True)
            parts.append(jnp.sum(p * cv_ref[i, :, h, :], axis=0, keepdims=True))
        att_ref[0, i:i + 1, :] = jnp.concatenate(parts, axis=1)


def _attn_sample(qx, ck, cv):
    n = qx.shape[0]
    tb = ATTN_STEP_TOKENS
    q3 = qx.reshape(n // tb, tb, D_MODEL)
    qspec = pl.BlockSpec((1, tb, D_MODEL), lambda i: (i, 0, 0))
    cspec = pl.BlockSpec((tb, N_MEM, X_HEADS, X_HEAD_DIM), lambda i: (i, 0, 0, 0))
    out = pl.pallas_call(
        _attn_sample_kernel,
        grid=(n // tb,),
        in_specs=[qspec, cspec, cspec],
        out_specs=qspec,
        out_shape=jax.ShapeDtypeStruct(q3.shape, F32),
        compiler_params=_cparams(("arbitrary",)),
        name="attn_sample",
    )(q3, ck, cv)
    return out.reshape(n, D_MODEL)


def _route_sample_kernel(x1_ref, att_ref, wo_ref, gmoe_ref, wr_ref, br_ref, tri_ref, x2_ref, h3r_ref, route_ref,
                         rt_ref, cnt_ref):
    x2 = x1_ref[...] + jnp.dot(att_ref[...].astype(BF16), wo_ref[...], preferred_element_type=F32)
    x2_ref[...] = x2
    h3, logits = _router_logits(x2, gmoe_ref[...], wr_ref, br_ref[...])
    route, counts = _route_finish(logits, jnp.zeros((1, LANES), F32), tri_ref[...])
    h3r_ref[...] = _pack_rows(h3)
    route_ref[...] = route
    rt_ref[...] = route.T[0:ROUTE_ROWS, :]
    cnt_ref[...] = jnp.broadcast_to(counts, cnt_ref.shape)


def _route_sample(x1, att, wts):
    n = x1.shape[0]
    in_arrays = (x1, att, wts["w_xo_b"], wts["g_moe"], wts["w_router"], wts["b_router"],
                 _strict_lower(n))
    shapes = [(n, D_MODEL), (n, PACKED), (n, LANES), (ROUTE_ROWS, n), (SUBLANES, LANES)]
    dtypes = [F32, jnp.uint32, F32, F32, F32]
    return pl.pallas_call(
        _route_sample_kernel,
        grid=(1,),
        in_specs=[_full(a.shape) for a in in_arrays],
        out_specs=[_full(s) for s in shapes],
        out_shape=[jax.ShapeDtypeStruct(s, d) for s, d in zip(shapes, dtypes)],
        compiler_params=_cparams(("arbitrary",)),
        name="route_sample",
    )(*in_arrays)


def _part_tiles(tiles):
    n_workers = SC_CORES * SC_SUBCORES
    while True:
        rows = tiles * MOE_TILE
        if rows % (n_workers * SUBLANES) == 0 and any(
                (rows // n_workers) % c == 0 for c in range(64, 24, -SUBLANES)):
            return tiles
        tiles += 1


def _sc_chunk(rows_per_worker):
    for c in range(64, 0, -SUBLANES):
        if rows_per_worker % c == 0:
            return c
    raise ValueError(rows_per_worker)


def _sc_gather_rows(table, idx):
    n_workers = SC_CORES * SC_SUBCORES
    b = idx.shape[0]
    assert b % (n_workers * SUBLANES) == 0
    per_worker = b // n_workers
    chunk = _sc_chunk(per_worker)
    row_shape = table.shape[1:]
    mesh = plsc.VectorSubcoreMesh(core_axis_name="c", subcore_axis_name="s")

    @functools.partial(
        pl.kernel, mesh=mesh,
        out_type=jax.ShapeDtypeStruct((b,) + row_shape, table.dtype),
        scratch_types=[pltpu.VMEM((chunk,), I32), pltpu.VMEM((chunk,) + row_shape, table.dtype),
                       pltpu.SemaphoreType.DMA],
        name="sc_gather_rows",
    )
    def gather(table_hbm, idx_hbm, out_hbm, idx_v, rows_v, sem):
        worker = lax.axis_index("s") * SC_CORES + lax.axis_index("c")
        base = worker * per_worker

        @pl.loop(0, per_worker // chunk)
        def _(c):
            off = pl.multiple_of(base + c * chunk, SUBLANES)
            pltpu.sync_copy(idx_hbm.at[pl.ds(off, chunk)], idx_v)
            pltpu.async_copy(table_hbm.at[idx_v], rows_v, sem).wait()
            pltpu.sync_copy(rows_v, out_hbm.at[pl.ds(off, chunk)])

    return gather(table, idx)


def _moe_kernel(te_ref, tn_ref, nt_ref, xs_ref, wgu_hbm, bgu_ref, wdn_hbm, bdn_ref, *rest, first_tile):
    ys_ref, wgu_f, wdn_f, wgu_b, wdn_b, sems = rest[-6:]
    step = pl.program_id(0)
    i = first_tile + step
    total = nt_ref[0]

    def weight_copies(e):
        return (pltpu.make_async_copy(wgu_hbm.at[e], wgu_f, sems.at[0]),
                pltpu.make_async_copy(wdn_hbm.at[e], wdn_f, sems.at[1]))

    def start(e):
        for cp in weight_copies(e):
            cp.start()

    @pl.when(i < total)
    def _():
        expert = te_ref[i]
        prev = te_ref[jnp.maximum(i - 1, 0)]
        fresh = jnp.logical_or(step == 0, expert != prev)

        @pl.when(step == 0)
        def _():
            start(expert)

        @pl.when(fresh)
        def _():
            for cp in weight_copies(expert):
                cp.wait()
            wgu_b[...] = wgu_f[...].astype(BF16)
            wdn_b[...] = wdn_f[...].astype(BF16)
            nxt = tn_ref[i]

            @pl.when(nxt >= 0)
            def _():
                start(nxt)

        x = _unpack_rows(xs_ref[...])

        def up(c):
            glu_cols = slice(c * MOE_COLS, (c + 1) * MOE_COLS)
            lin_cols = slice(D_EXPERT + c * MOE_COLS, D_EXPERT + (c + 1) * MOE_COLS)
            return (jnp.dot(x, wgu_b[:, glu_cols], preferred_element_type=F32) + bgu_ref[0, :, glu_cols],
                    jnp.dot(x, wgu_b[:, lin_cols], preferred_element_type=F32) + bgu_ref[0, :, lin_cols])

        n_chunks = D_EXPERT // MOE_COLS
        nxt = up(0)
        y = None
        for c in range(n_chunks):
            g, lin = nxt
            if c + 1 < n_chunks:
                nxt = up(c + 1)
            x_glu = jnp.minimum(g, SWIGLU_LIMIT)
            x_lin = jnp.clip(lin, -SWIGLU_LIMIT, SWIGLU_LIMIT)
            act = x_glu * jax.nn.sigmoid(SWIGLU_ALPHA * x_glu) * (x_lin + 1.0)
            part = jnp.dot(act.astype(BF16), wdn_b[c * MOE_COLS:(c + 1) * MOE_COLS, :],
                           preferred_element_type=F32)
            y = part if y is None else y + part
        ys_ref[...] = y + bdn_ref[0]

    @pl.when(i >= total)
    def _():
        ys_ref[...] = jnp.zeros(ys_ref.shape, F32)


def _moe(tile_e, tile_next, n_tiles, xs_part, ys_prev, first_tile, n_rows, w_gu, b_gu, w_dn, b_dn):
    tm = MOE_TILE

    def bias(shape):
        return pl.BlockSpec(shape, lambda i, te, tn, nt: (te[first_tile + i], 0, 0))

    hbm = pl.BlockSpec(memory_space=pl.ANY)
    in_specs = [pl.BlockSpec((tm, PACKED), lambda i, te, tn, nt: (i, 0)),
                hbm, bias((1, 1, 2 * D_EXPERT)), hbm, bias((1, 1, D_MODEL))]
    operands = [tile_e, tile_next, n_tiles, xs_part, w_gu, b_gu, w_dn, b_dn]
    aliases = {}
    if ys_prev is not None:
        in_specs.append(hbm)
        aliases = {len(operands): 0}
        operands.append(ys_prev)
    grid_spec = pltpu.PrefetchScalarGridSpec(
        num_scalar_prefetch=3,
        grid=(xs_part.shape[0] // tm,),
        in_specs=in_specs,
        out_specs=pl.BlockSpec((tm, D_MODEL), lambda i, te, tn, nt: (first_tile + i, 0)),
        scratch_shapes=[pltpu.VMEM((D_MODEL, 2 * D_EXPERT), F32),
                        pltpu.VMEM((D_EXPERT, D_MODEL), F32),
                        pltpu.VMEM((D_MODEL, 2 * D_EXPERT), BF16),
                        pltpu.VMEM((D_EXPERT, D_MODEL), BF16),
                        pltpu.SemaphoreType.DMA((2,))],
    )
    return pl.pallas_call(
        functools.partial(_moe_kernel, first_tile=first_tile),
        grid_spec=grid_spec,
        out_shape=jax.ShapeDtypeStruct((n_rows, D_MODEL), F32),
        input_output_aliases=aliases,
        compiler_params=_cparams(("arbitrary",), vmem=56 * 1024 * 1024),
        name="moe",
    )(*operands)


def _combine_kernel(x2_ref, route_ref, gfin_ref, yt_ref, *rest):
    y_ref = rest[-1]
    route = route_ref[...]
    acc = x2_ref[...]
    for j in range(TOP_K):
        acc = acc + route[:, TOP_K + j:TOP_K + j + 1] * yt_ref[j]
    y_ref[...] = _rms(acc, gfin_ref[...])


def _combine(x2, route, g_final, ys_tok, y_prev, tok0, n_tok, ys_block0):
    tc = min(COMBINE_TILE, n_tok)
    b0 = tok0 // tc

    def tok(width):
        return pl.BlockSpec((tc, width), lambda i: (b0 + i, 0))

    in_specs = [tok(D_MODEL), tok(LANES), pl.BlockSpec((1, D_MODEL), lambda i: (0, 0)),
                pl.BlockSpec((TOP_K, tc, D_MODEL), lambda i: (0, ys_block0 + i, 0))]
    operands = [x2, route, g_final, ys_tok]
    aliases = {}
    if y_prev is not None:
        in_specs.append(pl.BlockSpec(memory_space=pl.ANY))
        aliases = {len(operands): 0}
        operands.append(y_prev)
    return pl.pallas_call(
        _combine_kernel,
        grid=(n_tok // tc,),
        in_specs=in_specs,
        out_specs=tok(D_MODEL),
        out_shape=jax.ShapeDtypeStruct(x2.shape, F32),
        input_output_aliases=aliases,
        compiler_params=_cparams(("arbitrary",)),
        name="combine",
    )(*operands)


def _routing_tables(idx_t, rank_t, counts, n_rows, part_tiles):
    tm = MOE_TILE
    n_tok = idx_t.shape[1]
    n_assign = TOP_K * n_tok
    tok_mask = (1 << TOKEN_BITS) - 1
    tiles_e = (counts + tm - 1) // tm
    tile_end = jnp.cumsum(tiles_e)
    row_start = (tile_end - tiles_e) * tm
    total = tile_end[-1]
    expert_ids = jnp.arange(N_EXPERTS, dtype=I32)

    def lookup(table, e):
        return jnp.sum(jnp.where(e[..., None] == expert_ids, table, 0), axis=-1)

    pos = lookup(row_start, idx_t) + rank_t
    keys_real = (idx_t * (1 << TOKEN_BITS) + jnp.arange(n_tok, dtype=I32)[None, :]).reshape(-1)
    k = jnp.arange(n_rows - n_assign, dtype=I32)
    pad_e, pad_s = k // tm, k % tm
    pad_needed = lookup(tiles_e * tm - counts, pad_e)
    pad_key_e = jnp.where((pad_e < N_EXPERTS) & (pad_s < pad_needed), pad_e, N_EXPERTS)
    keys = lax.sort(jnp.concatenate([keys_real, pad_key_e * (1 << TOKEN_BITS) + tok_mask]),
                    is_stable=False)
    src_tok = jnp.where((keys & tok_mask) == tok_mask, jnp.arange(n_rows, dtype=I32) % n_tok,
                        keys & tok_mask)
    tid = jnp.minimum(jnp.arange(n_rows // tm, dtype=I32), total - 1)
    tile_e = jnp.minimum(jnp.sum((tid[:, None] >= tile_end[None, :]).astype(I32), axis=1), N_EXPERTS - 1)
    later = (expert_ids[None, :] > expert_ids[:, None]) & (tiles_e[None, :] > 0)
    next_e = jnp.min(jnp.where(later, expert_ids[None, :], N_EXPERTS), axis=1)
    nxt = lookup(next_e, tile_e)
    nxt_first_tile = lookup(tile_end - tiles_e, jnp.minimum(nxt, N_EXPERTS - 1))
    bounds = jnp.cumsum(jnp.asarray(part_tiles, I32))
    part_end = jnp.min(jnp.where(bounds[None, :] > tid[:, None], bounds[None, :], n_rows), axis=1)
    tile_next = jnp.where((nxt < N_EXPERTS) & (nxt_first_tile < part_end), nxt, -1)
    return (tile_e.astype(I32), tile_next.astype(I32), total.reshape(1).astype(I32), src_tok.astype(I32),
            pos.reshape(-1).astype(I32))


def _pad_rows(a, rows):
    return jnp.concatenate([a, jnp.zeros((rows - a.shape[0],) + a.shape[1:], a.dtype)], axis=0)


def _pad_lanes(a, lanes=LANES):
    return jnp.concatenate([a, jnp.zeros(a.shape[:-1] + (lanes - a.shape[-1],), a.dtype)], axis=-1)


def kernel(x_prompt, mem_prompt, x_sample, state_conformer_conv, state_gdn_conv, state_gdn, cache_mem_k,
           cache_mem_v, w_in, b_glu, w_dw, b_dw, ln_g, ln_b, w_sc, a_log, dt_bias, g_onorm, w_out, g_mix,
           g_xattn, g_mem, w_xq, w_mk, w_mv, w_xo, g_moe, w_router, b_router, w_gu, b_gu, w_dn, b_dn,
           g_final):
    assert w_in.shape[0] == 1, "single-layer configuration"
    batch, seq, _ = x_prompt.shape
    n_s = x_sample.shape[0]
    n_p = batch * seq
    n_all = n_p + n_s
    assert seq % TOKEN_TILE == 0 and n_p % n_s == 0 and n_all < (1 << TOKEN_BITS) - 1
    assert (n_all * TOP_K) % (SC_CORES * SC_SUBCORES * SUBLANES) == 0

    wts = {
        "g_mix": g_mix[0][None], "g_xattn": g_xattn[0][None], "g_moe": g_moe[0][None],
        "g_onorm": g_onorm[0][None],
        "w_in_b": w_in[0][:, :OFF_A].astype(BF16),
        "w_ab_b": _pad_lanes(w_in[0][:, OFF_A:]).astype(BF16),
        "b_glu": b_glu[0][None],
        "w_dw": _pad_rows(w_dw[0], 32), "b_dw": b_dw[0][None], "ln_g": ln_g[0][None], "ln_b": ln_b[0][None],
        "w_sc": _pad_rows(w_sc[0], 8),
        "gdn_cst": _pad_rows(_pad_lanes(jnp.stack([a_log[0], dt_bias[0]])), 8),
        "w_out_b": w_out[0].astype(BF16), "w_xq_b": w_xq[0].astype(BF16), "w_xo_b": w_xo[0].astype(BF16),
        "w_router": _pad_lanes(w_router[0]), "b_router": _pad_lanes(b_router[0][None]),
    }

    mk, mv, mk_b, mv_b = _mem_kv(mem_prompt.reshape(batch * N_MEM, D_MODEL), g_mem[0][None],
                                 w_mk[0].astype(BF16), w_mv[0].astype(BF16))
    xp = x_prompt.reshape(n_p, D_MODEL)
    conv_p, qkv_p, z_p, gb_p, cstate_p, sstate_p = _pre_prompt(xp, batch, seq, wts)
    o_p, gstate_p = _gdn_prompt(qkv_p, gb_p, wts["w_sc"], batch, seq)

    xs = x_sample.reshape(n_s, D_MODEL)
    conv_s, q_s, k_s, v_s, z_s, gb_s, cstate_s, sstate_s = _pre_sample(
        xs, state_conformer_conv[0], state_gdn_conv[0], wts)
    o_s, gstate_s = _gdn_sample(q_s, k_s, v_s, gb_s, state_gdn[0])
    x1_s, qx_s = _mix_sample(xs, conv_s, o_s, z_s, wts)
    att_s = _attn_sample(qx_s, cache_mem_k[0], cache_mem_v[0])
    x2_s, h3_s, route_s, rt_s, counts_s = _route_sample(x1_s, att_s, wts)

    x2_p, h3r, route_p, rt_p, counts_p = _post_prompt(xp, conv_p, o_p, z_p, mk_b, mv_b, h3_s, batch, seq,
                                                      wts)

    counts_p = counts_p[0, :N_EXPERTS].astype(I32)
    counts_s = counts_s[0, :N_EXPERTS].astype(I32)
    idx_s = rt_s[0:TOP_K].astype(I32)
    rank_s = rt_s[2 * TOP_K:3 * TOP_K].astype(I32) + jnp.sum(
        jnp.where(idx_s[..., None] == jnp.arange(N_EXPERTS, dtype=I32), counts_p, 0), axis=-1)
    idx_t = jnp.concatenate([rt_p[0:TOP_K].astype(I32), idx_s], axis=1)
    rank_t = jnp.concatenate([rt_p[2 * TOP_K:3 * TOP_K].astype(I32), rank_s], axis=1)
    min_tiles = -(-(n_all * TOP_K + N_EXPERTS * (MOE_TILE - 1)) // MOE_TILE)
    part_tiles = [_part_tiles(-(-min_tiles * w // sum(MOE_PART_WEIGHTS))) for w in MOE_PART_WEIGHTS]
    first_tiles = [sum(part_tiles[:k]) for k in range(len(part_tiles))]
    n_rows = sum(part_tiles) * MOE_TILE
    tile_e, tile_next, n_tiles, src_tok, pos = _routing_tables(idx_t, rank_t, counts_p + counts_s, n_rows,
                                                               part_tiles)
    pos_t = pos.reshape(TOP_K, n_all)
    xs_parts = [_sc_gather_rows(h3r, src_tok[f * MOE_TILE:(f + t) * MOE_TILE])
                for f, t in zip(first_tiles, part_tiles)]
    ys = None
    for k, f in enumerate(first_tiles):
        ys = _moe(tile_e, tile_next, n_tiles, xs_parts[k], ys, f, n_rows, w_gu[0],
                  b_gu[0][:, None, :], w_dn[0], b_dn[0][:, None, :])
    gfin = g_final[None]
    assert n_p % (sum(TOKEN_PART_WEIGHTS) * COMBINE_TILE) == 0
    tok_parts = [n_p * w // sum(TOKEN_PART_WEIGHTS) for w in TOKEN_PART_WEIGHTS]
    sample_part = 0 if (n_s % COMBINE_TILE == 0 and tok_parts[0] > n_s) else len(tok_parts) - 1
    if sample_part == 0:
        tok_parts[0] -= n_s
        tok_parts[-1] += n_s
    y_p = None
    for k, n_tok in enumerate(tok_parts):
        tok0 = sum(tok_parts[:k])
        pos_k = pos_t[:, tok0:tok0 + n_tok]
        if k == sample_part:
            pos_k = jnp.concatenate([pos_k, pos_t[:, n_p:]], axis=1)
        ys_tok = _sc_gather_rows(ys, pos_k.reshape(-1)).reshape(TOP_K, pos_k.shape[1], D_MODEL)
        y_p = _combine(x2_p, route_p, gfin, ys_tok, y_p, tok0, n_tok, 0)
        if k == sample_part:
            y_s = _combine(x2_s, route_s, gfin, ys_tok, None, 0, n_s, n_tok // min(COMBINE_TILE, n_s))

    return (y_p.reshape(batch, seq, D_MODEL), y_s.reshape(n_s, 1, D_MODEL),
            cstate_p[None], sstate_p[None], gstate_p[None],
            mk[None], mv[None],
            cstate_s[None], sstate_s[None], gstate_s[None])
```

```python
import functools

import jax
import jax.numpy as jnp
from jax import lax
from jax.experimental import pallas as pl
from jax.experimental.pallas import tpu as pltpu
from jax.experimental.pallas import tpu_sc as plsc

F32, BF16, I32 = jnp.float32, jnp.bfloat16, jnp.int32

D_MODEL = 1024
CONV_CH = 512
CONV_WIDTH = 31
GDN_HEADS = 4
GDN_DK = 128
GDN_V = 512
QKV_CH = 1536
SHORT_CONV = 4
N_MEM = 256
X_HEADS = 4
X_HEAD_DIM = 256
N_EXPERTS = 32
TOP_K = 4
D_EXPERT = 1024
SWIGLU_LIMIT = 7.0
SWIGLU_ALPHA = 1.702
NORM_EPS = 1e-6
OFF_QKV = 2 * CONV_CH
OFF_Z = OFF_QKV + QKV_CH
OFF_A = OFF_Z + GDN_V

LANES = 128
SUBLANES = 8
GDN_BLOCK = 128
TOKEN_TILE = 256
MOE_TILE = 384
MOE_COLS = 256
MOE_PART_WEIGHTS = (3, 6, 8, 9)
TOKEN_PART_WEIGHTS = (1, 2, 2, 3)
COMBINE_TILE = 256
TOKEN_BITS = 15
ROUTE_ROWS = 16
SC_CORES = 2
SC_SUBCORES = 16
VMEM_LIMIT = 48 * 1024 * 1024


def _cparams(sem, vmem=VMEM_LIMIT):
    return pltpu.CompilerParams(dimension_semantics=sem, vmem_limit_bytes=vmem)


def _mm(a, b):
    return jnp.dot(a.astype(BF16), b.astype(BF16), preferred_element_type=F32)


def _mm_nt(a, b):
    return lax.dot_general(a.astype(BF16), b.astype(BF16), (((1,), (1,)), ((), ())),
                           preferred_element_type=F32)


def _mm_tn(a, b):
    return lax.dot_general(a.astype(BF16), b.astype(BF16), (((0,), (0,)), ((), ())),
                           preferred_element_type=F32)


def _rms(x, g):
    return x * lax.rsqrt(jnp.mean(x * x, axis=-1, keepdims=True) + NORM_EPS) * g


def _silu(x):
    return x * jax.nn.sigmoid(x)


PACKED = D_MODEL // 2


def _pack_rows(h):
    lo = pltpu.bitcast(h[:, :PACKED].astype(BF16).astype(F32), jnp.uint32)
    hi = pltpu.bitcast(h[:, PACKED:].astype(BF16).astype(F32), jnp.uint32)
    return (lo >> 16) | (hi & jnp.uint32(0xFFFF0000))


def _unpack_rows(w):
    lo = pltpu.bitcast(w << 16, F32)
    hi = pltpu.bitcast(w & jnp.uint32(0xFFFF0000), F32)
    return jnp.concatenate([lo, hi], axis=1).astype(BF16)


def _full(shape):
    return pl.BlockSpec(shape, lambda *_: (0,) * len(shape))


def _strict_lower(n):
    return jnp.tril(jnp.ones((n, n), BF16), k=-1)


def _project(x, gmix, w_ref, wab_ref, bglu):
    h = _rms(x, gmix).astype(BF16)
    u_glu = jnp.dot(h, w_ref[:, 0:OFF_QKV], preferred_element_type=F32) + bglu
    glu = u_glu[:, :CONV_CH] * jax.nn.sigmoid(u_glu[:, CONV_CH:])
    qkv_pre = jnp.dot(h, w_ref[:, OFF_QKV:OFF_Z], preferred_element_type=F32)
    z = jnp.dot(h, w_ref[:, OFF_Z:OFF_A], preferred_element_type=F32)
    uab = jnp.dot(h, wab_ref[...], preferred_element_type=F32)
    return glu, qkv_pre, z, uab


def _gate_beta(uab, cst):
    lane = lax.broadcasted_iota(I32, uab.shape, 1)
    g = -jnp.exp(cst[0:1, :]) * jax.nn.softplus(uab + cst[1:2, :])
    return jnp.where(lane < GDN_HEADS, g, jax.nn.sigmoid(uab))


def _conv_post(c, b_dw, ln_g, ln_b):
    c = c + b_dw
    mu = jnp.mean(c, axis=-1, keepdims=True)
    d = c - mu
    var = jnp.mean(d * d, axis=-1, keepdims=True)
    return _silu(d * lax.rsqrt(var + NORM_EPS) * ln_g + ln_b)


def _qkv_post(cs):
    a = _silu(cs)
    parts = []
    for h in range(2 * GDN_HEADS):
        seg = a[:, h * GDN_DK:(h + 1) * GDN_DK]
        n = seg * lax.rsqrt(jnp.sum(seg * seg, axis=-1, keepdims=True) + NORM_EPS)
        if h < GDN_HEADS:
            n = n * (GDN_DK ** -0.5)
        parts.append(n)
    q = jnp.concatenate(parts[:GDN_HEADS], axis=1)
    k = jnp.concatenate(parts[GDN_HEADS:], axis=1)
    return q, k, a[:, 2 * GDN_HEADS * GDN_DK:]


def _mix_out(conv_b, o, z, gon, wout_ref, x):
    parts = []
    for h in range(GDN_HEADS):
        oh = o[:, h * 128:(h + 1) * 128]
        parts.append(oh * lax.rsqrt(jnp.mean(oh * oh, axis=-1, keepdims=True) + NORM_EPS) * gon)
    on = jnp.concatenate(parts, axis=1) * _silu(z)
    mixed = (jnp.dot(conv_b, wout_ref[0:CONV_CH, :], preferred_element_type=F32)
             + jnp.dot(on.astype(BF16), wout_ref[CONV_CH:, :], preferred_element_type=F32))
    return x + mixed


def _router_logits(x2, gmoe, wr_ref, br):
    h3 = _rms(x2, gmoe)
    h_hi = h3.astype(BF16)
    r1 = h3 - h_hi.astype(F32)
    h_mid = r1.astype(BF16)
    h_lo = (r1 - h_mid.astype(F32)).astype(BF16)
    w = wr_ref[...]
    w_hi = w.astype(BF16)
    w_lo = (w - w_hi.astype(F32)).astype(BF16)
    logits = (jnp.dot(h_hi, w_hi, preferred_element_type=F32)
              + jnp.dot(h_hi, w_lo, preferred_element_type=F32)
              + jnp.dot(h_mid, w_hi, preferred_element_type=F32)
              + jnp.dot(h_lo, w_hi, preferred_element_type=F32)) + br
    return h3, logits


def _route_finish(logits, carry, before):
    m = logits.shape[0]
    neg = jnp.float32(-jnp.inf)
    n_groups = 4 if m % 32 == 0 else 1
    rows = m // n_groups
    lanes_g = lax.broadcasted_iota(I32, (rows, LANES), 1).astype(F32)
    works = [jnp.where(lanes_g < N_EXPERTS, logits[g * rows:(g + 1) * rows], neg) for g in range(n_groups)]
    vals, idxs = [], []
    for _ in range(TOP_K):
        mxs = [jnp.max(w, axis=-1, keepdims=True) for w in works]
        ixs = [jnp.min(jnp.where(w == mx, lanes_g, float(LANES)), axis=-1, keepdims=True)
               for w, mx in zip(works, mxs)]
        works = [jnp.where(lanes_g == ix, neg, w) for w, ix in zip(works, ixs)]
        vals.append(mxs)
        idxs.append(ixs)
    sels, gates = [], []
    for g in range(n_groups):
        es = [jnp.exp(vals[r][g] - vals[0][g]) for r in range(TOP_K)]
        den = es[0] + es[1] + es[2] + es[3]
        gates.append([e / den for e in es])
        sel_g = jnp.zeros((rows, LANES), F32)
        for r in range(TOP_K):
            sel_g = sel_g + jnp.where(lanes_g == idxs[r][g], 1.0, 0.0)
        sels.append(sel_g)
    sel = jnp.concatenate(sels, axis=0)
    rank_full = jnp.dot(before, sel.astype(BF16), preferred_element_type=F32) + carry
    routes = []
    for g in range(n_groups):
        rank_g = rank_full[g * rows:(g + 1) * rows]
        route_g = jnp.zeros((rows, LANES), F32)
        for r in range(TOP_K):
            rk = jnp.sum(jnp.where(lanes_g == idxs[r][g], rank_g, 0.0), axis=-1, keepdims=True)
            route_g = (route_g + jnp.where(lanes_g == r, idxs[r][g], 0.0)
                       + jnp.where(lanes_g == TOP_K + r, gates[g][r], 0.0)
                       + jnp.where(lanes_g == 2 * TOP_K + r, rk, 0.0))
        routes.append(route_g)
    new_carry = carry + jnp.sum(sel, axis=0, keepdims=True)
    return jnp.concatenate(routes, axis=0), new_carry


def _mem_kv_kernel(mem_ref, g_ref, wk_ref, wv_ref, mk_ref, mv_ref, mkb_ref, mvb_ref):
    m = _rms(mem_ref[...], g_ref[...]).astype(BF16)
    mk = jnp.dot(m, wk_ref[...], preferred_element_type=F32)
    mv = jnp.dot(m, wv_ref[...], preferred_element_type=F32)
    for h in range(X_HEADS):
        sl = slice(h * X_HEAD_DIM, (h + 1) * X_HEAD_DIM)
        mk_ref[0, :, h, :] = mk[:, sl]
        mv_ref[0, :, h, :] = mv[:, sl]
    mkb_ref[...] = mk.astype(BF16)
    mvb_ref[...] = mv.astype(BF16)


def _mem_kv(mem, g_mem, wk_b, wv_b):
    rows = mem.shape[0]
    tm = N_MEM
    row_spec = pl.BlockSpec((tm, D_MODEL), lambda i: (i, 0))
    head_spec = pl.BlockSpec((1, tm, X_HEADS, X_HEAD_DIM), lambda i: (i, 0, 0, 0))
    head_shape = jax.ShapeDtypeStruct((rows // tm, tm, X_HEADS, X_HEAD_DIM), F32)
    return pl.pallas_call(
        _mem_kv_kernel,
        grid=(rows // tm,),
        in_specs=[row_spec, _full((1, D_MODEL)), _full((D_MODEL, D_MODEL)), _full((D_MODEL, D_MODEL))],
        out_specs=[head_spec, head_spec, row_spec, row_spec],
        out_shape=[head_shape, head_shape] + [jax.ShapeDtypeStruct((rows, D_MODEL), BF16)] * 2,
        compiler_params=_cparams(("arbitrary",)),
        name="mem_kv",
    )(mem, g_mem, wk_b, wv_b)


CONV_HALO = 32
SC_HALO = 8


def _pre_prompt_kernel(x_ref, gmix_ref, w_ref, wab_ref, bglu_ref, wdw_ref, bdw_ref, lng_ref, lnb_ref,
                       cst_ref, conv_ref, qkv_ref, z_ref, gb_ref, cstate_ref, sstate_ref, cbuf, *, tm):
    j = pl.program_id(1)

    @pl.when(j == 0)
    def _():
        cbuf[0:CONV_HALO, :] = jnp.zeros((CONV_HALO, CONV_CH), F32)

    glu, qkv_pre, z, uab = _project(x_ref[...], gmix_ref[...], w_ref, wab_ref, bglu_ref[...])
    cbuf[CONV_HALO:CONV_HALO + tm, :] = glu
    qkv_ref[...] = qkv_pre
    z_ref[...] = z
    gb_ref[...] = _gate_beta(uab, cst_ref[...])

    base = CONV_HALO - (CONV_WIDTH - 1)
    rows = CONV_HALO + tm
    accs = []
    for c in range(CONV_CH // LANES):
        lanes = slice(c * LANES, (c + 1) * LANES)
        block = cbuf[:, lanes]
        acc = None
        for r in range(SUBLANES):
            shifted = block if r == 0 else pltpu.roll(block, rows - r, 0)
            for a in range(base, base + CONV_WIDTH):
                if a % SUBLANES == r:
                    t = a - base
                    term = wdw_ref[t:t + 1, lanes] * shifted[a - r:a - r + tm, :]
                    acc = term if acc is None else acc + term
        accs.append(acc)
    acc = jnp.concatenate(accs, axis=1)
    conv_ref[...] = _conv_post(acc, bdw_ref[...], lng_ref[...], lnb_ref[...]).astype(BF16)

    @pl.when(j == pl.num_programs(1) - 1)
    def _():
        cstate_ref[0] = cbuf[pl.ds(CONV_HALO + tm - (CONV_WIDTH - 1), CONV_WIDTH - 1), :]
        sstate_ref[0] = qkv_pre[tm - (SHORT_CONV - 1):, :]

    cbuf[0:CONV_HALO, :] = cbuf[tm:tm + CONV_HALO, :]


def _pre_prompt(x2d, batch, seq, wts):
    tm = TOKEN_TILE
    nj = seq // tm
    rows = batch * seq

    def tok(width):
        return pl.BlockSpec((tm, width), lambda b, j: (b * nj + j, 0))

    in_specs = [tok(D_MODEL), _full((1, D_MODEL)), _full((D_MODEL, OFF_A)), _full((D_MODEL, LANES)),
                _full((1, OFF_QKV)), _full((32, CONV_CH)), _full((1, CONV_CH)), _full((1, CONV_CH)),
                _full((1, CONV_CH)), _full((8, LANES))]
    out_specs = [tok(CONV_CH), tok(QKV_CH), tok(GDN_V), tok(LANES),
                 pl.BlockSpec((1, CONV_WIDTH - 1, CONV_CH), lambda b, j: (b, 0, 0)),
                 pl.BlockSpec((1, SHORT_CONV - 1, QKV_CH), lambda b, j: (b, 0, 0))]
    out_shape = [jax.ShapeDtypeStruct((rows, CONV_CH), BF16),
                 jax.ShapeDtypeStruct((rows, QKV_CH), F32),
                 jax.ShapeDtypeStruct((rows, GDN_V), F32),
                 jax.ShapeDtypeStruct((rows, LANES), F32),
                 jax.ShapeDtypeStruct((batch, CONV_WIDTH - 1, CONV_CH), F32),
                 jax.ShapeDtypeStruct((batch, SHORT_CONV - 1, QKV_CH), F32)]
    return pl.pallas_call(
        functools.partial(_pre_prompt_kernel, tm=tm),
        grid=(batch, nj),
        in_specs=in_specs,
        out_specs=out_specs,
        out_shape=out_shape,
        scratch_shapes=[pltpu.VMEM((CONV_HALO + tm, CONV_CH), F32)],
        compiler_params=_cparams(("arbitrary", "arbitrary")),
        name="pre_prompt",
    )(x2d, wts["g_mix"], wts["w_in_b"], wts["w_ab_b"], wts["b_glu"], wts["w_dw"], wts["b_dw"],
      wts["ln_g"], wts["ln_b"], wts["gdn_cst"])


GDN_SEQS = 4


def _gdn_prompt_kernel(qkv_ref, gb_ref, wsc_ref, o_ref, sfin_ref, s_scr, sbuf):
    c = pl.program_id(1)
    n = GDN_BLOCK
    seqs = range(GDN_SEQS)

    @pl.when(c == 0)
    def _():
        s_scr[...] = jnp.zeros(s_scr.shape, F32)
        sbuf[:, 0:SC_HALO, :] = jnp.zeros((GDN_SEQS, SC_HALO, QKV_CH), F32)

    sbase = SC_HALO - (SHORT_CONV - 1)
    qkvs = []
    for b in seqs:
        sbuf[b, SC_HALO:SC_HALO + n, :] = qkv_ref[b]
        block = sbuf[b]
        cs = None
        for t in range(SHORT_CONV):
            r = (sbase + t) % SUBLANES
            shifted = block if r == 0 else pltpu.roll(block, SC_HALO + n - r, 0)
            term = wsc_ref[t:t + 1, :] * shifted[sbase + t - r:sbase + t - r + n, :]
            cs = term if cs is None else cs + term
        qkvs.append(_qkv_post(cs))
        sbuf[b, 0:SC_HALO, :] = sbuf[b, n:n + SC_HALO, :]

    row = lax.broadcasted_iota(I32, (n, n), 0)
    col = lax.broadcasted_iota(I32, (n, n), 1)
    causal = row >= col
    strict = row > col
    tri = jnp.where(causal, 1.0, 0.0).astype(BF16)
    eye = jnp.where(row == col, 1.0, 0.0)
    level_masks = []
    b = 1
    while b < n:
        same_pair = ((row ^ col) & ~(2 * b - 1)) == 0
        level_masks.append(same_pair & ((row & b) != 0) & ((col & b) == 0))
        b *= 2
    gbs, gcums, gcum_ts, egcs = [], [], [], []
    for b in seqs:
        gb = gb_ref[b]
        g1 = gb.astype(BF16)
        r1 = gb - g1.astype(F32)
        g2 = r1.astype(BF16)
        g3 = (r1 - g2.astype(F32)).astype(BF16)
        gcum = (jnp.dot(tri, g1, preferred_element_type=F32) + jnp.dot(tri, g2, preferred_element_type=F32)
                + jnp.dot(tri, g3, preferred_element_type=F32))
        gbs.append(gb)
        gcums.append(gcum)
        gcum_ts.append(gcum.T)
        egcs.append(jnp.exp(gcum))
    units = [(b, h) for b in seqs for h in range(GDN_HEADS)]
    idx = range(len(units))
    sls = [slice(h * GDN_DK, (h + 1) * GDN_DK) for h in range(GDN_HEADS)]
    qs = [qkvs[b][0][:, sls[h]] for b, h in units]
    ks = [qkvs[b][1][:, sls[h]] for b, h in units]
    vs = [qkvs[b][2][:, sls[h]] for b, h in units]
    ss = [s_scr[b, h] for b, h in units]
    gcols = [gcums[b][:, h:h + 1] for b, h in units]
    ecols = [egcs[b][:, h:h + 1] for b, h in units]
    betas = [gbs[b][:, GDN_HEADS + h:GDN_HEADS + h + 1] for b, h in units]
    glasts = [gcums[b][n - 1:n, h:h + 1] for b, h in units]
    decays = [jnp.where(causal, jnp.exp(jnp.where(causal, gcols[u] - gcum_ts[b][h:h + 1, :], 0.0)), 0.0)
              for u, (b, h) in enumerate(units)]
    kbs = [ks[u] * betas[u] for u in idx]
    lowers = [jnp.where(strict, _mm_nt(kbs[u], ks[u]) * decays[u], 0.0) for u in idx]
    intras = [jnp.where(causal, _mm_nt(qs[u], ks[u]) * decays[u], 0.0) for u in idx]
    xs = [eye - jnp.where(level_masks[0], lowers[u], 0.0) for u in idx]
    for mask in level_masks[1:]:
        ts = [_mm(xs[u], jnp.where(mask, lowers[u], 0.0)) for u in idx]
        xs = [xs[u] - _mm(ts[u], xs[u]) for u in idx]
    us = [_mm(xs[u], vs[u] * betas[u]) for u in idx]
    ws = [_mm(xs[u], kbs[u] * ecols[u]) for u in idx]
    v_news = [us[u] - _mm(ws[u], ss[u]) for u in idx]
    os_ = [_mm(qs[u] * ecols[u], ss[u]) + _mm(intras[u], v_news[u]) for u in idx]
    s_news = [ss[u] * jnp.exp(glasts[u]) + _mm_tn(ks[u] * jnp.exp(glasts[u] - gcols[u]), v_news[u])
              for u in idx]
    for u, (b, h) in enumerate(units):
        o_ref[b, :, sls[h]] = os_[u]
        s_scr[b, h] = s_news[u]

    @pl.when(c == pl.num_programs(1) - 1)
    def _():
        sfin_ref[...] = s_scr[...]


def _gdn_prompt(qkv, gb, w_sc, batch, seq):
    n = GDN_BLOCK
    nc = seq // n
    g = GDN_SEQS
    assert batch % g == 0

    def tok(width):
        return pl.BlockSpec((g, n, width), lambda b, c: (b, c, 0))

    state_shape = (g, GDN_HEADS, GDN_DK, GDN_DK)
    o, s_fin = pl.pallas_call(
        _gdn_prompt_kernel,
        grid=(batch // g, nc),
        in_specs=[tok(QKV_CH), tok(LANES), _full(w_sc.shape)],
        out_specs=[tok(GDN_V), pl.BlockSpec(state_shape, lambda b, c: (b, 0, 0, 0))],
        out_shape=[jax.ShapeDtypeStruct((batch, seq, GDN_V), F32),
                   jax.ShapeDtypeStruct((batch, GDN_HEADS, GDN_DK, GDN_DK), F32)],
        scratch_shapes=[pltpu.VMEM(state_shape, F32), pltpu.VMEM((g, SC_HALO + n, QKV_CH), F32)],
        compiler_params=_cparams(("arbitrary", "arbitrary")),
        name="gdn_prompt",
    )(qkv.reshape(batch, seq, QKV_CH), gb.reshape(batch, seq, LANES), w_sc)
    return o.reshape(batch * seq, GDN_V), s_fin


def _post_prompt_kernel(x_ref, conv_ref, o_ref, z_ref, gon_ref, wout_ref, gx_ref, wq_ref, mk_ref, mv_ref,
                        wo_ref, gmoe_ref, wr_ref, br_ref, tri_ref, h3s_ref, x2_ref, h3r_ref, route_ref, rt_ref,
                        cnt_ref, carry, logit_buf, *, n_steps):
    step = pl.program_id(0)

    @pl.when(step == 0)
    def _():
        carry[...] = jnp.zeros(carry.shape, F32)
        logit_buf[...] = jnp.zeros(logit_buf.shape, F32)

    x1 = _mix_out(conv_ref[...], o_ref[...], z_ref[...], gon_ref[...], wout_ref, x_ref[...])
    qx = jnp.dot(_rms(x1, gx_ref[...]).astype(BF16), wq_ref[...], preferred_element_type=F32)

    route, new_carry = _route_finish(logit_buf[...], carry[0:1, :], tri_ref[...])
    route_ref[...] = route
    rt_ref[...] = route.T[0:ROUTE_ROWS, :]
    kept = jnp.where(step >= 1, new_carry, carry[0:1, :])
    carry[0:1, :] = kept
    cnt_ref[...] = jnp.broadcast_to(kept, cnt_ref.shape)

    sls = [slice(h * X_HEAD_DIM, (h + 1) * X_HEAD_DIM) for h in range(X_HEADS)]
    qb = qx.astype(BF16)
    ss = [lax.dot_general(qb[:, sl], mk_ref[:, sl], (((1,), (1,)), ((), ())),
                          preferred_element_type=F32) * (X_HEAD_DIM ** -0.5) for sl in sls]
    es = [jnp.exp(s - jnp.max(s, axis=-1, keepdims=True)) for s in ss]
    ps = [(e / jnp.sum(e, axis=-1, keepdims=True)).astype(BF16) for e in es]
    att = jnp.concatenate(
        [jnp.dot(p, mv_ref[:, sl], preferred_element_type=F32) for p, sl in zip(ps, sls)], axis=1)
    x2 = x1 + jnp.dot(att.astype(BF16), wo_ref[...], preferred_element_type=F32)
    h3, logits = _router_logits(x2, gmoe_ref[...], wr_ref, br_ref[...])
    logit_buf[...] = logits

    @pl.when(step < n_steps)
    def _():
        x2_ref[...] = x2
        h3r_ref[...] = _pack_rows(h3)

    @pl.when(step == n_steps)
    def _():
        h3r_ref[0:h3s_ref.shape[0], :] = h3s_ref[...]


def _post_prompt(x2d, conv, o, z, mk_b, mv_b, h3_sample, batch, seq, wts):
    tm = TOKEN_TILE
    nj = seq // tm
    rows = batch * seq
    n_steps = batch * nj
    n_s = h3_sample.shape[0]
    assert n_s <= tm

    def tok(width):
        return pl.BlockSpec((tm, width), lambda s: (jnp.minimum(s, n_steps - 1), 0))

    mem_spec = pl.BlockSpec((N_MEM, D_MODEL), lambda s: (jnp.minimum(s, n_steps - 1) // nj, 0))
    sq = _full((D_MODEL, D_MODEL))
    in_specs = [tok(D_MODEL), tok(CONV_CH), tok(GDN_V), tok(GDN_V), _full((1, GDN_DK)), sq,
                _full((1, D_MODEL)), sq, mem_spec, mem_spec, sq, _full((1, D_MODEL)),
                _full((D_MODEL, LANES)), _full((1, LANES)), _full((tm, tm)), _full(h3_sample.shape)]
    out_specs = [tok(D_MODEL),
                 pl.BlockSpec((tm, PACKED), lambda s: (s, 0)),
                 pl.BlockSpec((tm, LANES), lambda s: (jnp.maximum(s - 1, 0), 0)),
                 pl.BlockSpec((ROUTE_ROWS, tm), lambda s: (0, jnp.maximum(s - 1, 0))),
                 _full((SUBLANES, LANES))]
    out_shape = [jax.ShapeDtypeStruct((rows, D_MODEL), F32),
                 jax.ShapeDtypeStruct((rows + n_s, PACKED), jnp.uint32),
                 jax.ShapeDtypeStruct((rows, LANES), F32),
                 jax.ShapeDtypeStruct((ROUTE_ROWS, rows), F32),
                 jax.ShapeDtypeStruct((SUBLANES, LANES), F32)]
    return pl.pallas_call(
        functools.partial(_post_prompt_kernel, n_steps=n_steps),
        grid=(n_steps + 1,),
        in_specs=in_specs,
        out_specs=out_specs,
        out_shape=out_shape,
        scratch_shapes=[pltpu.VMEM((SUBLANES, LANES), F32), pltpu.VMEM((tm, LANES), F32)],
        compiler_params=_cparams(("arbitrary",)),
        name="post_prompt",
    )(x2d, conv, o, z, wts["g_onorm"], wts["w_out_b"], wts["g_xattn"], wts["w_xq_b"], mk_b, mv_b,
      wts["w_xo_b"], wts["g_moe"], wts["w_router"], wts["b_router"], _strict_lower(tm), h3_sample)


def _pre_sample_kernel(x_ref, gmix_ref, w_ref, wab_ref, bglu_ref, wdw_ref, bdw_ref, lng_ref, lnb_ref,
                       wsc_ref, cst_ref, chist_ref, shist_ref, conv_ref, q_ref, k_ref, v_ref, z_ref,
                       gb_ref, cnew_ref, snew_ref):
    glu, qkv_pre, z, uab = _project(x_ref[...], gmix_ref[...], w_ref, wab_ref, bglu_ref[...])
    z_ref[...] = z
    gb_ref[...] = _gate_beta(uab, cst_ref[...])
    kw = CONV_WIDTH
    acc = wdw_ref[kw - 1:kw, :] * glu
    for t in range(kw - 1):
        row = chist_ref[:, t, :]
        acc = acc + wdw_ref[t:t + 1, :] * row
        if t >= 1:
            cnew_ref[:, t - 1, :] = row
    cnew_ref[:, kw - 2, :] = glu
    conv_ref[...] = _conv_post(acc, bdw_ref[...], lng_ref[...], lnb_ref[...]).astype(BF16)
    ks = SHORT_CONV
    cs = wsc_ref[ks - 1:ks, :] * qkv_pre
    for t in range(ks - 1):
        row = shist_ref[:, t, :]
        cs = cs + wsc_ref[t:t + 1, :] * row
        if t >= 1:
            snew_ref[:, t - 1, :] = row
    snew_ref[:, ks - 2, :] = qkv_pre
    q, k, v = _qkv_post(cs)
    q_ref[...] = q
    k_ref[...] = k
    v_ref[...] = v


PRE_SAMPLE_TOKENS = 32


def _pre_sample(xs, chist, shist, wts):
    n = xs.shape[0]
    tb = min(PRE_SAMPLE_TOKENS, n)

    def tok(width):
        return pl.BlockSpec((tb, width), lambda i: (i, 0))

    def hist(a):
        return pl.BlockSpec((tb,) + a.shape[1:], lambda i: (i, 0, 0))

    consts = (wts["g_mix"], wts["w_in_b"], wts["w_ab_b"], wts["b_glu"], wts["w_dw"], wts["b_dw"],
              wts["ln_g"], wts["ln_b"], wts["w_sc"], wts["gdn_cst"])
    return pl.pallas_call(
        _pre_sample_kernel,
        grid=(n // tb,),
        in_specs=[tok(D_MODEL)] + [_full(a.shape) for a in consts] + [hist(chist), hist(shist)],
        out_specs=[tok(CONV_CH), tok(GDN_V), tok(GDN_V), tok(GDN_V), tok(GDN_V), tok(LANES),
                   hist(chist), hist(shist)],
        out_shape=[jax.ShapeDtypeStruct((n, CONV_CH), BF16)]
        + [jax.ShapeDtypeStruct((n, GDN_V), F32)] * 4
        + [jax.ShapeDtypeStruct((n, LANES), F32), jax.ShapeDtypeStruct(chist.shape, F32),
           jax.ShapeDtypeStruct(shist.shape, F32)],
        compiler_params=_cparams(("arbitrary",)),
        name="pre_sample",
    )(xs, *consts, chist, shist)


GDN_STEP_TOKENS = 8


def _gdn_sample_kernel(q_ref, k_ref, v_ref, gb_ref, s_ref, o_ref, snew_ref):
    n = GDN_DK
    for i in range(GDN_STEP_TOKENS):
        for h in range(GDN_HEADS):
            sl = slice(h * GDN_DK, (h + 1) * GDN_DK)
            qrow = q_ref[i:i + 1, sl]
            krow = k_ref[i:i + 1, sl]
            vrow = v_ref[i:i + 1, sl]
            g = gb_ref[i:i + 1, h:h + 1]
            beta = gb_ref[i:i + 1, GDN_HEADS + h:GDN_HEADS + h + 1]
            kcol = jnp.broadcast_to(krow, (n, n)).T
            qcol = jnp.broadcast_to(qrow, (n, n)).T
            s1 = s_ref[i, h] * jnp.exp(g)
            sk = jnp.sum(s1 * kcol, axis=0, keepdims=True)
            vt = (vrow - sk) * beta
            s2 = s1 + kcol * vt
            snew_ref[i, h] = s2
            o_ref[i:i + 1, sl] = jnp.sum(s2 * qcol, axis=0, keepdims=True)


def _gdn_sample(q, k, v, gb, state):
    n = q.shape[0]
    tb = GDN_STEP_TOKENS

    def tok(width):
        return pl.BlockSpec((tb, width), lambda i: (i, 0))

    st = pl.BlockSpec((tb, GDN_HEADS, GDN_DK, GDN_DK), lambda i: (i, 0, 0, 0))
    return pl.pallas_call(
        _gdn_sample_kernel,
        grid=(n // tb,),
        in_specs=[tok(GDN_V), tok(GDN_V), tok(GDN_V), tok(LANES), st],
        out_specs=[tok(GDN_V), st],
        out_shape=[jax.ShapeDtypeStruct((n, GDN_V), F32), jax.ShapeDtypeStruct(state.shape, F32)],
        compiler_params=_cparams(("arbitrary",)),
        name="gdn_sample",
    )(q, k, v, gb, state)


def _mix_sample_kernel(x_ref, conv_ref, o_ref, z_ref, gon_ref, wout_ref, gx_ref, wq_ref, x1_ref, qx_ref):
    x1 = _mix_out(conv_ref[...], o_ref[...], z_ref[...], gon_ref[...], wout_ref, x_ref[...])
    x1_ref[...] = x1
    qx_ref[...] = jnp.dot(_rms(x1, gx_ref[...]).astype(BF16), wq_ref[...], preferred_element_type=F32)


def _mix_sample(xs, conv, o, z, wts):
    n = xs.shape[0]
    in_arrays = (xs, conv, o, z, wts["g_onorm"], wts["w_out_b"], wts["g_xattn"], wts["w_xq_b"])
    return pl.pallas_call(
        _mix_sample_kernel,
        grid=(1,),
        in_specs=[_full(a.shape) for a in in_arrays],
        out_specs=[_full((n, D_MODEL))] * 2,
        out_shape=[jax.ShapeDtypeStruct((n, D_MODEL), F32)] * 2,
        compiler_params=_cparams(("arbitrary",)),
        name="mix_sample",
    )(*in_arrays)


ATTN_STEP_TOKENS = 4


def _attn_sample_kernel(qx_ref, ck_ref, cv_ref, att_ref):
    for i in range(ATTN_STEP_TOKENS):
        parts = []
        for h in range(X_HEADS):
            sl = slice(h * X_HEAD_DIM, (h + 1) * X_HEAD_DIM)
            prod = ck_ref[i, :, h, :] * qx_ref[0, i:i + 1, sl]
            s = jnp.sum(prod, axis=-1, keepdims=True) * (X_HEAD_DIM ** -0.5)
            e = jnp.exp(s - jnp.max(s, axis=0, keepdims=True))
            p = e / jnp.sum(e, axis=0, keepdims=True)
            parts.append(jnp.sum(p * cv_ref[i, :, h, :], axis=0, keepdims=True))
        att_ref[0, i:i + 1, :] = jnp.concatenate(parts, axis=1)


def _attn_sample(qx, ck, cv):
    n = qx.shape[0]
    tb = ATTN_STEP_TOKENS
    q3 = qx.reshape(n // tb, tb, D_MODEL)
    qspec = pl.BlockSpec((1, tb, D_MODEL), lambda i: (i, 0, 0))
    cspec = pl.BlockSpec((tb, N_MEM, X_HEADS, X_HEAD_DIM), lambda i: (i, 0, 0, 0))
    out = pl.pallas_call(
        _attn_sample_kernel,
        grid=(n // tb,),
        in_specs=[qspec, cspec, cspec],
        out_specs=qspec,
        out_shape=jax.ShapeDtypeStruct(q3.shape, F32),
        compiler_params=_cparams(("arbitrary",)),
        name="attn_sample",
    )(q3, ck, cv)
    return out.reshape(n, D_MODEL)


def _route_sample_kernel(x1_ref, att_ref, wo_ref, gmoe_ref, wr_ref, br_ref, tri_ref, x2_ref, h3r_ref, route_ref,
                         rt_ref, cnt_ref):
    x2 = x1_ref[...] + jnp.dot(att_ref[...].astype(BF16), wo_ref[...], preferred_element_type=F32)
    x2_ref[...] = x2
    h3, logits = _router_logits(x2, gmoe_ref[...], wr_ref, br_ref[...])
    route, counts = _route_finish(logits, jnp.zeros((1, LANES), F32), tri_ref[...])
    h3r_ref[...] = _pack_rows(h3)
    route_ref[...] = route
    rt_ref[...] = route.T[0:ROUTE_ROWS, :]
    cnt_ref[...] = jnp.broadcast_to(counts, cnt_ref.shape)


def _route_sample(x1, att, wts):
    n = x1.shape[0]
    in_arrays = (x1, att, wts["w_xo_b"], wts["g_moe"], wts["w_router"], wts["b_router"],
                 _strict_lower(n))
    shapes = [(n, D_MODEL), (n, PACKED), (n, LANES), (ROUTE_ROWS, n), (SUBLANES, LANES)]
    dtypes = [F32, jnp.uint32, F32, F32, F32]
    return pl.pallas_call(
        _route_sample_kernel,
        grid=(1,),
        in_specs=[_full(a.shape) for a in in_arrays],
        out_specs=[_full(s) for s in shapes],
        out_shape=[jax.ShapeDtypeStruct(s, d) for s, d in zip(shapes, dtypes)],
        compiler_params=_cparams(("arbitrary",)),
        name="route_sample",
    )(*in_arrays)


def _part_tiles(tiles):
    n_workers = SC_CORES * SC_SUBCORES
    while True:
        rows = tiles * MOE_TILE
        if rows % (n_workers * SUBLANES) == 0 and any(
                (rows // n_workers) % c == 0 for c in range(64, 24, -SUBLANES)):
            return tiles
        tiles += 1


def _sc_chunk(rows_per_worker):
    for c in range(64, 0, -SUBLANES):
        if rows_per_worker % c == 0:
            return c
    raise ValueError(rows_per_worker)


def _sc_gather_rows(table, idx):
    n_workers = SC_CORES * SC_SUBCORES
    b = idx.shape[0]
    assert b % (n_workers * SUBLANES) == 0
    per_worker = b // n_workers
    chunk = _sc_chunk(per_worker)
    row_shape = table.shape[1:]
    mesh = plsc.VectorSubcoreMesh(core_axis_name="c", subcore_axis_name="s")

    @functools.partial(
        pl.kernel, mesh=mesh,
        out_type=jax.ShapeDtypeStruct((b,) + row_shape, table.dtype),
        scratch_types=[pltpu.VMEM((chunk,), I32), pltpu.VMEM((chunk,) + row_shape, table.dtype),
                       pltpu.SemaphoreType.DMA],
        name="sc_gather_rows",
    )
    def gather(table_hbm, idx_hbm, out_hbm, idx_v, rows_v, sem):
        worker = lax.axis_index("s") * SC_CORES + lax.axis_index("c")
        base = worker * per_worker

        @pl.loop(0, per_worker // chunk)
        def _(c):
            off = pl.multiple_of(base + c * chunk, SUBLANES)
            pltpu.sync_copy(idx_hbm.at[pl.ds(off, chunk)], idx_v)
            pltpu.async_copy(table_hbm.at[idx_v], rows_v, sem).wait()
            pltpu.sync_copy(rows_v, out_hbm.at[pl.ds(off, chunk)])

    return gather(table, idx)


def _moe_kernel(te_ref, tn_ref, nt_ref, xs_ref, wgu_hbm, bgu_ref, wdn_hbm, bdn_ref, *rest, first_tile):
    ys_ref, wgu_f, wdn_f, wgu_b, wdn_b, sems = rest[-6:]
    step = pl.program_id(0)
    i = first_tile + step
    total = nt_ref[0]

    def weight_copies(e):
        return (pltpu.make_async_copy(wgu_hbm.at[e], wgu_f, sems.at[0]),
                pltpu.make_async_copy(wdn_hbm.at[e], wdn_f, sems.at[1]))

    def start(e):
        for cp in weight_copies(e):
            cp.start()

    @pl.when(i < total)
    def _():
        expert = te_ref[i]
        prev = te_ref[jnp.maximum(i - 1, 0)]
        fresh = jnp.logical_or(step == 0, expert != prev)

        @pl.when(step == 0)
        def _():
            start(expert)

        @pl.when(fresh)
        def _():
            for cp in weight_copies(expert):
                cp.wait()
            wgu_b[...] = wgu_f[...].astype(BF16)
            wdn_b[...] = wdn_f[...].astype(BF16)
            nxt = tn_ref[i]

            @pl.when(nxt >= 0)
            def _():
                start(nxt)

        x = _unpack_rows(xs_ref[...])

        def up(c):
            glu_cols = slice(c * MOE_COLS, (c + 1) * MOE_COLS)
            lin_cols = slice(D_EXPERT + c * MOE_COLS, D_EXPERT + (c + 1) * MOE_COLS)
            return (jnp.dot(x, wgu_b[:, glu_cols], preferred_element_type=F32) + bgu_ref[0, :, glu_cols],
                    jnp.dot(x, wgu_b[:, lin_cols], preferred_element_type=F32) + bgu_ref[0, :, lin_cols])

        n_chunks = D_EXPERT // MOE_COLS
        nxt = up(0)
        y = None
        for c in range(n_chunks):
            g, lin = nxt
            if c + 1 < n_chunks:
                nxt = up(c + 1)
            x_glu = jnp.minimum(g, SWIGLU_LIMIT)
            x_lin = jnp.clip(lin, -SWIGLU_LIMIT, SWIGLU_LIMIT)
            act = x_glu * jax.nn.sigmoid(SWIGLU_ALPHA * x_glu) * (x_lin + 1.0)
            part = jnp.dot(act.astype(BF16), wdn_b[c * MOE_COLS:(c + 1) * MOE_COLS, :],
                           preferred_element_type=F32)
            y = part if y is None else y + part
        ys_ref[...] = y + bdn_ref[0]

    @pl.when(i >= total)
    def _():
        ys_ref[...] = jnp.zeros(ys_ref.shape, F32)


def _moe(tile_e, tile_next, n_tiles, xs_part, ys_prev, first_tile, n_rows, w_gu, b_gu, w_dn, b_dn):
    tm = MOE_TILE

    def bias(shape):
        return pl.BlockSpec(shape, lambda i, te, tn, nt: (te[first_tile + i], 0, 0))

    hbm = pl.BlockSpec(memory_space=pl.ANY)
    in_specs = [pl.BlockSpec((tm, PACKED), lambda i, te, tn, nt: (i, 0)),
                hbm, bias((1, 1, 2 * D_EXPERT)), hbm, bias((1, 1, D_MODEL))]
    operands = [tile_e, tile_next, n_tiles, xs_part, w_gu, b_gu, w_dn, b_dn]
    aliases = {}
    if ys_prev is not None:
        in_specs.append(hbm)
        aliases = {len(operands): 0}
        operands.append(ys_prev)
    grid_spec = pltpu.PrefetchScalarGridSpec(
        num_scalar_prefetch=3,
        grid=(xs_part.shape[0] // tm,),
        in_specs=in_specs,
        out_specs=pl.BlockSpec((tm, D_MODEL), lambda i, te, tn, nt: (first_tile + i, 0)),
        scratch_shapes=[pltpu.VMEM((D_MODEL, 2 * D_EXPERT), F32),
                        pltpu.VMEM((D_EXPERT, D_MODEL), F32),
                        pltpu.VMEM((D_MODEL, 2 * D_EXPERT), BF16),
                        pltpu.VMEM((D_EXPERT, D_MODEL), BF16),
                        pltpu.SemaphoreType.DMA((2,))],
    )
    return pl.pallas_call(
        functools.partial(_moe_kernel, first_tile=first_tile),
        grid_spec=grid_spec,
        out_shape=jax.ShapeDtypeStruct((n_rows, D_MODEL), F32),
        input_output_aliases=aliases,
        compiler_params=_cparams(("arbitrary",), vmem=56 * 1024 * 1024),
        name="moe",
    )(*operands)


def _combine_kernel(x2_ref, route_ref, gfin_ref, yt_ref, *rest):
    y_ref = rest[-1]
    route = route_ref[...]
    acc = x2_ref[...]
    for j in range(TOP_K):
        acc = acc + route[:, TOP_K + j:TOP_K + j + 1] * yt_ref[j]
    y_ref[...] = _rms(acc, gfin_ref[...])


def _combine(x2, route, g_final, ys_tok, y_prev, tok0, n_tok, ys_block0):
    tc = min(COMBINE_TILE, n_tok)
    b0 = tok0 // tc

    def tok(width):
        return pl.BlockSpec((tc, width), lambda i: (b0 + i, 0))

    in_specs = [tok(D_MODEL), tok(LANES), pl.BlockSpec((1, D_MODEL), lambda i: (0, 0)),
                pl.BlockSpec((TOP_K, tc, D_MODEL), lambda i: (0, ys_block0 + i, 0))]
    operands = [x2, route, g_final, ys_tok]
    aliases = {}
    if y_prev is not None:
        in_specs.append(pl.BlockSpec(memory_space=pl.ANY))
        aliases = {len(operands): 0}
        operands.append(y_prev)
    return pl.pallas_call(
        _combine_kernel,
        grid=(n_tok // tc,),
        in_specs=in_specs,
        out_specs=tok(D_MODEL),
        out_shape=jax.ShapeDtypeStruct(x2.shape, F32),
        input_output_aliases=aliases,
        compiler_params=_cparams(("arbitrary",)),
        name="combine",
    )(*operands)


def _routing_tables(idx_t, rank_t, counts, n_rows, part_tiles):
    tm = MOE_TILE
    n_tok = idx_t.shape[1]
    n_assign = TOP_K * n_tok
    tok_mask = (1 << TOKEN_BITS) - 1
    tiles_e = (counts + tm - 1) // tm
    tile_end = jnp.cumsum(tiles_e)
    row_start = (tile_end - tiles_e) * tm
    total = tile_end[-1]
    expert_ids = jnp.arange(N_EXPERTS, dtype=I32)

    def lookup(table, e):
        return jnp.sum(jnp.where(e[..., None] == expert_ids, table, 0), axis=-1)

    pos = lookup(row_start, idx_t) + rank_t
    keys_real = (idx_t * (1 << TOKEN_BITS) + jnp.arange(n_tok, dtype=I32)[None, :]).reshape(-1)
    k = jnp.arange(n_rows - n_assign, dtype=I32)
    pad_e, pad_s = k // tm, k % tm
    pad_needed = lookup(tiles_e * tm - counts, pad_e)
    pad_key_e = jnp.where((pad_e < N_EXPERTS) & (pad_s < pad_needed), pad_e, N_EXPERTS)
    keys = lax.sort(jnp.concatenate([keys_real, pad_key_e * (1 << TOKEN_BITS) + tok_mask]),
                    is_stable=False)
    src_tok = jnp.where((keys & tok_mask) == tok_mask, jnp.arange(n_rows, dtype=I32) % n_tok,
                        keys & tok_mask)
    tid = jnp.minimum(jnp.arange(n_rows // tm, dtype=I32), total - 1)
    tile_e = jnp.minimum(jnp.sum((tid[:, None] >= tile_end[None, :]).astype(I32), axis=1), N_EXPERTS - 1)
    later = (expert_ids[None, :] > expert_ids[:, None]) & (tiles_e[None, :] > 0)
    next_e = jnp.min(jnp.where(later, expert_ids[None, :], N_EXPERTS), axis=1)
    nxt = lookup(next_e, tile_e)
    nxt_first_tile = lookup(tile_end - tiles_e, jnp.minimum(nxt, N_EXPERTS - 1))
    bounds = jnp.cumsum(jnp.asarray(part_tiles, I32))
    part_end = jnp.min(jnp.where(bounds[None, :] > tid[:, None], bounds[None, :], n_rows), axis=1)
    tile_next = jnp.where((nxt < N_EXPERTS) & (nxt_first_tile < part_end), nxt, -1)
    return (tile_e.astype(I32), tile_next.astype(I32), total.reshape(1).astype(I32), src_tok.astype(I32),
            pos.reshape(-1).astype(I32))


def _pad_rows(a, rows):
    return jnp.concatenate([a, jnp.zeros((rows - a.shape[0],) + a.shape[1:], a.dtype)], axis=0)


def _pad_lanes(a, lanes=LANES):
    return jnp.concatenate([a, jnp.zeros(a.shape[:-1] + (lanes - a.shape[-1],), a.dtype)], axis=-1)


def kernel(x_prompt, mem_prompt, x_sample, state_conformer_conv, state_gdn_conv, state_gdn, cache_mem_k,
           cache_mem_v, w_in, b_glu, w_dw, b_dw, ln_g, ln_b, w_sc, a_log, dt_bias, g_onorm, w_out, g_mix,
           g_xattn, g_mem, w_xq, w_mk, w_mv, w_xo, g_moe, w_router, b_router, w_gu, b_gu, w_dn, b_dn,
           g_final):
    assert w_in.shape[0] == 1, "single-layer configuration"
    batch, seq, _ = x_prompt.shape
    n_s = x_sample.shape[0]
    n_p = batch * seq
    n_all = n_p + n_s
    assert seq % TOKEN_TILE == 0 and n_p % n_s == 0 and n_all < (1 << TOKEN_BITS) - 1
    assert (n_all * TOP_K) % (SC_CORES * SC_SUBCORES * SUBLANES) == 0

    wts = {
        "g_mix": g_mix[0][None], "g_xattn": g_xattn[0][None], "g_moe": g_moe[0][None],
        "g_onorm": g_onorm[0][None],
        "w_in_b": w_in[0][:, :OFF_A].astype(BF16),
        "w_ab_b": _pad_lanes(w_in[0][:, OFF_A:]).astype(BF16),
        "b_glu": b_glu[0][None],
        "w_dw": _pad_rows(w_dw[0], 32), "b_dw": b_dw[0][None], "ln_g": ln_g[0][None], "ln_b": ln_b[0][None],
        "w_sc": _pad_rows(w_sc[0], 8),
        "gdn_cst": _pad_rows(_pad_lanes(jnp.stack([a_log[0], dt_bias[0]])), 8),
        "w_out_b": w_out[0].astype(BF16), "w_xq_b": w_xq[0].astype(BF16), "w_xo_b": w_xo[0].astype(BF16),
        "w_router": _pad_lanes(w_router[0]), "b_router": _pad_lanes(b_router[0][None]),
    }

    mk, mv, mk_b, mv_b = _mem_kv(mem_prompt.reshape(batch * N_MEM, D_MODEL), g_mem[0][None],
                                 w_mk[0].astype(BF16), w_mv[0].astype(BF16))
    xp = x_prompt.reshape(n_p, D_MODEL)
    conv_p, qkv_p, z_p, gb_p, cstate_p, sstate_p = _pre_prompt(xp, batch, seq, wts)
    o_p, gstate_p = _gdn_prompt(qkv_p, gb_p, wts["w_sc"], batch, seq)

    xs = x_sample.reshape(n_s, D_MODEL)
    conv_s, q_s, k_s, v_s, z_s, gb_s, cstate_s, sstate_s = _pre_sample(
        xs, state_conformer_conv[0], state_gdn_conv[0], wts)
    o_s, gstate_s = _gdn_sample(q_s, k_s, v_s, gb_s, state_gdn[0])
    x1_s, qx_s = _mix_sample(xs, conv_s, o_s, z_s, wts)
    att_s = _attn_sample(qx_s, cache_mem_k[0], cache_mem_v[0])
    x2_s, h3_s, route_s, rt_s, counts_s = _route_sample(x1_s, att_s, wts)

    x2_p, h3r, route_p, rt_p, counts_p = _post_prompt(xp, conv_p, o_p, z_p, mk_b, mv_b, h3_s, batch, seq,
                                                      wts)

    counts_p = counts_p[0, :N_EXPERTS].astype(I32)
    counts_s = counts_s[0, :N_EXPERTS].astype(I32)
    idx_s = rt_s[0:TOP_K].astype(I32)
    rank_s = rt_s[2 * TOP_K:3 * TOP_K].astype(I32) + jnp.sum(
        jnp.where(idx_s[..., None] == jnp.arange(N_EXPERTS, dtype=I32), counts_p, 0), axis=-1)
    idx_t = jnp.concatenate([rt_p[0:TOP_K].astype(I32), idx_s], axis=1)
    rank_t = jnp.concatenate([rt_p[2 * TOP_K:3 * TOP_K].astype(I32), rank_s], axis=1)
    min_tiles = -(-(n_all * TOP_K + N_EXPERTS * (MOE_TILE - 1)) // MOE_TILE)
    part_tiles = [_part_tiles(-(-min_tiles * w // sum(MOE_PART_WEIGHTS))) for w in MOE_PART_WEIGHTS]
    first_tiles = [sum(part_tiles[:k]) for k in range(len(part_tiles))]
    n_rows = sum(part_tiles) * MOE_TILE
    tile_e, tile_next, n_tiles, src_tok, pos = _routing_tables(idx_t, rank_t, counts_p + counts_s, n_rows,
                                                               part_tiles)
    pos_t = pos.reshape(TOP_K, n_all)
    xs_parts = [_sc_gather_rows(h3r, src_tok[f * MOE_TILE:(f + t) * MOE_TILE])
                for f, t in zip(first_tiles, part_tiles)]
    ys = None
    for k, f in enumerate(first_tiles):
        ys = _moe(tile_e, tile_next, n_tiles, xs_parts[k], ys, f, n_rows, w_gu[0],
                  b_gu[0][:, None, :], w_dn[0], b_dn[0][:, None, :])
    gfin = g_final[None]
    assert n_p % (sum(TOKEN_PART_WEIGHTS) * COMBINE_TILE) == 0
    tok_parts = [n_p * w // sum(TOKEN_PART_WEIGHTS) for w in TOKEN_PART_WEIGHTS]
    cut = -(-n_s // COMBINE_TILE) * COMBINE_TILE
    sample_part = 0 if (tok_parts[0] > cut and tok_parts[0] % min(COMBINE_TILE, n_s) == 0) \
        else len(tok_parts) - 1
    if sample_part == 0:
        tok_parts[0] -= cut
        tok_parts[-1] += cut
    y_p = None
    for k, n_tok in enumerate(tok_parts):
        tok0 = sum(tok_parts[:k])
        pos_k = pos_t[:, tok0:tok0 + n_tok]
        if k == sample_part:
            pos_k = jnp.concatenate([pos_k, pos_t[:, n_p:]], axis=1)
        ys_tok = _sc_gather_rows(ys, pos_k.reshape(-1)).reshape(TOP_K, pos_k.shape[1], D_MODEL)
        y_p = _combine(x2_p, route_p, gfin, ys_tok, y_p, tok0, n_tok, 0)
        if k == sample_part:
            y_s = _combine(x2_s, route_s, gfin, ys_tok, None, 0, n_s, n_tok // min(COMBINE_TILE, n_s))

    return (y_p.reshape(batch, seq, D_MODEL), y_s.reshape(n_s, 1, D_MODEL),
            cstate_p[None], sstate_p[None], gstate_p[None],
            mk[None], mv[None],
            cstate_s[None], sstate_s[None], gstate_s[None])
```

```python
import functools

import jax
import jax.numpy as jnp
from jax import lax
from jax.experimental import pallas as pl
from jax.experimental.pallas import tpu as pltpu
from jax.experimental.pallas import tpu_sc as plsc

F32, BF16, I32 = jnp.float32, jnp.bfloat16, jnp.int32

D_MODEL = 1024
CONV_CH = 512
CONV_WIDTH = 31
GDN_HEADS = 4
GDN_DK = 128
GDN_V = 512
QKV_CH = 1536
SHORT_CONV = 4
N_MEM = 256
X_HEADS = 4
X_HEAD_DIM = 256
N_EXPERTS = 32
TOP_K = 4
D_EXPERT = 1024
SWIGLU_LIMIT = 7.0
SWIGLU_ALPHA = 1.702
NORM_EPS = 1e-6
OFF_QKV = 2 * CONV_CH
OFF_Z = OFF_QKV + QKV_CH
OFF_A = OFF_Z + GDN_V

LANES = 128
SUBLANES = 8
GDN_BLOCK = 128
TOKEN_TILE = 256
MOE_TILE = 384
MOE_COLS = 256
MOE_PART_WEIGHTS = (3, 6, 8, 9)
TOKEN_PART_WEIGHTS = (1, 2, 2, 3)
COMBINE_TILE = 256
TOKEN_BITS = 15
ROUTE_ROWS = 16
SC_CORES = 2
SC_SUBCORES = 16
VMEM_LIMIT = 48 * 1024 * 1024


def _cparams(sem, vmem=VMEM_LIMIT):
    return pltpu.CompilerParams(dimension_semantics=sem, vmem_limit_bytes=vmem)


def _mm(a, b):
    return jnp.dot(a.astype(BF16), b.astype(BF16), preferred_element_type=F32)


def _mm_nt(a, b):
    return lax.dot_general(a.astype(BF16), b.astype(BF16), (((1,), (1,)), ((), ())),
                           preferred_element_type=F32)


def _mm_tn(a, b):
    return lax.dot_general(a.astype(BF16), b.astype(BF16), (((0,), (0,)), ((), ())),
                           preferred_element_type=F32)


def _rms(x, g):
    return x * lax.rsqrt(jnp.mean(x * x, axis=-1, keepdims=True) + NORM_EPS) * g


def _silu(x):
    return x * jax.nn.sigmoid(x)


PACKED = D_MODEL // 2


def _pack_rows(h):
    lo = pltpu.bitcast(h[:, :PACKED].astype(BF16).astype(F32), jnp.uint32)
    hi = pltpu.bitcast(h[:, PACKED:].astype(BF16).astype(F32), jnp.uint32)
    return (lo >> 16) | (hi & jnp.uint32(0xFFFF0000))


def _unpack_rows(w):
    lo = pltpu.bitcast(w << 16, F32)
    hi = pltpu.bitcast(w & jnp.uint32(0xFFFF0000), F32)
    return jnp.concatenate([lo, hi], axis=1)


def _full(shape):
    return pl.BlockSpec(shape, lambda *_: (0,) * len(shape))


def _strict_lower(n):
    return jnp.tril(jnp.ones((n, n), BF16), k=-1)


def _project(x, gmix, w_ref, wab_ref, bglu):
    h = _rms(x, gmix).astype(BF16)
    u_glu = jnp.dot(h, w_ref[:, 0:OFF_QKV], preferred_element_type=F32) + bglu
    glu = u_glu[:, :CONV_CH] * jax.nn.sigmoid(u_glu[:, CONV_CH:])
    qkv_pre = jnp.dot(h, w_ref[:, OFF_QKV:OFF_Z], preferred_element_type=F32)
    z = jnp.dot(h, w_ref[:, OFF_Z:OFF_A], preferred_element_type=F32)
    uab = jnp.dot(h, wab_ref[...], preferred_element_type=F32)
    return glu, qkv_pre, z, uab


def _gate_beta(uab, cst):
    lane = lax.broadcasted_iota(I32, uab.shape, 1)
    g = -jnp.exp(cst[0:1, :]) * jax.nn.softplus(uab + cst[1:2, :])
    return jnp.where(lane < GDN_HEADS, g, jax.nn.sigmoid(uab))


def _conv_post(c, b_dw, ln_g, ln_b):
    c = c + b_dw
    mu = jnp.mean(c, axis=-1, keepdims=True)
    d = c - mu
    var = jnp.mean(d * d, axis=-1, keepdims=True)
    return _silu(d * lax.rsqrt(var + NORM_EPS) * ln_g + ln_b)


def _qkv_post(cs):
    a = _silu(cs)
    parts = []
    for h in range(2 * GDN_HEADS):
        seg = a[:, h * GDN_DK:(h + 1) * GDN_DK]
        n = seg * lax.rsqrt(jnp.sum(seg * seg, axis=-1, keepdims=True) + NORM_EPS)
        if h < GDN_HEADS:
            n = n * (GDN_DK ** -0.5)
        parts.append(n)
    q = jnp.concatenate(parts[:GDN_HEADS], axis=1)
    k = jnp.concatenate(parts[GDN_HEADS:], axis=1)
    return q, k, a[:, 2 * GDN_HEADS * GDN_DK:]


def _mix_out(conv_b, o, z, gon, wout_ref, x):
    parts = []
    for h in range(GDN_HEADS):
        oh = o[:, h * 128:(h + 1) * 128]
        parts.append(oh * lax.rsqrt(jnp.mean(oh * oh, axis=-1, keepdims=True) + NORM_EPS) * gon)
    on = jnp.concatenate(parts, axis=1) * _silu(z)
    mixed = (jnp.dot(conv_b, wout_ref[0:CONV_CH, :], preferred_element_type=F32)
             + jnp.dot(on.astype(BF16), wout_ref[CONV_CH:, :], preferred_element_type=F32))
    return x + mixed


def _router_logits(x2, gmoe, wr_ref, br):
    h3 = _rms(x2, gmoe)
    h_hi = h3.astype(BF16)
    r1 = h3 - h_hi.astype(F32)
    h_mid = r1.astype(BF16)
    h_lo = (r1 - h_mid.astype(F32)).astype(BF16)
    w = wr_ref[...]
    w_hi = w.astype(BF16)
    w_lo = (w - w_hi.astype(F32)).astype(BF16)
    logits = (jnp.dot(h_hi, w_hi, preferred_element_type=F32)
              + jnp.dot(h_hi, w_lo, preferred_element_type=F32)
              + jnp.dot(h_mid, w_hi, preferred_element_type=F32)
              + jnp.dot(h_lo, w_hi, preferred_element_type=F32)) + br
    return h3, logits


def _route_finish(logits, carry, before):
    m = logits.shape[0]
    neg = jnp.float32(-jnp.inf)
    n_groups = 4 if m % 32 == 0 else 1
    rows = m // n_groups
    lanes_g = lax.broadcasted_iota(I32, (rows, LANES), 1).astype(F32)
    works = [jnp.where(lanes_g < N_EXPERTS, logits[g * rows:(g + 1) * rows], neg) for g in range(n_groups)]
    vals, idxs = [], []
    for _ in range(TOP_K):
        mxs = [jnp.max(w, axis=-1, keepdims=True) for w in works]
        ixs = [jnp.min(jnp.where(w == mx, lanes_g, float(LANES)), axis=-1, keepdims=True)
               for w, mx in zip(works, mxs)]
        works = [jnp.where(lanes_g == ix, neg, w) for w, ix in zip(works, ixs)]
        vals.append(mxs)
        idxs.append(ixs)
    sels, gates = [], []
    for g in range(n_groups):
        es = [jnp.exp(vals[r][g] - vals[0][g]) for r in range(TOP_K)]
        den = es[0] + es[1] + es[2] + es[3]
        gates.append([e / den for e in es])
        sel_g = jnp.zeros((rows, LANES), F32)
        for r in range(TOP_K):
            sel_g = sel_g + jnp.where(lanes_g == idxs[r][g], 1.0, 0.0)
        sels.append(sel_g)
    sel = jnp.concatenate(sels, axis=0)
    rank_full = jnp.dot(before, sel.astype(BF16), preferred_element_type=F32) + carry
    routes = []
    for g in range(n_groups):
        rank_g = rank_full[g * rows:(g + 1) * rows]
        route_g = jnp.zeros((rows, LANES), F32)
        for r in range(TOP_K):
            rk = jnp.sum(jnp.where(lanes_g == idxs[r][g], rank_g, 0.0), axis=-1, keepdims=True)
            route_g = (route_g + jnp.where(lanes_g == r, idxs[r][g], 0.0)
                       + jnp.where(lanes_g == TOP_K + r, gates[g][r], 0.0)
                       + jnp.where(lanes_g == 2 * TOP_K + r, rk, 0.0))
        routes.append(route_g)
    new_carry = carry + jnp.sum(sel, axis=0, keepdims=True)
    return jnp.concatenate(routes, axis=0), new_carry


def _mem_kv_kernel(mem_ref, g_ref, wk_ref, wv_ref, mk_ref, mv_ref, mkb_ref, mvb_ref):
    m = _rms(mem_ref[...], g_ref[...]).astype(BF16)
    mk = jnp.dot(m, wk_ref[...], preferred_element_type=F32)
    mv = jnp.dot(m, wv_ref[...], preferred_element_type=F32)
    for h in range(X_HEADS):
        sl = slice(h * X_HEAD_DIM, (h + 1) * X_HEAD_DIM)
        mk_ref[0, :, h, :] = mk[:, sl]
        mv_ref[0, :, h, :] = mv[:, sl]
    mkb_ref[...] = mk.astype(BF16)
    mvb_ref[...] = mv.astype(BF16)


def _mem_kv(mem, g_mem, wk_b, wv_b):
    rows = mem.shape[0]
    tm = N_MEM
    row_spec = pl.BlockSpec((tm, D_MODEL), lambda i: (i, 0))
    head_spec = pl.BlockSpec((1, tm, X_HEADS, X_HEAD_DIM), lambda i: (i, 0, 0, 0))
    head_shape = jax.ShapeDtypeStruct((rows // tm, tm, X_HEADS, X_HEAD_DIM), F32)
    return pl.pallas_call(
        _mem_kv_kernel,
        grid=(rows // tm,),
        in_specs=[row_spec, _full((1, D_MODEL)), _full((D_MODEL, D_MODEL)), _full((D_MODEL, D_MODEL))],
        out_specs=[head_spec, head_spec, row_spec, row_spec],
        out_shape=[head_shape, head_shape] + [jax.ShapeDtypeStruct((rows, D_MODEL), BF16)] * 2,
        compiler_params=_cparams(("arbitrary",)),
        name="mem_kv",
    )(mem, g_mem, wk_b, wv_b)


CONV_HALO = 32
SC_HALO = 8


def _pre_prompt_kernel(x_ref, gmix_ref, w_ref, wab_ref, bglu_ref, wdw_ref, bdw_ref, lng_ref, lnb_ref,
                       cst_ref, conv_ref, qkv_ref, z_ref, gb_ref, cstate_ref, sstate_ref, cbuf, *, tm):
    j = pl.program_id(1)

    @pl.when(j == 0)
    def _():
        cbuf[0:CONV_HALO, :] = jnp.zeros((CONV_HALO, CONV_CH), F32)

    glu, qkv_pre, z, uab = _project(x_ref[...], gmix_ref[...], w_ref, wab_ref, bglu_ref[...])
    cbuf[CONV_HALO:CONV_HALO + tm, :] = glu
    qkv_ref[...] = qkv_pre
    z_ref[...] = z
    gb_ref[...] = _gate_beta(uab, cst_ref[...])

    base = CONV_HALO - (CONV_WIDTH - 1)
    rows = CONV_HALO + tm
    accs = []
    for c in range(CONV_CH // LANES):
        lanes = slice(c * LANES, (c + 1) * LANES)
        block = cbuf[:, lanes]
        acc = None
        for r in range(SUBLANES):
            shifted = block if r == 0 else pltpu.roll(block, rows - r, 0)
            for a in range(base, base + CONV_WIDTH):
                if a % SUBLANES == r:
                    t = a - base
                    term = wdw_ref[t:t + 1, lanes] * shifted[a - r:a - r + tm, :]
                    acc = term if acc is None else acc + term
        accs.append(acc)
    acc = jnp.concatenate(accs, axis=1)
    conv_ref[...] = _conv_post(acc, bdw_ref[...], lng_ref[...], lnb_ref[...]).astype(BF16)

    @pl.when(j == pl.num_programs(1) - 1)
    def _():
        cstate_ref[0] = cbuf[pl.ds(CONV_HALO + tm - (CONV_WIDTH - 1), CONV_WIDTH - 1), :]
        sstate_ref[0] = qkv_pre[tm - (SHORT_CONV - 1):, :]

    cbuf[0:CONV_HALO, :] = cbuf[tm:tm + CONV_HALO, :]


def _pre_prompt(x2d, batch, seq, wts):
    tm = TOKEN_TILE
    nj = seq // tm
    rows = batch * seq

    def tok(width):
        return pl.BlockSpec((tm, width), lambda b, j: (b * nj + j, 0))

    in_specs = [tok(D_MODEL), _full((1, D_MODEL)), _full((D_MODEL, OFF_A)), _full((D_MODEL, LANES)),
                _full((1, OFF_QKV)), _full((32, CONV_CH)), _full((1, CONV_CH)), _full((1, CONV_CH)),
                _full((1, CONV_CH)), _full((8, LANES))]
    out_specs = [tok(CONV_CH), tok(QKV_CH), tok(GDN_V), tok(LANES),
                 pl.BlockSpec((1, CONV_WIDTH - 1, CONV_CH), lambda b, j: (b, 0, 0)),
                 pl.BlockSpec((1, SHORT_CONV - 1, QKV_CH), lambda b, j: (b, 0, 0))]
    out_shape = [jax.ShapeDtypeStruct((rows, CONV_CH), BF16),
                 jax.ShapeDtypeStruct((rows, QKV_CH), F32),
                 jax.ShapeDtypeStruct((rows, GDN_V), F32),
                 jax.ShapeDtypeStruct((rows, LANES), F32),
                 jax.ShapeDtypeStruct((batch, CONV_WIDTH - 1, CONV_CH), F32),
                 jax.ShapeDtypeStruct((batch, SHORT_CONV - 1, QKV_CH), F32)]
    return pl.pallas_call(
        functools.partial(_pre_prompt_kernel, tm=tm),
        grid=(batch, nj),
        in_specs=in_specs,
        out_specs=out_specs,
        out_shape=out_shape,
        scratch_shapes=[pltpu.VMEM((CONV_HALO + tm, CONV_CH), F32)],
        compiler_params=_cparams(("arbitrary", "arbitrary")),
        name="pre_prompt",
    )(x2d, wts["g_mix"], wts["w_in_b"], wts["w_ab_b"], wts["b_glu"], wts["w_dw"], wts["b_dw"],
      wts["ln_g"], wts["ln_b"], wts["gdn_cst"])


GDN_SEQS = 4


def _gdn_prompt_kernel(qkv_ref, gb_ref, wsc_ref, o_ref, sfin_ref, s_scr, sbuf):
    c = pl.program_id(1)
    n = GDN_BLOCK
    seqs = range(GDN_SEQS)

    @pl.when(c == 0)
    def _():
        s_scr[...] = jnp.zeros(s_scr.shape, F32)
        sbuf[:, 0:SC_HALO, :] = jnp.zeros((GDN_SEQS, SC_HALO, QKV_CH), F32)

    sbase = SC_HALO - (SHORT_CONV - 1)
    qkvs = []
    for b in seqs:
        sbuf[b, SC_HALO:SC_HALO + n, :] = qkv_ref[b]
        block = sbuf[b]
        cs = None
        for t in range(SHORT_CONV):
            r = (sbase + t) % SUBLANES
            shifted = block if r == 0 else pltpu.roll(block, SC_HALO + n - r, 0)
            term = wsc_ref[t:t + 1, :] * shifted[sbase + t - r:sbase + t - r + n, :]
            cs = term if cs is None else cs + term
        qkvs.append(_qkv_post(cs))
        sbuf[b, 0:SC_HALO, :] = sbuf[b, n:n + SC_HALO, :]

    row = lax.broadcasted_iota(I32, (n, n), 0)
    col = lax.broadcasted_iota(I32, (n, n), 1)
    causal = row >= col
    strict = row > col
    tri = jnp.where(causal, 1.0, 0.0).astype(BF16)
    eye = jnp.where(row == col, 1.0, 0.0)
    level_masks = []
    b = 1
    while b < n:
        same_pair = ((row ^ col) & ~(2 * b - 1)) == 0
        level_masks.append(same_pair & ((row & b) != 0) & ((col & b) == 0))
        b *= 2
    gbs, gcums, gcum_ts, egcs = [], [], [], []
    for b in seqs:
        gb = gb_ref[b]
        g1 = gb.astype(BF16)
        r1 = gb - g1.astype(F32)
        g2 = r1.astype(BF16)
        g3 = (r1 - g2.astype(F32)).astype(BF16)
        gcum = (jnp.dot(tri, g1, preferred_element_type=F32) + jnp.dot(tri, g2, preferred_element_type=F32)
                + jnp.dot(tri, g3, preferred_element_type=F32))
        gbs.append(gb)
        gcums.append(gcum)
        gcum_ts.append(gcum.T)
        egcs.append(jnp.exp(gcum))
    units = [(b, h) for b in seqs for h in range(GDN_HEADS)]
    idx = range(len(units))
    sls = [slice(h * GDN_DK, (h + 1) * GDN_DK) for h in range(GDN_HEADS)]
    qs = [qkvs[b][0][:, sls[h]] for b, h in units]
    ks = [qkvs[b][1][:, sls[h]] for b, h in units]
    vs = [qkvs[b][2][:, sls[h]] for b, h in units]
    ss = [s_scr[b, h] for b, h in units]
    gcols = [gcums[b][:, h:h + 1] for b, h in units]
    ecols = [egcs[b][:, h:h + 1] for b, h in units]
    betas = [gbs[b][:, GDN_HEADS + h:GDN_HEADS + h + 1] for b, h in units]
    glasts = [gcums[b][n - 1:n, h:h + 1] for b, h in units]
    decays = [jnp.where(causal, jnp.exp(jnp.where(causal, gcols[u] - gcum_ts[b][h:h + 1, :], 0.0)), 0.0)
              for u, (b, h) in enumerate(units)]
    kbs = [ks[u] * betas[u] for u in idx]
    lowers = [jnp.where(strict, _mm_nt(kbs[u], ks[u]) * decays[u], 0.0) for u in idx]
    intras = [jnp.where(causal, _mm_nt(qs[u], ks[u]) * decays[u], 0.0) for u in idx]
    xs = [eye - jnp.where(level_masks[0], lowers[u], 0.0) for u in idx]
    for mask in level_masks[1:]:
        ts = [_mm(xs[u], jnp.where(mask, lowers[u], 0.0)) for u in idx]
        xs = [xs[u] - _mm(ts[u], xs[u]) for u in idx]
    us = [_mm(xs[u], vs[u] * betas[u]) for u in idx]
    ws = [_mm(xs[u], kbs[u] * ecols[u]) for u in idx]
    v_news = [us[u] - _mm(ws[u], ss[u]) for u in idx]
    os_ = [_mm(qs[u] * ecols[u], ss[u]) + _mm(intras[u], v_news[u]) for u in idx]
    s_news = [ss[u] * jnp.exp(glasts[u]) + _mm_tn(ks[u] * jnp.exp(glasts[u] - gcols[u]), v_news[u])
              for u in idx]
    for u, (b, h) in enumerate(units):
        o_ref[b, :, sls[h]] = os_[u]
        s_scr[b, h] = s_news[u]

    @pl.when(c == pl.num_programs(1) - 1)
    def _():
        sfin_ref[...] = s_scr[...]


def _gdn_prompt(qkv, gb, w_sc, batch, seq):
    n = GDN_BLOCK
    nc = seq // n
    g = GDN_SEQS
    assert batch % g == 0

    def tok(width):
        return pl.BlockSpec((g, n, width), lambda b, c: (b, c, 0))

    state_shape = (g, GDN_HEADS, GDN_DK, GDN_DK)
    o, s_fin = pl.pallas_call(
        _gdn_prompt_kernel,
        grid=(batch // g, nc),
        in_specs=[tok(QKV_CH), tok(LANES), _full(w_sc.shape)],
        out_specs=[tok(GDN_V), pl.BlockSpec(state_shape, lambda b, c: (b, 0, 0, 0))],
        out_shape=[jax.ShapeDtypeStruct((batch, seq, GDN_V), F32),
                   jax.ShapeDtypeStruct((batch, GDN_HEADS, GDN_DK, GDN_DK), F32)],
        scratch_shapes=[pltpu.VMEM(state_shape, F32), pltpu.VMEM((g, SC_HALO + n, QKV_CH), F32)],
        compiler_params=_cparams(("arbitrary", "arbitrary")),
        name="gdn_prompt",
    )(qkv.reshape(batch, seq, QKV_CH), gb.reshape(batch, seq, LANES), w_sc)
    return o.reshape(batch * seq, GDN_V), s_fin


def _post_prompt_kernel(x_ref, conv_ref, o_ref, z_ref, gon_ref, wout_ref, gx_ref, wq_ref, mk_ref, mv_ref,
                        wo_ref, gmoe_ref, wr_ref, br_ref, tri_ref, h3s_ref, x2_ref, h3r_ref, route_ref, rt_ref,
                        cnt_ref, carry, logit_buf, *, n_steps):
    step = pl.program_id(0)

    @pl.when(step == 0)
    def _():
        carry[...] = jnp.zeros(carry.shape, F32)
        logit_buf[...] = jnp.zeros(logit_buf.shape, F32)

    x1 = _mix_out(conv_ref[...], o_ref[...], z_ref[...], gon_ref[...], wout_ref, x_ref[...])
    qx = jnp.dot(_rms(x1, gx_ref[...]).astype(BF16), wq_ref[...], preferred_element_type=F32)

    route, new_carry = _route_finish(logit_buf[...], carry[0:1, :], tri_ref[...])
    route_ref[...] = route
    rt_ref[...] = route.T[0:ROUTE_ROWS, :]
    kept = jnp.where(step >= 1, new_carry, carry[0:1, :])
    carry[0:1, :] = kept
    cnt_ref[...] = jnp.broadcast_to(kept, cnt_ref.shape)

    sls = [slice(h * X_HEAD_DIM, (h + 1) * X_HEAD_DIM) for h in range(X_HEADS)]
    qb = qx.astype(BF16)
    ss = [lax.dot_general(qb[:, sl], mk_ref[:, sl], (((1,), (1,)), ((), ())),
                          preferred_element_type=F32) * (X_HEAD_DIM ** -0.5) for sl in sls]
    es = [jnp.exp(s - jnp.max(s, axis=-1, keepdims=True)) for s in ss]
    ps = [(e / jnp.sum(e, axis=-1, keepdims=True)).astype(BF16) for e in es]
    att = jnp.concatenate(
        [jnp.dot(p, mv_ref[:, sl], preferred_element_type=F32) for p, sl in zip(ps, sls)], axis=1)
    x2 = x1 + jnp.dot(att.astype(BF16), wo_ref[...], preferred_element_type=F32)
    h3, logits = _router_logits(x2, gmoe_ref[...], wr_ref, br_ref[...])
    logit_buf[...] = logits

    @pl.when(step < n_steps)
    def _():
        x2_ref[...] = x2
        h3r_ref[...] = _pack_rows(h3)

    @pl.when(step == n_steps)
    def _():
        h3r_ref[0:h3s_ref.shape[0], :] = h3s_ref[...]


def _post_prompt(x2d, conv, o, z, mk_b, mv_b, h3_sample, batch, seq, wts):
    tm = TOKEN_TILE
    nj = seq // tm
    rows = batch * seq
    n_steps = batch * nj
    n_s = h3_sample.shape[0]
    assert n_s <= tm

    def tok(width):
        return pl.BlockSpec((tm, width), lambda s: (jnp.minimum(s, n_steps - 1), 0))

    mem_spec = pl.BlockSpec((N_MEM, D_MODEL), lambda s: (jnp.minimum(s, n_steps - 1) // nj, 0))
    sq = _full((D_MODEL, D_MODEL))
    in_specs = [tok(D_MODEL), tok(CONV_CH), tok(GDN_V), tok(GDN_V), _full((1, GDN_DK)), sq,
                _full((1, D_MODEL)), sq, mem_spec, mem_spec, sq, _full((1, D_MODEL)),
                _full((D_MODEL, LANES)), _full((1, LANES)), _full((tm, tm)), _full(h3_sample.shape)]
    out_specs = [tok(D_MODEL),
                 pl.BlockSpec((tm, PACKED), lambda s: (s, 0)),
                 pl.BlockSpec((tm, LANES), lambda s: (jnp.maximum(s - 1, 0), 0)),
                 pl.BlockSpec((ROUTE_ROWS, tm), lambda s: (0, jnp.maximum(s - 1, 0))),
                 _full((SUBLANES, LANES))]
    out_shape = [jax.ShapeDtypeStruct((rows, D_MODEL), F32),
                 jax.ShapeDtypeStruct((rows + n_s, PACKED), jnp.uint32),
                 jax.ShapeDtypeStruct((rows, LANES), F32),
                 jax.ShapeDtypeStruct((ROUTE_ROWS, rows), F32),
                 jax.ShapeDtypeStruct((SUBLANES, LANES), F32)]
    return pl.pallas_call(
        functools.partial(_post_prompt_kernel, n_steps=n_steps),
        grid=(n_steps + 1,),
        in_specs=in_specs,
        out_specs=out_specs,
        out_shape=out_shape,
        scratch_shapes=[pltpu.VMEM((SUBLANES, LANES), F32), pltpu.VMEM((tm, LANES), F32)],
        compiler_params=_cparams(("arbitrary",)),
        name="post_prompt",
    )(x2d, conv, o, z, wts["g_onorm"], wts["w_out_b"], wts["g_xattn"], wts["w_xq_b"], mk_b, mv_b,
      wts["w_xo_b"], wts["g_moe"], wts["w_router"], wts["b_router"], _strict_lower(tm), h3_sample)


def _pre_sample_kernel(x_ref, gmix_ref, w_ref, wab_ref, bglu_ref, wdw_ref, bdw_ref, lng_ref, lnb_ref,
                       wsc_ref, cst_ref, chist_ref, shist_ref, conv_ref, q_ref, k_ref, v_ref, z_ref,
                       gb_ref, cnew_ref, snew_ref):
    glu, qkv_pre, z, uab = _project(x_ref[...], gmix_ref[...], w_ref, wab_ref, bglu_ref[...])
    z_ref[...] = z
    gb_ref[...] = _gate_beta(uab, cst_ref[...])
    kw = CONV_WIDTH
    acc = wdw_ref[kw - 1:kw, :] * glu
    for t in range(kw - 1):
        row = chist_ref[:, t, :]
        acc = acc + wdw_ref[t:t + 1, :] * row
        if t >= 1:
            cnew_ref[:, t - 1, :] = row
    cnew_ref[:, kw - 2, :] = glu
    conv_ref[...] = _conv_post(acc, bdw_ref[...], lng_ref[...], lnb_ref[...]).astype(BF16)
    ks = SHORT_CONV
    cs = wsc_ref[ks - 1:ks, :] * qkv_pre
    for t in range(ks - 1):
        row = shist_ref[:, t, :]
        cs = cs + wsc_ref[t:t + 1, :] * row
        if t >= 1:
            snew_ref[:, t - 1, :] = row
    snew_ref[:, ks - 2, :] = qkv_pre
    q, k, v = _qkv_post(cs)
    q_ref[...] = q
    k_ref[...] = k
    v_ref[...] = v


PRE_SAMPLE_TOKENS = 32


def _pre_sample(xs, chist, shist, wts):
    n = xs.shape[0]
    tb = min(PRE_SAMPLE_TOKENS, n)

    def tok(width):
        return pl.BlockSpec((tb, width), lambda i: (i, 0))

    def hist(a):
        return pl.BlockSpec((tb,) + a.shape[1:], lambda i: (i, 0, 0))

    consts = (wts["g_mix"], wts["w_in_b"], wts["w_ab_b"], wts["b_glu"], wts["w_dw"], wts["b_dw"],
              wts["ln_g"], wts["ln_b"], wts["w_sc"], wts["gdn_cst"])
    return pl.pallas_call(
        _pre_sample_kernel,
        grid=(n // tb,),
        in_specs=[tok(D_MODEL)] + [_full(a.shape) for a in consts] + [hist(chist), hist(shist)],
        out_specs=[tok(CONV_CH), tok(GDN_V), tok(GDN_V), tok(GDN_V), tok(GDN_V), tok(LANES),
                   hist(chist), hist(shist)],
        out_shape=[jax.ShapeDtypeStruct((n, CONV_CH), BF16)]
        + [jax.ShapeDtypeStruct((n, GDN_V), F32)] * 4
        + [jax.ShapeDtypeStruct((n, LANES), F32), jax.ShapeDtypeStruct(chist.shape, F32),
           jax.ShapeDtypeStruct(shist.shape, F32)],
        compiler_params=_cparams(("arbitrary",)),
        name="pre_sample",
    )(xs, *consts, chist, shist)


GDN_STEP_TOKENS = 8


def _gdn_sample_kernel(q_ref, k_ref, v_ref, gb_ref, s_ref, o_ref, snew_ref):
    n = GDN_DK
    for i in range(GDN_STEP_TOKENS):
        for h in range(GDN_HEADS):
            sl = slice(h * GDN_DK, (h + 1) * GDN_DK)
            qrow = q_ref[i:i + 1, sl]
            krow = k_ref[i:i + 1, sl]
            vrow = v_ref[i:i + 1, sl]
            g = gb_ref[i:i + 1, h:h + 1]
            beta = gb_ref[i:i + 1, GDN_HEADS + h:GDN_HEADS + h + 1]
            kcol = jnp.broadcast_to(krow, (n, n)).T
            qcol = jnp.broadcast_to(qrow, (n, n)).T
            s1 = s_ref[i, h] * jnp.exp(g)
            sk = jnp.sum(s1 * kcol, axis=0, keepdims=True)
            vt = (vrow - sk) * beta
            s2 = s1 + kcol * vt
            snew_ref[i, h] = s2
            o_ref[i:i + 1, sl] = jnp.sum(s2 * qcol, axis=0, keepdims=True)


def _gdn_sample(q, k, v, gb, state):
    n = q.shape[0]
    tb = GDN_STEP_TOKENS

    def tok(width):
        return pl.BlockSpec((tb, width), lambda i: (i, 0))

    st = pl.BlockSpec((tb, GDN_HEADS, GDN_DK, GDN_DK), lambda i: (i, 0, 0, 0))
    return pl.pallas_call(
        _gdn_sample_kernel,
        grid=(n // tb,),
        in_specs=[tok(GDN_V), tok(GDN_V), tok(GDN_V), tok(LANES), st],
        out_specs=[tok(GDN_V), st],
        out_shape=[jax.ShapeDtypeStruct((n, GDN_V), F32), jax.ShapeDtypeStruct(state.shape, F32)],
        compiler_params=_cparams(("arbitrary",)),
        name="gdn_sample",
    )(q, k, v, gb, state)


def _mix_sample_kernel(x_ref, conv_ref, o_ref, z_ref, gon_ref, wout_ref, gx_ref, wq_ref, x1_ref, qx_ref):
    x1 = _mix_out(conv_ref[...], o_ref[...], z_ref[...], gon_ref[...], wout_ref, x_ref[...])
    x1_ref[...] = x1
    qx_ref[...] = jnp.dot(_rms(x1, gx_ref[...]).astype(BF16), wq_ref[...], preferred_element_type=F32)


def _mix_sample(xs, conv, o, z, wts):
    n = xs.shape[0]
    in_arrays = (xs, conv, o, z, wts["g_onorm"], wts["w_out_b"], wts["g_xattn"], wts["w_xq_b"])
    return pl.pallas_call(
        _mix_sample_kernel,
        grid=(1,),
        in_specs=[_full(a.shape) for a in in_arrays],
        out_specs=[_full((n, D_MODEL))] * 2,
        out_shape=[jax.ShapeDtypeStruct((n, D_MODEL), F32)] * 2,
        compiler_params=_cparams(("arbitrary",)),
        name="mix_sample",
    )(*in_arrays)


ATTN_STEP_TOKENS = 4


def _attn_sample_kernel(qx_ref, ck_ref, cv_ref, att_ref):
    for i in range(ATTN_STEP_TOKENS):
        parts = []
        for h in range(X_HEADS):
            sl = slice(h * X_HEAD_DIM, (h + 1) * X_HEAD_DIM)
            prod = ck_ref[i, :, h, :] * qx_ref[0, i:i + 1, sl]
            s = jnp.sum(prod, axis=-1, keepdims=True) * (X_HEAD_DIM ** -0.5)
            e = jnp.exp(s - jnp.max(s, axis=0, keepdims=True))
            p = e / jnp.sum(e, axis=0, keepdims=True)
            parts.append(jnp.sum(p * cv_ref[i, :, h, :], axis=0, keepdims=True))
        att_ref[0, i:i + 1, :] = jnp.concatenate(parts, axis=1)


def _attn_sample(qx, ck, cv):
    n = qx.shape[0]
    tb = ATTN_STEP_TOKENS
    q3 = qx.reshape(n // tb, tb, D_MODEL)
    qspec = pl.BlockSpec((1, tb, D_MODEL), lambda i: (i, 0, 0))
    cspec = pl.BlockSpec((tb, N_MEM, X_HEADS, X_HEAD_DIM), lambda i: (i, 0, 0, 0))
    out = pl.pallas_call(
        _attn_sample_kernel,
        grid=(n // tb,),
        in_specs=[qspec, cspec, cspec],
        out_specs=qspec,
        out_shape=jax.ShapeDtypeStruct(q3.shape, F32),
        compiler_params=_cparams(("arbitrary",)),
        name="attn_sample",
    )(q3, ck, cv)
    return out.reshape(n, D_MODEL)


def _route_sample_kernel(x1_ref, att_ref, wo_ref, gmoe_ref, wr_ref, br_ref, tri_ref, x2_ref, h3r_ref, route_ref,
                         rt_ref, cnt_ref):
    x2 = x1_ref[...] + jnp.dot(att_ref[...].astype(BF16), wo_ref[...], preferred_element_type=F32)
    x2_ref[...] = x2
    h3, logits = _router_logits(x2, gmoe_ref[...], wr_ref, br_ref[...])
    route, counts = _route_finish(logits, jnp.zeros((1, LANES), F32), tri_ref[...])
    h3r_ref[...] = _pack_rows(h3)
    route_ref[...] = route
    rt_ref[...] = route.T[0:ROUTE_ROWS, :]
    cnt_ref[...] = jnp.broadcast_to(counts, cnt_ref.shape)


def _route_sample(x1, att, wts):
    n = x1.shape[0]
    in_arrays = (x1, att, wts["w_xo_b"], wts["g_moe"], wts["w_router"], wts["b_router"],
                 _strict_lower(n))
    shapes = [(n, D_MODEL), (n, PACKED), (n, LANES), (ROUTE_ROWS, n), (SUBLANES, LANES)]
    dtypes = [F32, jnp.uint32, F32, F32, F32]
    return pl.pallas_call(
        _route_sample_kernel,
        grid=(1,),
        in_specs=[_full(a.shape) for a in in_arrays],
        out_specs=[_full(s) for s in shapes],
        out_shape=[jax.ShapeDtypeStruct(s, d) for s, d in zip(shapes, dtypes)],
        compiler_params=_cparams(("arbitrary",)),
        name="route_sample",
    )(*in_arrays)


def _part_tiles(tiles):
    n_workers = SC_CORES * SC_SUBCORES
    while True:
        rows = tiles * MOE_TILE
        if rows % (n_workers * SUBLANES) == 0 and any(
                (rows // n_workers) % c == 0 for c in range(64, 24, -SUBLANES)):
            return tiles
        tiles += 1


def _sc_chunk(rows_per_worker):
    for c in range(64, 0, -SUBLANES):
        if rows_per_worker % c == 0:
            return c
    raise ValueError(rows_per_worker)


def _sc_gather_rows(table, idx):
    n_workers = SC_CORES * SC_SUBCORES
    b = idx.shape[0]
    assert b % (n_workers * SUBLANES) == 0
    per_worker = b // n_workers
    chunk = _sc_chunk(per_worker)
    row_shape = table.shape[1:]
    mesh = plsc.VectorSubcoreMesh(core_axis_name="c", subcore_axis_name="s")

    @functools.partial(
        pl.kernel, mesh=mesh,
        out_type=jax.ShapeDtypeStruct((b,) + row_shape, table.dtype),
        scratch_types=[pltpu.VMEM((chunk,), I32), pltpu.VMEM((chunk,) + row_shape, table.dtype),
                       pltpu.SemaphoreType.DMA],
        name="sc_gather_rows",
    )
    def gather(table_hbm, idx_hbm, out_hbm, idx_v, rows_v, sem):
        worker = lax.axis_index("s") * SC_CORES + lax.axis_index("c")
        base = worker * per_worker

        @pl.loop(0, per_worker // chunk)
        def _(c):
            off = pl.multiple_of(base + c * chunk, SUBLANES)
            pltpu.sync_copy(idx_hbm.at[pl.ds(off, chunk)], idx_v)
            pltpu.async_copy(table_hbm.at[idx_v], rows_v, sem).wait()
            pltpu.sync_copy(rows_v, out_hbm.at[pl.ds(off, chunk)])

    return gather(table, idx)


def _moe_kernel(te_ref, tn_ref, nt_ref, xs_ref, wgu_hbm, bgu_ref, wdn_hbm, bdn_ref, *rest, first_tile):
    ys_ref, wgu_f, wdn_f, wgu_b, wdn_b, sems = rest[-6:]
    step = pl.program_id(0)
    i = first_tile + step
    total = nt_ref[0]

    def weight_copies(e):
        return (pltpu.make_async_copy(wgu_hbm.at[e], wgu_f, sems.at[0]),
                pltpu.make_async_copy(wdn_hbm.at[e], wdn_f, sems.at[1]))

    def start(e):
        for cp in weight_copies(e):
            cp.start()

    @pl.when(i < total)
    def _():
        expert = te_ref[i]
        prev = te_ref[jnp.maximum(i - 1, 0)]
        fresh = jnp.logical_or(step == 0, expert != prev)

        @pl.when(step == 0)
        def _():
            start(expert)

        @pl.when(fresh)
        def _():
            for cp in weight_copies(expert):
                cp.wait()
            wgu_b[...] = wgu_f[...].astype(BF16)
            wdn_b[...] = wdn_f[...].astype(BF16)
            nxt = tn_ref[i]

            @pl.when(nxt >= 0)
            def _():
                start(nxt)

        x = _unpack_rows(xs_ref[...]).astype(BF16)

        def up(c):
            glu_cols = slice(c * MOE_COLS, (c + 1) * MOE_COLS)
            lin_cols = slice(D_EXPERT + c * MOE_COLS, D_EXPERT + (c + 1) * MOE_COLS)
            return (jnp.dot(x, wgu_b[:, glu_cols], preferred_element_type=F32) + bgu_ref[0, :, glu_cols],
                    jnp.dot(x, wgu_b[:, lin_cols], preferred_element_type=F32) + bgu_ref[0, :, lin_cols])

        n_chunks = D_EXPERT // MOE_COLS
        nxt = up(0)
        y = None
        for c in range(n_chunks):
            g, lin = nxt
            if c + 1 < n_chunks:
                nxt = up(c + 1)
            x_glu = jnp.minimum(g, SWIGLU_LIMIT)
            x_lin = jnp.clip(lin, -SWIGLU_LIMIT, SWIGLU_LIMIT)
            act = x_glu * jax.nn.sigmoid(SWIGLU_ALPHA * x_glu) * (x_lin + 1.0)
            part = jnp.dot(act.astype(BF16), wdn_b[c * MOE_COLS:(c + 1) * MOE_COLS, :],
                           preferred_element_type=F32)
            y = part if y is None else y + part
        ys_ref[...] = _pack_rows(y + bdn_ref[0])

    @pl.when(i >= total)
    def _():
        ys_ref[...] = jnp.zeros(ys_ref.shape, jnp.uint32)


def _moe(tile_e, tile_next, n_tiles, xs_part, ys_prev, first_tile, n_rows, w_gu, b_gu, w_dn, b_dn):
    tm = MOE_TILE

    def bias(shape):
        return pl.BlockSpec(shape, lambda i, te, tn, nt: (te[first_tile + i], 0, 0))

    hbm = pl.BlockSpec(memory_space=pl.ANY)
    in_specs = [pl.BlockSpec((tm, PACKED), lambda i, te, tn, nt: (i, 0)),
                hbm, bias((1, 1, 2 * D_EXPERT)), hbm, bias((1, 1, D_MODEL))]
    operands = [tile_e, tile_next, n_tiles, xs_part, w_gu, b_gu, w_dn, b_dn]
    aliases = {}
    if ys_prev is not None:
        in_specs.append(hbm)
        aliases = {len(operands): 0}
        operands.append(ys_prev)
    grid_spec = pltpu.PrefetchScalarGridSpec(
        num_scalar_prefetch=3,
        grid=(xs_part.shape[0] // tm,),
        in_specs=in_specs,
        out_specs=pl.BlockSpec((tm, PACKED), lambda i, te, tn, nt: (first_tile + i, 0)),
        scratch_shapes=[pltpu.VMEM((D_MODEL, 2 * D_EXPERT), F32),
                        pltpu.VMEM((D_EXPERT, D_MODEL), F32),
                        pltpu.VMEM((D_MODEL, 2 * D_EXPERT), BF16),
                        pltpu.VMEM((D_EXPERT, D_MODEL), BF16),
                        pltpu.SemaphoreType.DMA((2,))],
    )
    return pl.pallas_call(
        functools.partial(_moe_kernel, first_tile=first_tile),
        grid_spec=grid_spec,
        out_shape=jax.ShapeDtypeStruct((n_rows, PACKED), jnp.uint32),
        input_output_aliases=aliases,
        compiler_params=_cparams(("arbitrary",), vmem=56 * 1024 * 1024),
        name="moe",
    )(*operands)


def _combine_kernel(x2_ref, route_ref, gfin_ref, yt_ref, *rest):
    y_ref = rest[-1]
    route = route_ref[...]
    acc = x2_ref[...]
    for j in range(TOP_K):
        acc = acc + route[:, TOP_K + j:TOP_K + j + 1] * _unpack_rows(yt_ref[j])
    y_ref[...] = _rms(acc, gfin_ref[...])


def _combine(x2, route, g_final, ys_tok, y_prev, tok0, n_tok, ys_block0):
    tc = min(COMBINE_TILE, n_tok)
    b0 = tok0 // tc

    def tok(width):
        return pl.BlockSpec((tc, width), lambda i: (b0 + i, 0))

    in_specs = [tok(D_MODEL), tok(LANES), pl.BlockSpec((1, D_MODEL), lambda i: (0, 0)),
                pl.BlockSpec((TOP_K, tc, PACKED), lambda i: (0, ys_block0 + i, 0))]
    operands = [x2, route, g_final, ys_tok]
    aliases = {}
    if y_prev is not None:
        in_specs.append(pl.BlockSpec(memory_space=pl.ANY))
        aliases = {len(operands): 0}
        operands.append(y_prev)
    return pl.pallas_call(
        _combine_kernel,
        grid=(n_tok // tc,),
        in_specs=in_specs,
        out_specs=tok(D_MODEL),
        out_shape=jax.ShapeDtypeStruct(x2.shape, F32),
        input_output_aliases=aliases,
        compiler_params=_cparams(("arbitrary",)),
        name="combine",
    )(*operands)


def _routing_tables(idx_t, rank_t, counts, n_rows, part_tiles):
    tm = MOE_TILE
    n_tok = idx_t.shape[1]
    n_assign = TOP_K * n_tok
    tok_mask = (1 << TOKEN_BITS) - 1
    tiles_e = (counts + tm - 1) // tm
    tile_end = jnp.cumsum(tiles_e)
    row_start = (tile_end - tiles_e) * tm
    total = tile_end[-1]
    expert_ids = jnp.arange(N_EXPERTS, dtype=I32)

    def lookup(table, e):
        return jnp.sum(jnp.where(e[..., None] == expert_ids, table, 0), axis=-1)

    pos = lookup(row_start, idx_t) + rank_t
    keys_real = (idx_t * (1 << TOKEN_BITS) + jnp.arange(n_tok, dtype=I32)[None, :]).reshape(-1)
    k = jnp.arange(n_rows - n_assign, dtype=I32)
    pad_e, pad_s = k // tm, k % tm
    pad_needed = lookup(tiles_e * tm - counts, pad_e)
    pad_key_e = jnp.where((pad_e < N_EXPERTS) & (pad_s < pad_needed), pad_e, N_EXPERTS)
    keys = lax.sort(jnp.concatenate([keys_real, pad_key_e * (1 << TOKEN_BITS) + tok_mask]),
                    is_stable=False)
    src_tok = jnp.where((keys & tok_mask) == tok_mask, jnp.arange(n_rows, dtype=I32) % n_tok,
                        keys & tok_mask)
    tid = jnp.minimum(jnp.arange(n_rows // tm, dtype=I32), total - 1)
    tile_e = jnp.minimum(jnp.sum((tid[:, None] >= tile_end[None, :]).astype(I32), axis=1), N_EXPERTS - 1)
    later = (expert_ids[None, :] > expert_ids[:, None]) & (tiles_e[None, :] > 0)
    next_e = jnp.min(jnp.where(later, expert_ids[None, :], N_EXPERTS), axis=1)
    nxt = lookup(next_e, tile_e)
    nxt_first_tile = lookup(tile_end - tiles_e, jnp.minimum(nxt, N_EXPERTS - 1))
    bounds = jnp.cumsum(jnp.asarray(part_tiles, I32))
    part_end = jnp.min(jnp.where(bounds[None, :] > tid[:, None], bounds[None, :], n_rows), axis=1)
    tile_next = jnp.where((nxt < N_EXPERTS) & (nxt_first_tile < part_end), nxt, -1)
    return (tile_e.astype(I32), tile_next.astype(I32), total.reshape(1).astype(I32), src_tok.astype(I32),
            pos.reshape(-1).astype(I32))


def _pad_rows(a, rows):
    return jnp.concatenate([a, jnp.zeros((rows - a.shape[0],) + a.shape[1:], a.dtype)], axis=0)


def _pad_lanes(a, lanes=LANES):
    return jnp.concatenate([a, jnp.zeros(a.shape[:-1] + (lanes - a.shape[-1],), a.dtype)], axis=-1)


def kernel(x_prompt, mem_prompt, x_sample, state_conformer_conv, state_gdn_conv, state_gdn, cache_mem_k,
           cache_mem_v, w_in, b_glu, w_dw, b_dw, ln_g, ln_b, w_sc, a_log, dt_bias, g_onorm, w_out, g_mix,
           g_xattn, g_mem, w_xq, w_mk, w_mv, w_xo, g_moe, w_router, b_router, w_gu, b_gu, w_dn, b_dn,
           g_final):
    assert w_in.shape[0] == 1, "single-layer configuration"
    batch, seq, _ = x_prompt.shape
    n_s = x_sample.shape[0]
    n_p = batch * seq
    n_all = n_p + n_s
    assert seq % TOKEN_TILE == 0 and n_p % n_s == 0 and n_all < (1 << TOKEN_BITS) - 1
    assert (n_all * TOP_K) % (SC_CORES * SC_SUBCORES * SUBLANES) == 0

    wts = {
        "g_mix": g_mix[0][None], "g_xattn": g_xattn[0][None], "g_moe": g_moe[0][None],
        "g_onorm": g_onorm[0][None],
        "w_in_b": w_in[0][:, :OFF_A].astype(BF16),
        "w_ab_b": _pad_lanes(w_in[0][:, OFF_A:]).astype(BF16),
        "b_glu": b_glu[0][None],
        "w_dw": _pad_rows(w_dw[0], 32), "b_dw": b_dw[0][None], "ln_g": ln_g[0][None], "ln_b": ln_b[0][None],
        "w_sc": _pad_rows(w_sc[0], 8),
        "gdn_cst": _pad_rows(_pad_lanes(jnp.stack([a_log[0], dt_bias[0]])), 8),
        "w_out_b": w_out[0].astype(BF16), "w_xq_b": w_xq[0].astype(BF16), "w_xo_b": w_xo[0].astype(BF16),
        "w_router": _pad_lanes(w_router[0]), "b_router": _pad_lanes(b_router[0][None]),
    }

    mk, mv, mk_b, mv_b = _mem_kv(mem_prompt.reshape(batch * N_MEM, D_MODEL), g_mem[0][None],
                                 w_mk[0].astype(BF16), w_mv[0].astype(BF16))
    xp = x_prompt.reshape(n_p, D_MODEL)
    conv_p, qkv_p, z_p, gb_p, cstate_p, sstate_p = _pre_prompt(xp, batch, seq, wts)
    o_p, gstate_p = _gdn_prompt(qkv_p, gb_p, wts["w_sc"], batch, seq)

    xs = x_sample.reshape(n_s, D_MODEL)
    conv_s, q_s, k_s, v_s, z_s, gb_s, cstate_s, sstate_s = _pre_sample(
        xs, state_conformer_conv[0], state_gdn_conv[0], wts)
    o_s, gstate_s = _gdn_sample(q_s, k_s, v_s, gb_s, state_gdn[0])
    x1_s, qx_s = _mix_sample(xs, conv_s, o_s, z_s, wts)
    att_s = _attn_sample(qx_s, cache_mem_k[0], cache_mem_v[0])
    x2_s, h3_s, route_s, rt_s, counts_s = _route_sample(x1_s, att_s, wts)

    x2_p, h3r, route_p, rt_p, counts_p = _post_prompt(xp, conv_p, o_p, z_p, mk_b, mv_b, h3_s, batch, seq,
                                                      wts)

    counts_p = counts_p[0, :N_EXPERTS].astype(I32)
    counts_s = counts_s[0, :N_EXPERTS].astype(I32)
    idx_s = rt_s[0:TOP_K].astype(I32)
    rank_s = rt_s[2 * TOP_K:3 * TOP_K].astype(I32) + jnp.sum(
        jnp.where(idx_s[..., None] == jnp.arange(N_EXPERTS, dtype=I32), counts_p, 0), axis=-1)
    idx_t = jnp.concatenate([rt_p[0:TOP_K].astype(I32), idx_s], axis=1)
    rank_t = jnp.concatenate([rt_p[2 * TOP_K:3 * TOP_K].astype(I32), rank_s], axis=1)
    min_tiles = -(-(n_all * TOP_K + N_EXPERTS * (MOE_TILE - 1)) // MOE_TILE)
    part_tiles = [_part_tiles(-(-min_tiles * w // sum(MOE_PART_WEIGHTS))) for w in MOE_PART_WEIGHTS]
    first_tiles = [sum(part_tiles[:k]) for k in range(len(part_tiles))]
    n_rows = sum(part_tiles) * MOE_TILE
    tile_e, tile_next, n_tiles, src_tok, pos = _routing_tables(idx_t, rank_t, counts_p + counts_s, n_rows,
                                                               part_tiles)
    pos_t = pos.reshape(TOP_K, n_all)
    xs_parts = [_sc_gather_rows(h3r, src_tok[f * MOE_TILE:(f + t) * MOE_TILE])
                for f, t in zip(first_tiles, part_tiles)]
    ys = None
    for k, f in enumerate(first_tiles):
        ys = _moe(tile_e, tile_next, n_tiles, xs_parts[k], ys, f, n_rows, w_gu[0],
                  b_gu[0][:, None, :], w_dn[0], b_dn[0][:, None, :])
    gfin = g_final[None]
    assert n_p % (sum(TOKEN_PART_WEIGHTS) * COMBINE_TILE) == 0
    tok_parts = [n_p * w // sum(TOKEN_PART_WEIGHTS) for w in TOKEN_PART_WEIGHTS]
    cut = -(-n_s // COMBINE_TILE) * COMBINE_TILE
    sample_part = 0 if (tok_parts[0] > cut and tok_parts[0] % min(COMBINE_TILE, n_s) == 0) \
        else len(tok_parts) - 1
    if sample_part == 0:
        tok_parts[0] -= cut
        tok_parts[-1] += cut
    y_p = None
    for k, n_tok in enumerate(tok_parts):
        tok0 = sum(tok_parts[:k])
        pos_k = pos_t[:, tok0:tok0 + n_tok]
        if k == sample_part:
            pos_k = jnp.concatenate([pos_k, pos_t[:, n_p:]], axis=1)
        ys_tok = _sc_gather_rows(ys, pos_k.reshape(-1)).reshape(TOP_K, pos_k.shape[1], PACKED)
        y_p = _combine(x2_p, route_p, gfin, ys_tok, y_p, tok0, n_tok, 0)
        if k == sample_part:
            y_s = _combine(x2_s, route_s, gfin, ys_tok, None, 0, n_s, n_tok // min(COMBINE_TILE, n_s))

    return (y_p.reshape(batch, seq, D_MODEL), y_s.reshape(n_s, 1, D_MODEL),
            cstate_p[None], sstate_p[None], gstate_p[None],
            mk[None], mv[None],
            cstate_s[None], sstate_s[None], gstate_s[None])
```

```python
import functools

import jax
import jax.numpy as jnp
from jax import lax
from jax.experimental import pallas as pl
from jax.experimental.pallas import tpu as pltpu
from jax.experimental.pallas import tpu_sc as plsc

F32, BF16, I32 = jnp.float32, jnp.bfloat16, jnp.int32

D_MODEL = 1024
CONV_CH = 512
CONV_WIDTH = 31
GDN_HEADS = 4
GDN_DK = 128
GDN_V = 512
QKV_CH = 1536
SHORT_CONV = 4
N_MEM = 256
X_HEADS = 4
X_HEAD_DIM = 256
N_EXPERTS = 32
TOP_K = 4
D_EXPERT = 1024
SWIGLU_LIMIT = 7.0
SWIGLU_ALPHA = 1.702
NORM_EPS = 1e-6
OFF_QKV = 2 * CONV_CH
OFF_Z = OFF_QKV + QKV_CH
OFF_A = OFF_Z + GDN_V

LANES = 128
SUBLANES = 8
GDN_BLOCK = 128
TOKEN_TILE = 512
MOE_TILE = 384
MOE_COLS = 256
MOE_PART_WEIGHTS = (3, 6, 8, 9)
TOKEN_PART_WEIGHTS = (1, 2, 2, 3)
COMBINE_TILE = 256
TOKEN_BITS = 15
ROUTE_ROWS = 16
SC_CORES = 2
SC_SUBCORES = 16
VMEM_LIMIT = 48 * 1024 * 1024


def _cparams(sem, vmem=VMEM_LIMIT):
    return pltpu.CompilerParams(dimension_semantics=sem, vmem_limit_bytes=vmem)


def _mm(a, b):
    return jnp.dot(a.astype(BF16), b.astype(BF16), preferred_element_type=F32)


def _mm_nt(a, b):
    return lax.dot_general(a.astype(BF16), b.astype(BF16), (((1,), (1,)), ((), ())),
                           preferred_element_type=F32)


def _mm_tn(a, b):
    return lax.dot_general(a.astype(BF16), b.astype(BF16), (((0,), (0,)), ((), ())),
                           preferred_element_type=F32)


def _rms(x, g):
    return x * lax.rsqrt(jnp.mean(x * x, axis=-1, keepdims=True) + NORM_EPS) * g


def _silu(x):
    return x * jax.nn.sigmoid(x)


PACKED = D_MODEL // 2


def _pack_rows(h):
    lo = pltpu.bitcast(h[:, :PACKED].astype(BF16).astype(F32), jnp.uint32)
    hi = pltpu.bitcast(h[:, PACKED:].astype(BF16).astype(F32), jnp.uint32)
    return (lo >> 16) | (hi & jnp.uint32(0xFFFF0000))


def _unpack_rows(w):
    lo = pltpu.bitcast(w << 16, F32)
    hi = pltpu.bitcast(w & jnp.uint32(0xFFFF0000), F32)
    return jnp.concatenate([lo, hi], axis=1)


def _full(shape):
    return pl.BlockSpec(shape, lambda *_: (0,) * len(shape))


def _strict_lower(n):
    return jnp.tril(jnp.ones((n, n), BF16), k=-1)


def _project(x, gmix, w_ref, wab_ref, bglu):
    h = _rms(x, gmix).astype(BF16)
    u_glu = jnp.dot(h, w_ref[:, 0:OFF_QKV], preferred_element_type=F32) + bglu
    glu = u_glu[:, :CONV_CH] * jax.nn.sigmoid(u_glu[:, CONV_CH:])
    qkv_pre = jnp.dot(h, w_ref[:, OFF_QKV:OFF_Z], preferred_element_type=F32)
    z = jnp.dot(h, w_ref[:, OFF_Z:OFF_A], preferred_element_type=F32)
    uab = jnp.dot(h, wab_ref[...], preferred_element_type=F32)
    return glu, qkv_pre, z, uab


def _gate_beta(uab, cst):
    lane = lax.broadcasted_iota(I32, uab.shape, 1)
    g = -jnp.exp(cst[0:1, :]) * jax.nn.softplus(uab + cst[1:2, :])
    return jnp.where(lane < GDN_HEADS, g, jax.nn.sigmoid(uab))


def _conv_post(c, b_dw, ln_g, ln_b):
    c = c + b_dw
    mu = jnp.mean(c, axis=-1, keepdims=True)
    d = c - mu
    var = jnp.mean(d * d, axis=-1, keepdims=True)
    return _silu(d * lax.rsqrt(var + NORM_EPS) * ln_g + ln_b)


def _qkv_post(cs):
    a = _silu(cs)
    parts = []
    for h in range(2 * GDN_HEADS):
        seg = a[:, h * GDN_DK:(h + 1) * GDN_DK]
        n = seg * lax.rsqrt(jnp.sum(seg * seg, axis=-1, keepdims=True) + NORM_EPS)
        if h < GDN_HEADS:
            n = n * (GDN_DK ** -0.5)
        parts.append(n)
    q = jnp.concatenate(parts[:GDN_HEADS], axis=1)
    k = jnp.concatenate(parts[GDN_HEADS:], axis=1)
    return q, k, a[:, 2 * GDN_HEADS * GDN_DK:]


def _mix_out(conv_b, o, z, gon, wout_ref, x):
    parts = []
    for h in range(GDN_HEADS):
        oh = o[:, h * 128:(h + 1) * 128]
        parts.append(oh * lax.rsqrt(jnp.mean(oh * oh, axis=-1, keepdims=True) + NORM_EPS) * gon)
    on = jnp.concatenate(parts, axis=1) * _silu(z)
    mixed = (jnp.dot(conv_b, wout_ref[0:CONV_CH, :], preferred_element_type=F32)
             + jnp.dot(on.astype(BF16), wout_ref[CONV_CH:, :], preferred_element_type=F32))
    return x + mixed


def _router_logits(x2, gmoe, wr_ref, br):
    h3 = _rms(x2, gmoe)
    h_hi = h3.astype(BF16)
    r1 = h3 - h_hi.astype(F32)
    h_mid = r1.astype(BF16)
    h_lo = (r1 - h_mid.astype(F32)).astype(BF16)
    w = wr_ref[...]
    w_hi = w.astype(BF16)
    w_lo = (w - w_hi.astype(F32)).astype(BF16)
    logits = (jnp.dot(h_hi, w_hi, preferred_element_type=F32)
              + jnp.dot(h_hi, w_lo, preferred_element_type=F32)
              + jnp.dot(h_mid, w_hi, preferred_element_type=F32)
              + jnp.dot(h_lo, w_hi, preferred_element_type=F32)) + br
    return h3, logits


def _route_finish(logits, carry, before):
    m = logits.shape[0]
    neg = jnp.float32(-jnp.inf)
    n_groups = 4 if m % 32 == 0 else 1
    rows = m // n_groups
    lanes_g = lax.broadcasted_iota(I32, (rows, LANES), 1).astype(F32)
    works = [jnp.where(lanes_g < N_EXPERTS, logits[g * rows:(g + 1) * rows], neg) for g in range(n_groups)]
    vals, idxs = [], []
    for _ in range(TOP_K):
        mxs = [jnp.max(w, axis=-1, keepdims=True) for w in works]
        ixs = [jnp.min(jnp.where(w == mx, lanes_g, float(LANES)), axis=-1, keepdims=True)
               for w, mx in zip(works, mxs)]
        works = [jnp.where(lanes_g == ix, neg, w) for w, ix in zip(works, ixs)]
        vals.append(mxs)
        idxs.append(ixs)
    sels, gates = [], []
    for g in range(n_groups):
        es = [jnp.exp(vals[r][g] - vals[0][g]) for r in range(TOP_K)]
        den = es[0] + es[1] + es[2] + es[3]
        gates.append([e / den for e in es])
        sel_g = jnp.zeros((rows, LANES), F32)
        for r in range(TOP_K):
            sel_g = sel_g + jnp.where(lanes_g == idxs[r][g], 1.0, 0.0)
        sels.append(sel_g)
    sel = jnp.concatenate(sels, axis=0)
    rank_full = jnp.dot(before, sel.astype(BF16), preferred_element_type=F32) + carry
    routes = []
    for g in range(n_groups):
        rank_g = rank_full[g * rows:(g + 1) * rows]
        route_g = jnp.zeros((rows, LANES), F32)
        for r in range(TOP_K):
            rk = jnp.sum(jnp.where(lanes_g == idxs[r][g], rank_g, 0.0), axis=-1, keepdims=True)
            route_g = (route_g + jnp.where(lanes_g == r, idxs[r][g], 0.0)
                       + jnp.where(lanes_g == TOP_K + r, gates[g][r], 0.0)
                       + jnp.where(lanes_g == 2 * TOP_K + r, rk, 0.0))
        routes.append(route_g)
    new_carry = carry + jnp.sum(sel, axis=0, keepdims=True)
    return jnp.concatenate(routes, axis=0), new_carry


def _mem_kv_kernel(mem_ref, g_ref, wk_ref, wv_ref, mk_ref, mv_ref, mkb_ref, mvb_ref):
    m = _rms(mem_ref[...], g_ref[...]).astype(BF16)
    mk = jnp.dot(m, wk_ref[...], preferred_element_type=F32)
    mv = jnp.dot(m, wv_ref[...], preferred_element_type=F32)
    for h in range(X_HEADS):
        sl = slice(h * X_HEAD_DIM, (h + 1) * X_HEAD_DIM)
        mk_ref[0, :, h, :] = mk[:, sl]
        mv_ref[0, :, h, :] = mv[:, sl]
    mkb_ref[...] = mk.astype(BF16)
    mvb_ref[...] = mv.astype(BF16)


def _mem_kv(mem, g_mem, wk_b, wv_b):
    rows = mem.shape[0]
    tm = N_MEM
    row_spec = pl.BlockSpec((tm, D_MODEL), lambda i: (i, 0))
    head_spec = pl.BlockSpec((1, tm, X_HEADS, X_HEAD_DIM), lambda i: (i, 0, 0, 0))
    head_shape = jax.ShapeDtypeStruct((rows // tm, tm, X_HEADS, X_HEAD_DIM), F32)
    return pl.pallas_call(
        _mem_kv_kernel,
        grid=(rows // tm,),
        in_specs=[row_spec, _full((1, D_MODEL)), _full((D_MODEL, D_MODEL)), _full((D_MODEL, D_MODEL))],
        out_specs=[head_spec, head_spec, row_spec, row_spec],
        out_shape=[head_shape, head_shape] + [jax.ShapeDtypeStruct((rows, D_MODEL), BF16)] * 2,
        compiler_params=_cparams(("arbitrary",)),
        name="mem_kv",
    )(mem, g_mem, wk_b, wv_b)


CONV_HALO = 32
SC_HALO = 8


def _pre_prompt_kernel(x_ref, gmix_ref, w_ref, wab_ref, bglu_ref, wdw_ref, bdw_ref, lng_ref, lnb_ref,
                       cst_ref, conv_ref, qkv_ref, z_ref, gb_ref, cstate_ref, sstate_ref, cbuf, *, tm):
    j = pl.program_id(1)

    @pl.when(j == 0)
    def _():
        cbuf[0:CONV_HALO, :] = jnp.zeros((CONV_HALO, CONV_CH), F32)

    glu, qkv_pre, z, uab = _project(x_ref[...], gmix_ref[...], w_ref, wab_ref, bglu_ref[...])
    cbuf[CONV_HALO:CONV_HALO + tm, :] = glu
    qkv_ref[...] = qkv_pre
    z_ref[...] = z
    gb_ref[...] = _gate_beta(uab, cst_ref[...])

    base = CONV_HALO - (CONV_WIDTH - 1)
    rows = CONV_HALO + tm
    accs = []
    for c in range(CONV_CH // LANES):
        lanes = slice(c * LANES, (c + 1) * LANES)
        block = cbuf[:, lanes]
        acc = None
        for r in range(SUBLANES):
            shifted = block if r == 0 else pltpu.roll(block, rows - r, 0)
            for a in range(base, base + CONV_WIDTH):
                if a % SUBLANES == r:
                    t = a - base
                    term = wdw_ref[t:t + 1, lanes] * shifted[a - r:a - r + tm, :]
                    acc = term if acc is None else acc + term
        accs.append(acc)
    acc = jnp.concatenate(accs, axis=1)
    conv_ref[...] = _conv_post(acc, bdw_ref[...], lng_ref[...], lnb_ref[...]).astype(BF16)

    @pl.when(j == pl.num_programs(1) - 1)
    def _():
        cstate_ref[0] = cbuf[pl.ds(CONV_HALO + tm - (CONV_WIDTH - 1), CONV_WIDTH - 1), :]
        sstate_ref[0] = qkv_pre[tm - (SHORT_CONV - 1):, :]

    cbuf[0:CONV_HALO, :] = cbuf[tm:tm + CONV_HALO, :]


def _pre_prompt(x2d, batch, seq, wts):
    tm = TOKEN_TILE
    nj = seq // tm
    rows = batch * seq

    def tok(width):
        return pl.BlockSpec((tm, width), lambda b, j: (b * nj + j, 0))

    in_specs = [tok(D_MODEL), _full((1, D_MODEL)), _full((D_MODEL, OFF_A)), _full((D_MODEL, LANES)),
                _full((1, OFF_QKV)), _full((32, CONV_CH)), _full((1, CONV_CH)), _full((1, CONV_CH)),
                _full((1, CONV_CH)), _full((8, LANES))]
    out_specs = [tok(CONV_CH), tok(QKV_CH), tok(GDN_V), tok(LANES),
                 pl.BlockSpec((1, CONV_WIDTH - 1, CONV_CH), lambda b, j: (b, 0, 0)),
                 pl.BlockSpec((1, SHORT_CONV - 1, QKV_CH), lambda b, j: (b, 0, 0))]
    out_shape = [jax.ShapeDtypeStruct((rows, CONV_CH), BF16),
                 jax.ShapeDtypeStruct((rows, QKV_CH), F32),
                 jax.ShapeDtypeStruct((rows, GDN_V), F32),
                 jax.ShapeDtypeStruct((rows, LANES), F32),
                 jax.ShapeDtypeStruct((batch, CONV_WIDTH - 1, CONV_CH), F32),
                 jax.ShapeDtypeStruct((batch, SHORT_CONV - 1, QKV_CH), F32)]
    return pl.pallas_call(
        functools.partial(_pre_prompt_kernel, tm=tm),
        grid=(batch, nj),
        in_specs=in_specs,
        out_specs=out_specs,
        out_shape=out_shape,
        scratch_shapes=[pltpu.VMEM((CONV_HALO + tm, CONV_CH), F32)],
        compiler_params=_cparams(("arbitrary", "arbitrary")),
        name="pre_prompt",
    )(x2d, wts["g_mix"], wts["w_in_b"], wts["w_ab_b"], wts["b_glu"], wts["w_dw"], wts["b_dw"],
      wts["ln_g"], wts["ln_b"], wts["gdn_cst"])


GDN_SEQS = 4


def _gdn_prompt_kernel(qkv_ref, gb_ref, wsc_ref, o_ref, sfin_ref, s_scr, sbuf):
    c = pl.program_id(1)
    n = GDN_BLOCK
    seqs = range(GDN_SEQS)

    @pl.when(c == 0)
    def _():
        s_scr[...] = jnp.zeros(s_scr.shape, F32)
        sbuf[:, 0:SC_HALO, :] = jnp.zeros((GDN_SEQS, SC_HALO, QKV_CH), F32)

    sbase = SC_HALO - (SHORT_CONV - 1)
    qkvs = []
    for b in seqs:
        sbuf[b, SC_HALO:SC_HALO + n, :] = qkv_ref[b]
        block = sbuf[b]
        cs = None
        for t in range(SHORT_CONV):
            r = (sbase + t) % SUBLANES
            shifted = block if r == 0 else pltpu.roll(block, SC_HALO + n - r, 0)
            term = wsc_ref[t:t + 1, :] * shifted[sbase + t - r:sbase + t - r + n, :]
            cs = term if cs is None else cs + term
        qkvs.append(_qkv_post(cs))
        sbuf[b, 0:SC_HALO, :] = sbuf[b, n:n + SC_HALO, :]

    row = lax.broadcasted_iota(I32, (n, n), 0)
    col = lax.broadcasted_iota(I32, (n, n), 1)
    causal = row >= col
    strict = row > col
    tri = jnp.where(causal, 1.0, 0.0).astype(BF16)
    eye = jnp.where(row == col, 1.0, 0.0)
    level_masks = []
    b = 1
    while b < n:
        same_pair = ((row ^ col) & ~(2 * b - 1)) == 0
        level_masks.append(same_pair & ((row & b) != 0) & ((col & b) == 0))
        b *= 2
    gbs, gcums, gcum_ts, egcs = [], [], [], []
    for b in seqs:
        gb = gb_ref[b]
        g1 = gb.astype(BF16)
        r1 = gb - g1.astype(F32)
        g2 = r1.astype(BF16)
        g3 = (r1 - g2.astype(F32)).astype(BF16)
        gcum = (jnp.dot(tri, g1, preferred_element_type=F32) + jnp.dot(tri, g2, preferred_element_type=F32)
                + jnp.dot(tri, g3, preferred_element_type=F32))
        gbs.append(gb)
        gcums.append(gcum)
        gcum_ts.append(gcum.T)
        egcs.append(jnp.exp(gcum))
    units = [(b, h) for b in seqs for h in range(GDN_HEADS)]
    idx = range(len(units))
    sls = [slice(h * GDN_DK, (h + 1) * GDN_DK) for h in range(GDN_HEADS)]
    qs = [qkvs[b][0][:, sls[h]] for b, h in units]
    ks = [qkvs[b][1][:, sls[h]] for b, h in units]
    vs = [qkvs[b][2][:, sls[h]] for b, h in units]
    ss = [s_scr[b, h] for b, h in units]
    gcols = [gcums[b][:, h:h + 1] for b, h in units]
    ecols = [egcs[b][:, h:h + 1] for b, h in units]
    betas = [gbs[b][:, GDN_HEADS + h:GDN_HEADS + h + 1] for b, h in units]
    glasts = [gcums[b][n - 1:n, h:h + 1] for b, h in units]
    decays = [jnp.where(causal, jnp.exp(jnp.where(causal, gcols[u] - gcum_ts[b][h:h + 1, :], 0.0)), 0.0)
              for u, (b, h) in enumerate(units)]
    kbs = [ks[u] * betas[u] for u in idx]
    lowers = [jnp.where(strict, _mm_nt(kbs[u], ks[u]) * decays[u], 0.0) for u in idx]
    intras = [jnp.where(causal, _mm_nt(qs[u], ks[u]) * decays[u], 0.0) for u in idx]
    xs = [eye - jnp.where(level_masks[0], lowers[u], 0.0) for u in idx]
    for mask in level_masks[1:]:
        ts = [_mm(xs[u], jnp.where(mask, lowers[u], 0.0)) for u in idx]
        xs = [xs[u] - _mm(ts[u], xs[u]) for u in idx]
    us = [_mm(xs[u], vs[u] * betas[u]) for u in idx]
    ws = [_mm(xs[u], kbs[u] * ecols[u]) for u in idx]
    v_news = [us[u] - _mm(ws[u], ss[u]) for u in idx]
    os_ = [_mm(qs[u] * ecols[u], ss[u]) + _mm(intras[u], v_news[u]) for u in idx]
    s_news = [ss[u] * jnp.exp(glasts[u]) + _mm_tn(ks[u] * jnp.exp(glasts[u] - gcols[u]), v_news[u])
              for u in idx]
    for u, (b, h) in enumerate(units):
        o_ref[b, :, sls[h]] = os_[u]
        s_scr[b, h] = s_news[u]

    @pl.when(c == pl.num_programs(1) - 1)
    def _():
        sfin_ref[...] = s_scr[...]


def _gdn_prompt(qkv, gb, w_sc, batch, seq):
    n = GDN_BLOCK
    nc = seq // n
    g = GDN_SEQS
    assert batch % g == 0

    def tok(width):
        return pl.BlockSpec((g, n, width), lambda b, c: (b, c, 0))

    state_shape = (g, GDN_HEADS, GDN_DK, GDN_DK)
    o, s_fin = pl.pallas_call(
        _gdn_prompt_kernel,
        grid=(batch // g, nc),
        in_specs=[tok(QKV_CH), tok(LANES), _full(w_sc.shape)],
        out_specs=[tok(GDN_V), pl.BlockSpec(state_shape, lambda b, c: (b, 0, 0, 0))],
        out_shape=[jax.ShapeDtypeStruct((batch, seq, GDN_V), F32),
                   jax.ShapeDtypeStruct((batch, GDN_HEADS, GDN_DK, GDN_DK), F32)],
        scratch_shapes=[pltpu.VMEM(state_shape, F32), pltpu.VMEM((g, SC_HALO + n, QKV_CH), F32)],
        compiler_params=_cparams(("arbitrary", "arbitrary")),
        name="gdn_prompt",
    )(qkv.reshape(batch, seq, QKV_CH), gb.reshape(batch, seq, LANES), w_sc)
    return o.reshape(batch * seq, GDN_V), s_fin


def _post_prompt_kernel(x_ref, conv_ref, o_ref, z_ref, gon_ref, wout_ref, gx_ref, wq_ref, mk_ref, mv_ref,
                        wo_ref, gmoe_ref, wr_ref, br_ref, tri_ref, h3s_ref, x2_ref, h3r_ref, route_ref, rt_ref,
                        cnt_ref, carry, logit_buf, *, n_steps):
    step = pl.program_id(0)

    @pl.when(step == 0)
    def _():
        carry[...] = jnp.zeros(carry.shape, F32)
        logit_buf[...] = jnp.zeros(logit_buf.shape, F32)

    x1 = _mix_out(conv_ref[...], o_ref[...], z_ref[...], gon_ref[...], wout_ref, x_ref[...])
    qx = jnp.dot(_rms(x1, gx_ref[...]).astype(BF16), wq_ref[...], preferred_element_type=F32)

    route, new_carry = _route_finish(logit_buf[...], carry[0:1, :], tri_ref[...])
    route_ref[...] = route
    rt_ref[...] = route.T[0:ROUTE_ROWS, :]
    kept = jnp.where(step >= 1, new_carry, carry[0:1, :])
    carry[0:1, :] = kept
    cnt_ref[...] = jnp.broadcast_to(kept, cnt_ref.shape)

    sls = [slice(h * X_HEAD_DIM, (h + 1) * X_HEAD_DIM) for h in range(X_HEADS)]
    qb = qx.astype(BF16)
    ss = [lax.dot_general(qb[:, sl], mk_ref[:, sl], (((1,), (1,)), ((), ())),
                          preferred_element_type=F32) * (X_HEAD_DIM ** -0.5) for sl in sls]
    es = [jnp.exp(s - jnp.max(s, axis=-1, keepdims=True)) for s in ss]
    ps = [(e / jnp.sum(e, axis=-1, keepdims=True)).astype(BF16) for e in es]
    att = jnp.concatenate(
        [jnp.dot(p, mv_ref[:, sl], preferred_element_type=F32) for p, sl in zip(ps, sls)], axis=1)
    x2 = x1 + jnp.dot(att.astype(BF16), wo_ref[...], preferred_element_type=F32)
    h3, logits = _router_logits(x2, gmoe_ref[...], wr_ref, br_ref[...])
    logit_buf[...] = logits

    @pl.when(step < n_steps)
    def _():
        x2_ref[...] = x2
        h3r_ref[...] = _pack_rows(h3)

    @pl.when(step == n_steps)
    def _():
        h3r_ref[0:h3s_ref.shape[0], :] = h3s_ref[...]


def _post_prompt(x2d, conv, o, z, mk_b, mv_b, h3_sample, batch, seq, wts):
    tm = TOKEN_TILE
    nj = seq // tm
    rows = batch * seq
    n_steps = batch * nj
    n_s = h3_sample.shape[0]
    assert n_s <= tm

    def tok(width):
        return pl.BlockSpec((tm, width), lambda s: (jnp.minimum(s, n_steps - 1), 0))

    mem_spec = pl.BlockSpec((N_MEM, D_MODEL), lambda s: (jnp.minimum(s, n_steps - 1) // nj, 0))
    sq = _full((D_MODEL, D_MODEL))
    in_specs = [tok(D_MODEL), tok(CONV_CH), tok(GDN_V), tok(GDN_V), _full((1, GDN_DK)), sq,
                _full((1, D_MODEL)), sq, mem_spec, mem_spec, sq, _full((1, D_MODEL)),
                _full((D_MODEL, LANES)), _full((1, LANES)), _full((tm, tm)), _full(h3_sample.shape)]
    out_specs = [tok(D_MODEL),
                 pl.BlockSpec((tm, PACKED), lambda s: (s, 0)),
                 pl.BlockSpec((tm, LANES), lambda s: (jnp.maximum(s - 1, 0), 0)),
                 pl.BlockSpec((ROUTE_ROWS, tm), lambda s: (0, jnp.maximum(s - 1, 0))),
                 _full((SUBLANES, LANES))]
    out_shape = [jax.ShapeDtypeStruct((rows, D_MODEL), F32),
                 jax.ShapeDtypeStruct((rows + n_s, PACKED), jnp.uint32),
                 jax.ShapeDtypeStruct((rows, LANES), F32),
                 jax.ShapeDtypeStruct((ROUTE_ROWS, rows), F32),
                 jax.ShapeDtypeStruct((SUBLANES, LANES), F32)]
    return pl.pallas_call(
        functools.partial(_post_prompt_kernel, n_steps=n_steps),
        grid=(n_steps + 1,),
        in_specs=in_specs,
        out_specs=out_specs,
        out_shape=out_shape,
        scratch_shapes=[pltpu.VMEM((SUBLANES, LANES), F32), pltpu.VMEM((tm, LANES), F32)],
        compiler_params=_cparams(("arbitrary",)),
        name="post_prompt",
    )(x2d, conv, o, z, wts["g_onorm"], wts["w_out_b"], wts["g_xattn"], wts["w_xq_b"], mk_b, mv_b,
      wts["w_xo_b"], wts["g_moe"], wts["w_router"], wts["b_router"], _strict_lower(tm), h3_sample)


def _pre_sample_kernel(x_ref, gmix_ref, w_ref, wab_ref, bglu_ref, wdw_ref, bdw_ref, lng_ref, lnb_ref,
                       wsc_ref, cst_ref, chist_ref, shist_ref, conv_ref, q_ref, k_ref, v_ref, z_ref,
                       gb_ref, cnew_ref, snew_ref):
    glu, qkv_pre, z, uab = _project(x_ref[...], gmix_ref[...], w_ref, wab_ref, bglu_ref[...])
    z_ref[...] = z
    gb_ref[...] = _gate_beta(uab, cst_ref[...])
    kw = CONV_WIDTH
    acc = wdw_ref[kw - 1:kw, :] * glu
    for t in range(kw - 1):
        row = chist_ref[:, t, :]
        acc = acc + wdw_ref[t:t + 1, :] * row
        if t >= 1:
            cnew_ref[:, t - 1, :] = row
    cnew_ref[:, kw - 2, :] = glu
    conv_ref[...] = _conv_post(acc, bdw_ref[...], lng_ref[...], lnb_ref[...]).astype(BF16)
    ks = SHORT_CONV
    cs = wsc_ref[ks - 1:ks, :] * qkv_pre
    for t in range(ks - 1):
        row = shist_ref[:, t, :]
        cs = cs + wsc_ref[t:t + 1, :] * row
        if t >= 1:
            snew_ref[:, t - 1, :] = row
    snew_ref[:, ks - 2, :] = qkv_pre
    q, k, v = _qkv_post(cs)
    q_ref[...] = q
    k_ref[...] = k
    v_ref[...] = v


PRE_SAMPLE_TOKENS = 32


def _pre_sample(xs, chist, shist, wts):
    n = xs.shape[0]
    tb = min(PRE_SAMPLE_TOKENS, n)

    def tok(width):
        return pl.BlockSpec((tb, width), lambda i: (i, 0))

    def hist(a):
        return pl.BlockSpec((tb,) + a.shape[1:], lambda i: (i, 0, 0))

    consts = (wts["g_mix"], wts["w_in_b"], wts["w_ab_b"], wts["b_glu"], wts["w_dw"], wts["b_dw"],
              wts["ln_g"], wts["ln_b"], wts["w_sc"], wts["gdn_cst"])
    return pl.pallas_call(
        _pre_sample_kernel,
        grid=(n // tb,),
        in_specs=[tok(D_MODEL)] + [_full(a.shape) for a in consts] + [hist(chist), hist(shist)],
        out_specs=[tok(CONV_CH), tok(GDN_V), tok(GDN_V), tok(GDN_V), tok(GDN_V), tok(LANES),
                   hist(chist), hist(shist)],
        out_shape=[jax.ShapeDtypeStruct((n, CONV_CH), BF16)]
        + [jax.ShapeDtypeStruct((n, GDN_V), F32)] * 4
        + [jax.ShapeDtypeStruct((n, LANES), F32), jax.ShapeDtypeStruct(chist.shape, F32),
           jax.ShapeDtypeStruct(shist.shape, F32)],
        compiler_params=_cparams(("arbitrary",)),
        name="pre_sample",
    )(xs, *consts, chist, shist)


GDN_STEP_TOKENS = 8


def _gdn_sample_kernel(q_ref, k_ref, v_ref, gb_ref, s_ref, o_ref, snew_ref):
    n = GDN_DK
    for i in range(GDN_STEP_TOKENS):
        for h in range(GDN_HEADS):
            sl = slice(h * GDN_DK, (h + 1) * GDN_DK)
            qrow = q_ref[i:i + 1, sl]
            krow = k_ref[i:i + 1, sl]
            vrow = v_ref[i:i + 1, sl]
            g = gb_ref[i:i + 1, h:h + 1]
            beta = gb_ref[i:i + 1, GDN_HEADS + h:GDN_HEADS + h + 1]
            kcol = jnp.broadcast_to(krow, (n, n)).T
            qcol = jnp.broadcast_to(qrow, (n, n)).T
            s1 = s_ref[i, h] * jnp.exp(g)
            sk = jnp.sum(s1 * kcol, axis=0, keepdims=True)
            vt = (vrow - sk) * beta
            s2 = s1 + kcol * vt
            snew_ref[i, h] = s2
            o_ref[i:i + 1, sl] = jnp.sum(s2 * qcol, axis=0, keepdims=True)


def _gdn_sample(q, k, v, gb, state):
    n = q.shape[0]
    tb = GDN_STEP_TOKENS

    def tok(width):
        return pl.BlockSpec((tb, width), lambda i: (i, 0))

    st = pl.BlockSpec((tb, GDN_HEADS, GDN_DK, GDN_DK), lambda i: (i, 0, 0, 0))
    return pl.pallas_call(
        _gdn_sample_kernel,
        grid=(n // tb,),
        in_specs=[tok(GDN_V), tok(GDN_V), tok(GDN_V), tok(LANES), st],
        out_specs=[tok(GDN_V), st],
        out_shape=[jax.ShapeDtypeStruct((n, GDN_V), F32), jax.ShapeDtypeStruct(state.shape, F32)],
        compiler_params=_cparams(("arbitrary",)),
        name="gdn_sample",
    )(q, k, v, gb, state)


def _mix_sample_kernel(x_ref, conv_ref, o_ref, z_ref, gon_ref, wout_ref, gx_ref, wq_ref, x1_ref, qx_ref):
    x1 = _mix_out(conv_ref[...], o_ref[...], z_ref[...], gon_ref[...], wout_ref, x_ref[...])
    x1_ref[...] = x1
    qx_ref[...] = jnp.dot(_rms(x1, gx_ref[...]).astype(BF16), wq_ref[...], preferred_element_type=F32)


def _mix_sample(xs, conv, o, z, wts):
    n = xs.shape[0]
    in_arrays = (xs, conv, o, z, wts["g_onorm"], wts["w_out_b"], wts["g_xattn"], wts["w_xq_b"])
    return pl.pallas_call(
        _mix_sample_kernel,
        grid=(1,),
        in_specs=[_full(a.shape) for a in in_arrays],
        out_specs=[_full((n, D_MODEL))] * 2,
        out_shape=[jax.ShapeDtypeStruct((n, D_MODEL), F32)] * 2,
        compiler_params=_cparams(("arbitrary",)),
        name="mix_sample",
    )(*in_arrays)


ATTN_STEP_TOKENS = 4


def _attn_sample_kernel(qx_ref, ck_ref, cv_ref, att_ref):
    for i in range(ATTN_STEP_TOKENS):
        parts = []
        for h in range(X_HEADS):
            sl = slice(h * X_HEAD_DIM, (h + 1) * X_HEAD_DIM)
            prod = ck_ref[i, :, h, :] * qx_ref[0, i:i + 1, sl]
            s = jnp.sum(prod, axis=-1, keepdims=True) * (X_HEAD_DIM ** -0.5)
            e = jnp.exp(s - jnp.max(s, axis=0, keepdims=True))
            p = e / jnp.sum(e, axis=0, keepdims=True)
            parts.append(jnp.sum(p * cv_ref[i, :, h, :], axis=0, keepdims=True))
        att_ref[0, i:i + 1, :] = jnp.concatenate(parts, axis=1)


def _attn_sample(qx, ck, cv):
    n = qx.shape[0]
    tb = ATTN_STEP_TOKENS
    q3 = qx.reshape(n // tb, tb, D_MODEL)
    qspec = pl.BlockSpec((1, tb, D_MODEL), lambda i: (i, 0, 0))
    cspec = pl.BlockSpec((tb, N_MEM, X_HEADS, X_HEAD_DIM), lambda i: (i, 0, 0, 0))
    out = pl.pallas_call(
        _attn_sample_kernel,
        grid=(n // tb,),
        in_specs=[qspec, cspec, cspec],
        out_specs=qspec,
        out_shape=jax.ShapeDtypeStruct(q3.shape, F32),
        compiler_params=_cparams(("arbitrary",)),
        name="attn_sample",
    )(q3, ck, cv)
    return out.reshape(n, D_MODEL)


def _route_sample_kernel(x1_ref, att_ref, wo_ref, gmoe_ref, wr_ref, br_ref, tri_ref, x2_ref, h3r_ref, route_ref,
                         rt_ref, cnt_ref):
    x2 = x1_ref[...] + jnp.dot(att_ref[...].astype(BF16), wo_ref[...], preferred_element_type=F32)
    x2_ref[...] = x2
    h3, logits = _router_logits(x2, gmoe_ref[...], wr_ref, br_ref[...])
    route, counts = _route_finish(logits, jnp.zeros((1, LANES), F32), tri_ref[...])
    h3r_ref[...] = _pack_rows(h3)
    route_ref[...] = route
    rt_ref[...] = route.T[0:ROUTE_ROWS, :]
    cnt_ref[...] = jnp.broadcast_to(counts, cnt_ref.shape)


def _route_sample(x1, att, wts):
    n = x1.shape[0]
    in_arrays = (x1, att, wts["w_xo_b"], wts["g_moe"], wts["w_router"], wts["b_router"],
                 _strict_lower(n))
    shapes = [(n, D_MODEL), (n, PACKED), (n, LANES), (ROUTE_ROWS, n), (SUBLANES, LANES)]
    dtypes = [F32, jnp.uint32, F32, F32, F32]
    return pl.pallas_call(
        _route_sample_kernel,
        grid=(1,),
        in_specs=[_full(a.shape) for a in in_arrays],
        out_specs=[_full(s) for s in shapes],
        out_shape=[jax.ShapeDtypeStruct(s, d) for s, d in zip(shapes, dtypes)],
        compiler_params=_cparams(("arbitrary",)),
        name="route_sample",
    )(*in_arrays)


def _part_tiles(tiles):
    n_workers = SC_CORES * SC_SUBCORES
    while True:
        rows = tiles * MOE_TILE
        if rows % (n_workers * SUBLANES) == 0 and any(
                (rows // n_workers) % c == 0 for c in range(64, 24, -SUBLANES)):
            return tiles
        tiles += 1


def _sc_chunk(rows_per_worker):
    for c in range(64, 0, -SUBLANES):
        if rows_per_worker % c == 0:
            return c
    raise ValueError(rows_per_worker)


def _sc_gather_rows(table, idx):
    n_workers = SC_CORES * SC_SUBCORES
    b = idx.shape[0]
    assert b % (n_workers * SUBLANES) == 0
    per_worker = b // n_workers
    chunk = _sc_chunk(per_worker)
    row_shape = table.shape[1:]
    mesh = plsc.VectorSubcoreMesh(core_axis_name="c", subcore_axis_name="s")

    @functools.partial(
        pl.kernel, mesh=mesh,
        out_type=jax.ShapeDtypeStruct((b,) + row_shape, table.dtype),
        scratch_types=[pltpu.VMEM((chunk,), I32), pltpu.VMEM((chunk,) + row_shape, table.dtype),
                       pltpu.SemaphoreType.DMA],
        name="sc_gather_rows",
    )
    def gather(table_hbm, idx_hbm, out_hbm, idx_v, rows_v, sem):
        worker = lax.axis_index("s") * SC_CORES + lax.axis_index("c")
        base = worker * per_worker

        @pl.loop(0, per_worker // chunk)
        def _(c):
            off = pl.multiple_of(base + c * chunk, SUBLANES)
            pltpu.sync_copy(idx_hbm.at[pl.ds(off, chunk)], idx_v)
            pltpu.async_copy(table_hbm.at[idx_v], rows_v, sem).wait()
            pltpu.sync_copy(rows_v, out_hbm.at[pl.ds(off, chunk)])

    return gather(table, idx)


def _moe_kernel(te_ref, tn_ref, nt_ref, xs_ref, wgu_hbm, bgu_ref, wdn_hbm, bdn_ref, *rest, first_tile):
    ys_ref, wgu_f, wdn_f, wgu_b, wdn_b, sems = rest[-6:]
    step = pl.program_id(0)
    i = first_tile + step
    total = nt_ref[0]

    def weight_copies(e):
        return (pltpu.make_async_copy(wgu_hbm.at[e], wgu_f, sems.at[0]),
                pltpu.make_async_copy(wdn_hbm.at[e], wdn_f, sems.at[1]))

    def start(e):
        for cp in weight_copies(e):
            cp.start()

    @pl.when(i < total)
    def _():
        expert = te_ref[i]
        prev = te_ref[jnp.maximum(i - 1, 0)]
        fresh = jnp.logical_or(step == 0, expert != prev)

        @pl.when(step == 0)
        def _():
            start(expert)

        @pl.when(fresh)
        def _():
            for cp in weight_copies(expert):
                cp.wait()
            wgu_b[...] = wgu_f[...].astype(BF16)
            wdn_b[...] = wdn_f[...].astype(BF16)
            nxt = tn_ref[i]

            @pl.when(nxt >= 0)
            def _():
                start(nxt)

        x = _unpack_rows(xs_ref[...]).astype(BF16)

        def up(c):
            glu_cols = slice(c * MOE_COLS, (c + 1) * MOE_COLS)
            lin_cols = slice(D_EXPERT + c * MOE_COLS, D_EXPERT + (c + 1) * MOE_COLS)
            return (jnp.dot(x, wgu_b[:, glu_cols], preferred_element_type=F32) + bgu_ref[expert, :, glu_cols],
                    jnp.dot(x, wgu_b[:, lin_cols], preferred_element_type=F32) + bgu_ref[expert, :, lin_cols])

        n_chunks = D_EXPERT // MOE_COLS
        nxt = up(0)
        y = None
        for c in range(n_chunks):
            g, lin = nxt
            if c + 1 < n_chunks:
                nxt = up(c + 1)
            x_glu = jnp.minimum(g, SWIGLU_LIMIT)
            x_lin = jnp.clip(lin, -SWIGLU_LIMIT, SWIGLU_LIMIT)
            act = x_glu * jax.nn.sigmoid(SWIGLU_ALPHA * x_glu) * (x_lin + 1.0)
            part = jnp.dot(act.astype(BF16), wdn_b[c * MOE_COLS:(c + 1) * MOE_COLS, :],
                           preferred_element_type=F32)
            y = part if y is None else y + part
        ys_ref[...] = _pack_rows(y + bdn_ref[expert])

    @pl.when(i >= total)
    def _():
        ys_ref[...] = jnp.zeros(ys_ref.shape, jnp.uint32)


def _moe(tile_e, tile_next, n_tiles, xs_part, ys_prev, first_tile, n_rows, w_gu, b_gu, w_dn, b_dn):
    tm = MOE_TILE

    def resident(a):
        return pl.BlockSpec(a.shape, lambda i, te, tn, nt: (0, 0, 0))

    hbm = pl.BlockSpec(memory_space=pl.ANY)
    in_specs = [pl.BlockSpec((tm, PACKED), lambda i, te, tn, nt: (i, 0)),
                hbm, resident(b_gu), hbm, resident(b_dn)]
    operands = [tile_e, tile_next, n_tiles, xs_part, w_gu, b_gu, w_dn, b_dn]
    aliases = {}
    if ys_prev is not None:
        in_specs.append(hbm)
        aliases = {len(operands): 0}
        operands.append(ys_prev)
    grid_spec = pltpu.PrefetchScalarGridSpec(
        num_scalar_prefetch=3,
        grid=(xs_part.shape[0] // tm,),
        in_specs=in_specs,
        out_specs=pl.BlockSpec((tm, PACKED), lambda i, te, tn, nt: (first_tile + i, 0)),
        scratch_shapes=[pltpu.VMEM((D_MODEL, 2 * D_EXPERT), F32),
                        pltpu.VMEM((D_EXPERT, D_MODEL), F32),
                        pltpu.VMEM((D_MODEL, 2 * D_EXPERT), BF16),
                        pltpu.VMEM((D_EXPERT, D_MODEL), BF16),
                        pltpu.SemaphoreType.DMA((2,))],
    )
    return pl.pallas_call(
        functools.partial(_moe_kernel, first_tile=first_tile),
        grid_spec=grid_spec,
        out_shape=jax.ShapeDtypeStruct((n_rows, PACKED), jnp.uint32),
        input_output_aliases=aliases,
        compiler_params=_cparams(("arbitrary",), vmem=56 * 1024 * 1024),
        name="moe",
    )(*operands)


def _combine_kernel(x2_ref, route_ref, gfin_ref, yt_ref, *rest):
    y_ref = rest[-1]
    route = route_ref[...]
    acc = x2_ref[...]
    for j in range(TOP_K):
        acc = acc + route[:, TOP_K + j:TOP_K + j + 1] * _unpack_rows(yt_ref[j])
    y_ref[...] = _rms(acc, gfin_ref[...])


def _combine(x2, route, g_final, ys_tok, y_prev, tok0, n_tok, ys_block0):
    tc = min(COMBINE_TILE, n_tok)
    b0 = tok0 // tc

    def tok(width):
        return pl.BlockSpec((tc, width), lambda i: (b0 + i, 0))

    in_specs = [tok(D_MODEL), tok(LANES), pl.BlockSpec((1, D_MODEL), lambda i: (0, 0)),
                pl.BlockSpec((TOP_K, tc, PACKED), lambda i: (0, ys_block0 + i, 0))]
    operands = [x2, route, g_final, ys_tok]
    aliases = {}
    if y_prev is not None:
        in_specs.append(pl.BlockSpec(memory_space=pl.ANY))
        aliases = {len(operands): 0}
        operands.append(y_prev)
    return pl.pallas_call(
        _combine_kernel,
        grid=(n_tok // tc,),
        in_specs=in_specs,
        out_specs=tok(D_MODEL),
        out_shape=jax.ShapeDtypeStruct(x2.shape, F32),
        input_output_aliases=aliases,
        compiler_params=_cparams(("arbitrary",)),
        name="combine",
    )(*operands)


def _routing_tables(idx_t, rank_t, counts, n_rows, part_tiles):
    tm = MOE_TILE
    n_tok = idx_t.shape[1]
    n_assign = TOP_K * n_tok
    tok_mask = (1 << TOKEN_BITS) - 1
    tiles_e = (counts + tm - 1) // tm
    tile_end = jnp.cumsum(tiles_e)
    row_start = (tile_end - tiles_e) * tm
    total = tile_end[-1]
    expert_ids = jnp.arange(N_EXPERTS, dtype=I32)

    def lookup(table, e):
        return jnp.sum(jnp.where(e[..., None] == expert_ids, table, 0), axis=-1)

    pos = lookup(row_start, idx_t) + rank_t
    keys_real = (idx_t * (1 << TOKEN_BITS) + jnp.arange(n_tok, dtype=I32)[None, :]).reshape(-1)
    k = jnp.arange(n_rows - n_assign, dtype=I32)
    pad_e, pad_s = k // tm, k % tm
    pad_needed = lookup(tiles_e * tm - counts, pad_e)
    pad_key_e = jnp.where((pad_e < N_EXPERTS) & (pad_s < pad_needed), pad_e, N_EXPERTS)
    keys = lax.sort(jnp.concatenate([keys_real, pad_key_e * (1 << TOKEN_BITS) + tok_mask]),
                    is_stable=False)
    src_tok = jnp.where((keys & tok_mask) == tok_mask, jnp.arange(n_rows, dtype=I32) % n_tok,
                        keys & tok_mask)
    tid = jnp.minimum(jnp.arange(n_rows // tm, dtype=I32), total - 1)
    tile_e = jnp.minimum(jnp.sum((tid[:, None] >= tile_end[None, :]).astype(I32), axis=1), N_EXPERTS - 1)
    later = (expert_ids[None, :] > expert_ids[:, None]) & (tiles_e[None, :] > 0)
    next_e = jnp.min(jnp.where(later, expert_ids[None, :], N_EXPERTS), axis=1)
    nxt = lookup(next_e, tile_e)
    nxt_first_tile = lookup(tile_end - tiles_e, jnp.minimum(nxt, N_EXPERTS - 1))
    bounds = jnp.cumsum(jnp.asarray(part_tiles, I32))
    part_end = jnp.min(jnp.where(bounds[None, :] > tid[:, None], bounds[None, :], n_rows), axis=1)
    tile_next = jnp.where((nxt < N_EXPERTS) & (nxt_first_tile < part_end), nxt, -1)
    return (tile_e.astype(I32), tile_next.astype(I32), total.reshape(1).astype(I32), src_tok.astype(I32),
            pos.reshape(-1).astype(I32))


def _pad_rows(a, rows):
    return jnp.concatenate([a, jnp.zeros((rows - a.shape[0],) + a.shape[1:], a.dtype)], axis=0)


def _pad_lanes(a, lanes=LANES):
    return jnp.concatenate([a, jnp.zeros(a.shape[:-1] + (lanes - a.shape[-1],), a.dtype)], axis=-1)


def kernel(x_prompt, mem_prompt, x_sample, state_conformer_conv, state_gdn_conv, state_gdn, cache_mem_k,
           cache_mem_v, w_in, b_glu, w_dw, b_dw, ln_g, ln_b, w_sc, a_log, dt_bias, g_onorm, w_out, g_mix,
           g_xattn, g_mem, w_xq, w_mk, w_mv, w_xo, g_moe, w_router, b_router, w_gu, b_gu, w_dn, b_dn,
           g_final):
    assert w_in.shape[0] == 1, "single-layer configuration"
    batch, seq, _ = x_prompt.shape
    n_s = x_sample.shape[0]
    n_p = batch * seq
    n_all = n_p + n_s
    assert seq % TOKEN_TILE == 0 and n_p % n_s == 0 and n_all < (1 << TOKEN_BITS) - 1
    assert (n_all * TOP_K) % (SC_CORES * SC_SUBCORES * SUBLANES) == 0

    wts = {
        "g_mix": g_mix[0][None], "g_xattn": g_xattn[0][None], "g_moe": g_moe[0][None],
        "g_onorm": g_onorm[0][None],
        "w_in_b": w_in[0][:, :OFF_A].astype(BF16),
        "w_ab_b": _pad_lanes(w_in[0][:, OFF_A:]).astype(BF16),
        "b_glu": b_glu[0][None],
        "w_dw": _pad_rows(w_dw[0], 32), "b_dw": b_dw[0][None], "ln_g": ln_g[0][None], "ln_b": ln_b[0][None],
        "w_sc": _pad_rows(w_sc[0], 8),
        "gdn_cst": _pad_rows(_pad_lanes(jnp.stack([a_log[0], dt_bias[0]])), 8),
        "w_out_b": w_out[0].astype(BF16), "w_xq_b": w_xq[0].astype(BF16), "w_xo_b": w_xo[0].astype(BF16),
        "w_router": _pad_lanes(w_router[0]), "b_router": _pad_lanes(b_router[0][None]),
    }

    mk, mv, mk_b, mv_b = _mem_kv(mem_prompt.reshape(batch * N_MEM, D_MODEL), g_mem[0][None],
                                 w_mk[0].astype(BF16), w_mv[0].astype(BF16))
    xp = x_prompt.reshape(n_p, D_MODEL)
    conv_p, qkv_p, z_p, gb_p, cstate_p, sstate_p = _pre_prompt(xp, batch, seq, wts)
    o_p, gstate_p = _gdn_prompt(qkv_p, gb_p, wts["w_sc"], batch, seq)

    xs = x_sample.reshape(n_s, D_MODEL)
    conv_s, q_s, k_s, v_s, z_s, gb_s, cstate_s, sstate_s = _pre_sample(
        xs, state_conformer_conv[0], state_gdn_conv[0], wts)
    o_s, gstate_s = _gdn_sample(q_s, k_s, v_s, gb_s, state_gdn[0])
    x1_s, qx_s = _mix_sample(xs, conv_s, o_s, z_s, wts)
    att_s = _attn_sample(qx_s, cache_mem_k[0], cache_mem_v[0])
    x2_s, h3_s, route_s, rt_s, counts_s = _route_sample(x1_s, att_s, wts)

    x2_p, h3r, route_p, rt_p, counts_p = _post_prompt(xp, conv_p, o_p, z_p, mk_b, mv_b, h3_s, batch, seq,
                                                      wts)

    counts_p = counts_p[0, :N_EXPERTS].astype(I32)
    counts_s = counts_s[0, :N_EXPERTS].astype(I32)
    idx_s = rt_s[0:TOP_K].astype(I32)
    rank_s = rt_s[2 * TOP_K:3 * TOP_K].astype(I32) + jnp.sum(
        jnp.where(idx_s[..., None] == jnp.arange(N_EXPERTS, dtype=I32), counts_p, 0), axis=-1)
    idx_t = jnp.concatenate([rt_p[0:TOP_K].astype(I32), idx_s], axis=1)
    rank_t = jnp.concatenate([rt_p[2 * TOP_K:3 * TOP_K].astype(I32), rank_s], axis=1)
    min_tiles = -(-(n_all * TOP_K + N_EXPERTS * (MOE_TILE - 1)) // MOE_TILE)
    part_tiles = [_part_tiles(-(-min_tiles * w // sum(MOE_PART_WEIGHTS))) for w in MOE_PART_WEIGHTS]
    first_tiles = [sum(part_tiles[:k]) for k in range(len(part_tiles))]
    n_rows = sum(part_tiles) * MOE_TILE
    tile_e, tile_next, n_tiles, src_tok, pos = _routing_tables(idx_t, rank_t, counts_p + counts_s, n_rows,
                                                               part_tiles)
    pos_t = pos.reshape(TOP_K, n_all)
    xs_parts = [_sc_gather_rows(h3r, src_tok[f * MOE_TILE:(f + t) * MOE_TILE])
                for f, t in zip(first_tiles, part_tiles)]
    ys = None
    for k, f in enumerate(first_tiles):
        ys = _moe(tile_e, tile_next, n_tiles, xs_parts[k], ys, f, n_rows, w_gu[0],
                  b_gu[0][:, None, :], w_dn[0], b_dn[0][:, None, :])
    gfin = g_final[None]
    assert n_p % (sum(TOKEN_PART_WEIGHTS) * COMBINE_TILE) == 0
    tok_parts = [n_p * w // sum(TOKEN_PART_WEIGHTS) for w in TOKEN_PART_WEIGHTS]
    cut = -(-n_s // COMBINE_TILE) * COMBINE_TILE
    sample_part = 0 if (tok_parts[0] > cut and tok_parts[0] % min(COMBINE_TILE, n_s) == 0) \
        else len(tok_parts) - 1
    if sample_part == 0:
        tok_parts[0] -= cut
        tok_parts[-1] += cut
    y_p = None
    for k, n_tok in enumerate(tok_parts):
        tok0 = sum(tok_parts[:k])
        pos_k = pos_t[:, tok0:tok0 + n_tok]
        if k == sample_part:
            pos_k = jnp.concatenate([pos_k, pos_t[:, n_p:]], axis=1)
        ys_tok = _sc_gather_rows(ys, pos_k.reshape(-1)).reshape(TOP_K, pos_k.shape[1], PACKED)
        y_p = _combine(x2_p, route_p, gfin, ys_tok, y_p, tok0, n_tok, 0)
        if k == sample_part:
            y_s = _combine(x2_s, route_s, gfin, ys_tok, None, 0, n_s, n_tok // min(COMBINE_TILE, n_s))

    return (y_p.reshape(batch, seq, D_MODEL), y_s.reshape(n_s, 1, D_MODEL),
            cstate_p[None], sstate_p[None], gstate_p[None],
            mk[None], mv[None],
            cstate_s[None], sstate_s[None], gstate_s[None])
```

```python
import functools

import jax
import jax.numpy as jnp
from jax import lax
from jax.experimental import pallas as pl
from jax.experimental.pallas import tpu as pltpu
from jax.experimental.pallas import tpu_sc as plsc

F32, BF16, I32 = jnp.float32, jnp.bfloat16, jnp.int32

D_MODEL = 1024
CONV_CH = 512
CONV_WIDTH = 31
GDN_HEADS = 4
GDN_DK = 128
GDN_V = 512
QKV_CH = 1536
SHORT_CONV = 4
N_MEM = 256
X_HEADS = 4
X_HEAD_DIM = 256
N_EXPERTS = 32
TOP_K = 4
D_EXPERT = 1024
SWIGLU_LIMIT = 7.0
SWIGLU_ALPHA = 1.702
NORM_EPS = 1e-6
OFF_QKV = 2 * CONV_CH
OFF_Z = OFF_QKV + QKV_CH
OFF_A = OFF_Z + GDN_V

LANES = 128
SUBLANES = 8
GDN_BLOCK = 128
TOKEN_TILE = 512
MOE_TILE = 384
MOE_COLS = 256
MOE_PART_WEIGHTS = (3, 6, 8, 9)
TOKEN_PART_WEIGHTS = (1, 2, 2, 3)
COMBINE_TILE = 256
TOKEN_BITS = 15
ROUTE_ROWS = 16
SC_CORES = 2
SC_SUBCORES = 16
VMEM_LIMIT = 48 * 1024 * 1024


def _cparams(sem, vmem=VMEM_LIMIT):
    return pltpu.CompilerParams(dimension_semantics=sem, vmem_limit_bytes=vmem)


def _mm(a, b):
    return jnp.dot(a.astype(BF16), b.astype(BF16), preferred_element_type=F32)


def _mm_nt(a, b):
    return lax.dot_general(a.astype(BF16), b.astype(BF16), (((1,), (1,)), ((), ())),
                           preferred_element_type=F32)


def _mm_tn(a, b):
    return lax.dot_general(a.astype(BF16), b.astype(BF16), (((0,), (0,)), ((), ())),
                           preferred_element_type=F32)


def _rms(x, g):
    return x * lax.rsqrt(jnp.mean(x * x, axis=-1, keepdims=True) + NORM_EPS) * g


def _silu(x):
    return x * jax.nn.sigmoid(x)


PACKED = D_MODEL // 2


def _pack_rows(h):
    lo = pltpu.bitcast(h[:, :PACKED].astype(BF16).astype(F32), jnp.uint32)
    hi = pltpu.bitcast(h[:, PACKED:].astype(BF16).astype(F32), jnp.uint32)
    return (lo >> 16) | (hi & jnp.uint32(0xFFFF0000))


def _unpack_rows(w):
    lo = pltpu.bitcast(w << 16, F32)
    hi = pltpu.bitcast(w & jnp.uint32(0xFFFF0000), F32)
    return jnp.concatenate([lo, hi], axis=1)


def _full(shape):
    return pl.BlockSpec(shape, lambda *_: (0,) * len(shape))


def _strict_lower(n):
    return jnp.tril(jnp.ones((n, n), BF16), k=-1)


def _project(x, gmix, w_ref, wab_ref, bglu):
    h = _rms(x, gmix).astype(BF16)
    u_glu = jnp.dot(h, w_ref[:, 0:OFF_QKV], preferred_element_type=F32) + bglu
    glu = u_glu[:, :CONV_CH] * jax.nn.sigmoid(u_glu[:, CONV_CH:])
    qkv_pre = jnp.dot(h, w_ref[:, OFF_QKV:OFF_Z], preferred_element_type=F32)
    z = jnp.dot(h, w_ref[:, OFF_Z:OFF_A], preferred_element_type=F32)
    uab = jnp.dot(h, wab_ref[...], preferred_element_type=F32)
    return glu, qkv_pre, z, uab


def _gate_beta(uab, cst):
    lane = lax.broadcasted_iota(I32, uab.shape, 1)
    g = -jnp.exp(cst[0:1, :]) * jax.nn.softplus(uab + cst[1:2, :])
    return jnp.where(lane < GDN_HEADS, g, jax.nn.sigmoid(uab))


def _conv_post(c, b_dw, ln_g, ln_b):
    c = c + b_dw
    mu = jnp.mean(c, axis=-1, keepdims=True)
    d = c - mu
    var = jnp.mean(d * d, axis=-1, keepdims=True)
    return _silu(d * lax.rsqrt(var + NORM_EPS) * ln_g + ln_b)


def _qkv_post(cs):
    a = _silu(cs)
    parts = []
    for h in range(2 * GDN_HEADS):
        seg = a[:, h * GDN_DK:(h + 1) * GDN_DK]
        n = seg * lax.rsqrt(jnp.sum(seg * seg, axis=-1, keepdims=True) + NORM_EPS)
        if h < GDN_HEADS:
            n = n * (GDN_DK ** -0.5)
        parts.append(n)
    q = jnp.concatenate(parts[:GDN_HEADS], axis=1)
    k = jnp.concatenate(parts[GDN_HEADS:], axis=1)
    return q, k, a[:, 2 * GDN_HEADS * GDN_DK:]


def _mix_out(conv_b, o, z, gon, wout_ref, x):
    parts = []
    for h in range(GDN_HEADS):
        oh = o[:, h * 128:(h + 1) * 128]
        parts.append(oh * lax.rsqrt(jnp.mean(oh * oh, axis=-1, keepdims=True) + NORM_EPS) * gon)
    on = jnp.concatenate(parts, axis=1) * _silu(z)
    mixed = (jnp.dot(conv_b, wout_ref[0:CONV_CH, :], preferred_element_type=F32)
             + jnp.dot(on.astype(BF16), wout_ref[CONV_CH:, :], preferred_element_type=F32))
    return x + mixed


def _router_logits(x2, gmoe, wr_ref, br):
    h3 = _rms(x2, gmoe)
    h_hi = h3.astype(BF16)
    r1 = h3 - h_hi.astype(F32)
    h_mid = r1.astype(BF16)
    h_lo = (r1 - h_mid.astype(F32)).astype(BF16)
    w = wr_ref[...]
    w_hi = w.astype(BF16)
    w_lo = (w - w_hi.astype(F32)).astype(BF16)
    logits = (jnp.dot(h_hi, w_hi, preferred_element_type=F32)
              + jnp.dot(h_hi, w_lo, preferred_element_type=F32)
              + jnp.dot(h_mid, w_hi, preferred_element_type=F32)
              + jnp.dot(h_lo, w_hi, preferred_element_type=F32)) + br
    return h3, logits


def _route_finish(logits, carry, before):
    m = logits.shape[0]
    neg = jnp.float32(-jnp.inf)
    n_groups = 4 if m % 32 == 0 else 1
    rows = m // n_groups
    lanes_g = lax.broadcasted_iota(I32, (rows, LANES), 1).astype(F32)
    works = [jnp.where(lanes_g < N_EXPERTS, logits[g * rows:(g + 1) * rows], neg) for g in range(n_groups)]
    vals, idxs = [], []
    for _ in range(TOP_K):
        mxs = [jnp.max(w, axis=-1, keepdims=True) for w in works]
        ixs = [jnp.min(jnp.where(w == mx, lanes_g, float(LANES)), axis=-1, keepdims=True)
               for w, mx in zip(works, mxs)]
        works = [jnp.where(lanes_g == ix, neg, w) for w, ix in zip(works, ixs)]
        vals.append(mxs)
        idxs.append(ixs)
    sels, gates = [], []
    for g in range(n_groups):
        es = [jnp.exp(vals[r][g] - vals[0][g]) for r in range(TOP_K)]
        den = es[0] + es[1] + es[2] + es[3]
        gates.append([e / den for e in es])
        sel_g = jnp.zeros((rows, LANES), F32)
        for r in range(TOP_K):
            sel_g = sel_g + jnp.where(lanes_g == idxs[r][g], 1.0, 0.0)
        sels.append(sel_g)
    sel = jnp.concatenate(sels, axis=0)
    rank_full = jnp.dot(before, sel.astype(BF16), preferred_element_type=F32) + carry
    routes = []
    for g in range(n_groups):
        rank_g = rank_full[g * rows:(g + 1) * rows]
        route_g = jnp.zeros((rows, LANES), F32)
        for r in range(TOP_K):
            rk = jnp.sum(jnp.where(lanes_g == idxs[r][g], rank_g, 0.0), axis=-1, keepdims=True)
            route_g = (route_g + jnp.where(lanes_g == r, idxs[r][g], 0.0)
                       + jnp.where(lanes_g == TOP_K + r, gates[g][r], 0.0)
                       + jnp.where(lanes_g == 2 * TOP_K + r, rk, 0.0))
        routes.append(route_g)
    new_carry = carry + jnp.sum(sel, axis=0, keepdims=True)
    return jnp.concatenate(routes, axis=0), new_carry


def _mem_kv_kernel(mem_ref, g_ref, wk_ref, wv_ref, mk_ref, mv_ref, mkb_ref, mvb_ref):
    m = _rms(mem_ref[...], g_ref[...]).astype(BF16)
    mk = jnp.dot(m, wk_ref[...], preferred_element_type=F32)
    mv = jnp.dot(m, wv_ref[...], preferred_element_type=F32)
    for h in range(X_HEADS):
        sl = slice(h * X_HEAD_DIM, (h + 1) * X_HEAD_DIM)
        mk_ref[0, :, h, :] = mk[:, sl]
        mv_ref[0, :, h, :] = mv[:, sl]
    mkb_ref[...] = mk.astype(BF16)
    mvb_ref[...] = mv.astype(BF16)


def _mem_kv(mem, g_mem, wk_b, wv_b):
    rows = mem.shape[0]
    tm = N_MEM
    row_spec = pl.BlockSpec((tm, D_MODEL), lambda i: (i, 0))
    head_spec = pl.BlockSpec((1, tm, X_HEADS, X_HEAD_DIM), lambda i: (i, 0, 0, 0))
    head_shape = jax.ShapeDtypeStruct((rows // tm, tm, X_HEADS, X_HEAD_DIM), F32)
    return pl.pallas_call(
        _mem_kv_kernel,
        grid=(rows // tm,),
        in_specs=[row_spec, _full((1, D_MODEL)), _full((D_MODEL, D_MODEL)), _full((D_MODEL, D_MODEL))],
        out_specs=[head_spec, head_spec, row_spec, row_spec],
        out_shape=[head_shape, head_shape] + [jax.ShapeDtypeStruct((rows, D_MODEL), BF16)] * 2,
        compiler_params=_cparams(("arbitrary",)),
        name="mem_kv",
    )(mem, g_mem, wk_b, wv_b)


CONV_HALO = 32
SC_HALO = 8


def _pre_prompt_kernel(x_ref, gmix_ref, w_ref, wab_ref, bglu_ref, wdw_ref, bdw_ref, lng_ref, lnb_ref,
                       cst_ref, conv_ref, qkv_ref, z_ref, gb_ref, cstate_ref, sstate_ref, cbuf, *, tm):
    j = pl.program_id(1)

    @pl.when(j == 0)
    def _():
        cbuf[0:CONV_HALO, :] = jnp.zeros((CONV_HALO, CONV_CH), F32)

    glu, qkv_pre, z, uab = _project(x_ref[...], gmix_ref[...], w_ref, wab_ref, bglu_ref[...])
    cbuf[CONV_HALO:CONV_HALO + tm, :] = glu
    qkv_ref[...] = qkv_pre
    z_ref[...] = z
    gb_ref[...] = _gate_beta(uab, cst_ref[...])

    base = CONV_HALO - (CONV_WIDTH - 1)
    rows = CONV_HALO + tm
    accs = []
    for c in range(CONV_CH // LANES):
        lanes = slice(c * LANES, (c + 1) * LANES)
        block = cbuf[:, lanes]
        acc = None
        for r in range(SUBLANES):
            shifted = block if r == 0 else pltpu.roll(block, rows - r, 0)
            for a in range(base, base + CONV_WIDTH):
                if a % SUBLANES == r:
                    t = a - base
                    term = wdw_ref[t:t + 1, lanes] * shifted[a - r:a - r + tm, :]
                    acc = term if acc is None else acc + term
        accs.append(acc)
    acc = jnp.concatenate(accs, axis=1)
    conv_ref[...] = _conv_post(acc, bdw_ref[...], lng_ref[...], lnb_ref[...]).astype(BF16)

    @pl.when(j == pl.num_programs(1) - 1)
    def _():
        cstate_ref[0] = cbuf[pl.ds(CONV_HALO + tm - (CONV_WIDTH - 1), CONV_WIDTH - 1), :]
        sstate_ref[0] = qkv_pre[tm - (SHORT_CONV - 1):, :]

    cbuf[0:CONV_HALO, :] = cbuf[tm:tm + CONV_HALO, :]


def _pre_prompt(x2d, batch, seq, wts):
    tm = TOKEN_TILE
    nj = seq // tm
    rows = batch * seq

    def tok(width):
        return pl.BlockSpec((tm, width), lambda b, j: (b * nj + j, 0))

    in_specs = [tok(D_MODEL), _full((1, D_MODEL)), _full((D_MODEL, OFF_A)), _full((D_MODEL, LANES)),
                _full((1, OFF_QKV)), _full((32, CONV_CH)), _full((1, CONV_CH)), _full((1, CONV_CH)),
                _full((1, CONV_CH)), _full((8, LANES))]
    out_specs = [tok(CONV_CH), tok(QKV_CH), tok(GDN_V), tok(LANES),
                 pl.BlockSpec((1, CONV_WIDTH - 1, CONV_CH), lambda b, j: (b, 0, 0)),
                 pl.BlockSpec((1, SHORT_CONV - 1, QKV_CH), lambda b, j: (b, 0, 0))]
    out_shape = [jax.ShapeDtypeStruct((rows, CONV_CH), BF16),
                 jax.ShapeDtypeStruct((rows, QKV_CH), F32),
                 jax.ShapeDtypeStruct((rows, GDN_V), F32),
                 jax.ShapeDtypeStruct((rows, LANES), F32),
                 jax.ShapeDtypeStruct((batch, CONV_WIDTH - 1, CONV_CH), F32),
                 jax.ShapeDtypeStruct((batch, SHORT_CONV - 1, QKV_CH), F32)]
    return pl.pallas_call(
        functools.partial(_pre_prompt_kernel, tm=tm),
        grid=(batch, nj),
        in_specs=in_specs,
        out_specs=out_specs,
        out_shape=out_shape,
        scratch_shapes=[pltpu.VMEM((CONV_HALO + tm, CONV_CH), F32)],
        compiler_params=_cparams(("arbitrary", "arbitrary")),
        name="pre_prompt",
    )(x2d, wts["g_mix"], wts["w_in_b"], wts["w_ab_b"], wts["b_glu"], wts["w_dw"], wts["b_dw"],
      wts["ln_g"], wts["ln_b"], wts["gdn_cst"])


GDN_SEQS = 4


def _gdn_prompt_kernel(qkv_ref, gb_ref, wsc_ref, o_ref, sfin_ref, s_scr, sbuf):
    c = pl.program_id(1)
    n = GDN_BLOCK
    seqs = range(GDN_SEQS)

    @pl.when(c == 0)
    def _():
        s_scr[...] = jnp.zeros(s_scr.shape, F32)
        sbuf[:, 0:SC_HALO, :] = jnp.zeros((GDN_SEQS, SC_HALO, QKV_CH), F32)

    sbase = SC_HALO - (SHORT_CONV - 1)
    qkvs = []
    for b in seqs:
        sbuf[b, SC_HALO:SC_HALO + n, :] = qkv_ref[b]
        block = sbuf[b]
        cs = None
        for t in range(SHORT_CONV):
            r = (sbase + t) % SUBLANES
            shifted = block if r == 0 else pltpu.roll(block, SC_HALO + n - r, 0)
            term = wsc_ref[t:t + 1, :] * shifted[sbase + t - r:sbase + t - r + n, :]
            cs = term if cs is None else cs + term
        qkvs.append(_qkv_post(cs))
        sbuf[b, 0:SC_HALO, :] = sbuf[b, n:n + SC_HALO, :]

    row = lax.broadcasted_iota(I32, (n, n), 0)
    col = lax.broadcasted_iota(I32, (n, n), 1)
    causal = row >= col
    strict = row > col
    tri = jnp.where(causal, 1.0, 0.0).astype(BF16)
    eye = jnp.where(row == col, 1.0, 0.0)
    level_masks = []
    b = 1
    while b < n:
        same_pair = ((row ^ col) & ~(2 * b - 1)) == 0
        level_masks.append(same_pair & ((row & b) != 0) & ((col & b) == 0))
        b *= 2
    gbs, gcums, gcum_ts, egcs = [], [], [], []
    for b in seqs:
        gb = gb_ref[b]
        g1 = gb.astype(BF16)
        r1 = gb - g1.astype(F32)
        g2 = r1.astype(BF16)
        g3 = (r1 - g2.astype(F32)).astype(BF16)
        gcum = (jnp.dot(tri, g1, preferred_element_type=F32) + jnp.dot(tri, g2, preferred_element_type=F32)
                + jnp.dot(tri, g3, preferred_element_type=F32))
        gbs.append(gb)
        gcums.append(gcum)
        gcum_ts.append(gcum.T)
        egcs.append(jnp.exp(gcum))
    units = [(b, h) for b in seqs for h in range(GDN_HEADS)]
    idx = range(len(units))
    sls = [slice(h * GDN_DK, (h + 1) * GDN_DK) for h in range(GDN_HEADS)]
    qs = [qkvs[b][0][:, sls[h]] for b, h in units]
    ks = [qkvs[b][1][:, sls[h]] for b, h in units]
    vs = [qkvs[b][2][:, sls[h]] for b, h in units]
    ss = [s_scr[b, h] for b, h in units]
    gcols = [gcums[b][:, h:h + 1] for b, h in units]
    ecols = [egcs[b][:, h:h + 1] for b, h in units]
    betas = [gbs[b][:, GDN_HEADS + h:GDN_HEADS + h + 1] for b, h in units]
    glasts = [gcums[b][n - 1:n, h:h + 1] for b, h in units]
    decays = [jnp.where(causal, jnp.exp(jnp.where(causal, gcols[u] - gcum_ts[b][h:h + 1, :], 0.0)), 0.0)
              for u, (b, h) in enumerate(units)]
    kbs = [ks[u] * betas[u] for u in idx]
    lowers = [jnp.where(strict, _mm_nt(kbs[u], ks[u]) * decays[u], 0.0) for u in idx]
    intras = [jnp.where(causal, _mm_nt(qs[u], ks[u]) * decays[u], 0.0) for u in idx]
    xs = [eye - jnp.where(level_masks[0], lowers[u], 0.0) for u in idx]
    for mask in level_masks[1:]:
        ts = [_mm(xs[u], jnp.where(mask, lowers[u], 0.0)) for u in idx]
        xs = [xs[u] - _mm(ts[u], xs[u]) for u in idx]
    us = [_mm(xs[u], vs[u] * betas[u]) for u in idx]
    ws = [_mm(xs[u], kbs[u] * ecols[u]) for u in idx]
    v_news = [us[u] - _mm(ws[u], ss[u]) for u in idx]
    os_ = [_mm(qs[u] * ecols[u], ss[u]) + _mm(intras[u], v_news[u]) for u in idx]
    s_news = [ss[u] * jnp.exp(glasts[u]) + _mm_tn(ks[u] * jnp.exp(glasts[u] - gcols[u]), v_news[u])
              for u in idx]
    for u, (b, h) in enumerate(units):
        o_ref[b, :, sls[h]] = os_[u]
        s_scr[b, h] = s_news[u]

    @pl.when(c == pl.num_programs(1) - 1)
    def _():
        sfin_ref[...] = s_scr[...]


def _gdn_prompt(qkv, gb, w_sc, batch, seq):
    n = GDN_BLOCK
    nc = seq // n
    g = GDN_SEQS
    assert batch % g == 0

    def tok(width):
        return pl.BlockSpec((g, n, width), lambda b, c: (b, c, 0))

    state_shape = (g, GDN_HEADS, GDN_DK, GDN_DK)
    o, s_fin = pl.pallas_call(
        _gdn_prompt_kernel,
        grid=(batch // g, nc),
        in_specs=[tok(QKV_CH), tok(LANES), _full(w_sc.shape)],
        out_specs=[tok(GDN_V), pl.BlockSpec(state_shape, lambda b, c: (b, 0, 0, 0))],
        out_shape=[jax.ShapeDtypeStruct((batch, seq, GDN_V), F32),
                   jax.ShapeDtypeStruct((batch, GDN_HEADS, GDN_DK, GDN_DK), F32)],
        scratch_shapes=[pltpu.VMEM(state_shape, F32), pltpu.VMEM((g, SC_HALO + n, QKV_CH), F32)],
        compiler_params=_cparams(("arbitrary", "arbitrary")),
        name="gdn_prompt",
    )(qkv.reshape(batch, seq, QKV_CH), gb.reshape(batch, seq, LANES), w_sc)
    return o.reshape(batch * seq, GDN_V), s_fin


def _post_prompt_kernel(x_ref, conv_ref, o_ref, z_ref, gon_ref, wout_ref, gx_ref, wq_ref, mk_ref, mv_ref,
                        wo_ref, gmoe_ref, wr_ref, br_ref, tri_ref, h3s_ref, x2_ref, h3r_ref, route_ref, rt_ref,
                        cnt_ref, carry, logit_buf, *, n_steps):
    step = pl.program_id(0)

    @pl.when(step == 0)
    def _():
        carry[...] = jnp.zeros(carry.shape, F32)
        logit_buf[...] = jnp.zeros(logit_buf.shape, F32)

    x1 = _mix_out(conv_ref[...], o_ref[...], z_ref[...], gon_ref[...], wout_ref, x_ref[...])
    qx = jnp.dot(_rms(x1, gx_ref[...]).astype(BF16), wq_ref[...], preferred_element_type=F32)

    route, new_carry = _route_finish(logit_buf[...], carry[0:1, :], tri_ref[...])
    route_ref[...] = route
    rt_ref[...] = route.T[0:ROUTE_ROWS, :]
    kept = jnp.where(step >= 1, new_carry, carry[0:1, :])
    carry[0:1, :] = kept
    cnt_ref[...] = jnp.broadcast_to(kept, cnt_ref.shape)

    sls = [slice(h * X_HEAD_DIM, (h + 1) * X_HEAD_DIM) for h in range(X_HEADS)]
    qb = qx.astype(BF16)
    ss = [lax.dot_general(qb[:, sl], mk_ref[:, sl], (((1,), (1,)), ((), ())),
                          preferred_element_type=F32) * (X_HEAD_DIM ** -0.5) for sl in sls]
    es = [jnp.exp(s - jnp.max(s, axis=-1, keepdims=True)) for s in ss]
    ps = [(e / jnp.sum(e, axis=-1, keepdims=True)).astype(BF16) for e in es]
    att = jnp.concatenate(
        [jnp.dot(p, mv_ref[:, sl], preferred_element_type=F32) for p, sl in zip(ps, sls)], axis=1)
    x2 = x1 + jnp.dot(att.astype(BF16), wo_ref[...], preferred_element_type=F32)
    h3, logits = _router_logits(x2, gmoe_ref[...], wr_ref, br_ref[...])
    logit_buf[...] = logits

    @pl.when(step < n_steps)
    def _():
        x2_ref[...] = x2
        h3r_ref[...] = _pack_rows(h3)

    @pl.when(step == n_steps)
    def _():
        h3r_ref[0:h3s_ref.shape[0], :] = h3s_ref[...]


def _post_prompt(x2d, conv, o, z, mk_b, mv_b, h3_sample, batch, seq, wts):
    tm = TOKEN_TILE
    nj = seq // tm
    rows = batch * seq
    n_steps = batch * nj
    n_s = h3_sample.shape[0]
    assert n_s <= tm

    def tok(width):
        return pl.BlockSpec((tm, width), lambda s: (jnp.minimum(s, n_steps - 1), 0))

    mem_spec = pl.BlockSpec((N_MEM, D_MODEL), lambda s: (jnp.minimum(s, n_steps - 1) // nj, 0))
    sq = _full((D_MODEL, D_MODEL))
    in_specs = [tok(D_MODEL), tok(CONV_CH), tok(GDN_V), tok(GDN_V), _full((1, GDN_DK)), sq,
                _full((1, D_MODEL)), sq, mem_spec, mem_spec, sq, _full((1, D_MODEL)),
                _full((D_MODEL, LANES)), _full((1, LANES)), _full((tm, tm)), _full(h3_sample.shape)]
    out_specs = [tok(D_MODEL),
                 pl.BlockSpec((tm, PACKED), lambda s: (s, 0)),
                 pl.BlockSpec((tm, LANES), lambda s: (jnp.maximum(s - 1, 0), 0)),
                 pl.BlockSpec((ROUTE_ROWS, tm), lambda s: (0, jnp.maximum(s - 1, 0))),
                 _full((SUBLANES, LANES))]
    out_shape = [jax.ShapeDtypeStruct((rows, D_MODEL), F32),
                 jax.ShapeDtypeStruct((rows + n_s, PACKED), jnp.uint32),
                 jax.ShapeDtypeStruct((rows, LANES), F32),
                 jax.ShapeDtypeStruct((ROUTE_ROWS, rows), F32),
                 jax.ShapeDtypeStruct((SUBLANES, LANES), F32)]
    return pl.pallas_call(
        functools.partial(_post_prompt_kernel, n_steps=n_steps),
        grid=(n_steps + 1,),
        in_specs=in_specs,
        out_specs=out_specs,
        out_shape=out_shape,
        scratch_shapes=[pltpu.VMEM((SUBLANES, LANES), F32), pltpu.VMEM((tm, LANES), F32)],
        compiler_params=_cparams(("arbitrary",)),
        name="post_prompt",
    )(x2d, conv, o, z, wts["g_onorm"], wts["w_out_b"], wts["g_xattn"], wts["w_xq_b"], mk_b, mv_b,
      wts["w_xo_b"], wts["g_moe"], wts["w_router"], wts["b_router"], _strict_lower(tm), h3_sample)


def _pre_sample_kernel(x_ref, gmix_ref, w_ref, wab_ref, bglu_ref, wdw_ref, bdw_ref, lng_ref, lnb_ref,
                       wsc_ref, cst_ref, chist_ref, shist_ref, conv_ref, q_ref, k_ref, v_ref, z_ref,
                       gb_ref, cnew_ref, snew_ref):
    glu, qkv_pre, z, uab = _project(x_ref[...], gmix_ref[...], w_ref, wab_ref, bglu_ref[...])
    z_ref[...] = z
    gb_ref[...] = _gate_beta(uab, cst_ref[...])
    kw = CONV_WIDTH
    acc = wdw_ref[kw - 1:kw, :] * glu
    for t in range(kw - 1):
        row = chist_ref[:, t, :]
        acc = acc + wdw_ref[t:t + 1, :] * row
        if t >= 1:
            cnew_ref[:, t - 1, :] = row
    cnew_ref[:, kw - 2, :] = glu
    conv_ref[...] = _conv_post(acc, bdw_ref[...], lng_ref[...], lnb_ref[...]).astype(BF16)
    ks = SHORT_CONV
    cs = wsc_ref[ks - 1:ks, :] * qkv_pre
    for t in range(ks - 1):
        row = shist_ref[:, t, :]
        cs = cs + wsc_ref[t:t + 1, :] * row
        if t >= 1:
            snew_ref[:, t - 1, :] = row
    snew_ref[:, ks - 2, :] = qkv_pre
    q, k, v = _qkv_post(cs)
    q_ref[...] = q
    k_ref[...] = k
    v_ref[...] = v


PRE_SAMPLE_TOKENS = 32


def _pre_sample(xs, chist, shist, wts):
    n = xs.shape[0]
    tb = min(PRE_SAMPLE_TOKENS, n)

    def tok(width):
        return pl.BlockSpec((tb, width), lambda i: (i, 0))

    def hist(a):
        return pl.BlockSpec((tb,) + a.shape[1:], lambda i: (i, 0, 0))

    consts = (wts["g_mix"], wts["w_in_b"], wts["w_ab_b"], wts["b_glu"], wts["w_dw"], wts["b_dw"],
              wts["ln_g"], wts["ln_b"], wts["w_sc"], wts["gdn_cst"])
    return pl.pallas_call(
        _pre_sample_kernel,
        grid=(n // tb,),
        in_specs=[tok(D_MODEL)] + [_full(a.shape) for a in consts] + [hist(chist), hist(shist)],
        out_specs=[tok(CONV_CH), tok(GDN_V), tok(GDN_V), tok(GDN_V), tok(GDN_V), tok(LANES),
                   hist(chist), hist(shist)],
        out_shape=[jax.ShapeDtypeStruct((n, CONV_CH), BF16)]
        + [jax.ShapeDtypeStruct((n, GDN_V), F32)] * 4
        + [jax.ShapeDtypeStruct((n, LANES), F32), jax.ShapeDtypeStruct(chist.shape, F32),
           jax.ShapeDtypeStruct(shist.shape, F32)],
        compiler_params=_cparams(("arbitrary",)),
        name="pre_sample",
    )(xs, *consts, chist, shist)


GDN_STEP_TOKENS = 8


def _gdn_sample_kernel(q_ref, k_ref, v_ref, gb_ref, s_ref, o_ref, snew_ref):
    n = GDN_DK
    for i in range(GDN_STEP_TOKENS):
        for h in range(GDN_HEADS):
            sl = slice(h * GDN_DK, (h + 1) * GDN_DK)
            qrow = q_ref[i:i + 1, sl]
            krow = k_ref[i:i + 1, sl]
            vrow = v_ref[i:i + 1, sl]
            g = gb_ref[i:i + 1, h:h + 1]
            beta = gb_ref[i:i + 1, GDN_HEADS + h:GDN_HEADS + h + 1]
            kcol = jnp.broadcast_to(krow, (n, n)).T
            qcol = jnp.broadcast_to(qrow, (n, n)).T
            s1 = s_ref[i, h] * jnp.exp(g)
            sk = jnp.sum(s1 * kcol, axis=0, keepdims=True)
            vt = (vrow - sk) * beta
            s2 = s1 + kcol * vt
            snew_ref[i, h] = s2
            o_ref[i:i + 1, sl] = jnp.sum(s2 * qcol, axis=0, keepdims=True)


def _gdn_sample(q, k, v, gb, state):
    n = q.shape[0]
    tb = GDN_STEP_TOKENS

    def tok(width):
        return pl.BlockSpec((tb, width), lambda i: (i, 0))

    st = pl.BlockSpec((tb, GDN_HEADS, GDN_DK, GDN_DK), lambda i: (i, 0, 0, 0))
    return pl.pallas_call(
        _gdn_sample_kernel,
        grid=(n // tb,),
        in_specs=[tok(GDN_V), tok(GDN_V), tok(GDN_V), tok(LANES), st],
        out_specs=[tok(GDN_V), st],
        out_shape=[jax.ShapeDtypeStruct((n, GDN_V), F32), jax.ShapeDtypeStruct(state.shape, F32)],
        compiler_params=_cparams(("arbitrary",)),
        name="gdn_sample",
    )(q, k, v, gb, state)


def _mix_sample_kernel(x_ref, conv_ref, o_ref, z_ref, gon_ref, wout_ref, gx_ref, wq_ref, x1_ref, qx_ref):
    x1 = _mix_out(conv_ref[...], o_ref[...], z_ref[...], gon_ref[...], wout_ref, x_ref[...])
    x1_ref[...] = x1
    qx_ref[...] = jnp.dot(_rms(x1, gx_ref[...]).astype(BF16), wq_ref[...], preferred_element_type=F32)


def _mix_sample(xs, conv, o, z, wts):
    n = xs.shape[0]
    in_arrays = (xs, conv, o, z, wts["g_onorm"], wts["w_out_b"], wts["g_xattn"], wts["w_xq_b"])
    return pl.pallas_call(
        _mix_sample_kernel,
        grid=(1,),
        in_specs=[_full(a.shape) for a in in_arrays],
        out_specs=[_full((n, D_MODEL))] * 2,
        out_shape=[jax.ShapeDtypeStruct((n, D_MODEL), F32)] * 2,
        compiler_params=_cparams(("arbitrary",)),
        name="mix_sample",
    )(*in_arrays)


ATTN_STEP_TOKENS = 4


def _attn_sample_kernel(qx_ref, ck0_ref, ck1_ref, cv0_ref, cv1_ref, att_ref):
    half = ATTN_STEP_TOKENS // 2
    for i in range(ATTN_STEP_TOKENS):
        ck_ref, cv_ref = (ck0_ref, cv0_ref) if i < half else (ck1_ref, cv1_ref)
        r = i % half
        parts = []
        for h in range(X_HEADS):
            sl = slice(h * X_HEAD_DIM, (h + 1) * X_HEAD_DIM)
            prod = ck_ref[r, :, h, :] * qx_ref[0, i:i + 1, sl]
            s = jnp.sum(prod, axis=-1, keepdims=True) * (X_HEAD_DIM ** -0.5)
            e = jnp.exp(s - jnp.max(s, axis=0, keepdims=True))
            p = e / jnp.sum(e, axis=0, keepdims=True)
            parts.append(jnp.sum(p * cv_ref[r, :, h, :], axis=0, keepdims=True))
        att_ref[0, i:i + 1, :] = jnp.concatenate(parts, axis=1)


def _attn_sample(qx, ck, cv):
    n = qx.shape[0]
    tb = ATTN_STEP_TOKENS
    half = tb // 2
    q3 = qx.reshape(n // tb, tb, D_MODEL)
    qspec = pl.BlockSpec((1, tb, D_MODEL), lambda i: (i, 0, 0))
    c0 = pl.BlockSpec((half, N_MEM, X_HEADS, X_HEAD_DIM), lambda i: (2 * i, 0, 0, 0))
    c1 = pl.BlockSpec((half, N_MEM, X_HEADS, X_HEAD_DIM), lambda i: (2 * i + 1, 0, 0, 0))
    out = pl.pallas_call(
        _attn_sample_kernel,
        grid=(n // tb,),
        in_specs=[qspec, c0, c1, c0, c1],
        out_specs=qspec,
        out_shape=jax.ShapeDtypeStruct(q3.shape, F32),
        compiler_params=_cparams(("arbitrary",)),
        name="attn_sample",
    )(q3, ck, ck, cv, cv)
    return out.reshape(n, D_MODEL)


def _route_sample_kernel(x1_ref, att_ref, wo_ref, gmoe_ref, wr_ref, br_ref, tri_ref, x2_ref, h3r_ref, route_ref,
                         rt_ref, cnt_ref):
    x2 = x1_ref[...] + jnp.dot(att_ref[...].astype(BF16), wo_ref[...], preferred_element_type=F32)
    x2_ref[...] = x2
    h3, logits = _router_logits(x2, gmoe_ref[...], wr_ref, br_ref[...])
    route, counts = _route_finish(logits, jnp.zeros((1, LANES), F32), tri_ref[...])
    h3r_ref[...] = _pack_rows(h3)
    route_ref[...] = route
    rt_ref[...] = route.T[0:ROUTE_ROWS, :]
    cnt_ref[...] = jnp.broadcast_to(counts, cnt_ref.shape)


def _route_sample(x1, att, wts):
    n = x1.shape[0]
    in_arrays = (x1, att, wts["w_xo_b"], wts["g_moe"], wts["w_router"], wts["b_router"],
                 _strict_lower(n))
    shapes = [(n, D_MODEL), (n, PACKED), (n, LANES), (ROUTE_ROWS, n), (SUBLANES, LANES)]
    dtypes = [F32, jnp.uint32, F32, F32, F32]
    return pl.pallas_call(
        _route_sample_kernel,
        grid=(1,),
        in_specs=[_full(a.shape) for a in in_arrays],
        out_specs=[_full(s) for s in shapes],
        out_shape=[jax.ShapeDtypeStruct(s, d) for s, d in zip(shapes, dtypes)],
        compiler_params=_cparams(("arbitrary",)),
        name="route_sample",
    )(*in_arrays)


def _part_tiles(tiles):
    n_workers = SC_CORES * SC_SUBCORES
    while True:
        rows = tiles * MOE_TILE
        if rows % (n_workers * SUBLANES) == 0 and any(
                (rows // n_workers) % c == 0 for c in range(64, 24, -SUBLANES)):
            return tiles
        tiles += 1


def _sc_chunk(rows_per_worker):
    for c in range(64, 0, -SUBLANES):
        if rows_per_worker % c == 0:
            return c
    raise ValueError(rows_per_worker)


def _sc_gather_rows(table, idx):
    n_workers = SC_CORES * SC_SUBCORES
    b = idx.shape[0]
    assert b % (n_workers * SUBLANES) == 0
    per_worker = b // n_workers
    chunk = _sc_chunk(per_worker)
    row_shape = table.shape[1:]
    mesh = plsc.VectorSubcoreMesh(core_axis_name="c", subcore_axis_name="s")

    @functools.partial(
        pl.kernel, mesh=mesh,
        out_type=jax.ShapeDtypeStruct((b,) + row_shape, table.dtype),
        scratch_types=[pltpu.VMEM((chunk,), I32), pltpu.VMEM((chunk,) + row_shape, table.dtype),
                       pltpu.SemaphoreType.DMA],
        name="sc_gather_rows",
    )
    def gather(table_hbm, idx_hbm, out_hbm, idx_v, rows_v, sem):
        worker = lax.axis_index("s") * SC_CORES + lax.axis_index("c")
        base = worker * per_worker

        @pl.loop(0, per_worker // chunk)
        def _(c):
            off = pl.multiple_of(base + c * chunk, SUBLANES)
            pltpu.sync_copy(idx_hbm.at[pl.ds(off, chunk)], idx_v)
            pltpu.async_copy(table_hbm.at[idx_v], rows_v, sem).wait()
            pltpu.sync_copy(rows_v, out_hbm.at[pl.ds(off, chunk)])

    return gather(table, idx)


def _moe_kernel(te_ref, tn_ref, nt_ref, xs_ref, wgu_hbm, bgu_ref, wdn_hbm, bdn_ref, *rest, first_tile):
    ys_ref, wgu_f, wdn_f, wgu_b, wdn_b, sems = rest[-6:]
    step = pl.program_id(0)
    i = first_tile + step
    total = nt_ref[0]

    def weight_copies(e):
        return (pltpu.make_async_copy(wgu_hbm.at[e], wgu_f, sems.at[0]),
                pltpu.make_async_copy(wdn_hbm.at[e], wdn_f, sems.at[1]))

    def start(e):
        for cp in weight_copies(e):
            cp.start()

    @pl.when(i < total)
    def _():
        expert = te_ref[i]
        prev = te_ref[jnp.maximum(i - 1, 0)]
        fresh = jnp.logical_or(step == 0, expert != prev)

        @pl.when(step == 0)
        def _():
            start(expert)

        @pl.when(fresh)
        def _():
            for cp in weight_copies(expert):
                cp.wait()
            wgu_b[...] = wgu_f[...].astype(BF16)
            wdn_b[...] = wdn_f[...].astype(BF16)
            nxt = tn_ref[i]

            @pl.when(nxt >= 0)
            def _():
                start(nxt)

        x = _unpack_rows(xs_ref[...]).astype(BF16)

        def up(c):
            glu_cols = slice(c * MOE_COLS, (c + 1) * MOE_COLS)
            lin_cols = slice(D_EXPERT + c * MOE_COLS, D_EXPERT + (c + 1) * MOE_COLS)
            return (jnp.dot(x, wgu_b[:, glu_cols], preferred_element_type=F32) + bgu_ref[expert, :, glu_cols],
                    jnp.dot(x, wgu_b[:, lin_cols], preferred_element_type=F32) + bgu_ref[expert, :, lin_cols])

        n_chunks = D_EXPERT // MOE_COLS
        nxt = up(0)
        y = None
        for c in range(n_chunks):
            g, lin = nxt
            if c + 1 < n_chunks:
                nxt = up(c + 1)
            x_glu = jnp.minimum(g, SWIGLU_LIMIT)
            x_lin = jnp.clip(lin, -SWIGLU_LIMIT, SWIGLU_LIMIT)
            act = x_glu * jax.nn.sigmoid(SWIGLU_ALPHA * x_glu) * (x_lin + 1.0)
            part = jnp.dot(act.astype(BF16), wdn_b[c * MOE_COLS:(c + 1) * MOE_COLS, :],
                           preferred_element_type=F32)
            y = part if y is None else y + part
        ys_ref[...] = _pack_rows(y + bdn_ref[expert])

    @pl.when(i >= total)
    def _():
        ys_ref[...] = jnp.zeros(ys_ref.shape, jnp.uint32)


def _moe(tile_e, tile_next, n_tiles, xs_part, ys_prev, first_tile, n_rows, w_gu, b_gu, w_dn, b_dn):
    tm = MOE_TILE

    def resident(a):
        return pl.BlockSpec(a.shape, lambda i, te, tn, nt: (0, 0, 0))

    hbm = pl.BlockSpec(memory_space=pl.ANY)
    in_specs = [pl.BlockSpec((tm, PACKED), lambda i, te, tn, nt: (i, 0)),
                hbm, resident(b_gu), hbm, resident(b_dn)]
    operands = [tile_e, tile_next, n_tiles, xs_part, w_gu, b_gu, w_dn, b_dn]
    aliases = {}
    if ys_prev is not None:
        in_specs.append(hbm)
        aliases = {len(operands): 0}
        operands.append(ys_prev)
    grid_spec = pltpu.PrefetchScalarGridSpec(
        num_scalar_prefetch=3,
        grid=(xs_part.shape[0] // tm,),
        in_specs=in_specs,
        out_specs=pl.BlockSpec((tm, PACKED), lambda i, te, tn, nt: (first_tile + i, 0)),
        scratch_shapes=[pltpu.VMEM((D_MODEL, 2 * D_EXPERT), F32),
                        pltpu.VMEM((D_EXPERT, D_MODEL), F32),
                        pltpu.VMEM((D_MODEL, 2 * D_EXPERT), BF16),
                        pltpu.VMEM((D_EXPERT, D_MODEL), BF16),
                        pltpu.SemaphoreType.DMA((2,))],
    )
    return pl.pallas_call(
        functools.partial(_moe_kernel, first_tile=first_tile),
        grid_spec=grid_spec,
        out_shape=jax.ShapeDtypeStruct((n_rows, PACKED), jnp.uint32),
        input_output_aliases=aliases,
        compiler_params=_cparams(("arbitrary",), vmem=56 * 1024 * 1024),
        name="moe",
    )(*operands)


def _combine_kernel(x2_ref, route_ref, gfin_ref, yt_ref, *rest):
    y_ref = rest[-1]
    route = route_ref[...]
    acc = x2_ref[...]
    for j in range(TOP_K):
        acc = acc + route[:, TOP_K + j:TOP_K + j + 1] * _unpack_rows(yt_ref[j])
    y_ref[...] = _rms(acc, gfin_ref[...])


def _combine(x2, route, g_final, ys_tok, y_prev, tok0, n_tok, ys_block0):
    tc = min(COMBINE_TILE, n_tok)
    b0 = tok0 // tc

    def tok(width):
        return pl.BlockSpec((tc, width), lambda i: (b0 + i, 0))

    in_specs = [tok(D_MODEL), tok(LANES), pl.BlockSpec((1, D_MODEL), lambda i: (0, 0)),
                pl.BlockSpec((TOP_K, tc, PACKED), lambda i: (0, ys_block0 + i, 0))]
    operands = [x2, route, g_final, ys_tok]
    aliases = {}
    if y_prev is not None:
        in_specs.append(pl.BlockSpec(memory_space=pl.ANY))
        aliases = {len(operands): 0}
        operands.append(y_prev)
    return pl.pallas_call(
        _combine_kernel,
        grid=(n_tok // tc,),
        in_specs=in_specs,
        out_specs=tok(D_MODEL),
        out_shape=jax.ShapeDtypeStruct(x2.shape, F32),
        input_output_aliases=aliases,
        compiler_params=_cparams(("arbitrary",)),
        name="combine",
    )(*operands)


def _routing_tables(idx_t, rank_t, counts, n_rows, part_tiles):
    tm = MOE_TILE
    n_tok = idx_t.shape[1]
    n_assign = TOP_K * n_tok
    tok_mask = (1 << TOKEN_BITS) - 1
    tiles_e = (counts + tm - 1) // tm
    tile_end = jnp.cumsum(tiles_e)
    row_start = (tile_end - tiles_e) * tm
    total = tile_end[-1]
    expert_ids = jnp.arange(N_EXPERTS, dtype=I32)

    def lookup(table, e):
        return jnp.sum(jnp.where(e[..., None] == expert_ids, table, 0), axis=-1)

    pos = lookup(row_start, idx_t) + rank_t
    keys_real = (idx_t * (1 << TOKEN_BITS) + jnp.arange(n_tok, dtype=I32)[None, :]).reshape(-1)
    k = jnp.arange(n_rows - n_assign, dtype=I32)
    pad_e, pad_s = k // tm, k % tm
    pad_needed = lookup(tiles_e * tm - counts, pad_e)
    pad_key_e = jnp.where((pad_e < N_EXPERTS) & (pad_s < pad_needed), pad_e, N_EXPERTS)
    keys = lax.sort(jnp.concatenate([keys_real, pad_key_e * (1 << TOKEN_BITS) + tok_mask]),
                    is_stable=False)
    src_tok = jnp.where((keys & tok_mask) == tok_mask, jnp.arange(n_rows, dtype=I32) % n_tok,
                        keys & tok_mask)
    tid = jnp.minimum(jnp.arange(n_rows // tm, dtype=I32), total - 1)
    tile_e = jnp.minimum(jnp.sum((tid[:, None] >= tile_end[None, :]).astype(I32), axis=1), N_EXPERTS - 1)
    later = (expert_ids[None, :] > expert_ids[:, None]) & (tiles_e[None, :] > 0)
    next_e = jnp.min(jnp.where(later, expert_ids[None, :], N_EXPERTS), axis=1)
    nxt = lookup(next_e, tile_e)
    nxt_first_tile = lookup(tile_end - tiles_e, jnp.minimum(nxt, N_EXPERTS - 1))
    bounds = jnp.cumsum(jnp.asarray(part_tiles, I32))
    part_end = jnp.min(jnp.where(bounds[None, :] > tid[:, None], bounds[None, :], n_rows), axis=1)
    tile_next = jnp.where((nxt < N_EXPERTS) & (nxt_first_tile < part_end), nxt, -1)
    return (tile_e.astype(I32), tile_next.astype(I32), total.reshape(1).astype(I32), src_tok.astype(I32),
            pos.reshape(-1).astype(I32))


def _pad_rows(a, rows):
    return jnp.concatenate([a, jnp.zeros((rows - a.shape[0],) + a.shape[1:], a.dtype)], axis=0)


def _pad_lanes(a, lanes=LANES):
    return jnp.concatenate([a, jnp.zeros(a.shape[:-1] + (lanes - a.shape[-1],), a.dtype)], axis=-1)


def kernel(x_prompt, mem_prompt, x_sample, state_conformer_conv, state_gdn_conv, state_gdn, cache_mem_k,
           cache_mem_v, w_in, b_glu, w_dw, b_dw, ln_g, ln_b, w_sc, a_log, dt_bias, g_onorm, w_out, g_mix,
           g_xattn, g_mem, w_xq, w_mk, w_mv, w_xo, g_moe, w_router, b_router, w_gu, b_gu, w_dn, b_dn,
           g_final):
    assert w_in.shape[0] == 1, "single-layer configuration"
    batch, seq, _ = x_prompt.shape
    n_s = x_sample.shape[0]
    n_p = batch * seq
    n_all = n_p + n_s
    assert seq % TOKEN_TILE == 0 and n_p % n_s == 0 and n_all < (1 << TOKEN_BITS) - 1
    assert (n_all * TOP_K) % (SC_CORES * SC_SUBCORES * SUBLANES) == 0

    wts = {
        "g_mix": g_mix[0][None], "g_xattn": g_xattn[0][None], "g_moe": g_moe[0][None],
        "g_onorm": g_onorm[0][None],
        "w_in_b": w_in[0][:, :OFF_A].astype(BF16),
        "w_ab_b": _pad_lanes(w_in[0][:, OFF_A:]).astype(BF16),
        "b_glu": b_glu[0][None],
        "w_dw": _pad_rows(w_dw[0], 32), "b_dw": b_dw[0][None], "ln_g": ln_g[0][None], "ln_b": ln_b[0][None],
        "w_sc": _pad_rows(w_sc[0], 8),
        "gdn_cst": _pad_rows(_pad_lanes(jnp.stack([a_log[0], dt_bias[0]])), 8),
        "w_out_b": w_out[0].astype(BF16), "w_xq_b": w_xq[0].astype(BF16), "w_xo_b": w_xo[0].astype(BF16),
        "w_router": _pad_lanes(w_router[0]), "b_router": _pad_lanes(b_router[0][None]),
    }

    mk, mv, mk_b, mv_b = _mem_kv(mem_prompt.reshape(batch * N_MEM, D_MODEL), g_mem[0][None],
                                 w_mk[0].astype(BF16), w_mv[0].astype(BF16))
    xp = x_prompt.reshape(n_p, D_MODEL)
    conv_p, qkv_p, z_p, gb_p, cstate_p, sstate_p = _pre_prompt(xp, batch, seq, wts)
    o_p, gstate_p = _gdn_prompt(qkv_p, gb_p, wts["w_sc"], batch, seq)

    xs = x_sample.reshape(n_s, D_MODEL)
    conv_s, q_s, k_s, v_s, z_s, gb_s, cstate_s, sstate_s = _pre_sample(
        xs, state_conformer_conv[0], state_gdn_conv[0], wts)
    o_s, gstate_s = _gdn_sample(q_s, k_s, v_s, gb_s, state_gdn[0])
    x1_s, qx_s = _mix_sample(xs, conv_s, o_s, z_s, wts)
    att_s = _attn_sample(qx_s, cache_mem_k[0], cache_mem_v[0])
    x2_s, h3_s, route_s, rt_s, counts_s = _route_sample(x1_s, att_s, wts)

    x2_p, h3r, route_p, rt_p, counts_p = _post_prompt(xp, conv_p, o_p, z_p, mk_b, mv_b, h3_s, batch, seq,
                                                      wts)

    counts_p = counts_p[0, :N_EXPERTS].astype(I32)
    counts_s = counts_s[0, :N_EXPERTS].astype(I32)
    idx_s = rt_s[0:TOP_K].astype(I32)
    rank_s = rt_s[2 * TOP_K:3 * TOP_K].astype(I32) + jnp.sum(
        jnp.where(idx_s[..., None] == jnp.arange(N_EXPERTS, dtype=I32), counts_p, 0), axis=-1)
    idx_t = jnp.concatenate([rt_p[0:TOP_K].astype(I32), idx_s], axis=1)
    rank_t = jnp.concatenate([rt_p[2 * TOP_K:3 * TOP_K].astype(I32), rank_s], axis=1)
    min_tiles = -(-(n_all * TOP_K + N_EXPERTS * (MOE_TILE - 1)) // MOE_TILE)
    part_tiles = [_part_tiles(-(-min_tiles * w // sum(MOE_PART_WEIGHTS))) for w in MOE_PART_WEIGHTS]
    first_tiles = [sum(part_tiles[:k]) for k in range(len(part_tiles))]
    n_rows = sum(part_tiles) * MOE_TILE
    tile_e, tile_next, n_tiles, src_tok, pos = _routing_tables(idx_t, rank_t, counts_p + counts_s, n_rows,
                                                               part_tiles)
    pos_t = pos.reshape(TOP_K, n_all)
    xs_parts = [_sc_gather_rows(h3r, src_tok[f * MOE_TILE:(f + t) * MOE_TILE])
                for f, t in zip(first_tiles, part_tiles)]
    ys = None
    for k, f in enumerate(first_tiles):
        ys = _moe(tile_e, tile_next, n_tiles, xs_parts[k], ys, f, n_rows, w_gu[0],
                  b_gu[0][:, None, :], w_dn[0], b_dn[0][:, None, :])
    gfin = g_final[None]
    assert n_p % (sum(TOKEN_PART_WEIGHTS) * COMBINE_TILE) == 0
    tok_parts = [n_p * w // sum(TOKEN_PART_WEIGHTS) for w in TOKEN_PART_WEIGHTS]
    cut = -(-n_s // COMBINE_TILE) * COMBINE_TILE
    sample_part = 0 if (tok_parts[0] > cut and tok_parts[0] % min(COMBINE_TILE, n_s) == 0) \
        else len(tok_parts) - 1
    if sample_part == 0:
        tok_parts[0] -= cut
        tok_parts[-1] += cut
    y_p = None
    for k, n_tok in enumerate(tok_parts):
        tok0 = sum(tok_parts[:k])
        pos_k = pos_t[:, tok0:tok0 + n_tok]
        if k == sample_part:
            pos_k = jnp.concatenate([pos_k, pos_t[:, n_p:]], axis=1)
        ys_tok = _sc_gather_rows(ys, pos_k.reshape(-1)).reshape(TOP_K, pos_k.shape[1], PACKED)
        y_p = _combine(x2_p, route_p, gfin, ys_tok, y_p, tok0, n_tok, 0)
        if k == sample_part:
            y_s = _combine(x2_s, route_s, gfin, ys_tok, None, 0, n_s, n_tok // min(COMBINE_TILE, n_s))

    return (y_p.reshape(batch, seq, D_MODEL), y_s.reshape(n_s, 1, D_MODEL),
            cstate_p[None], sstate_p[None], gstate_p[None],
            mk[None], mv[None],
            cstate_s[None], sstate_s[None], gstate_s[None])
```

```python
import functools

import jax
import jax.numpy as jnp
from jax import lax
from jax.experimental import pallas as pl
from jax.experimental.pallas import tpu as pltpu
from jax.experimental.pallas import tpu_sc as plsc

F32, BF16, I32 = jnp.float32, jnp.bfloat16, jnp.int32

D_MODEL = 1024
CONV_CH = 512
CONV_WIDTH = 31
GDN_HEADS = 4
GDN_DK = 128
GDN_V = 512
QKV_CH = 1536
SHORT_CONV = 4
N_MEM = 256
X_HEADS = 4
X_HEAD_DIM = 256
N_EXPERTS = 32
TOP_K = 4
D_EXPERT = 1024
SWIGLU_LIMIT = 7.0
SWIGLU_ALPHA = 1.702
NORM_EPS = 1e-6
OFF_QKV = 2 * CONV_CH
OFF_Z = OFF_QKV + QKV_CH
OFF_A = OFF_Z + GDN_V

LANES = 128
SUBLANES = 8
GDN_BLOCK = 128
TOKEN_TILE = 512
MOE_TILE = 384
MOE_COLS = 256
MOE_PART_WEIGHTS = (3, 6, 8, 9)
TOKEN_PART_WEIGHTS = (1, 2, 2, 3)
COMBINE_TILE = 256
TOKEN_BITS = 15
ROUTE_ROWS = 16
SC_CORES = 2
SC_SUBCORES = 16
VMEM_LIMIT = 48 * 1024 * 1024


def _cparams(sem, vmem=VMEM_LIMIT):
    return pltpu.CompilerParams(dimension_semantics=sem, vmem_limit_bytes=vmem)


def _mm(a, b):
    return jnp.dot(a.astype(BF16), b.astype(BF16), preferred_element_type=F32)


def _mm_nt(a, b):
    return lax.dot_general(a.astype(BF16), b.astype(BF16), (((1,), (1,)), ((), ())),
                           preferred_element_type=F32)


def _mm_tn(a, b):
    return lax.dot_general(a.astype(BF16), b.astype(BF16), (((0,), (0,)), ((), ())),
                           preferred_element_type=F32)


def _rms(x, g):
    return x * lax.rsqrt(jnp.mean(x * x, axis=-1, keepdims=True) + NORM_EPS) * g


def _silu(x):
    return x * jax.nn.sigmoid(x)


PACKED = D_MODEL // 2


def _pack_rows(h):
    lo = pltpu.bitcast(h[:, :PACKED].astype(BF16).astype(F32), jnp.uint32)
    hi = pltpu.bitcast(h[:, PACKED:].astype(BF16).astype(F32), jnp.uint32)
    return (lo >> 16) | (hi & jnp.uint32(0xFFFF0000))


def _unpack_rows(w):
    lo = pltpu.bitcast(w << 16, F32)
    hi = pltpu.bitcast(w & jnp.uint32(0xFFFF0000), F32)
    return jnp.concatenate([lo, hi], axis=1)


def _full(shape):
    return pl.BlockSpec(shape, lambda *_: (0,) * len(shape))


def _strict_lower(n):
    return jnp.tril(jnp.ones((n, n), BF16), k=-1)


def _project(x, gmix, w_ref, wab_ref, bglu):
    h = _rms(x, gmix).astype(BF16)
    u_glu = jnp.dot(h, w_ref[:, 0:OFF_QKV], preferred_element_type=F32) + bglu
    glu = u_glu[:, :CONV_CH] * jax.nn.sigmoid(u_glu[:, CONV_CH:])
    qkv_pre = jnp.dot(h, w_ref[:, OFF_QKV:OFF_Z], preferred_element_type=F32)
    z = jnp.dot(h, w_ref[:, OFF_Z:OFF_A], preferred_element_type=F32)
    uab = jnp.dot(h, wab_ref[...], preferred_element_type=F32)
    return glu, qkv_pre, z, uab


def _gate_beta(uab, cst):
    lane = lax.broadcasted_iota(I32, uab.shape, 1)
    g = -jnp.exp(cst[0:1, :]) * jax.nn.softplus(uab + cst[1:2, :])
    return jnp.where(lane < GDN_HEADS, g, jax.nn.sigmoid(uab))


def _conv_post(c, b_dw, ln_g, ln_b):
    c = c + b_dw
    mu = jnp.mean(c, axis=-1, keepdims=True)
    d = c - mu
    var = jnp.mean(d * d, axis=-1, keepdims=True)
    return _silu(d * lax.rsqrt(var + NORM_EPS) * ln_g + ln_b)


def _qkv_post(cs):
    a = _silu(cs)
    parts = []
    for h in range(2 * GDN_HEADS):
        seg = a[:, h * GDN_DK:(h + 1) * GDN_DK]
        n = seg * lax.rsqrt(jnp.sum(seg * seg, axis=-1, keepdims=True) + NORM_EPS)
        if h < GDN_HEADS:
            n = n * (GDN_DK ** -0.5)
        parts.append(n)
    q = jnp.concatenate(parts[:GDN_HEADS], axis=1)
    k = jnp.concatenate(parts[GDN_HEADS:], axis=1)
    return q, k, a[:, 2 * GDN_HEADS * GDN_DK:]


def _mix_out(conv_b, o, z, gon, wout_ref, x):
    parts = []
    for h in range(GDN_HEADS):
        oh = o[:, h * 128:(h + 1) * 128]
        parts.append(oh * lax.rsqrt(jnp.mean(oh * oh, axis=-1, keepdims=True) + NORM_EPS) * gon)
    on = jnp.concatenate(parts, axis=1) * _silu(z)
    mixed = (jnp.dot(conv_b, wout_ref[0:CONV_CH, :], preferred_element_type=F32)
             + jnp.dot(on.astype(BF16), wout_ref[CONV_CH:, :], preferred_element_type=F32))
    return x + mixed


def _router_logits(x2, gmoe, wr_ref, br):
    h3 = _rms(x2, gmoe)
    h_hi = h3.astype(BF16)
    r1 = h3 - h_hi.astype(F32)
    h_mid = r1.astype(BF16)
    h_lo = (r1 - h_mid.astype(F32)).astype(BF16)
    w = wr_ref[...]
    w_hi = w.astype(BF16)
    w_lo = (w - w_hi.astype(F32)).astype(BF16)
    logits = (jnp.dot(h_hi, w_hi, preferred_element_type=F32)
              + jnp.dot(h_hi, w_lo, preferred_element_type=F32)
              + jnp.dot(h_mid, w_hi, preferred_element_type=F32)
              + jnp.dot(h_lo, w_hi, preferred_element_type=F32)) + br
    return h3, logits


def _route_finish(logits, carry, before):
    m = logits.shape[0]
    neg = jnp.float32(-jnp.inf)
    n_groups = 4 if m % 32 == 0 else 1
    rows = m // n_groups
    lanes_g = lax.broadcasted_iota(I32, (rows, LANES), 1).astype(F32)
    works = [jnp.where(lanes_g < N_EXPERTS, logits[g * rows:(g + 1) * rows], neg) for g in range(n_groups)]
    vals, idxs = [], []
    for _ in range(TOP_K):
        mxs = [jnp.max(w, axis=-1, keepdims=True) for w in works]
        ixs = [jnp.min(jnp.where(w == mx, lanes_g, float(LANES)), axis=-1, keepdims=True)
               for w, mx in zip(works, mxs)]
        works = [jnp.where(lanes_g == ix, neg, w) for w, ix in zip(works, ixs)]
        vals.append(mxs)
        idxs.append(ixs)
    sels, gates = [], []
    for g in range(n_groups):
        es = [jnp.exp(vals[r][g] - vals[0][g]) for r in range(TOP_K)]
        den = es[0] + es[1] + es[2] + es[3]
        gates.append([e / den for e in es])
        sel_g = jnp.zeros((rows, LANES), F32)
        for r in range(TOP_K):
            sel_g = sel_g + jnp.where(lanes_g == idxs[r][g], 1.0, 0.0)
        sels.append(sel_g)
    sel = jnp.concatenate(sels, axis=0)
    rank_full = jnp.dot(before, sel.astype(BF16), preferred_element_type=F32) + carry
    routes = []
    for g in range(n_groups):
        rank_g = rank_full[g * rows:(g + 1) * rows]
        route_g = jnp.zeros((rows, LANES), F32)
        for r in range(TOP_K):
            rk = jnp.sum(jnp.where(lanes_g == idxs[r][g], rank_g, 0.0), axis=-1, keepdims=True)
            route_g = (route_g + jnp.where(lanes_g == r, idxs[r][g], 0.0)
                       + jnp.where(lanes_g == TOP_K + r, gates[g][r], 0.0)
                       + jnp.where(lanes_g == 2 * TOP_K + r, rk, 0.0))
        routes.append(route_g)
    new_carry = carry + jnp.sum(sel, axis=0, keepdims=True)
    return jnp.concatenate(routes, axis=0), new_carry


def _mem_kv_kernel(mem_ref, g_ref, wk_ref, wv_ref, mk_ref, mv_ref, mkb_ref, mvb_ref):
    m = _rms(mem_ref[...], g_ref[...]).astype(BF16)
    mk = jnp.dot(m, wk_ref[...], preferred_element_type=F32)
    mv = jnp.dot(m, wv_ref[...], preferred_element_type=F32)
    for h in range(X_HEADS):
        sl = slice(h * X_HEAD_DIM, (h + 1) * X_HEAD_DIM)
        mk_ref[0, :, h, :] = mk[:, sl]
        mv_ref[0, :, h, :] = mv[:, sl]
    mkb_ref[...] = mk.astype(BF16)
    mvb_ref[...] = mv.astype(BF16)


def _mem_kv(mem, g_mem, wk_b, wv_b):
    rows = mem.shape[0]
    tm = N_MEM
    row_spec = pl.BlockSpec((tm, D_MODEL), lambda i: (i, 0))
    head_spec = pl.BlockSpec((1, tm, X_HEADS, X_HEAD_DIM), lambda i: (i, 0, 0, 0))
    head_shape = jax.ShapeDtypeStruct((rows // tm, tm, X_HEADS, X_HEAD_DIM), F32)
    return pl.pallas_call(
        _mem_kv_kernel,
        grid=(rows // tm,),
        in_specs=[row_spec, _full((1, D_MODEL)), _full((D_MODEL, D_MODEL)), _full((D_MODEL, D_MODEL))],
        out_specs=[head_spec, head_spec, row_spec, row_spec],
        out_shape=[head_shape, head_shape] + [jax.ShapeDtypeStruct((rows, D_MODEL), BF16)] * 2,
        compiler_params=_cparams(("arbitrary",)),
        name="mem_kv",
    )(mem, g_mem, wk_b, wv_b)


CONV_HALO = 32
SC_HALO = 8


def _pre_prompt_kernel(x_ref, gmix_ref, w_ref, wab_ref, bglu_ref, wdw_ref, bdw_ref, lng_ref, lnb_ref,
                       cst_ref, conv_ref, qkv_ref, z_ref, gb_ref, cstate_ref, sstate_ref, cbuf, *, tm):
    j = pl.program_id(1)

    @pl.when(j == 0)
    def _():
        cbuf[0:CONV_HALO, :] = jnp.zeros((CONV_HALO, CONV_CH), F32)

    glu, qkv_pre, z, uab = _project(x_ref[...], gmix_ref[...], w_ref, wab_ref, bglu_ref[...])
    cbuf[CONV_HALO:CONV_HALO + tm, :] = glu
    qkv_ref[...] = qkv_pre
    z_ref[...] = z
    gb_ref[...] = _gate_beta(uab, cst_ref[...])

    base = CONV_HALO - (CONV_WIDTH - 1)
    rows = CONV_HALO + tm
    accs = []
    for c in range(CONV_CH // LANES):
        lanes = slice(c * LANES, (c + 1) * LANES)
        block = cbuf[:, lanes]
        acc = None
        for r in range(SUBLANES):
            shifted = block if r == 0 else pltpu.roll(block, rows - r, 0)
            for a in range(base, base + CONV_WIDTH):
                if a % SUBLANES == r:
                    t = a - base
                    term = wdw_ref[t:t + 1, lanes] * shifted[a - r:a - r + tm, :]
                    acc = term if acc is None else acc + term
        accs.append(acc)
    acc = jnp.concatenate(accs, axis=1)
    conv_ref[...] = _conv_post(acc, bdw_ref[...], lng_ref[...], lnb_ref[...]).astype(BF16)

    @pl.when(j == pl.num_programs(1) - 1)
    def _():
        cstate_ref[0] = cbuf[pl.ds(CONV_HALO + tm - (CONV_WIDTH - 1), CONV_WIDTH - 1), :]
        sstate_ref[0] = qkv_pre[tm - (SHORT_CONV - 1):, :]

    cbuf[0:CONV_HALO, :] = cbuf[tm:tm + CONV_HALO, :]


def _pre_prompt(x2d, batch, seq, wts):
    tm = TOKEN_TILE
    nj = seq // tm
    rows = batch * seq

    def tok(width):
        return pl.BlockSpec((tm, width), lambda b, j: (b * nj + j, 0))

    in_specs = [tok(D_MODEL), _full((1, D_MODEL)), _full(wts["w_in_b"].shape), _full((D_MODEL, LANES)),
                _full((1, OFF_QKV)), _full((32, CONV_CH)), _full((1, CONV_CH)), _full((1, CONV_CH)),
                _full((1, CONV_CH)), _full((8, LANES))]
    out_specs = [tok(CONV_CH), tok(QKV_CH), tok(GDN_V), tok(LANES),
                 pl.BlockSpec((1, CONV_WIDTH - 1, CONV_CH), lambda b, j: (b, 0, 0)),
                 pl.BlockSpec((1, SHORT_CONV - 1, QKV_CH), lambda b, j: (b, 0, 0))]
    out_shape = [jax.ShapeDtypeStruct((rows, CONV_CH), BF16),
                 jax.ShapeDtypeStruct((rows, QKV_CH), F32),
                 jax.ShapeDtypeStruct((rows, GDN_V), F32),
                 jax.ShapeDtypeStruct((rows, LANES), F32),
                 jax.ShapeDtypeStruct((batch, CONV_WIDTH - 1, CONV_CH), F32),
                 jax.ShapeDtypeStruct((batch, SHORT_CONV - 1, QKV_CH), F32)]
    return pl.pallas_call(
        functools.partial(_pre_prompt_kernel, tm=tm),
        grid=(batch, nj),
        in_specs=in_specs,
        out_specs=out_specs,
        out_shape=out_shape,
        scratch_shapes=[pltpu.VMEM((CONV_HALO + tm, CONV_CH), F32)],
        compiler_params=_cparams(("arbitrary", "arbitrary")),
        name="pre_prompt",
    )(x2d, wts["g_mix"], wts["w_in_b"], wts["w_ab_b"], wts["b_glu"], wts["w_dw"], wts["b_dw"],
      wts["ln_g"], wts["ln_b"], wts["gdn_cst"])


GDN_SEQS = 4


def _gdn_prompt_kernel(qkv_ref, gb_ref, wsc_ref, o_ref, sfin_ref, s_scr, sbuf):
    c = pl.program_id(1)
    n = GDN_BLOCK
    seqs = range(GDN_SEQS)

    @pl.when(c == 0)
    def _():
        s_scr[...] = jnp.zeros(s_scr.shape, F32)
        sbuf[:, 0:SC_HALO, :] = jnp.zeros((GDN_SEQS, SC_HALO, QKV_CH), F32)

    sbase = SC_HALO - (SHORT_CONV - 1)
    qkvs = []
    for b in seqs:
        sbuf[b, SC_HALO:SC_HALO + n, :] = qkv_ref[b]
        block = sbuf[b]
        cs = None
        for t in range(SHORT_CONV):
            r = (sbase + t) % SUBLANES
            shifted = block if r == 0 else pltpu.roll(block, SC_HALO + n - r, 0)
            term = wsc_ref[t:t + 1, :] * shifted[sbase + t - r:sbase + t - r + n, :]
            cs = term if cs is None else cs + term
        qkvs.append(_qkv_post(cs))
        sbuf[b, 0:SC_HALO, :] = sbuf[b, n:n + SC_HALO, :]

    row = lax.broadcasted_iota(I32, (n, n), 0)
    col = lax.broadcasted_iota(I32, (n, n), 1)
    causal = row >= col
    strict = row > col
    tri = jnp.where(causal, 1.0, 0.0).astype(BF16)
    eye = jnp.where(row == col, 1.0, 0.0)
    level_masks = []
    b = 1
    while b < n:
        same_pair = ((row ^ col) & ~(2 * b - 1)) == 0
        level_masks.append(same_pair & ((row & b) != 0) & ((col & b) == 0))
        b *= 2
    gbs, gcums, gcum_ts, egcs = [], [], [], []
    for b in seqs:
        gb = gb_ref[b]
        g1 = gb.astype(BF16)
        r1 = gb - g1.astype(F32)
        g2 = r1.astype(BF16)
        g3 = (r1 - g2.astype(F32)).astype(BF16)
        gcum = (jnp.dot(tri, g1, preferred_element_type=F32) + jnp.dot(tri, g2, preferred_element_type=F32)
                + jnp.dot(tri, g3, preferred_element_type=F32))
        gbs.append(gb)
        gcums.append(gcum)
        gcum_ts.append(gcum.T)
        egcs.append(jnp.exp(gcum))
    units = [(b, h) for b in seqs for h in range(GDN_HEADS)]
    idx = range(len(units))
    sls = [slice(h * GDN_DK, (h + 1) * GDN_DK) for h in range(GDN_HEADS)]
    qs = [qkvs[b][0][:, sls[h]] for b, h in units]
    ks = [qkvs[b][1][:, sls[h]] for b, h in units]
    vs = [qkvs[b][2][:, sls[h]] for b, h in units]
    ss = [s_scr[b, h] for b, h in units]
    gcols = [gcums[b][:, h:h + 1] for b, h in units]
    ecols = [egcs[b][:, h:h + 1] for b, h in units]
    betas = [gbs[b][:, GDN_HEADS + h:GDN_HEADS + h + 1] for b, h in units]
    glasts = [gcums[b][n - 1:n, h:h + 1] for b, h in units]
    decays = [jnp.where(causal, jnp.exp(jnp.where(causal, gcols[u] - gcum_ts[b][h:h + 1, :], 0.0)), 0.0)
              for u, (b, h) in enumerate(units)]
    kbs = [ks[u] * betas[u] for u in idx]
    lowers = [jnp.where(strict, _mm_nt(kbs[u], ks[u]) * decays[u], 0.0) for u in idx]
    intras = [jnp.where(causal, _mm_nt(qs[u], ks[u]) * decays[u], 0.0) for u in idx]
    xs = [eye - jnp.where(level_masks[0], lowers[u], 0.0) for u in idx]
    for mask in level_masks[1:]:
        ts = [_mm(xs[u], jnp.where(mask, lowers[u], 0.0)) for u in idx]
        xs = [xs[u] - _mm(ts[u], xs[u]) for u in idx]
    us = [_mm(xs[u], vs[u] * betas[u]) for u in idx]
    ws = [_mm(xs[u], kbs[u] * ecols[u]) for u in idx]
    v_news = [us[u] - _mm(ws[u], ss[u]) for u in idx]
    os_ = [_mm(qs[u] * ecols[u], ss[u]) + _mm(intras[u], v_news[u]) for u in idx]
    s_news = [ss[u] * jnp.exp(glasts[u]) + _mm_tn(ks[u] * jnp.exp(glasts[u] - gcols[u]), v_news[u])
              for u in idx]
    for u, (b, h) in enumerate(units):
        o_ref[b, :, sls[h]] = os_[u]
        s_scr[b, h] = s_news[u]

    @pl.when(c == pl.num_programs(1) - 1)
    def _():
        sfin_ref[...] = s_scr[...]


def _gdn_prompt(qkv, gb, w_sc, batch, seq):
    n = GDN_BLOCK
    nc = seq // n
    g = GDN_SEQS
    assert batch % g == 0

    def tok(width):
        return pl.BlockSpec((g, n, width), lambda b, c: (b, c, 0))

    state_shape = (g, GDN_HEADS, GDN_DK, GDN_DK)
    o, s_fin = pl.pallas_call(
        _gdn_prompt_kernel,
        grid=(batch // g, nc),
        in_specs=[tok(QKV_CH), tok(LANES), _full(w_sc.shape)],
        out_specs=[tok(GDN_V), pl.BlockSpec(state_shape, lambda b, c: (b, 0, 0, 0))],
        out_shape=[jax.ShapeDtypeStruct((batch, seq, GDN_V), F32),
                   jax.ShapeDtypeStruct((batch, GDN_HEADS, GDN_DK, GDN_DK), F32)],
        scratch_shapes=[pltpu.VMEM(state_shape, F32), pltpu.VMEM((g, SC_HALO + n, QKV_CH), F32)],
        compiler_params=_cparams(("arbitrary", "arbitrary")),
        name="gdn_prompt",
    )(qkv.reshape(batch, seq, QKV_CH), gb.reshape(batch, seq, LANES), w_sc)
    return o.reshape(batch * seq, GDN_V), s_fin


def _post_prompt_kernel(x_ref, conv_ref, o_ref, z_ref, gon_ref, wout_ref, gx_ref, wq_ref, mk_ref, mv_ref,
                        wo_ref, gmoe_ref, wr_ref, br_ref, tri_ref, h3s_ref, x2_ref, h3r_ref, route_ref, rt_ref,
                        cnt_ref, carry, logit_buf, *, n_steps):
    step = pl.program_id(0)

    @pl.when(step == 0)
    def _():
        carry[...] = jnp.zeros(carry.shape, F32)
        logit_buf[...] = jnp.zeros(logit_buf.shape, F32)

    x1 = _mix_out(conv_ref[...], o_ref[...], z_ref[...], gon_ref[...], wout_ref, x_ref[...])
    qx = jnp.dot(_rms(x1, gx_ref[...]).astype(BF16), wq_ref[...], preferred_element_type=F32)

    route, new_carry = _route_finish(logit_buf[...], carry[0:1, :], tri_ref[...])
    route_ref[...] = route
    rt_ref[...] = route.T[0:ROUTE_ROWS, :]
    kept = jnp.where(step >= 1, new_carry, carry[0:1, :])
    carry[0:1, :] = kept
    cnt_ref[...] = jnp.broadcast_to(kept, cnt_ref.shape)

    sls = [slice(h * X_HEAD_DIM, (h + 1) * X_HEAD_DIM) for h in range(X_HEADS)]
    qb = qx.astype(BF16)
    ss = [lax.dot_general(qb[:, sl], mk_ref[:, sl], (((1,), (1,)), ((), ())),
                          preferred_element_type=F32) * (X_HEAD_DIM ** -0.5) for sl in sls]
    es = [jnp.exp(s - jnp.max(s, axis=-1, keepdims=True)) for s in ss]
    ps = [(e / jnp.sum(e, axis=-1, keepdims=True)).astype(BF16) for e in es]
    att = jnp.concatenate(
        [jnp.dot(p, mv_ref[:, sl], preferred_element_type=F32) for p, sl in zip(ps, sls)], axis=1)
    x2 = x1 + jnp.dot(att.astype(BF16), wo_ref[...], preferred_element_type=F32)
    h3, logits = _router_logits(x2, gmoe_ref[...], wr_ref, br_ref[...])
    logit_buf[...] = logits

    @pl.when(step < n_steps)
    def _():
        x2_ref[...] = x2
        h3r_ref[...] = _pack_rows(h3)

    @pl.when(step == n_steps)
    def _():
        h3r_ref[0:h3s_ref.shape[0], :] = h3s_ref[...]


def _post_prompt(x2d, conv, o, z, mk_b, mv_b, h3_sample, batch, seq, wts):
    tm = TOKEN_TILE
    nj = seq // tm
    rows = batch * seq
    n_steps = batch * nj
    n_s = h3_sample.shape[0]
    assert n_s <= tm

    def tok(width):
        return pl.BlockSpec((tm, width), lambda s: (jnp.minimum(s, n_steps - 1), 0))

    mem_spec = pl.BlockSpec((N_MEM, D_MODEL), lambda s: (jnp.minimum(s, n_steps - 1) // nj, 0))
    sq = _full((D_MODEL, D_MODEL))
    in_specs = [tok(D_MODEL), tok(CONV_CH), tok(GDN_V), tok(GDN_V), _full((1, GDN_DK)), sq,
                _full((1, D_MODEL)), sq, mem_spec, mem_spec, sq, _full((1, D_MODEL)),
                _full((D_MODEL, LANES)), _full((1, LANES)), _full((tm, tm)), _full(h3_sample.shape)]
    out_specs = [tok(D_MODEL),
                 pl.BlockSpec((tm, PACKED), lambda s: (s, 0)),
                 pl.BlockSpec((tm, LANES), lambda s: (jnp.maximum(s - 1, 0), 0)),
                 pl.BlockSpec((ROUTE_ROWS, tm), lambda s: (0, jnp.maximum(s - 1, 0))),
                 _full((SUBLANES, LANES))]
    out_shape = [jax.ShapeDtypeStruct((rows, D_MODEL), F32),
                 jax.ShapeDtypeStruct((rows + n_s, PACKED), jnp.uint32),
                 jax.ShapeDtypeStruct((rows, LANES), F32),
                 jax.ShapeDtypeStruct((ROUTE_ROWS, rows), F32),
                 jax.ShapeDtypeStruct((SUBLANES, LANES), F32)]
    return pl.pallas_call(
        functools.partial(_post_prompt_kernel, n_steps=n_steps),
        grid=(n_steps + 1,),
        in_specs=in_specs,
        out_specs=out_specs,
        out_shape=out_shape,
        scratch_shapes=[pltpu.VMEM((SUBLANES, LANES), F32), pltpu.VMEM((tm, LANES), F32)],
        compiler_params=_cparams(("arbitrary",)),
        name="post_prompt",
    )(x2d, conv, o, z, wts["g_onorm"], wts["w_out_b"], wts["g_xattn"], wts["w_xq_b"], mk_b, mv_b,
      wts["w_xo_b"], wts["g_moe"], wts["w_router"], wts["b_router"], _strict_lower(tm), h3_sample)


def _pre_sample_kernel(x_ref, gmix_ref, w_ref, wab_ref, bglu_ref, wdw_ref, bdw_ref, lng_ref, lnb_ref,
                       wsc_ref, cst_ref, chist_ref, shist_ref, conv_ref, q_ref, k_ref, v_ref, z_ref,
                       gb_ref, cnew_ref, snew_ref):
    glu, qkv_pre, z, uab = _project(x_ref[...], gmix_ref[...], w_ref, wab_ref, bglu_ref[...])
    z_ref[...] = z
    gb_ref[...] = _gate_beta(uab, cst_ref[...])
    kw = CONV_WIDTH
    acc = wdw_ref[kw - 1:kw, :] * glu
    for t in range(kw - 1):
        row = chist_ref[:, t, :]
        acc = acc + wdw_ref[t:t + 1, :] * row
        if t >= 1:
            cnew_ref[:, t - 1, :] = row
    cnew_ref[:, kw - 2, :] = glu
    conv_ref[...] = _conv_post(acc, bdw_ref[...], lng_ref[...], lnb_ref[...]).astype(BF16)
    ks = SHORT_CONV
    cs = wsc_ref[ks - 1:ks, :] * qkv_pre
    for t in range(ks - 1):
        row = shist_ref[:, t, :]
        cs = cs + wsc_ref[t:t + 1, :] * row
        if t >= 1:
            snew_ref[:, t - 1, :] = row
    snew_ref[:, ks - 2, :] = qkv_pre
    q, k, v = _qkv_post(cs)
    q_ref[...] = q
    k_ref[...] = k
    v_ref[...] = v


PRE_SAMPLE_TOKENS = 32


def _pre_sample(xs, chist, shist, wts):
    n = xs.shape[0]
    tb = min(PRE_SAMPLE_TOKENS, n)

    def tok(width):
        return pl.BlockSpec((tb, width), lambda i: (i, 0))

    def hist(a):
        return pl.BlockSpec((tb,) + a.shape[1:], lambda i: (i, 0, 0))

    consts = (wts["g_mix"], wts["w_in_b"], wts["w_ab_b"], wts["b_glu"], wts["w_dw"], wts["b_dw"],
              wts["ln_g"], wts["ln_b"], wts["w_sc"], wts["gdn_cst"])
    return pl.pallas_call(
        _pre_sample_kernel,
        grid=(n // tb,),
        in_specs=[tok(D_MODEL)] + [_full(a.shape) for a in consts] + [hist(chist), hist(shist)],
        out_specs=[tok(CONV_CH), tok(GDN_V), tok(GDN_V), tok(GDN_V), tok(GDN_V), tok(LANES),
                   hist(chist), hist(shist)],
        out_shape=[jax.ShapeDtypeStruct((n, CONV_CH), BF16)]
        + [jax.ShapeDtypeStruct((n, GDN_V), F32)] * 4
        + [jax.ShapeDtypeStruct((n, LANES), F32), jax.ShapeDtypeStruct(chist.shape, F32),
           jax.ShapeDtypeStruct(shist.shape, F32)],
        compiler_params=_cparams(("arbitrary",)),
        name="pre_sample",
    )(xs, *consts, chist, shist)


GDN_STEP_TOKENS = 8


def _gdn_sample_kernel(q_ref, k_ref, v_ref, gb_ref, s_ref, o_ref, snew_ref):
    n = GDN_DK
    for i in range(GDN_STEP_TOKENS):
        for h in range(GDN_HEADS):
            sl = slice(h * GDN_DK, (h + 1) * GDN_DK)
            qrow = q_ref[i:i + 1, sl]
            krow = k_ref[i:i + 1, sl]
            vrow = v_ref[i:i + 1, sl]
            g = gb_ref[i:i + 1, h:h + 1]
            beta = gb_ref[i:i + 1, GDN_HEADS + h:GDN_HEADS + h + 1]
            kcol = jnp.broadcast_to(krow, (n, n)).T
            qcol = jnp.broadcast_to(qrow, (n, n)).T
            s1 = s_ref[i, h] * jnp.exp(g)
            sk = jnp.sum(s1 * kcol, axis=0, keepdims=True)
            vt = (vrow - sk) * beta
            s2 = s1 + kcol * vt
            snew_ref[i, h] = s2
            o_ref[i:i + 1, sl] = jnp.sum(s2 * qcol, axis=0, keepdims=True)


def _gdn_sample(q, k, v, gb, state):
    n = q.shape[0]
    tb = GDN_STEP_TOKENS

    def tok(width):
        return pl.BlockSpec((tb, width), lambda i: (i, 0))

    st = pl.BlockSpec((tb, GDN_HEADS, GDN_DK, GDN_DK), lambda i: (i, 0, 0, 0))
    return pl.pallas_call(
        _gdn_sample_kernel,
        grid=(n // tb,),
        in_specs=[tok(GDN_V), tok(GDN_V), tok(GDN_V), tok(LANES), st],
        out_specs=[tok(GDN_V), st],
        out_shape=[jax.ShapeDtypeStruct((n, GDN_V), F32), jax.ShapeDtypeStruct(state.shape, F32)],
        compiler_params=_cparams(("arbitrary",)),
        name="gdn_sample",
    )(q, k, v, gb, state)


def _mix_sample_kernel(x_ref, conv_ref, o_ref, z_ref, gon_ref, wout_ref, gx_ref, wq_ref, x1_ref, qx_ref):
    x1 = _mix_out(conv_ref[...], o_ref[...], z_ref[...], gon_ref[...], wout_ref, x_ref[...])
    x1_ref[...] = x1
    qx_ref[...] = jnp.dot(_rms(x1, gx_ref[...]).astype(BF16), wq_ref[...], preferred_element_type=F32)


def _mix_sample(xs, conv, o, z, wts):
    n = xs.shape[0]
    in_arrays = (xs, conv, o, z, wts["g_onorm"], wts["w_out_b"], wts["g_xattn"], wts["w_xq_b"])
    return pl.pallas_call(
        _mix_sample_kernel,
        grid=(1,),
        in_specs=[_full(a.shape) for a in in_arrays],
        out_specs=[_full((n, D_MODEL))] * 2,
        out_shape=[jax.ShapeDtypeStruct((n, D_MODEL), F32)] * 2,
        compiler_params=_cparams(("arbitrary",)),
        name="mix_sample",
    )(*in_arrays)


ATTN_STEP_TOKENS = 4


def _attn_sample_kernel(qx_ref, ck_ref, cv_ref, att_ref):
    for i in range(ATTN_STEP_TOKENS):
        parts = []
        for h in range(X_HEADS):
            sl = slice(h * X_HEAD_DIM, (h + 1) * X_HEAD_DIM)
            prod = ck_ref[i, :, h, :] * qx_ref[0, i:i + 1, sl]
            s = jnp.sum(prod, axis=-1, keepdims=True) * (X_HEAD_DIM ** -0.5)
            e = jnp.exp(s - jnp.max(s, axis=0, keepdims=True))
            p = e / jnp.sum(e, axis=0, keepdims=True)
            parts.append(jnp.sum(p * cv_ref[i, :, h, :], axis=0, keepdims=True))
        att_ref[0, i:i + 1, :] = jnp.concatenate(parts, axis=1)


def _attn_sample(qx, ck, cv):
    n = qx.shape[0]
    tb = ATTN_STEP_TOKENS
    q3 = qx.reshape(n // tb, tb, D_MODEL)
    qspec = pl.BlockSpec((1, tb, D_MODEL), lambda i: (i, 0, 0))
    cspec = pl.BlockSpec((tb, N_MEM, X_HEADS, X_HEAD_DIM), lambda i: (i, 0, 0, 0))
    out = pl.pallas_call(
        _attn_sample_kernel,
        grid=(n // tb,),
        in_specs=[qspec, cspec, cspec],
        out_specs=qspec,
        out_shape=jax.ShapeDtypeStruct(q3.shape, F32),
        compiler_params=_cparams(("arbitrary",)),
        name="attn_sample",
    )(q3, ck, cv)
    return out.reshape(n, D_MODEL)


def _route_sample_kernel(x1_ref, att_ref, wo_ref, gmoe_ref, wr_ref, br_ref, tri_ref, x2_ref, h3r_ref, route_ref,
                         rt_ref, cnt_ref):
    x2 = x1_ref[...] + jnp.dot(att_ref[...].astype(BF16), wo_ref[...], preferred_element_type=F32)
    x2_ref[...] = x2
    h3, logits = _router_logits(x2, gmoe_ref[...], wr_ref, br_ref[...])
    route, counts = _route_finish(logits, jnp.zeros((1, LANES), F32), tri_ref[...])
    h3r_ref[...] = _pack_rows(h3)
    route_ref[...] = route
    rt_ref[...] = route.T[0:ROUTE_ROWS, :]
    cnt_ref[...] = jnp.broadcast_to(counts, cnt_ref.shape)


def _route_sample(x1, att, wts):
    n = x1.shape[0]
    in_arrays = (x1, att, wts["w_xo_b"], wts["g_moe"], wts["w_router"], wts["b_router"],
                 _strict_lower(n))
    shapes = [(n, D_MODEL), (n, PACKED), (n, LANES), (ROUTE_ROWS, n), (SUBLANES, LANES)]
    dtypes = [F32, jnp.uint32, F32, F32, F32]
    return pl.pallas_call(
        _route_sample_kernel,
        grid=(1,),
        in_specs=[_full(a.shape) for a in in_arrays],
        out_specs=[_full(s) for s in shapes],
        out_shape=[jax.ShapeDtypeStruct(s, d) for s, d in zip(shapes, dtypes)],
        compiler_params=_cparams(("arbitrary",)),
        name="route_sample",
    )(*in_arrays)


def _part_tiles(tiles):
    n_workers = SC_CORES * SC_SUBCORES
    while True:
        rows = tiles * MOE_TILE
        if rows % (n_workers * SUBLANES) == 0 and any(
                (rows // n_workers) % c == 0 for c in range(64, 24, -SUBLANES)):
            return tiles
        tiles += 1


def _sc_chunk(rows_per_worker):
    for c in range(64, 0, -SUBLANES):
        if rows_per_worker % c == 0:
            return c
    raise ValueError(rows_per_worker)


def _sc_gather_rows(table, idx):
    n_workers = SC_CORES * SC_SUBCORES
    b = idx.shape[0]
    assert b % (n_workers * SUBLANES) == 0
    per_worker = b // n_workers
    chunk = _sc_chunk(per_worker)
    row_shape = table.shape[1:]
    mesh = plsc.VectorSubcoreMesh(core_axis_name="c", subcore_axis_name="s")

    @functools.partial(
        pl.kernel, mesh=mesh,
        out_type=jax.ShapeDtypeStruct((b,) + row_shape, table.dtype),
        scratch_types=[pltpu.VMEM((chunk,), I32), pltpu.VMEM((chunk,) + row_shape, table.dtype),
                       pltpu.SemaphoreType.DMA],
        name="sc_gather_rows",
    )
    def gather(table_hbm, idx_hbm, out_hbm, idx_v, rows_v, sem):
        worker = lax.axis_index("s") * SC_CORES + lax.axis_index("c")
        base = worker * per_worker

        @pl.loop(0, per_worker // chunk)
        def _(c):
            off = pl.multiple_of(base + c * chunk, SUBLANES)
            pltpu.sync_copy(idx_hbm.at[pl.ds(off, chunk)], idx_v)
            pltpu.async_copy(table_hbm.at[idx_v], rows_v, sem).wait()
            pltpu.sync_copy(rows_v, out_hbm.at[pl.ds(off, chunk)])

    return gather(table, idx)


def _moe_kernel(te_ref, tn_ref, nt_ref, xs_ref, wgu_hbm, bgu_ref, wdn_hbm, bdn_ref, *rest, first_tile):
    ys_ref, wgu_f, wdn_f, wgu_b, wdn_b, sems = rest[-6:]
    step = pl.program_id(0)
    i = first_tile + step
    total = nt_ref[0]

    def weight_copies(e):
        return (pltpu.make_async_copy(wgu_hbm.at[e], wgu_f, sems.at[0]),
                pltpu.make_async_copy(wdn_hbm.at[e], wdn_f, sems.at[1]))

    def start(e):
        for cp in weight_copies(e):
            cp.start()

    @pl.when(i < total)
    def _():
        expert = te_ref[i]
        prev = te_ref[jnp.maximum(i - 1, 0)]
        fresh = jnp.logical_or(step == 0, expert != prev)

        @pl.when(step == 0)
        def _():
            start(expert)

        @pl.when(fresh)
        def _():
            for cp in weight_copies(expert):
                cp.wait()
            wgu_b[...] = wgu_f[...].astype(BF16)
            wdn_b[...] = wdn_f[...].astype(BF16)
            nxt = tn_ref[i]

            @pl.when(nxt >= 0)
            def _():
                start(nxt)

        x = _unpack_rows(xs_ref[...]).astype(BF16)

        def up(c):
            glu_cols = slice(c * MOE_COLS, (c + 1) * MOE_COLS)
            lin_cols = slice(D_EXPERT + c * MOE_COLS, D_EXPERT + (c + 1) * MOE_COLS)
            return (jnp.dot(x, wgu_b[:, glu_cols], preferred_element_type=F32) + bgu_ref[expert, :, glu_cols],
                    jnp.dot(x, wgu_b[:, lin_cols], preferred_element_type=F32) + bgu_ref[expert, :, lin_cols])

        n_chunks = D_EXPERT // MOE_COLS
        nxt = up(0)
        y = None
        for c in range(n_chunks):
            g, lin = nxt
            if c + 1 < n_chunks:
                nxt = up(c + 1)
            x_glu = jnp.minimum(g, SWIGLU_LIMIT)
            x_lin = jnp.clip(lin, -SWIGLU_LIMIT, SWIGLU_LIMIT)
            act = x_glu * jax.nn.sigmoid(SWIGLU_ALPHA * x_glu) * (x_lin + 1.0)
            part = jnp.dot(act.astype(BF16), wdn_b[c * MOE_COLS:(c + 1) * MOE_COLS, :],
                           preferred_element_type=F32)
            y = part if y is None else y + part
        ys_ref[...] = _pack_rows(y + bdn_ref[expert])

    @pl.when(i >= total)
    def _():
        ys_ref[...] = jnp.zeros(ys_ref.shape, jnp.uint32)


def _moe(tile_e, tile_next, n_tiles, xs_part, ys_prev, first_tile, n_rows, w_gu, b_gu, w_dn, b_dn):
    tm = MOE_TILE

    def resident(a):
        return pl.BlockSpec(a.shape, lambda i, te, tn, nt: (0, 0, 0))

    hbm = pl.BlockSpec(memory_space=pl.ANY)
    in_specs = [pl.BlockSpec((tm, PACKED), lambda i, te, tn, nt: (i, 0)),
                hbm, resident(b_gu), hbm, resident(b_dn)]
    operands = [tile_e, tile_next, n_tiles, xs_part, w_gu, b_gu, w_dn, b_dn]
    aliases = {}
    if ys_prev is not None:
        in_specs.append(hbm)
        aliases = {len(operands): 0}
        operands.append(ys_prev)
    grid_spec = pltpu.PrefetchScalarGridSpec(
        num_scalar_prefetch=3,
        grid=(xs_part.shape[0] // tm,),
        in_specs=in_specs,
        out_specs=pl.BlockSpec((tm, PACKED), lambda i, te, tn, nt: (first_tile + i, 0)),
        scratch_shapes=[pltpu.VMEM((D_MODEL, 2 * D_EXPERT), F32),
                        pltpu.VMEM((D_EXPERT, D_MODEL), F32),
                        pltpu.VMEM((D_MODEL, 2 * D_EXPERT), BF16),
                        pltpu.VMEM((D_EXPERT, D_MODEL), BF16),
                        pltpu.SemaphoreType.DMA((2,))],
    )
    return pl.pallas_call(
        functools.partial(_moe_kernel, first_tile=first_tile),
        grid_spec=grid_spec,
        out_shape=jax.ShapeDtypeStruct((n_rows, PACKED), jnp.uint32),
        input_output_aliases=aliases,
        compiler_params=_cparams(("arbitrary",)),
        name="moe",
    )(*operands)


def _combine_kernel(x2_ref, route_ref, gfin_ref, yt_ref, *rest):
    y_ref = rest[-1]
    route = route_ref[...]
    acc = x2_ref[...]
    for j in range(TOP_K):
        acc = acc + route[:, TOP_K + j:TOP_K + j + 1] * _unpack_rows(yt_ref[j])
    y_ref[...] = _rms(acc, gfin_ref[...])


def _combine(x2, route, g_final, ys_tok, y_prev, tok0, n_tok, ys_block0):
    tc = min(COMBINE_TILE, n_tok)
    b0 = tok0 // tc

    def tok(width):
        return pl.BlockSpec((tc, width), lambda i: (b0 + i, 0))

    in_specs = [tok(D_MODEL), tok(LANES), pl.BlockSpec((1, D_MODEL), lambda i: (0, 0)),
                pl.BlockSpec((TOP_K, tc, PACKED), lambda i: (0, ys_block0 + i, 0))]
    operands = [x2, route, g_final, ys_tok]
    aliases = {}
    if y_prev is not None:
        in_specs.append(pl.BlockSpec(memory_space=pl.ANY))
        aliases = {len(operands): 0}
        operands.append(y_prev)
    return pl.pallas_call(
        _combine_kernel,
        grid=(n_tok // tc,),
        in_specs=in_specs,
        out_specs=tok(D_MODEL),
        out_shape=jax.ShapeDtypeStruct(x2.shape, F32),
        input_output_aliases=aliases,
        compiler_params=_cparams(("arbitrary",)),
        name="combine",
    )(*operands)


def _routing_tables(idx_t, rank_t, counts, n_rows, part_tiles):
    tm = MOE_TILE
    n_tok = idx_t.shape[1]
    n_assign = TOP_K * n_tok
    tok_mask = (1 << TOKEN_BITS) - 1
    tiles_e = (counts + tm - 1) // tm
    tile_end = jnp.cumsum(tiles_e)
    row_start = (tile_end - tiles_e) * tm
    total = tile_end[-1]
    expert_ids = jnp.arange(N_EXPERTS, dtype=I32)

    def lookup(table, e):
        return jnp.sum(jnp.where(e[..., None] == expert_ids, table, 0), axis=-1)

    pos = lookup(row_start, idx_t) + rank_t
    keys_real = (idx_t * (1 << TOKEN_BITS) + jnp.arange(n_tok, dtype=I32)[None, :]).reshape(-1)
    k = jnp.arange(n_rows - n_assign, dtype=I32)
    pad_e, pad_s = k // tm, k % tm
    pad_needed = lookup(tiles_e * tm - counts, pad_e)
    pad_key_e = jnp.where((pad_e < N_EXPERTS) & (pad_s < pad_needed), pad_e, N_EXPERTS)
    keys = lax.sort(jnp.concatenate([keys_real, pad_key_e * (1 << TOKEN_BITS) + tok_mask]),
                    is_stable=False)
    src_tok = jnp.where((keys & tok_mask) == tok_mask, jnp.arange(n_rows, dtype=I32) % n_tok,
                        keys & tok_mask)
    tid = jnp.minimum(jnp.arange(n_rows // tm, dtype=I32), total - 1)
    tile_e = jnp.minimum(jnp.sum((tid[:, None] >= tile_end[None, :]).astype(I32), axis=1), N_EXPERTS - 1)
    later = (expert_ids[None, :] > expert_ids[:, None]) & (tiles_e[None, :] > 0)
    next_e = jnp.min(jnp.where(later, expert_ids[None, :], N_EXPERTS), axis=1)
    nxt = lookup(next_e, tile_e)
    nxt_first_tile = lookup(tile_end - tiles_e, jnp.minimum(nxt, N_EXPERTS - 1))
    bounds = jnp.cumsum(jnp.asarray(part_tiles, I32))
    part_end = jnp.min(jnp.where(bounds[None, :] > tid[:, None], bounds[None, :], n_rows), axis=1)
    tile_next = jnp.where((nxt < N_EXPERTS) & (nxt_first_tile < part_end), nxt, -1)
    return (tile_e.astype(I32), tile_next.astype(I32), total.reshape(1).astype(I32), src_tok.astype(I32),
            pos.reshape(-1).astype(I32))


def _pad_rows(a, rows):
    return jnp.concatenate([a, jnp.zeros((rows - a.shape[0],) + a.shape[1:], a.dtype)], axis=0)


def _pad_lanes(a, lanes=LANES):
    return jnp.concatenate([a, jnp.zeros(a.shape[:-1] + (lanes - a.shape[-1],), a.dtype)], axis=-1)


def kernel(x_prompt, mem_prompt, x_sample, state_conformer_conv, state_gdn_conv, state_gdn, cache_mem_k,
           cache_mem_v, w_in, b_glu, w_dw, b_dw, ln_g, ln_b, w_sc, a_log, dt_bias, g_onorm, w_out, g_mix,
           g_xattn, g_mem, w_xq, w_mk, w_mv, w_xo, g_moe, w_router, b_router, w_gu, b_gu, w_dn, b_dn,
           g_final):
    assert w_in.shape[0] == 1, "single-layer configuration"
    batch, seq, _ = x_prompt.shape
    n_s = x_sample.shape[0]
    n_p = batch * seq
    n_all = n_p + n_s
    assert seq % TOKEN_TILE == 0 and n_p % n_s == 0 and n_all < (1 << TOKEN_BITS) - 1
    assert (n_all * TOP_K) % (SC_CORES * SC_SUBCORES * SUBLANES) == 0

    wts = {
        "g_mix": g_mix[0][None], "g_xattn": g_xattn[0][None], "g_moe": g_moe[0][None],
        "g_onorm": g_onorm[0][None],
        "w_in_b": w_in[0].astype(BF16),
        "w_ab_b": _pad_lanes(w_in[0][:, OFF_A:]).astype(BF16),
        "b_glu": b_glu[0][None],
        "w_dw": _pad_rows(w_dw[0], 32), "b_dw": b_dw[0][None], "ln_g": ln_g[0][None], "ln_b": ln_b[0][None],
        "w_sc": _pad_rows(w_sc[0], 8),
        "gdn_cst": _pad_rows(_pad_lanes(jnp.stack([a_log[0], dt_bias[0]])), 8),
        "w_out_b": w_out[0].astype(BF16), "w_xq_b": w_xq[0].astype(BF16), "w_xo_b": w_xo[0].astype(BF16),
        "w_router": _pad_lanes(w_router[0]), "b_router": _pad_lanes(b_router[0][None]),
    }

    mk, mv, mk_b, mv_b = _mem_kv(mem_prompt.reshape(batch * N_MEM, D_MODEL), g_mem[0][None],
                                 w_mk[0].astype(BF16), w_mv[0].astype(BF16))
    xp = x_prompt.reshape(n_p, D_MODEL)
    conv_p, qkv_p, z_p, gb_p, cstate_p, sstate_p = _pre_prompt(xp, batch, seq, wts)
    o_p, gstate_p = _gdn_prompt(qkv_p, gb_p, wts["w_sc"], batch, seq)

    xs = x_sample.reshape(n_s, D_MODEL)
    conv_s, q_s, k_s, v_s, z_s, gb_s, cstate_s, sstate_s = _pre_sample(
        xs, state_conformer_conv[0], state_gdn_conv[0], wts)
    o_s, gstate_s = _gdn_sample(q_s, k_s, v_s, gb_s, state_gdn[0])
    x1_s, qx_s = _mix_sample(xs, conv_s, o_s, z_s, wts)
    att_s = _attn_sample(qx_s, cache_mem_k[0], cache_mem_v[0])
    x2_s, h3_s, route_s, rt_s, counts_s = _route_sample(x1_s, att_s, wts)

    x2_p, h3r, route_p, rt_p, counts_p = _post_prompt(xp, conv_p, o_p, z_p, mk_b, mv_b, h3_s, batch, seq,
                                                      wts)

    counts_p = counts_p[0, :N_EXPERTS].astype(I32)
    counts_s = counts_s[0, :N_EXPERTS].astype(I32)
    idx_s = rt_s[0:TOP_K].astype(I32)
    rank_s = rt_s[2 * TOP_K:3 * TOP_K].astype(I32) + jnp.sum(
        jnp.where(idx_s[..., None] == jnp.arange(N_EXPERTS, dtype=I32), counts_p, 0), axis=-1)
    idx_t = jnp.concatenate([rt_p[0:TOP_K].astype(I32), idx_s], axis=1)
    rank_t = jnp.concatenate([rt_p[2 * TOP_K:3 * TOP_K].astype(I32), rank_s], axis=1)
    min_tiles = -(-(n_all * TOP_K + N_EXPERTS * (MOE_TILE - 1)) // MOE_TILE)
    part_tiles = [_part_tiles(-(-min_tiles * w // sum(MOE_PART_WEIGHTS))) for w in MOE_PART_WEIGHTS]
    first_tiles = [sum(part_tiles[:k]) for k in range(len(part_tiles))]
    n_rows = sum(part_tiles) * MOE_TILE
    tile_e, tile_next, n_tiles, src_tok, pos = _routing_tables(idx_t, rank_t, counts_p + counts_s, n_rows,
                                                               part_tiles)
    pos_t = pos.reshape(TOP_K, n_all)
    xs_parts = [_sc_gather_rows(h3r, src_tok[f * MOE_TILE:(f + t) * MOE_TILE])
                for f, t in zip(first_tiles, part_tiles)]
    ys = None
    for k, f in enumerate(first_tiles):
        ys = _moe(tile_e, tile_next, n_tiles, xs_parts[k], ys, f, n_rows, w_gu[0],
                  b_gu[0][:, None, :], w_dn[0], b_dn[0][:, None, :])
    gfin = g_final[None]
    assert n_p % (sum(TOKEN_PART_WEIGHTS) * COMBINE_TILE) == 0
    tok_parts = [n_p * w // sum(TOKEN_PART_WEIGHTS) for w in TOKEN_PART_WEIGHTS]
    cut = -(-n_s // COMBINE_TILE) * COMBINE_TILE
    sample_part = 0 if (tok_parts[0] > cut and tok_parts[0] % min(COMBINE_TILE, n_s) == 0) \
        else len(tok_parts) - 1
    if sample_part == 0:
        tok_parts[0] -= cut
        tok_parts[-1] += cut
    y_p = None
    for k, n_tok in enumerate(tok_parts):
        tok0 = sum(tok_parts[:k])
        pos_k = pos_t[:, tok0:tok0 + n_tok]
        if k == sample_part:
            pos_k = jnp.concatenate([pos_k, pos_t[:, n_p:]], axis=1)
        ys_tok = _sc_gather_rows(ys, pos_k.reshape(-1)).reshape(TOP_K, pos_k.shape[1], PACKED)
        y_p = _combine(x2_p, route_p, gfin, ys_tok, y_p, tok0, n_tok, 0)
        if k == sample_part:
            y_s = _combine(x2_s, route_s, gfin, ys_tok, None, 0, n_s, n_tok // min(COMBINE_TILE, n_s))

    return (y_p.reshape(batch, seq, D_MODEL), y_s.reshape(n_s, 1, D_MODEL),
            cstate_p[None], sstate_p[None], gstate_p[None],
            mk[None], mv[None],
            cstate_s[None], sstate_s[None], gstate_s[None])
```

```python
import functools

import jax
import jax.numpy as jnp
from jax import lax
from jax.experimental import pallas as pl
from jax.experimental.pallas import tpu as pltpu
from jax.experimental.pallas import tpu_sc as plsc

F32, BF16, I32 = jnp.float32, jnp.bfloat16, jnp.int32

D_MODEL = 1024
CONV_CH = 512
CONV_WIDTH = 31
GDN_HEADS = 4
GDN_DK = 128
GDN_V = 512
QKV_CH = 1536
SHORT_CONV = 4
N_MEM = 256
X_HEADS = 4
X_HEAD_DIM = 256
N_EXPERTS = 32
TOP_K = 4
D_EXPERT = 1024
SWIGLU_LIMIT = 7.0
SWIGLU_ALPHA = 1.702
NORM_EPS = 1e-6
OFF_QKV = 2 * CONV_CH
OFF_Z = OFF_QKV + QKV_CH
OFF_A = OFF_Z + GDN_V

LANES = 128
SUBLANES = 8
GDN_BLOCK = 128
TOKEN_TILE = 512
MOE_TILE = 384
MOE_COLS = 256
MOE_PART_WEIGHTS = (3, 6, 8, 9)
TOKEN_PART_WEIGHTS = (1, 2, 2, 3)
COMBINE_TILE = 256
TOKEN_BITS = 15
ROUTE_ROWS = 16
SC_CORES = 2
SC_SUBCORES = 16
VMEM_LIMIT = 48 * 1024 * 1024


def _cparams(sem, vmem=VMEM_LIMIT):
    return pltpu.CompilerParams(dimension_semantics=sem, vmem_limit_bytes=vmem)


def _mm(a, b):
    return jnp.dot(a.astype(BF16), b.astype(BF16), preferred_element_type=F32)


def _mm_nt(a, b):
    return lax.dot_general(a.astype(BF16), b.astype(BF16), (((1,), (1,)), ((), ())),
                           preferred_element_type=F32)


def _mm_tn(a, b):
    return lax.dot_general(a.astype(BF16), b.astype(BF16), (((0,), (0,)), ((), ())),
                           preferred_element_type=F32)


def _rms(x, g):
    return x * lax.rsqrt(jnp.mean(x * x, axis=-1, keepdims=True) + NORM_EPS) * g


def _silu(x):
    return x * jax.nn.sigmoid(x)


PACKED = D_MODEL // 2


def _pack_rows(h):
    lo = pltpu.bitcast(h[:, :PACKED].astype(BF16).astype(F32), jnp.uint32)
    hi = pltpu.bitcast(h[:, PACKED:].astype(BF16).astype(F32), jnp.uint32)
    return (lo >> 16) | (hi & jnp.uint32(0xFFFF0000))


def _unpack_rows(w):
    lo = pltpu.bitcast(w << 16, F32)
    hi = pltpu.bitcast(w & jnp.uint32(0xFFFF0000), F32)
    return jnp.concatenate([lo, hi], axis=1)


def _full(shape):
    return pl.BlockSpec(shape, lambda *_: (0,) * len(shape))


def _strict_lower(n):
    return jnp.tril(jnp.ones((n, n), BF16), k=-1)


def _project(x, gmix, w_ref, wab_ref, bglu):
    h = _rms(x, gmix).astype(BF16)
    u_glu = jnp.dot(h, w_ref[:, 0:OFF_QKV], preferred_element_type=F32) + bglu
    glu = u_glu[:, :CONV_CH] * jax.nn.sigmoid(u_glu[:, CONV_CH:])
    qkv_pre = jnp.dot(h, w_ref[:, OFF_QKV:OFF_Z], preferred_element_type=F32)
    z = jnp.dot(h, w_ref[:, OFF_Z:OFF_A], preferred_element_type=F32)
    uab = jnp.dot(h, wab_ref[...], preferred_element_type=F32)
    return glu, qkv_pre, z, uab


def _gate_beta(uab, cst):
    lane = lax.broadcasted_iota(I32, uab.shape, 1)
    g = -jnp.exp(cst[0:1, :]) * jax.nn.softplus(uab + cst[1:2, :])
    return jnp.where(lane < GDN_HEADS, g, jax.nn.sigmoid(uab))


def _conv_post(c, b_dw, ln_g, ln_b):
    c = c + b_dw
    mu = jnp.mean(c, axis=-1, keepdims=True)
    d = c - mu
    var = jnp.mean(d * d, axis=-1, keepdims=True)
    return _silu(d * lax.rsqrt(var + NORM_EPS) * ln_g + ln_b)


def _qkv_post(cs):
    a = _silu(cs)
    parts = []
    for h in range(2 * GDN_HEADS):
        seg = a[:, h * GDN_DK:(h + 1) * GDN_DK]
        n = seg * lax.rsqrt(jnp.sum(seg * seg, axis=-1, keepdims=True) + NORM_EPS)
        if h < GDN_HEADS:
            n = n * (GDN_DK ** -0.5)
        parts.append(n)
    q = jnp.concatenate(parts[:GDN_HEADS], axis=1)
    k = jnp.concatenate(parts[GDN_HEADS:], axis=1)
    return q, k, a[:, 2 * GDN_HEADS * GDN_DK:]


def _mix_out(conv_b, o, z, gon, wout_ref, x):
    parts = []
    for h in range(GDN_HEADS):
        oh = o[:, h * 128:(h + 1) * 128]
        parts.append(oh * lax.rsqrt(jnp.mean(oh * oh, axis=-1, keepdims=True) + NORM_EPS) * gon)
    on = jnp.concatenate(parts, axis=1) * _silu(z)
    mixed = (jnp.dot(conv_b, wout_ref[0:CONV_CH, :], preferred_element_type=F32)
             + jnp.dot(on.astype(BF16), wout_ref[CONV_CH:, :], preferred_element_type=F32))
    return x + mixed


def _router_logits(x2, gmoe, wr_ref, br):
    h3 = _rms(x2, gmoe)
    h_hi = h3.astype(BF16)
    r1 = h3 - h_hi.astype(F32)
    h_mid = r1.astype(BF16)
    h_lo = (r1 - h_mid.astype(F32)).astype(BF16)
    w = wr_ref[...]
    w_hi = w.astype(BF16)
    w_lo = (w - w_hi.astype(F32)).astype(BF16)
    logits = (jnp.dot(h_hi, w_hi, preferred_element_type=F32)
              + jnp.dot(h_hi, w_lo, preferred_element_type=F32)
              + jnp.dot(h_mid, w_hi, preferred_element_type=F32)
              + jnp.dot(h_lo, w_hi, preferred_element_type=F32)) + br
    return h3, logits


def _route_finish(logits, carry, before):
    m = logits.shape[0]
    neg = jnp.float32(-jnp.inf)
    n_groups = 4 if m % 32 == 0 else 1
    rows = m // n_groups
    lanes_g = lax.broadcasted_iota(I32, (rows, LANES), 1).astype(F32)
    works = [jnp.where(lanes_g < N_EXPERTS, logits[g * rows:(g + 1) * rows], neg) for g in range(n_groups)]
    vals, idxs = [], []
    for _ in range(TOP_K):
        mxs = [jnp.max(w, axis=-1, keepdims=True) for w in works]
        ixs = [jnp.min(jnp.where(w == mx, lanes_g, float(LANES)), axis=-1, keepdims=True)
               for w, mx in zip(works, mxs)]
        works = [jnp.where(lanes_g == ix, neg, w) for w, ix in zip(works, ixs)]
        vals.append(mxs)
        idxs.append(ixs)
    sels, gates = [], []
    for g in range(n_groups):
        es = [jnp.exp(vals[r][g] - vals[0][g]) for r in range(TOP_K)]
        den = es[0] + es[1] + es[2] + es[3]
        gates.append([e / den for e in es])
        sel_g = jnp.zeros((rows, LANES), F32)
        for r in range(TOP_K):
            sel_g = sel_g + jnp.where(lanes_g == idxs[r][g], 1.0, 0.0)
        sels.append(sel_g)
    sel = jnp.concatenate(sels, axis=0)
    rank_full = jnp.dot(before, sel.astype(BF16), preferred_element_type=F32) + carry
    routes = []
    for g in range(n_groups):
        rank_g = rank_full[g * rows:(g + 1) * rows]
        route_g = jnp.zeros((rows, LANES), F32)
        for r in range(TOP_K):
            rk = jnp.sum(jnp.where(lanes_g == idxs[r][g], rank_g, 0.0), axis=-1, keepdims=True)
            route_g = (route_g + jnp.where(lanes_g == r, idxs[r][g], 0.0)
                       + jnp.where(lanes_g == TOP_K + r, gates[g][r], 0.0)
                       + jnp.where(lanes_g == 2 * TOP_K + r, rk, 0.0))
        routes.append(route_g)
    new_carry = carry + jnp.sum(sel, axis=0, keepdims=True)
    return jnp.concatenate(routes, axis=0), new_carry


def _mem_kv_kernel(mem_ref, g_ref, wk_ref, wv_ref, mk_ref, mv_ref, mkb_ref, mvb_ref):
    m = _rms(mem_ref[...], g_ref[...]).astype(BF16)
    mk = jnp.dot(m, wk_ref[...], preferred_element_type=F32)
    mv = jnp.dot(m, wv_ref[...], preferred_element_type=F32)
    for h in range(X_HEADS):
        sl = slice(h * X_HEAD_DIM, (h + 1) * X_HEAD_DIM)
        mk_ref[0, :, h, :] = mk[:, sl]
        mv_ref[0, :, h, :] = mv[:, sl]
    mkb_ref[...] = mk.astype(BF16)
    mvb_ref[...] = mv.astype(BF16)


def _mem_kv(mem, g_mem, wk_b, wv_b):
    rows = mem.shape[0]
    tm = N_MEM
    row_spec = pl.BlockSpec((tm, D_MODEL), lambda i: (i, 0))
    head_spec = pl.BlockSpec((1, tm, X_HEADS, X_HEAD_DIM), lambda i: (i, 0, 0, 0))
    head_shape = jax.ShapeDtypeStruct((rows // tm, tm, X_HEADS, X_HEAD_DIM), F32)
    return pl.pallas_call(
        _mem_kv_kernel,
        grid=(rows // tm,),
        in_specs=[row_spec, _full((1, D_MODEL)), _full((D_MODEL, D_MODEL)), _full((D_MODEL, D_MODEL))],
        out_specs=[head_spec, head_spec, row_spec, row_spec],
        out_shape=[head_shape, head_shape] + [jax.ShapeDtypeStruct((rows, D_MODEL), BF16)] * 2,
        compiler_params=_cparams(("arbitrary",)),
        name="mem_kv",
    )(mem, g_mem, wk_b, wv_b)


CONV_HALO = 32
SC_HALO = 8


def _pre_prompt_kernel(x_ref, gmix_ref, w_ref, wab_ref, bglu_ref, wdw_ref, bdw_ref, lng_ref, lnb_ref,
                       cst_ref, conv_ref, qkv_ref, z_ref, gb_ref, cstate_ref, sstate_ref, cbuf, *, tm):
    j = pl.program_id(1)

    @pl.when(j == 0)
    def _():
        cbuf[0:CONV_HALO, :] = jnp.zeros((CONV_HALO, CONV_CH), F32)

    glu, qkv_pre, z, uab = _project(x_ref[...], gmix_ref[...], w_ref, wab_ref, bglu_ref[...])
    cbuf[CONV_HALO:CONV_HALO + tm, :] = glu
    qkv_ref[...] = qkv_pre
    z_ref[...] = z
    gb_ref[...] = _gate_beta(uab, cst_ref[...])

    base = CONV_HALO - (CONV_WIDTH - 1)
    rows = CONV_HALO + tm
    accs = []
    for c in range(CONV_CH // LANES):
        lanes = slice(c * LANES, (c + 1) * LANES)
        block = cbuf[:, lanes]
        acc = None
        for r in range(SUBLANES):
            shifted = block if r == 0 else pltpu.roll(block, rows - r, 0)
            for a in range(base, base + CONV_WIDTH):
                if a % SUBLANES == r:
                    t = a - base
                    term = wdw_ref[t:t + 1, lanes] * shifted[a - r:a - r + tm, :]
                    acc = term if acc is None else acc + term
        accs.append(acc)
    acc = jnp.concatenate(accs, axis=1)
    conv_ref[...] = _conv_post(acc, bdw_ref[...], lng_ref[...], lnb_ref[...]).astype(BF16)

    @pl.when(j == pl.num_programs(1) - 1)
    def _():
        cstate_ref[0] = cbuf[pl.ds(CONV_HALO + tm - (CONV_WIDTH - 1), CONV_WIDTH - 1), :]
        sstate_ref[0] = qkv_pre[tm - (SHORT_CONV - 1):, :]

    cbuf[0:CONV_HALO, :] = cbuf[tm:tm + CONV_HALO, :]


def _pre_prompt(x2d, batch, seq, wts):
    tm = TOKEN_TILE
    nj = seq // tm
    rows = batch * seq

    def tok(width):
        return pl.BlockSpec((tm, width), lambda b, j: (b * nj + j, 0))

    in_specs = [tok(D_MODEL), _full((1, D_MODEL)), _full(wts["w_in_b"].shape), _full((D_MODEL, LANES)),
                _full((1, OFF_QKV)), _full((32, CONV_CH)), _full((1, CONV_CH)), _full((1, CONV_CH)),
                _full((1, CONV_CH)), _full((8, LANES))]
    out_specs = [tok(CONV_CH), tok(QKV_CH), tok(GDN_V), tok(LANES),
                 pl.BlockSpec((1, CONV_WIDTH - 1, CONV_CH), lambda b, j: (b, 0, 0)),
                 pl.BlockSpec((1, SHORT_CONV - 1, QKV_CH), lambda b, j: (b, 0, 0))]
    out_shape = [jax.ShapeDtypeStruct((rows, CONV_CH), BF16),
                 jax.ShapeDtypeStruct((rows, QKV_CH), F32),
                 jax.ShapeDtypeStruct((rows, GDN_V), F32),
                 jax.ShapeDtypeStruct((rows, LANES), F32),
                 jax.ShapeDtypeStruct((batch, CONV_WIDTH - 1, CONV_CH), F32),
                 jax.ShapeDtypeStruct((batch, SHORT_CONV - 1, QKV_CH), F32)]
    return pl.pallas_call(
        functools.partial(_pre_prompt_kernel, tm=tm),
        grid=(batch, nj),
        in_specs=in_specs,
        out_specs=out_specs,
        out_shape=out_shape,
        scratch_shapes=[pltpu.VMEM((CONV_HALO + tm, CONV_CH), F32)],
        compiler_params=_cparams(("arbitrary", "arbitrary")),
        name="pre_prompt",
    )(x2d, wts["g_mix"], wts["w_in_b"], wts["w_ab_b"], wts["b_glu"], wts["w_dw"], wts["b_dw"],
      wts["ln_g"], wts["ln_b"], wts["gdn_cst"])


GDN_SEQS = 4


def _gdn_prompt_kernel(qkv_ref, gb_ref, wsc_ref, o_ref, sfin_ref, s_scr, sbuf):
    c = pl.program_id(1)
    n = GDN_BLOCK
    seqs = range(GDN_SEQS)

    @pl.when(c == 0)
    def _():
        s_scr[...] = jnp.zeros(s_scr.shape, F32)
        sbuf[:, 0:SC_HALO, :] = jnp.zeros((GDN_SEQS, SC_HALO, QKV_CH), F32)

    sbase = SC_HALO - (SHORT_CONV - 1)
    qkvs = []
    for b in seqs:
        sbuf[b, SC_HALO:SC_HALO + n, :] = qkv_ref[b]
        block = sbuf[b]
        cs = None
        for t in range(SHORT_CONV):
            r = (sbase + t) % SUBLANES
            shifted = block if r == 0 else pltpu.roll(block, SC_HALO + n - r, 0)
            term = wsc_ref[t:t + 1, :] * shifted[sbase + t - r:sbase + t - r + n, :]
            cs = term if cs is None else cs + term
        qkvs.append(_qkv_post(cs))
        sbuf[b, 0:SC_HALO, :] = sbuf[b, n:n + SC_HALO, :]

    row = lax.broadcasted_iota(I32, (n, n), 0)
    col = lax.broadcasted_iota(I32, (n, n), 1)
    causal = row >= col
    strict = row > col
    tri = jnp.where(causal, 1.0, 0.0).astype(BF16)
    eye = jnp.where(row == col, 1.0, 0.0)
    level_masks = []
    b = 1
    while b < n:
        same_pair = ((row ^ col) & ~(2 * b - 1)) == 0
        level_masks.append(same_pair & ((row & b) != 0) & ((col & b) == 0))
        b *= 2
    gbs, gcums, gcum_ts, egcs = [], [], [], []
    for b in seqs:
        gb = gb_ref[b]
        g1 = gb.astype(BF16)
        r1 = gb - g1.astype(F32)
        g2 = r1.astype(BF16)
        g3 = (r1 - g2.astype(F32)).astype(BF16)
        gcum = (jnp.dot(tri, g1, preferred_element_type=F32) + jnp.dot(tri, g2, preferred_element_type=F32)
                + jnp.dot(tri, g3, preferred_element_type=F32))
        gbs.append(gb)
        gcums.append(gcum)
        gcum_ts.append(gcum.T)
        egcs.append(jnp.exp(gcum))
    units = [(b, h) for b in seqs for h in range(GDN_HEADS)]
    idx = range(len(units))
    sls = [slice(h * GDN_DK, (h + 1) * GDN_DK) for h in range(GDN_HEADS)]
    qs = [qkvs[b][0][:, sls[h]] for b, h in units]
    ks = [qkvs[b][1][:, sls[h]] for b, h in units]
    vs = [qkvs[b][2][:, sls[h]] for b, h in units]
    ss = [s_scr[b, h] for b, h in units]
    gcols = [gcums[b][:, h:h + 1] for b, h in units]
    ecols = [egcs[b][:, h:h + 1] for b, h in units]
    betas = [gbs[b][:, GDN_HEADS + h:GDN_HEADS + h + 1] for b, h in units]
    glasts = [gcums[b][n - 1:n, h:h + 1] for b, h in units]
    decays = [jnp.where(causal, jnp.exp(jnp.where(causal, gcols[u] - gcum_ts[b][h:h + 1, :], 0.0)), 0.0)
              for u, (b, h) in enumerate(units)]
    kbs = [ks[u] * betas[u] for u in idx]
    lowers = [jnp.where(strict, _mm_nt(kbs[u], ks[u]) * decays[u], 0.0) for u in idx]
    intras = [jnp.where(causal, _mm_nt(qs[u], ks[u]) * decays[u], 0.0) for u in idx]
    xs = [eye - jnp.where(level_masks[0], lowers[u], 0.0) for u in idx]
    for mask in level_masks[1:]:
        ts = [_mm(xs[u], jnp.where(mask, lowers[u], 0.0)) for u in idx]
        xs = [xs[u] - _mm(ts[u], xs[u]) for u in idx]
    us = [_mm(xs[u], vs[u] * betas[u]) for u in idx]
    ws = [_mm(xs[u], kbs[u] * ecols[u]) for u in idx]
    v_news = [us[u] - _mm(ws[u], ss[u]) for u in idx]
    os_ = [_mm(qs[u] * ecols[u], ss[u]) + _mm(intras[u], v_news[u]) for u in idx]
    s_news = [ss[u] * jnp.exp(glasts[u]) + _mm_tn(ks[u] * jnp.exp(glasts[u] - gcols[u]), v_news[u])
              for u in idx]
    for u, (b, h) in enumerate(units):
        o_ref[b, :, sls[h]] = os_[u]
        s_scr[b, h] = s_news[u]

    @pl.when(c == pl.num_programs(1) - 1)
    def _():
        sfin_ref[...] = s_scr[...]


def _gdn_prompt(qkv, gb, w_sc, batch, seq):
    n = GDN_BLOCK
    nc = seq // n
    g = GDN_SEQS
    assert batch % g == 0

    def tok(width):
        return pl.BlockSpec((g, n, width), lambda b, c: (b, c, 0))

    state_shape = (g, GDN_HEADS, GDN_DK, GDN_DK)
    o, s_fin = pl.pallas_call(
        _gdn_prompt_kernel,
        grid=(batch // g, nc),
        in_specs=[tok(QKV_CH), tok(LANES), _full(w_sc.shape)],
        out_specs=[tok(GDN_V), pl.BlockSpec(state_shape, lambda b, c: (b, 0, 0, 0))],
        out_shape=[jax.ShapeDtypeStruct((batch, seq, GDN_V), F32),
                   jax.ShapeDtypeStruct((batch, GDN_HEADS, GDN_DK, GDN_DK), F32)],
        scratch_shapes=[pltpu.VMEM(state_shape, F32), pltpu.VMEM((g, SC_HALO + n, QKV_CH), F32)],
        compiler_params=_cparams(("arbitrary", "arbitrary")),
        name="gdn_prompt",
    )(qkv.reshape(batch, seq, QKV_CH), gb.reshape(batch, seq, LANES), w_sc)
    return o.reshape(batch * seq, GDN_V), s_fin


def _post_prompt_kernel(x_ref, conv_ref, o_ref, z_ref, gon_ref, wout_ref, gx_ref, wq_ref, mk_ref, mv_ref,
                        wo_ref, gmoe_ref, wr_ref, br_ref, tri_ref, h3s_ref, x2_ref, h3r_ref, route_ref, rt_ref,
                        cnt_ref, carry, logit_buf, *, n_steps):
    step = pl.program_id(0)

    @pl.when(step == 0)
    def _():
        carry[...] = jnp.zeros(carry.shape, F32)
        logit_buf[...] = jnp.zeros(logit_buf.shape, F32)

    x1 = _mix_out(conv_ref[...], o_ref[...], z_ref[...], gon_ref[...], wout_ref, x_ref[...])
    qx = jnp.dot(_rms(x1, gx_ref[...]).astype(BF16), wq_ref[...], preferred_element_type=F32)

    route, new_carry = _route_finish(logit_buf[...], carry[0:1, :], tri_ref[...])
    route_ref[...] = route
    rt_ref[...] = route.T[0:ROUTE_ROWS, :]
    kept = jnp.where(step >= 1, new_carry, carry[0:1, :])
    carry[0:1, :] = kept
    cnt_ref[...] = jnp.broadcast_to(kept, cnt_ref.shape)

    sls = [slice(h * X_HEAD_DIM, (h + 1) * X_HEAD_DIM) for h in range(X_HEADS)]
    qb = qx.astype(BF16)
    ss = [lax.dot_general(qb[:, sl], mk_ref[:, sl], (((1,), (1,)), ((), ())),
                          preferred_element_type=F32) * (X_HEAD_DIM ** -0.5) for sl in sls]
    es = [jnp.exp(s - jnp.max(s, axis=-1, keepdims=True)) for s in ss]
    ps = [(e / jnp.sum(e, axis=-1, keepdims=True)).astype(BF16) for e in es]
    att = jnp.concatenate(
        [jnp.dot(p, mv_ref[:, sl], preferred_element_type=F32) for p, sl in zip(ps, sls)], axis=1)
    x2 = x1 + jnp.dot(att.astype(BF16), wo_ref[...], preferred_element_type=F32)
    h3, logits = _router_logits(x2, gmoe_ref[...], wr_ref, br_ref[...])
    logit_buf[...] = logits

    @pl.when(step < n_steps)
    def _():
        x2_ref[...] = x2
        h3r_ref[...] = _pack_rows(h3)

    @pl.when(step == n_steps)
    def _():
        h3r_ref[0:h3s_ref.shape[0], :] = h3s_ref[...]


def _post_prompt(x2d, conv, o, z, mk_b, mv_b, h3_sample, batch, seq, wts):
    tm = TOKEN_TILE
    nj = seq // tm
    rows = batch * seq
    n_steps = batch * nj
    n_s = h3_sample.shape[0]
    assert n_s <= tm

    def tok(width):
        return pl.BlockSpec((tm, width), lambda s: (jnp.minimum(s, n_steps - 1), 0))

    mem_spec = pl.BlockSpec((N_MEM, D_MODEL), lambda s: (jnp.minimum(s, n_steps - 1) // nj, 0))
    sq = _full((D_MODEL, D_MODEL))
    in_specs = [tok(D_MODEL), tok(CONV_CH), tok(GDN_V), tok(GDN_V), _full((1, GDN_DK)), sq,
                _full((1, D_MODEL)), sq, mem_spec, mem_spec, sq, _full((1, D_MODEL)),
                _full((D_MODEL, LANES)), _full((1, LANES)), _full((tm, tm)), _full(h3_sample.shape)]
    out_specs = [tok(D_MODEL),
                 pl.BlockSpec((tm, PACKED), lambda s: (s, 0)),
                 pl.BlockSpec((tm, LANES), lambda s: (jnp.maximum(s - 1, 0), 0)),
                 pl.BlockSpec((ROUTE_ROWS, tm), lambda s: (0, jnp.maximum(s - 1, 0))),
                 _full((SUBLANES, LANES))]
    out_shape = [jax.ShapeDtypeStruct((rows, D_MODEL), F32),
                 jax.ShapeDtypeStruct((rows + n_s, PACKED), jnp.uint32),
                 jax.ShapeDtypeStruct((rows, LANES), F32),
                 jax.ShapeDtypeStruct((ROUTE_ROWS, rows), F32),
                 jax.ShapeDtypeStruct((SUBLANES, LANES), F32)]
    return pl.pallas_call(
        functools.partial(_post_prompt_kernel, n_steps=n_steps),
        grid=(n_steps + 1,),
        in_specs=in_specs,
        out_specs=out_specs,
        out_shape=out_shape,
        scratch_shapes=[pltpu.VMEM((SUBLANES, LANES), F32), pltpu.VMEM((tm, LANES), F32)],
        compiler_params=_cparams(("arbitrary",)),
        name="post_prompt",
    )(x2d, conv, o, z, wts["g_onorm"], wts["w_out_b"], wts["g_xattn"], wts["w_xq_b"], mk_b, mv_b,
      wts["w_xo_b"], wts["g_moe"], wts["w_router"], wts["b_router"], _strict_lower(tm), h3_sample)


def _pre_sample_kernel(x_ref, gmix_ref, w_ref, wab_ref, bglu_ref, wdw_ref, bdw_ref, lng_ref, lnb_ref,
                       wsc_ref, cst_ref, chist_ref, shist_ref, conv_ref, q_ref, k_ref, v_ref, z_ref,
                       gb_ref, cnew_ref, snew_ref):
    glu, qkv_pre, z, uab = _project(x_ref[...], gmix_ref[...], w_ref, wab_ref, bglu_ref[...])
    z_ref[...] = z
    gb_ref[...] = _gate_beta(uab, cst_ref[...])
    kw = CONV_WIDTH
    acc = wdw_ref[kw - 1:kw, :] * glu
    for t in range(kw - 1):
        row = chist_ref[:, t, :]
        acc = acc + wdw_ref[t:t + 1, :] * row
        if t >= 1:
            cnew_ref[:, t - 1, :] = row
    cnew_ref[:, kw - 2, :] = glu
    conv_ref[...] = _conv_post(acc, bdw_ref[...], lng_ref[...], lnb_ref[...]).astype(BF16)
    ks = SHORT_CONV
    cs = wsc_ref[ks - 1:ks, :] * qkv_pre
    for t in range(ks - 1):
        row = shist_ref[:, t, :]
        cs = cs + wsc_ref[t:t + 1, :] * row
        if t >= 1:
            snew_ref[:, t - 1, :] = row
    snew_ref[:, ks - 2, :] = qkv_pre
    q, k, v = _qkv_post(cs)
    q_ref[...] = q
    k_ref[...] = k
    v_ref[...] = v


PRE_SAMPLE_TOKENS = 32


def _pre_sample(xs, chist, shist, wts):
    n = xs.shape[0]
    tb = min(PRE_SAMPLE_TOKENS, n)

    def tok(width):
        return pl.BlockSpec((tb, width), lambda i: (i, 0))

    def hist(a):
        return pl.BlockSpec((tb,) + a.shape[1:], lambda i: (i, 0, 0))

    consts = (wts["g_mix"], wts["w_in_b"], wts["w_ab_b"], wts["b_glu"], wts["w_dw"], wts["b_dw"],
              wts["ln_g"], wts["ln_b"], wts["w_sc"], wts["gdn_cst"])
    return pl.pallas_call(
        _pre_sample_kernel,
        grid=(n // tb,),
        in_specs=[tok(D_MODEL)] + [_full(a.shape) for a in consts] + [hist(chist), hist(shist)],
        out_specs=[tok(CONV_CH), tok(GDN_V), tok(GDN_V), tok(GDN_V), tok(GDN_V), tok(LANES),
                   hist(chist), hist(shist)],
        out_shape=[jax.ShapeDtypeStruct((n, CONV_CH), BF16)]
        + [jax.ShapeDtypeStruct((n, GDN_V), F32)] * 4
        + [jax.ShapeDtypeStruct((n, LANES), F32), jax.ShapeDtypeStruct(chist.shape, F32),
           jax.ShapeDtypeStruct(shist.shape, F32)],
        compiler_params=_cparams(("arbitrary",)),
        name="pre_sample",
    )(xs, *consts, chist, shist)


GDN_STEP_TOKENS = 8


def _gdn_sample_kernel(q_ref, k_ref, v_ref, gb_ref, s_ref, o_ref, snew_ref):
    n = GDN_DK
    for i in range(GDN_STEP_TOKENS):
        for h in range(GDN_HEADS):
            sl = slice(h * GDN_DK, (h + 1) * GDN_DK)
            qrow = q_ref[i:i + 1, sl]
            krow = k_ref[i:i + 1, sl]
            vrow = v_ref[i:i + 1, sl]
            g = gb_ref[i:i + 1, h:h + 1]
            beta = gb_ref[i:i + 1, GDN_HEADS + h:GDN_HEADS + h + 1]
            kcol = jnp.broadcast_to(krow, (n, n)).T
            qcol = jnp.broadcast_to(qrow, (n, n)).T
            s1 = s_ref[i, h] * jnp.exp(g)
            sk = jnp.sum(s1 * kcol, axis=0, keepdims=True)
            vt = (vrow - sk) * beta
            s2 = s1 + kcol * vt
            snew_ref[i, h] = s2
            o_ref[i:i + 1, sl] = jnp.sum(s2 * qcol, axis=0, keepdims=True)


def _gdn_sample(q, k, v, gb, state):
    n = q.shape[0]
    tb = GDN_STEP_TOKENS

    def tok(width):
        return pl.BlockSpec((tb, width), lambda i: (i, 0))

    st = pl.BlockSpec((tb, GDN_HEADS, GDN_DK, GDN_DK), lambda i: (i, 0, 0, 0))
    return pl.pallas_call(
        _gdn_sample_kernel,
        grid=(n // tb,),
        in_specs=[tok(GDN_V), tok(GDN_V), tok(GDN_V), tok(LANES), st],
        out_specs=[tok(GDN_V), st],
        out_shape=[jax.ShapeDtypeStruct((n, GDN_V), F32), jax.ShapeDtypeStruct(state.shape, F32)],
        compiler_params=_cparams(("arbitrary",)),
        name="gdn_sample",
    )(q, k, v, gb, state)


def _mix_sample_kernel(x_ref, conv_ref, o_ref, z_ref, gon_ref, wout_ref, gx_ref, wq_ref, x1_ref, qx_ref):
    x1 = _mix_out(conv_ref[...], o_ref[...], z_ref[...], gon_ref[...], wout_ref, x_ref[...])
    x1_ref[...] = x1
    qx_ref[...] = jnp.dot(_rms(x1, gx_ref[...]).astype(BF16), wq_ref[...], preferred_element_type=F32)


def _mix_sample(xs, conv, o, z, wts):
    n = xs.shape[0]
    in_arrays = (xs, conv, o, z, wts["g_onorm"], wts["w_out_b"], wts["g_xattn"], wts["w_xq_b"])
    return pl.pallas_call(
        _mix_sample_kernel,
        grid=(1,),
        in_specs=[_full(a.shape) for a in in_arrays],
        out_specs=[_full((n, D_MODEL))] * 2,
        out_shape=[jax.ShapeDtypeStruct((n, D_MODEL), F32)] * 2,
        compiler_params=_cparams(("arbitrary",)),
        name="mix_sample",
    )(*in_arrays)


ATTN_STEP_TOKENS = 4


def _attn_sample_kernel(qx_ref, ck_ref, cv_ref, att_ref):
    for i in range(ATTN_STEP_TOKENS):
        parts = []
        for h in range(X_HEADS):
            sl = slice(h * X_HEAD_DIM, (h + 1) * X_HEAD_DIM)
            prod = ck_ref[i, :, h, :] * qx_ref[0, i:i + 1, sl]
            s = jnp.sum(prod, axis=-1, keepdims=True) * (X_HEAD_DIM ** -0.5)
            e = jnp.exp(s - jnp.max(s, axis=0, keepdims=True))
            p = e / jnp.sum(e, axis=0, keepdims=True)
            parts.append(jnp.sum(p * cv_ref[i, :, h, :], axis=0, keepdims=True))
        att_ref[0, i:i + 1, :] = jnp.concatenate(parts, axis=1)


def _attn_sample(qx, ck, cv):
    n = qx.shape[0]
    tb = ATTN_STEP_TOKENS
    q3 = qx.reshape(n // tb, tb, D_MODEL)
    qspec = pl.BlockSpec((1, tb, D_MODEL), lambda i: (i, 0, 0))
    cspec = pl.BlockSpec((tb, N_MEM, X_HEADS, X_HEAD_DIM), lambda i: (i, 0, 0, 0))
    out = pl.pallas_call(
        _attn_sample_kernel,
        grid=(n // tb,),
        in_specs=[qspec, cspec, cspec],
        out_specs=qspec,
        out_shape=jax.ShapeDtypeStruct(q3.shape, F32),
        compiler_params=_cparams(("arbitrary",)),
        name="attn_sample",
    )(q3, ck, cv)
    return out.reshape(n, D_MODEL)


def _route_sample_kernel(x1_ref, att_ref, wo_ref, gmoe_ref, wr_ref, br_ref, tri_ref, x2_ref, h3r_ref, route_ref,
                         rt_ref, cnt_ref):
    x2 = x1_ref[...] + jnp.dot(att_ref[...].astype(BF16), wo_ref[...], preferred_element_type=F32)
    x2_ref[...] = x2
    h3, logits = _router_logits(x2, gmoe_ref[...], wr_ref, br_ref[...])
    route, counts = _route_finish(logits, jnp.zeros((1, LANES), F32), tri_ref[...])
    h3r_ref[...] = _pack_rows(h3)
    route_ref[...] = route
    rt_ref[...] = route.T[0:ROUTE_ROWS, :]
    cnt_ref[...] = jnp.broadcast_to(counts, cnt_ref.shape)


def _route_sample(x1, att, wts):
    n = x1.shape[0]
    in_arrays = (x1, att, wts["w_xo_b"], wts["g_moe"], wts["w_router"], wts["b_router"],
                 _strict_lower(n))
    shapes = [(n, D_MODEL), (n, PACKED), (n, LANES), (ROUTE_ROWS, n), (SUBLANES, LANES)]
    dtypes = [F32, jnp.uint32, F32, F32, F32]
    return pl.pallas_call(
        _route_sample_kernel,
        grid=(1,),
        in_specs=[_full(a.shape) for a in in_arrays],
        out_specs=[_full(s) for s in shapes],
        out_shape=[jax.ShapeDtypeStruct(s, d) for s, d in zip(shapes, dtypes)],
        compiler_params=_cparams(("arbitrary",)),
        name="route_sample",
    )(*in_arrays)


def _part_tiles(tiles):
    n_workers = SC_CORES * SC_SUBCORES
    while True:
        rows = tiles * MOE_TILE
        if rows % (n_workers * SUBLANES) == 0 and any(
                (rows // n_workers) % c == 0 for c in range(64, 24, -SUBLANES)):
            return tiles
        tiles += 1


SC_CHUNK_BYTES = 192 * 1024


def _sc_chunk(rows_per_worker, row_bytes):
    for c in range(min(128, SC_CHUNK_BYTES // row_bytes) // SUBLANES * SUBLANES, 0, -SUBLANES):
        if rows_per_worker % c == 0 and (rows_per_worker // c) % 2 == 0:
            return c
    raise ValueError(rows_per_worker)


def _sc_gather_rows(table, idx):
    n_workers = SC_CORES * SC_SUBCORES
    b = idx.shape[0]
    assert b % (n_workers * SUBLANES) == 0
    per_worker = b // n_workers
    row_shape = table.shape[1:]
    chunk = _sc_chunk(per_worker, 4 * functools.reduce(lambda a, c: a * c, row_shape, 1))
    mesh = plsc.VectorSubcoreMesh(core_axis_name="c", subcore_axis_name="s")
    rows_t = pltpu.VMEM((chunk,) + row_shape, table.dtype)

    @functools.partial(
        pl.kernel, mesh=mesh,
        out_type=jax.ShapeDtypeStruct((b,) + row_shape, table.dtype),
        scratch_types=[pltpu.VMEM((chunk,), I32), pltpu.VMEM((chunk,), I32), rows_t, rows_t,
                       pltpu.SemaphoreType.DMA, pltpu.SemaphoreType.DMA],
        name="sc_gather_rows",
    )
    def gather(table_hbm, idx_hbm, out_hbm, idx_a, idx_b, rows_a, rows_b, sem_a, sem_b):
        worker = lax.axis_index("s") * SC_CORES + lax.axis_index("c")
        base = worker * per_worker

        @pl.loop(0, per_worker // (2 * chunk))
        def _(c):
            off_a = pl.multiple_of(base + 2 * c * chunk, SUBLANES)
            off_b = pl.multiple_of(off_a + chunk, SUBLANES)
            pltpu.sync_copy(idx_hbm.at[pl.ds(off_a, chunk)], idx_a)
            gather_a = pltpu.async_copy(table_hbm.at[idx_a], rows_a, sem_a)
            pltpu.sync_copy(idx_hbm.at[pl.ds(off_b, chunk)], idx_b)
            gather_b = pltpu.async_copy(table_hbm.at[idx_b], rows_b, sem_b)
            gather_a.wait()
            pltpu.sync_copy(rows_a, out_hbm.at[pl.ds(off_a, chunk)])
            gather_b.wait()
            pltpu.sync_copy(rows_b, out_hbm.at[pl.ds(off_b, chunk)])

    return gather(table, idx)


def _moe_kernel(te_ref, tn_ref, nt_ref, xs_ref, wgu_hbm, bgu_ref, wdn_hbm, bdn_ref, *rest, first_tile):
    ys_ref, wgu_f, wdn_f, wgu_b, wdn_b, sems = rest[-6:]
    step = pl.program_id(0)
    i = first_tile + step
    total = nt_ref[0]

    def weight_copies(e):
        return (pltpu.make_async_copy(wgu_hbm.at[e], wgu_f, sems.at[0]),
                pltpu.make_async_copy(wdn_hbm.at[e], wdn_f, sems.at[1]))

    def start(e):
        for cp in weight_copies(e):
            cp.start()

    @pl.when(i < total)
    def _():
        expert = te_ref[i]
        prev = te_ref[jnp.maximum(i - 1, 0)]
        fresh = jnp.logical_or(step == 0, expert != prev)

        @pl.when(step == 0)
        def _():
            start(expert)

        @pl.when(fresh)
        def _():
            for cp in weight_copies(expert):
                cp.wait()
            wgu_b[...] = wgu_f[...].astype(BF16)
            wdn_b[...] = wdn_f[...].astype(BF16)
            nxt = tn_ref[i]

            @pl.when(nxt >= 0)
            def _():
                start(nxt)

        x = _unpack_rows(xs_ref[...]).astype(BF16)

        def up(c):
            glu_cols = slice(c * MOE_COLS, (c + 1) * MOE_COLS)
            lin_cols = slice(D_EXPERT + c * MOE_COLS, D_EXPERT + (c + 1) * MOE_COLS)
            return (jnp.dot(x, wgu_b[:, glu_cols], preferred_element_type=F32) + bgu_ref[expert, :, glu_cols],
                    jnp.dot(x, wgu_b[:, lin_cols], preferred_element_type=F32) + bgu_ref[expert, :, lin_cols])

        n_chunks = D_EXPERT // MOE_COLS
        nxt = up(0)
        y = None
        for c in range(n_chunks):
            g, lin = nxt
            if c + 1 < n_chunks:
                nxt = up(c + 1)
            x_glu = jnp.minimum(g, SWIGLU_LIMIT)
            x_lin = jnp.clip(lin, -SWIGLU_LIMIT, SWIGLU_LIMIT)
            act = x_glu * jax.nn.sigmoid(SWIGLU_ALPHA * x_glu) * (x_lin + 1.0)
            part = jnp.dot(act.astype(BF16), wdn_b[c * MOE_COLS:(c + 1) * MOE_COLS, :],
                           preferred_element_type=F32)
            y = part if y is None else y + part
        ys_ref[...] = _pack_rows(y + bdn_ref[expert])

    @pl.when(i >= total)
    def _():
        ys_ref[...] = jnp.zeros(ys_ref.shape, jnp.uint32)


def _moe(tile_e, tile_next, n_tiles, xs_part, ys_prev, first_tile, n_rows, w_gu, b_gu, w_dn, b_dn):
    tm = MOE_TILE

    def resident(a):
        return pl.BlockSpec(a.shape, lambda i, te, tn, nt: (0, 0, 0))

    hbm = pl.BlockSpec(memory_space=pl.ANY)
    in_specs = [pl.BlockSpec((tm, PACKED), lambda i, te, tn, nt: (i, 0)),
                hbm, resident(b_gu), hbm, resident(b_dn)]
    operands = [tile_e, tile_next, n_tiles, xs_part, w_gu, b_gu, w_dn, b_dn]
    aliases = {}
    if ys_prev is not None:
        in_specs.append(hbm)
        aliases = {len(operands): 0}
        operands.append(ys_prev)
    grid_spec = pltpu.PrefetchScalarGridSpec(
        num_scalar_prefetch=3,
        grid=(xs_part.shape[0] // tm,),
        in_specs=in_specs,
        out_specs=pl.BlockSpec((tm, PACKED), lambda i, te, tn, nt: (first_tile + i, 0)),
        scratch_shapes=[pltpu.VMEM((D_MODEL, 2 * D_EXPERT), F32),
                        pltpu.VMEM((D_EXPERT, D_MODEL), F32),
                        pltpu.VMEM((D_MODEL, 2 * D_EXPERT), BF16),
                        pltpu.VMEM((D_EXPERT, D_MODEL), BF16),
                        pltpu.SemaphoreType.DMA((2,))],
    )
    return pl.pallas_call(
        functools.partial(_moe_kernel, first_tile=first_tile),
        grid_spec=grid_spec,
        out_shape=jax.ShapeDtypeStruct((n_rows, PACKED), jnp.uint32),
        input_output_aliases=aliases,
        compiler_params=_cparams(("arbitrary",)),
        name="moe",
    )(*operands)


def _combine_kernel(x2_ref, route_ref, gfin_ref, yt_ref, *rest):
    y_ref = rest[-1]
    route = route_ref[...]
    acc = x2_ref[...]
    for j in range(TOP_K):
        acc = acc + route[:, TOP_K + j:TOP_K + j + 1] * _unpack_rows(yt_ref[j])
    y_ref[...] = _rms(acc, gfin_ref[...])


def _combine(x2, route, g_final, ys_tok, y_prev, tok0, n_tok, ys_block0):
    tc = min(COMBINE_TILE, n_tok)
    b0 = tok0 // tc

    def tok(width):
        return pl.BlockSpec((tc, width), lambda i: (b0 + i, 0))

    in_specs = [tok(D_MODEL), tok(LANES), pl.BlockSpec((1, D_MODEL), lambda i: (0, 0)),
                pl.BlockSpec((TOP_K, tc, PACKED), lambda i: (0, ys_block0 + i, 0))]
    operands = [x2, route, g_final, ys_tok]
    aliases = {}
    if y_prev is not None:
        in_specs.append(pl.BlockSpec(memory_space=pl.ANY))
        aliases = {len(operands): 0}
        operands.append(y_prev)
    return pl.pallas_call(
        _combine_kernel,
        grid=(n_tok // tc,),
        in_specs=in_specs,
        out_specs=tok(D_MODEL),
        out_shape=jax.ShapeDtypeStruct(x2.shape, F32),
        input_output_aliases=aliases,
        compiler_params=_cparams(("arbitrary",)),
        name="combine",
    )(*operands)


def _routing_tables(idx_t, rank_t, counts, n_rows, part_tiles):
    tm = MOE_TILE
    n_tok = idx_t.shape[1]
    n_assign = TOP_K * n_tok
    tok_mask = (1 << TOKEN_BITS) - 1
    tiles_e = (counts + tm - 1) // tm
    tile_end = jnp.cumsum(tiles_e)
    row_start = (tile_end - tiles_e) * tm
    total = tile_end[-1]
    expert_ids = jnp.arange(N_EXPERTS, dtype=I32)

    def lookup(table, e):
        return jnp.sum(jnp.where(e[..., None] == expert_ids, table, 0), axis=-1)

    pos = lookup(row_start, idx_t) + rank_t
    keys_real = (idx_t * (1 << TOKEN_BITS) + jnp.arange(n_tok, dtype=I32)[None, :]).reshape(-1)
    k = jnp.arange(n_rows - n_assign, dtype=I32)
    pad_e, pad_s = k // tm, k % tm
    pad_needed = lookup(tiles_e * tm - counts, pad_e)
    pad_key_e = jnp.where((pad_e < N_EXPERTS) & (pad_s < pad_needed), pad_e, N_EXPERTS)
    keys = lax.sort(jnp.concatenate([keys_real, pad_key_e * (1 << TOKEN_BITS) + tok_mask]),
                    is_stable=False)
    src_tok = jnp.where((keys & tok_mask) == tok_mask, jnp.arange(n_rows, dtype=I32) % n_tok,
                        keys & tok_mask)
    tid = jnp.minimum(jnp.arange(n_rows // tm, dtype=I32), total - 1)
    tile_e = jnp.minimum(jnp.sum((tid[:, None] >= tile_end[None, :]).astype(I32), axis=1), N_EXPERTS - 1)
    later = (expert_ids[None, :] > expert_ids[:, None]) & (tiles_e[None, :] > 0)
    next_e = jnp.min(jnp.where(later, expert_ids[None, :], N_EXPERTS), axis=1)
    nxt = lookup(next_e, tile_e)
    nxt_first_tile = lookup(tile_end - tiles_e, jnp.minimum(nxt, N_EXPERTS - 1))
    bounds = jnp.cumsum(jnp.asarray(part_tiles, I32))
    part_end = jnp.min(jnp.where(bounds[None, :] > tid[:, None], bounds[None, :], n_rows), axis=1)
    tile_next = jnp.where((nxt < N_EXPERTS) & (nxt_first_tile < part_end), nxt, -1)
    return (tile_e.astype(I32), tile_next.astype(I32), total.reshape(1).astype(I32), src_tok.astype(I32),
            pos.reshape(-1).astype(I32))


def _pad_rows(a, rows):
    return jnp.concatenate([a, jnp.zeros((rows - a.shape[0],) + a.shape[1:], a.dtype)], axis=0)


def _pad_lanes(a, lanes=LANES):
    return jnp.concatenate([a, jnp.zeros(a.shape[:-1] + (lanes - a.shape[-1],), a.dtype)], axis=-1)


def kernel(x_prompt, mem_prompt, x_sample, state_conformer_conv, state_gdn_conv, state_gdn, cache_mem_k,
           cache_mem_v, w_in, b_glu, w_dw, b_dw, ln_g, ln_b, w_sc, a_log, dt_bias, g_onorm, w_out, g_mix,
           g_xattn, g_mem, w_xq, w_mk, w_mv, w_xo, g_moe, w_router, b_router, w_gu, b_gu, w_dn, b_dn,
           g_final):
    assert w_in.shape[0] == 1, "single-layer configuration"
    batch, seq, _ = x_prompt.shape
    n_s = x_sample.shape[0]
    n_p = batch * seq
    n_all = n_p + n_s
    assert seq % TOKEN_TILE == 0 and n_p % n_s == 0 and n_all < (1 << TOKEN_BITS) - 1
    assert (n_all * TOP_K) % (SC_CORES * SC_SUBCORES * SUBLANES) == 0

    wts = {
        "g_mix": g_mix[0][None], "g_xattn": g_xattn[0][None], "g_moe": g_moe[0][None],
        "g_onorm": g_onorm[0][None],
        "w_in_b": w_in[0].astype(BF16),
        "w_ab_b": _pad_lanes(w_in[0][:, OFF_A:]).astype(BF16),
        "b_glu": b_glu[0][None],
        "w_dw": _pad_rows(w_dw[0], 32), "b_dw": b_dw[0][None], "ln_g": ln_g[0][None], "ln_b": ln_b[0][None],
        "w_sc": _pad_rows(w_sc[0], 8),
        "gdn_cst": _pad_rows(_pad_lanes(jnp.stack([a_log[0], dt_bias[0]])), 8),
        "w_out_b": w_out[0].astype(BF16), "w_xq_b": w_xq[0].astype(BF16), "w_xo_b": w_xo[0].astype(BF16),
        "w_router": _pad_lanes(w_router[0]), "b_router": _pad_lanes(b_router[0][None]),
    }

    mk, mv, mk_b, mv_b = _mem_kv(mem_prompt.reshape(batch * N_MEM, D_MODEL), g_mem[0][None],
                                 w_mk[0].astype(BF16), w_mv[0].astype(BF16))
    xp = x_prompt.reshape(n_p, D_MODEL)
    conv_p, qkv_p, z_p, gb_p, cstate_p, sstate_p = _pre_prompt(xp, batch, seq, wts)
    o_p, gstate_p = _gdn_prompt(qkv_p, gb_p, wts["w_sc"], batch, seq)

    xs = x_sample.reshape(n_s, D_MODEL)
    conv_s, q_s, k_s, v_s, z_s, gb_s, cstate_s, sstate_s = _pre_sample(
        xs, state_conformer_conv[0], state_gdn_conv[0], wts)
    o_s, gstate_s = _gdn_sample(q_s, k_s, v_s, gb_s, state_gdn[0])
    x1_s, qx_s = _mix_sample(xs, conv_s, o_s, z_s, wts)
    att_s = _attn_sample(qx_s, cache_mem_k[0], cache_mem_v[0])
    x2_s, h3_s, route_s, rt_s, counts_s = _route_sample(x1_s, att_s, wts)

    x2_p, h3r, route_p, rt_p, counts_p = _post_prompt(xp, conv_p, o_p, z_p, mk_b, mv_b, h3_s, batch, seq,
                                                      wts)

    counts_p = counts_p[0, :N_EXPERTS].astype(I32)
    counts_s = counts_s[0, :N_EXPERTS].astype(I32)
    idx_s = rt_s[0:TOP_K].astype(I32)
    rank_s = rt_s[2 * TOP_K:3 * TOP_K].astype(I32) + jnp.sum(
        jnp.where(idx_s[..., None] == jnp.arange(N_EXPERTS, dtype=I32), counts_p, 0), axis=-1)
    idx_t = jnp.concatenate([rt_p[0:TOP_K].astype(I32), idx_s], axis=1)
    rank_t = jnp.concatenate([rt_p[2 * TOP_K:3 * TOP_K].astype(I32), rank_s], axis=1)
    min_tiles = -(-(n_all * TOP_K + N_EXPERTS * (MOE_TILE - 1)) // MOE_TILE)
    part_tiles = [_part_tiles(-(-min_tiles * w // sum(MOE_PART_WEIGHTS))) for w in MOE_PART_WEIGHTS]
    first_tiles = [sum(part_tiles[:k]) for k in range(len(part_tiles))]
    n_rows = sum(part_tiles) * MOE_TILE
    tile_e, tile_next, n_tiles, src_tok, pos = _routing_tables(idx_t, rank_t, counts_p + counts_s, n_rows,
                                                               part_tiles)
    pos_t = pos.reshape(TOP_K, n_all)
    xs_parts = [_sc_gather_rows(h3r, src_tok[f * MOE_TILE:(f + t) * MOE_TILE])
                for f, t in zip(first_tiles, part_tiles)]
    ys = None
    for k, f in enumerate(first_tiles):
        ys = _moe(tile_e, tile_next, n_tiles, xs_parts[k], ys, f, n_rows, w_gu[0],
                  b_gu[0][:, None, :], w_dn[0], b_dn[0][:, None, :])
    gfin = g_final[None]
    assert n_p % (sum(TOKEN_PART_WEIGHTS) * COMBINE_TILE) == 0
    tok_parts = [n_p * w // sum(TOKEN_PART_WEIGHTS) for w in TOKEN_PART_WEIGHTS]
    cut = -(-n_s // COMBINE_TILE) * COMBINE_TILE
    sample_part = 0 if (tok_parts[0] > cut and tok_parts[0] % min(COMBINE_TILE, n_s) == 0) \
        else len(tok_parts) - 1
    if sample_part == 0:
        tok_parts[0] -= cut
        tok_parts[-1] += cut
    y_p = None
    for k, n_tok in enumerate(tok_parts):
        tok0 = sum(tok_parts[:k])
        pos_k = pos_t[:, tok0:tok0 + n_tok]
        if k == sample_part:
            pos_k = jnp.concatenate([pos_k, pos_t[:, n_p:]], axis=1)
        ys_tok = _sc_gather_rows(ys, pos_k.reshape(-1)).reshape(TOP_K, pos_k.shape[1], PACKED)
        y_p = _combine(x2_p, route_p, gfin, ys_tok, y_p, tok0, n_tok, 0)
        if k == sample_part:
            y_s = _combine(x2_s, route_s, gfin, ys_tok, None, 0, n_s, n_tok // min(COMBINE_TILE, n_s))

    return (y_p.reshape(batch, seq, D_MODEL), y_s.reshape(n_s, 1, D_MODEL),
            cstate_p[None], sstate_p[None], gstate_p[None],
            mk[None], mv[None],
            cstate_s[None], sstate_s[None], gstate_s[None])
```

```python
import functools

import jax
import jax.numpy as jnp
from jax import lax
from jax.experimental import pallas as pl
from jax.experimental.pallas import tpu as pltpu
from jax.experimental.pallas import tpu_sc as plsc

F32, BF16, I32 = jnp.float32, jnp.bfloat16, jnp.int32

D_MODEL = 1024
CONV_CH = 512
CONV_WIDTH = 31
GDN_HEADS = 4
GDN_DK = 128
GDN_V = 512
QKV_CH = 1536
SHORT_CONV = 4
N_MEM = 256
X_HEADS = 4
X_HEAD_DIM = 256
N_EXPERTS = 32
TOP_K = 4
D_EXPERT = 1024
SWIGLU_LIMIT = 7.0
SWIGLU_ALPHA = 1.702
NORM_EPS = 1e-6
OFF_QKV = 2 * CONV_CH
OFF_Z = OFF_QKV + QKV_CH
OFF_A = OFF_Z + GDN_V

LANES = 128
SUBLANES = 8
GDN_BLOCK = 128
TOKEN_TILE = 512
MOE_TILE = 256
MOE_COLS = 256
MOE_PART_WEIGHTS = (3, 6, 8, 9)
TOKEN_PART_WEIGHTS = (1, 2, 2, 3)
COMBINE_TILE = 256
TOKEN_BITS = 15
ROUTE_ROWS = 16
SC_CORES = 2
SC_SUBCORES = 16
VMEM_LIMIT = 48 * 1024 * 1024


def _cparams(sem, vmem=VMEM_LIMIT):
    return pltpu.CompilerParams(dimension_semantics=sem, vmem_limit_bytes=vmem)


def _mm(a, b):
    return jnp.dot(a.astype(BF16), b.astype(BF16), preferred_element_type=F32)


def _mm_nt(a, b):
    return lax.dot_general(a.astype(BF16), b.astype(BF16), (((1,), (1,)), ((), ())),
                           preferred_element_type=F32)


def _mm_tn(a, b):
    return lax.dot_general(a.astype(BF16), b.astype(BF16), (((0,), (0,)), ((), ())),
                           preferred_element_type=F32)


def _rms(x, g):
    return x * lax.rsqrt(jnp.mean(x * x, axis=-1, keepdims=True) + NORM_EPS) * g


def _silu(x):
    return x * jax.nn.sigmoid(x)


PACKED = D_MODEL // 2


def _pack_rows(h):
    lo = pltpu.bitcast(h[:, :PACKED].astype(BF16).astype(F32), jnp.uint32)
    hi = pltpu.bitcast(h[:, PACKED:].astype(BF16).astype(F32), jnp.uint32)
    return (lo >> 16) | (hi & jnp.uint32(0xFFFF0000))


def _unpack_rows(w):
    lo = pltpu.bitcast(w << 16, F32)
    hi = pltpu.bitcast(w & jnp.uint32(0xFFFF0000), F32)
    return jnp.concatenate([lo, hi], axis=1)


def _full(shape):
    return pl.BlockSpec(shape, lambda *_: (0,) * len(shape))


def _strict_lower(n):
    return jnp.tril(jnp.ones((n, n), BF16), k=-1)


def _project(x, gmix, w_ref, wab_ref, bglu):
    h = _rms(x, gmix).astype(BF16)
    u_glu = jnp.dot(h, w_ref[:, 0:OFF_QKV], preferred_element_type=F32) + bglu
    glu = u_glu[:, :CONV_CH] * jax.nn.sigmoid(u_glu[:, CONV_CH:])
    qkv_pre = jnp.dot(h, w_ref[:, OFF_QKV:OFF_Z], preferred_element_type=F32)
    z = jnp.dot(h, w_ref[:, OFF_Z:OFF_A], preferred_element_type=F32)
    uab = jnp.dot(h, wab_ref[...], preferred_element_type=F32)
    return glu, qkv_pre, z, uab


def _gate_beta(uab, cst):
    lane = lax.broadcasted_iota(I32, uab.shape, 1)
    g = -jnp.exp(cst[0:1, :]) * jax.nn.softplus(uab + cst[1:2, :])
    return jnp.where(lane < GDN_HEADS, g, jax.nn.sigmoid(uab))


def _conv_post(c, b_dw, ln_g, ln_b):
    c = c + b_dw
    mu = jnp.mean(c, axis=-1, keepdims=True)
    d = c - mu
    var = jnp.mean(d * d, axis=-1, keepdims=True)
    return _silu(d * lax.rsqrt(var + NORM_EPS) * ln_g + ln_b)


def _qkv_post(cs):
    a = _silu(cs)
    parts = []
    for h in range(2 * GDN_HEADS):
        seg = a[:, h * GDN_DK:(h + 1) * GDN_DK]
        n = seg * lax.rsqrt(jnp.sum(seg * seg, axis=-1, keepdims=True) + NORM_EPS)
        if h < GDN_HEADS:
            n = n * (GDN_DK ** -0.5)
        parts.append(n)
    q = jnp.concatenate(parts[:GDN_HEADS], axis=1)
    k = jnp.concatenate(parts[GDN_HEADS:], axis=1)
    return q, k, a[:, 2 * GDN_HEADS * GDN_DK:]


def _mix_out(conv_b, o, z, gon, wout_ref, x):
    parts = []
    for h in range(GDN_HEADS):
        oh = o[:, h * 128:(h + 1) * 128]
        parts.append(oh * lax.rsqrt(jnp.mean(oh * oh, axis=-1, keepdims=True) + NORM_EPS) * gon)
    on = jnp.concatenate(parts, axis=1) * _silu(z)
    mixed = (jnp.dot(conv_b, wout_ref[0:CONV_CH, :], preferred_element_type=F32)
             + jnp.dot(on.astype(BF16), wout_ref[CONV_CH:, :], preferred_element_type=F32))
    return x + mixed


def _router_logits(x2, gmoe, wr_ref, br):
    h3 = _rms(x2, gmoe)
    h_hi = h3.astype(BF16)
    r1 = h3 - h_hi.astype(F32)
    h_mid = r1.astype(BF16)
    h_lo = (r1 - h_mid.astype(F32)).astype(BF16)
    w = wr_ref[...]
    w_hi = w.astype(BF16)
    w_lo = (w - w_hi.astype(F32)).astype(BF16)
    logits = (jnp.dot(h_hi, w_hi, preferred_element_type=F32)
              + jnp.dot(h_hi, w_lo, preferred_element_type=F32)
              + jnp.dot(h_mid, w_hi, preferred_element_type=F32)
              + jnp.dot(h_lo, w_hi, preferred_element_type=F32)) + br
    return h3, logits


def _route_finish(logits, carry, before):
    m = logits.shape[0]
    neg = jnp.float32(-jnp.inf)
    n_groups = 4 if m % 32 == 0 else 1
    rows = m // n_groups
    lanes_g = lax.broadcasted_iota(I32, (rows, LANES), 1).astype(F32)
    works = [jnp.where(lanes_g < N_EXPERTS, logits[g * rows:(g + 1) * rows], neg) for g in range(n_groups)]
    vals, idxs = [], []
    for _ in range(TOP_K):
        mxs = [jnp.max(w, axis=-1, keepdims=True) for w in works]
        ixs = [jnp.min(jnp.where(w == mx, lanes_g, float(LANES)), axis=-1, keepdims=True)
               for w, mx in zip(works, mxs)]
        works = [jnp.where(lanes_g == ix, neg, w) for w, ix in zip(works, ixs)]
        vals.append(mxs)
        idxs.append(ixs)
    sels, gates = [], []
    for g in range(n_groups):
        es = [jnp.exp(vals[r][g] - vals[0][g]) for r in range(TOP_K)]
        den = es[0] + es[1] + es[2] + es[3]
        gates.append([e / den for e in es])
        sel_g = jnp.zeros((rows, LANES), F32)
        for r in range(TOP_K):
            sel_g = sel_g + jnp.where(lanes_g == idxs[r][g], 1.0, 0.0)
        sels.append(sel_g)
    sel = jnp.concatenate(sels, axis=0)
    rank_full = jnp.dot(before, sel.astype(BF16), preferred_element_type=F32) + carry
    routes = []
    for g in range(n_groups):
        rank_g = rank_full[g * rows:(g + 1) * rows]
        route_g = jnp.zeros((rows, LANES), F32)
        for r in range(TOP_K):
            rk = jnp.sum(jnp.where(lanes_g == idxs[r][g], rank_g, 0.0), axis=-1, keepdims=True)
            route_g = (route_g + jnp.where(lanes_g == r, idxs[r][g], 0.0)
                       + jnp.where(lanes_g == TOP_K + r, gates[g][r], 0.0)
                       + jnp.where(lanes_g == 2 * TOP_K + r, rk, 0.0))
        routes.append(route_g)
    new_carry = carry + jnp.sum(sel, axis=0, keepdims=True)
    return jnp.concatenate(routes, axis=0), new_carry


def _mem_kv_kernel(mem_ref, g_ref, wk_ref, wv_ref, mk_ref, mv_ref, mkb_ref, mvb_ref):
    m = _rms(mem_ref[...], g_ref[...]).astype(BF16)
    mk = jnp.dot(m, wk_ref[...], preferred_element_type=F32)
    mv = jnp.dot(m, wv_ref[...], preferred_element_type=F32)
    for h in range(X_HEADS):
        sl = slice(h * X_HEAD_DIM, (h + 1) * X_HEAD_DIM)
        mk_ref[0, :, h, :] = mk[:, sl]
        mv_ref[0, :, h, :] = mv[:, sl]
    mkb_ref[...] = mk.astype(BF16)
    mvb_ref[...] = mv.astype(BF16)


def _mem_kv(mem, g_mem, wk_b, wv_b):
    rows = mem.shape[0]
    tm = N_MEM
    row_spec = pl.BlockSpec((tm, D_MODEL), lambda i: (i, 0))
    head_spec = pl.BlockSpec((1, tm, X_HEADS, X_HEAD_DIM), lambda i: (i, 0, 0, 0))
    head_shape = jax.ShapeDtypeStruct((rows // tm, tm, X_HEADS, X_HEAD_DIM), F32)
    return pl.pallas_call(
        _mem_kv_kernel,
        grid=(rows // tm,),
        in_specs=[row_spec, _full((1, D_MODEL)), _full((D_MODEL, D_MODEL)), _full((D_MODEL, D_MODEL))],
        out_specs=[head_spec, head_spec, row_spec, row_spec],
        out_shape=[head_shape, head_shape] + [jax.ShapeDtypeStruct((rows, D_MODEL), BF16)] * 2,
        compiler_params=_cparams(("arbitrary",)),
        name="mem_kv",
    )(mem, g_mem, wk_b, wv_b)


CONV_HALO = 32
SC_HALO = 8


def _pre_prompt_kernel(x_ref, gmix_ref, w_ref, wab_ref, bglu_ref, wdw_ref, bdw_ref, lng_ref, lnb_ref,
                       cst_ref, conv_ref, qkv_ref, z_ref, gb_ref, cstate_ref, sstate_ref, cbuf, *, tm):
    j = pl.program_id(1)

    @pl.when(j == 0)
    def _():
        cbuf[0:CONV_HALO, :] = jnp.zeros((CONV_HALO, CONV_CH), F32)

    glu, qkv_pre, z, uab = _project(x_ref[...], gmix_ref[...], w_ref, wab_ref, bglu_ref[...])
    cbuf[CONV_HALO:CONV_HALO + tm, :] = glu
    qkv_ref[...] = qkv_pre
    z_ref[...] = z
    gb_ref[...] = _gate_beta(uab, cst_ref[...])

    base = CONV_HALO - (CONV_WIDTH - 1)
    rows = CONV_HALO + tm
    accs = []
    for c in range(CONV_CH // LANES):
        lanes = slice(c * LANES, (c + 1) * LANES)
        block = cbuf[:, lanes]
        acc = None
        for r in range(SUBLANES):
            shifted = block if r == 0 else pltpu.roll(block, rows - r, 0)
            for a in range(base, base + CONV_WIDTH):
                if a % SUBLANES == r:
                    t = a - base
                    term = wdw_ref[t:t + 1, lanes] * shifted[a - r:a - r + tm, :]
                    acc = term if acc is None else acc + term
        accs.append(acc)
    acc = jnp.concatenate(accs, axis=1)
    conv_ref[...] = _conv_post(acc, bdw_ref[...], lng_ref[...], lnb_ref[...]).astype(BF16)

    @pl.when(j == pl.num_programs(1) - 1)
    def _():
        cstate_ref[0] = cbuf[pl.ds(CONV_HALO + tm - (CONV_WIDTH - 1), CONV_WIDTH - 1), :]
        sstate_ref[0] = qkv_pre[tm - (SHORT_CONV - 1):, :]

    cbuf[0:CONV_HALO, :] = cbuf[tm:tm + CONV_HALO, :]


def _pre_prompt(x2d, batch, seq, wts):
    tm = TOKEN_TILE
    nj = seq // tm
    rows = batch * seq

    def tok(width):
        return pl.BlockSpec((tm, width), lambda b, j: (b * nj + j, 0))

    in_specs = [tok(D_MODEL), _full((1, D_MODEL)), _full(wts["w_in_b"].shape), _full((D_MODEL, LANES)),
                _full((1, OFF_QKV)), _full((32, CONV_CH)), _full((1, CONV_CH)), _full((1, CONV_CH)),
                _full((1, CONV_CH)), _full((8, LANES))]
    out_specs = [tok(CONV_CH), tok(QKV_CH), tok(GDN_V), tok(LANES),
                 pl.BlockSpec((1, CONV_WIDTH - 1, CONV_CH), lambda b, j: (b, 0, 0)),
                 pl.BlockSpec((1, SHORT_CONV - 1, QKV_CH), lambda b, j: (b, 0, 0))]
    out_shape = [jax.ShapeDtypeStruct((rows, CONV_CH), BF16),
                 jax.ShapeDtypeStruct((rows, QKV_CH), F32),
                 jax.ShapeDtypeStruct((rows, GDN_V), F32),
                 jax.ShapeDtypeStruct((rows, LANES), F32),
                 jax.ShapeDtypeStruct((batch, CONV_WIDTH - 1, CONV_CH), F32),
                 jax.ShapeDtypeStruct((batch, SHORT_CONV - 1, QKV_CH), F32)]
    return pl.pallas_call(
        functools.partial(_pre_prompt_kernel, tm=tm),
        grid=(batch, nj),
        in_specs=in_specs,
        out_specs=out_specs,
        out_shape=out_shape,
        scratch_shapes=[pltpu.VMEM((CONV_HALO + tm, CONV_CH), F32)],
        compiler_params=_cparams(("arbitrary", "arbitrary")),
        name="pre_prompt",
    )(x2d, wts["g_mix"], wts["w_in_b"], wts["w_ab_b"], wts["b_glu"], wts["w_dw"], wts["b_dw"],
      wts["ln_g"], wts["ln_b"], wts["gdn_cst"])


GDN_SEQS = 4


def _gdn_prompt_kernel(qkv_ref, gb_ref, wsc_ref, o_ref, sfin_ref, s_scr, sbuf):
    c = pl.program_id(1)
    n = GDN_BLOCK
    seqs = range(GDN_SEQS)

    @pl.when(c == 0)
    def _():
        s_scr[...] = jnp.zeros(s_scr.shape, F32)
        sbuf[:, 0:SC_HALO, :] = jnp.zeros((GDN_SEQS, SC_HALO, QKV_CH), F32)

    sbase = SC_HALO - (SHORT_CONV - 1)
    qkvs = []
    for b in seqs:
        sbuf[b, SC_HALO:SC_HALO + n, :] = qkv_ref[b]
        block = sbuf[b]
        cs = None
        for t in range(SHORT_CONV):
            r = (sbase + t) % SUBLANES
            shifted = block if r == 0 else pltpu.roll(block, SC_HALO + n - r, 0)
            term = wsc_ref[t:t + 1, :] * shifted[sbase + t - r:sbase + t - r + n, :]
            cs = term if cs is None else cs + term
        qkvs.append(_qkv_post(cs))
        sbuf[b, 0:SC_HALO, :] = sbuf[b, n:n + SC_HALO, :]

    row = lax.broadcasted_iota(I32, (n, n), 0)
    col = lax.broadcasted_iota(I32, (n, n), 1)
    causal = row >= col
    strict = row > col
    tri = jnp.where(causal, 1.0, 0.0).astype(BF16)
    eye = jnp.where(row == col, 1.0, 0.0)
    level_masks = []
    b = 1
    while b < n:
        same_pair = ((row ^ col) & ~(2 * b - 1)) == 0
        level_masks.append(same_pair & ((row & b) != 0) & ((col & b) == 0))
        b *= 2
    gbs, gcums, gcum_ts, egcs = [], [], [], []
    for b in seqs:
        gb = gb_ref[b]
        g1 = gb.astype(BF16)
        r1 = gb - g1.astype(F32)
        g2 = r1.astype(BF16)
        g3 = (r1 - g2.astype(F32)).astype(BF16)
        gcum = (jnp.dot(tri, g1, preferred_element_type=F32) + jnp.dot(tri, g2, preferred_element_type=F32)
                + jnp.dot(tri, g3, preferred_element_type=F32))
        gbs.append(gb)
        gcums.append(gcum)
        gcum_ts.append(gcum.T)
        egcs.append(jnp.exp(gcum))
    units = [(b, h) for b in seqs for h in range(GDN_HEADS)]
    idx = range(len(units))
    sls = [slice(h * GDN_DK, (h + 1) * GDN_DK) for h in range(GDN_HEADS)]
    qs = [qkvs[b][0][:, sls[h]] for b, h in units]
    ks = [qkvs[b][1][:, sls[h]] for b, h in units]
    vs = [qkvs[b][2][:, sls[h]] for b, h in units]
    ss = [s_scr[b, h] for b, h in units]
    gcols = [gcums[b][:, h:h + 1] for b, h in units]
    ecols = [egcs[b][:, h:h + 1] for b, h in units]
    betas = [gbs[b][:, GDN_HEADS + h:GDN_HEADS + h + 1] for b, h in units]
    glasts = [gcums[b][n - 1:n, h:h + 1] for b, h in units]
    decays = [jnp.where(causal, jnp.exp(jnp.where(causal, gcols[u] - gcum_ts[b][h:h + 1, :], 0.0)), 0.0)
              for u, (b, h) in enumerate(units)]
    kbs = [ks[u] * betas[u] for u in idx]
    lowers = [jnp.where(strict, _mm_nt(kbs[u], ks[u]) * decays[u], 0.0) for u in idx]
    intras = [jnp.where(causal, _mm_nt(qs[u], ks[u]) * decays[u], 0.0) for u in idx]
    xs = [eye - jnp.where(level_masks[0], lowers[u], 0.0) for u in idx]
    for mask in level_masks[1:]:
        ts = [_mm(xs[u], jnp.where(mask, lowers[u], 0.0)) for u in idx]
        xs = [xs[u] - _mm(ts[u], xs[u]) for u in idx]
    us = [_mm(xs[u], vs[u] * betas[u]) for u in idx]
    ws = [_mm(xs[u], kbs[u] * ecols[u]) for u in idx]
    v_news = [us[u] - _mm(ws[u], ss[u]) for u in idx]
    os_ = [_mm(qs[u] * ecols[u], ss[u]) + _mm(intras[u], v_news[u]) for u in idx]
    s_news = [ss[u] * jnp.exp(glasts[u]) + _mm_tn(ks[u] * jnp.exp(glasts[u] - gcols[u]), v_news[u])
              for u in idx]
    for u, (b, h) in enumerate(units):
        o_ref[b, :, sls[h]] = os_[u]
        s_scr[b, h] = s_news[u]

    @pl.when(c == pl.num_programs(1) - 1)
    def _():
        sfin_ref[...] = s_scr[...]


def _gdn_prompt(qkv, gb, w_sc, batch, seq):
    n = GDN_BLOCK
    nc = seq // n
    g = GDN_SEQS
    assert batch % g == 0

    def tok(width):
        return pl.BlockSpec((g, n, width), lambda b, c: (b, c, 0))

    state_shape = (g, GDN_HEADS, GDN_DK, GDN_DK)
    o, s_fin = pl.pallas_call(
        _gdn_prompt_kernel,
        grid=(batch // g, nc),
        in_specs=[tok(QKV_CH), tok(LANES), _full(w_sc.shape)],
        out_specs=[tok(GDN_V), pl.BlockSpec(state_shape, lambda b, c: (b, 0, 0, 0))],
        out_shape=[jax.ShapeDtypeStruct((batch, seq, GDN_V), F32),
                   jax.ShapeDtypeStruct((batch, GDN_HEADS, GDN_DK, GDN_DK), F32)],
        scratch_shapes=[pltpu.VMEM(state_shape, F32), pltpu.VMEM((g, SC_HALO + n, QKV_CH), F32)],
        compiler_params=_cparams(("arbitrary", "arbitrary")),
        name="gdn_prompt",
    )(qkv.reshape(batch, seq, QKV_CH), gb.reshape(batch, seq, LANES), w_sc)
    return o.reshape(batch * seq, GDN_V), s_fin


def _post_prompt_kernel(x_ref, conv_ref, o_ref, z_ref, gon_ref, wout_ref, gx_ref, wq_ref, mk_ref, mv_ref,
                        wo_ref, gmoe_ref, wr_ref, br_ref, tri_ref, h3s_ref, x2_ref, h3r_ref, route_ref, rt_ref,
                        cnt_ref, carry, logit_buf, *, n_steps):
    step = pl.program_id(0)

    @pl.when(step == 0)
    def _():
        carry[...] = jnp.zeros(carry.shape, F32)
        logit_buf[...] = jnp.zeros(logit_buf.shape, F32)

    x1 = _mix_out(conv_ref[...], o_ref[...], z_ref[...], gon_ref[...], wout_ref, x_ref[...])
    qx = jnp.dot(_rms(x1, gx_ref[...]).astype(BF16), wq_ref[...], preferred_element_type=F32)

    route, new_carry = _route_finish(logit_buf[...], carry[0:1, :], tri_ref[...])
    route_ref[...] = route
    rt_ref[...] = route.T[0:ROUTE_ROWS, :]
    kept = jnp.where(step >= 1, new_carry, carry[0:1, :])
    carry[0:1, :] = kept
    cnt_ref[...] = jnp.broadcast_to(kept, cnt_ref.shape)

    sls = [slice(h * X_HEAD_DIM, (h + 1) * X_HEAD_DIM) for h in range(X_HEADS)]
    qb = qx.astype(BF16)
    ss = [lax.dot_general(qb[:, sl], mk_ref[:, sl], (((1,), (1,)), ((), ())),
                          preferred_element_type=F32) * (X_HEAD_DIM ** -0.5) for sl in sls]
    es = [jnp.exp(s - jnp.max(s, axis=-1, keepdims=True)) for s in ss]
    ps = [(e / jnp.sum(e, axis=-1, keepdims=True)).astype(BF16) for e in es]
    att = jnp.concatenate(
        [jnp.dot(p, mv_ref[:, sl], preferred_element_type=F32) for p, sl in zip(ps, sls)], axis=1)
    x2 = x1 + jnp.dot(att.astype(BF16), wo_ref[...], preferred_element_type=F32)
    h3, logits = _router_logits(x2, gmoe_ref[...], wr_ref, br_ref[...])
    logit_buf[...] = logits

    @pl.when(step < n_steps)
    def _():
        x2_ref[...] = x2
        h3r_ref[...] = _pack_rows(h3)

    @pl.when(step == n_steps)
    def _():
        h3r_ref[0:h3s_ref.shape[0], :] = h3s_ref[...]


def _post_prompt(x2d, conv, o, z, mk_b, mv_b, h3_sample, batch, seq, wts):
    tm = TOKEN_TILE
    nj = seq // tm
    rows = batch * seq
    n_steps = batch * nj
    n_s = h3_sample.shape[0]
    assert n_s <= tm

    def tok(width):
        return pl.BlockSpec((tm, width), lambda s: (jnp.minimum(s, n_steps - 1), 0))

    mem_spec = pl.BlockSpec((N_MEM, D_MODEL), lambda s: (jnp.minimum(s, n_steps - 1) // nj, 0))
    sq = _full((D_MODEL, D_MODEL))
    in_specs = [tok(D_MODEL), tok(CONV_CH), tok(GDN_V), tok(GDN_V), _full((1, GDN_DK)), sq,
                _full((1, D_MODEL)), sq, mem_spec, mem_spec, sq, _full((1, D_MODEL)),
                _full((D_MODEL, LANES)), _full((1, LANES)), _full((tm, tm)), _full(h3_sample.shape)]
    out_specs = [tok(D_MODEL),
                 pl.BlockSpec((tm, PACKED), lambda s: (s, 0)),
                 pl.BlockSpec((tm, LANES), lambda s: (jnp.maximum(s - 1, 0), 0)),
                 pl.BlockSpec((ROUTE_ROWS, tm), lambda s: (0, jnp.maximum(s - 1, 0))),
                 _full((SUBLANES, LANES))]
    out_shape = [jax.ShapeDtypeStruct((rows, D_MODEL), F32),
                 jax.ShapeDtypeStruct((rows + n_s, PACKED), jnp.uint32),
                 jax.ShapeDtypeStruct((rows, LANES), F32),
                 jax.ShapeDtypeStruct((ROUTE_ROWS, rows), F32),
                 jax.ShapeDtypeStruct((SUBLANES, LANES), F32)]
    return pl.pallas_call(
        functools.partial(_post_prompt_kernel, n_steps=n_steps),
        grid=(n_steps + 1,),
        in_specs=in_specs,
        out_specs=out_specs,
        out_shape=out_shape,
        scratch_shapes=[pltpu.VMEM((SUBLANES, LANES), F32), pltpu.VMEM((tm, LANES), F32)],
        compiler_params=_cparams(("arbitrary",)),
        name="post_prompt",
    )(x2d, conv, o, z, wts["g_onorm"], wts["w_out_b"], wts["g_xattn"], wts["w_xq_b"], mk_b, mv_b,
      wts["w_xo_b"], wts["g_moe"], wts["w_router"], wts["b_router"], _strict_lower(tm), h3_sample)


def _pre_sample_kernel(x_ref, gmix_ref, w_ref, wab_ref, bglu_ref, wdw_ref, bdw_ref, lng_ref, lnb_ref,
                       wsc_ref, cst_ref, chist_ref, shist_ref, conv_ref, q_ref, k_ref, v_ref, z_ref,
                       gb_ref, cnew_ref, snew_ref):
    glu, qkv_pre, z, uab = _project(x_ref[...], gmix_ref[...], w_ref, wab_ref, bglu_ref[...])
    z_ref[...] = z
    gb_ref[...] = _gate_beta(uab, cst_ref[...])
    kw = CONV_WIDTH
    acc = wdw_ref[kw - 1:kw, :] * glu
    for t in range(kw - 1):
        row = chist_ref[:, t, :]
        acc = acc + wdw_ref[t:t + 1, :] * row
        if t >= 1:
            cnew_ref[:, t - 1, :] = row
    cnew_ref[:, kw - 2, :] = glu
    conv_ref[...] = _conv_post(acc, bdw_ref[...], lng_ref[...], lnb_ref[...]).astype(BF16)
    ks = SHORT_CONV
    cs = wsc_ref[ks - 1:ks, :] * qkv_pre
    for t in range(ks - 1):
        row = shist_ref[:, t, :]
        cs = cs + wsc_ref[t:t + 1, :] * row
        if t >= 1:
            snew_ref[:, t - 1, :] = row
    snew_ref[:, ks - 2, :] = qkv_pre
    q, k, v = _qkv_post(cs)
    q_ref[...] = q
    k_ref[...] = k
    v_ref[...] = v


PRE_SAMPLE_TOKENS = 32


def _pre_sample(xs, chist, shist, wts):
    n = xs.shape[0]
    tb = min(PRE_SAMPLE_TOKENS, n)

    def tok(width):
        return pl.BlockSpec((tb, width), lambda i: (i, 0))

    def hist(a):
        return pl.BlockSpec((tb,) + a.shape[1:], lambda i: (i, 0, 0))

    consts = (wts["g_mix"], wts["w_in_b"], wts["w_ab_b"], wts["b_glu"], wts["w_dw"], wts["b_dw"],
              wts["ln_g"], wts["ln_b"], wts["w_sc"], wts["gdn_cst"])
    return pl.pallas_call(
        _pre_sample_kernel,
        grid=(n // tb,),
        in_specs=[tok(D_MODEL)] + [_full(a.shape) for a in consts] + [hist(chist), hist(shist)],
        out_specs=[tok(CONV_CH), tok(GDN_V), tok(GDN_V), tok(GDN_V), tok(GDN_V), tok(LANES),
                   hist(chist), hist(shist)],
        out_shape=[jax.ShapeDtypeStruct((n, CONV_CH), BF16)]
        + [jax.ShapeDtypeStruct((n, GDN_V), F32)] * 4
        + [jax.ShapeDtypeStruct((n, LANES), F32), jax.ShapeDtypeStruct(chist.shape, F32),
           jax.ShapeDtypeStruct(shist.shape, F32)],
        compiler_params=_cparams(("arbitrary",)),
        name="pre_sample",
    )(xs, *consts, chist, shist)


GDN_STEP_TOKENS = 8


def _gdn_sample_kernel(q_ref, k_ref, v_ref, gb_ref, s_ref, o_ref, snew_ref):
    n = GDN_DK
    for i in range(GDN_STEP_TOKENS):
        for h in range(GDN_HEADS):
            sl = slice(h * GDN_DK, (h + 1) * GDN_DK)
            qrow = q_ref[i:i + 1, sl]
            krow = k_ref[i:i + 1, sl]
            vrow = v_ref[i:i + 1, sl]
            g = gb_ref[i:i + 1, h:h + 1]
            beta = gb_ref[i:i + 1, GDN_HEADS + h:GDN_HEADS + h + 1]
            kcol = jnp.broadcast_to(krow, (n, n)).T
            qcol = jnp.broadcast_to(qrow, (n, n)).T
            s1 = s_ref[i, h] * jnp.exp(g)
            sk = jnp.sum(s1 * kcol, axis=0, keepdims=True)
            vt = (vrow - sk) * beta
            s2 = s1 + kcol * vt
            snew_ref[i, h] = s2
            o_ref[i:i + 1, sl] = jnp.sum(s2 * qcol, axis=0, keepdims=True)


def _gdn_sample(q, k, v, gb, state):
    n = q.shape[0]
    tb = GDN_STEP_TOKENS

    def tok(width):
        return pl.BlockSpec((tb, width), lambda i: (i, 0))

    st = pl.BlockSpec((tb, GDN_HEADS, GDN_DK, GDN_DK), lambda i: (i, 0, 0, 0))
    return pl.pallas_call(
        _gdn_sample_kernel,
        grid=(n // tb,),
        in_specs=[tok(GDN_V), tok(GDN_V), tok(GDN_V), tok(LANES), st],
        out_specs=[tok(GDN_V), st],
        out_shape=[jax.ShapeDtypeStruct((n, GDN_V), F32), jax.ShapeDtypeStruct(state.shape, F32)],
        compiler_params=_cparams(("arbitrary",)),
        name="gdn_sample",
    )(q, k, v, gb, state)


def _mix_sample_kernel(x_ref, conv_ref, o_ref, z_ref, gon_ref, wout_ref, gx_ref, wq_ref, x1_ref, qx_ref):
    x1 = _mix_out(conv_ref[...], o_ref[...], z_ref[...], gon_ref[...], wout_ref, x_ref[...])
    x1_ref[...] = x1
    qx_ref[...] = jnp.dot(_rms(x1, gx_ref[...]).astype(BF16), wq_ref[...], preferred_element_type=F32)


def _mix_sample(xs, conv, o, z, wts):
    n = xs.shape[0]
    in_arrays = (xs, conv, o, z, wts["g_onorm"], wts["w_out_b"], wts["g_xattn"], wts["w_xq_b"])
    return pl.pallas_call(
        _mix_sample_kernel,
        grid=(1,),
        in_specs=[_full(a.shape) for a in in_arrays],
        out_specs=[_full((n, D_MODEL))] * 2,
        out_shape=[jax.ShapeDtypeStruct((n, D_MODEL), F32)] * 2,
        compiler_params=_cparams(("arbitrary",)),
        name="mix_sample",
    )(*in_arrays)


ATTN_STEP_TOKENS = 4


def _attn_sample_kernel(qx_ref, ck_ref, cv_ref, att_ref):
    for i in range(ATTN_STEP_TOKENS):
        parts = []
        for h in range(X_HEADS):
            sl = slice(h * X_HEAD_DIM, (h + 1) * X_HEAD_DIM)
            prod = ck_ref[i, :, h, :] * qx_ref[0, i:i + 1, sl]
            s = jnp.sum(prod, axis=-1, keepdims=True) * (X_HEAD_DIM ** -0.5)
            e = jnp.exp(s - jnp.max(s, axis=0, keepdims=True))
            p = e / jnp.sum(e, axis=0, keepdims=True)
            parts.append(jnp.sum(p * cv_ref[i, :, h, :], axis=0, keepdims=True))
        att_ref[0, i:i + 1, :] = jnp.concatenate(parts, axis=1)


def _attn_sample(qx, ck, cv):
    n = qx.shape[0]
    tb = ATTN_STEP_TOKENS
    q3 = qx.reshape(n // tb, tb, D_MODEL)
    qspec = pl.BlockSpec((1, tb, D_MODEL), lambda i: (i, 0, 0))
    cspec = pl.BlockSpec((tb, N_MEM, X_HEADS, X_HEAD_DIM), lambda i: (i, 0, 0, 0))
    out = pl.pallas_call(
        _attn_sample_kernel,
        grid=(n // tb,),
        in_specs=[qspec, cspec, cspec],
        out_specs=qspec,
        out_shape=jax.ShapeDtypeStruct(q3.shape, F32),
        compiler_params=_cparams(("arbitrary",)),
        name="attn_sample",
    )(q3, ck, cv)
    return out.reshape(n, D_MODEL)


def _route_sample_kernel(x1_ref, att_ref, wo_ref, gmoe_ref, wr_ref, br_ref, tri_ref, x2_ref, h3r_ref, route_ref,
                         rt_ref, cnt_ref):
    x2 = x1_ref[...] + jnp.dot(att_ref[...].astype(BF16), wo_ref[...], preferred_element_type=F32)
    x2_ref[...] = x2
    h3, logits = _router_logits(x2, gmoe_ref[...], wr_ref, br_ref[...])
    route, counts = _route_finish(logits, jnp.zeros((1, LANES), F32), tri_ref[...])
    h3r_ref[...] = _pack_rows(h3)
    route_ref[...] = route
    rt_ref[...] = route.T[0:ROUTE_ROWS, :]
    cnt_ref[...] = jnp.broadcast_to(counts, cnt_ref.shape)


def _route_sample(x1, att, wts):
    n = x1.shape[0]
    in_arrays = (x1, att, wts["w_xo_b"], wts["g_moe"], wts["w_router"], wts["b_router"],
                 _strict_lower(n))
    shapes = [(n, D_MODEL), (n, PACKED), (n, LANES), (ROUTE_ROWS, n), (SUBLANES, LANES)]
    dtypes = [F32, jnp.uint32, F32, F32, F32]
    return pl.pallas_call(
        _route_sample_kernel,
        grid=(1,),
        in_specs=[_full(a.shape) for a in in_arrays],
        out_specs=[_full(s) for s in shapes],
        out_shape=[jax.ShapeDtypeStruct(s, d) for s, d in zip(shapes, dtypes)],
        compiler_params=_cparams(("arbitrary",)),
        name="route_sample",
    )(*in_arrays)


def _part_tiles(tiles):
    n_workers = SC_CORES * SC_SUBCORES
    while True:
        rows = tiles * MOE_TILE
        if rows % (n_workers * SUBLANES) == 0 and (_sc_chunk(rows // n_workers, 4 * PACKED) or 0) >= 32:
            return tiles
        tiles += 1


SC_CHUNK_BYTES = 192 * 1024


def _sc_chunk(rows_per_worker, row_bytes):
    for c in range(min(128, SC_CHUNK_BYTES // row_bytes) // SUBLANES * SUBLANES, 0, -SUBLANES):
        if rows_per_worker % c == 0 and (rows_per_worker // c) % 2 == 0:
            return c
    return None


def _sc_gather_rows(table, idx):
    n_workers = SC_CORES * SC_SUBCORES
    b = idx.shape[0]
    assert b % (n_workers * SUBLANES) == 0
    per_worker = b // n_workers
    row_shape = table.shape[1:]
    chunk = _sc_chunk(per_worker, 4 * functools.reduce(lambda a, c: a * c, row_shape, 1))
    assert chunk is not None, per_worker
    mesh = plsc.VectorSubcoreMesh(core_axis_name="c", subcore_axis_name="s")
    rows_t = pltpu.VMEM((chunk,) + row_shape, table.dtype)

    @functools.partial(
        pl.kernel, mesh=mesh,
        out_type=jax.ShapeDtypeStruct((b,) + row_shape, table.dtype),
        scratch_types=[pltpu.VMEM((chunk,), I32), pltpu.VMEM((chunk,), I32), rows_t, rows_t,
                       pltpu.SemaphoreType.DMA, pltpu.SemaphoreType.DMA],
        name="sc_gather_rows",
    )
    def gather(table_hbm, idx_hbm, out_hbm, idx_a, idx_b, rows_a, rows_b, sem_a, sem_b):
        worker = lax.axis_index("s") * SC_CORES + lax.axis_index("c")
        base = worker * per_worker

        @pl.loop(0, per_worker // (2 * chunk))
        def _(c):
            off_a = pl.multiple_of(base + 2 * c * chunk, SUBLANES)
            off_b = pl.multiple_of(off_a + chunk, SUBLANES)
            pltpu.sync_copy(idx_hbm.at[pl.ds(off_a, chunk)], idx_a)
            gather_a = pltpu.async_copy(table_hbm.at[idx_a], rows_a, sem_a)
            pltpu.sync_copy(idx_hbm.at[pl.ds(off_b, chunk)], idx_b)
            gather_b = pltpu.async_copy(table_hbm.at[idx_b], rows_b, sem_b)
            gather_a.wait()
            pltpu.sync_copy(rows_a, out_hbm.at[pl.ds(off_a, chunk)])
            gather_b.wait()
            pltpu.sync_copy(rows_b, out_hbm.at[pl.ds(off_b, chunk)])

    return gather(table, idx)


def _moe_kernel(te_ref, tn_ref, nt_ref, xs_ref, wgu_hbm, bgu_ref, wdn_hbm, bdn_ref, *rest, first_tile):
    ys_ref, wgu_f, wdn_f, wgu_b, wdn_b, sems = rest[-6:]
    step = pl.program_id(0)
    i = first_tile + step
    total = nt_ref[0]

    def weight_copies(e):
        return (pltpu.make_async_copy(wgu_hbm.at[e], wgu_f, sems.at[0]),
                pltpu.make_async_copy(wdn_hbm.at[e], wdn_f, sems.at[1]))

    def start(e):
        for cp in weight_copies(e):
            cp.start()

    @pl.when(i < total)
    def _():
        expert = te_ref[i]
        prev = te_ref[jnp.maximum(i - 1, 0)]
        fresh = jnp.logical_or(step == 0, expert != prev)

        @pl.when(step == 0)
        def _():
            start(expert)

        @pl.when(fresh)
        def _():
            for cp in weight_copies(expert):
                cp.wait()
            wgu_b[...] = wgu_f[...].astype(BF16)
            wdn_b[...] = wdn_f[...].astype(BF16)
            nxt = tn_ref[i]

            @pl.when(nxt >= 0)
            def _():
                start(nxt)

        x = _unpack_rows(xs_ref[...]).astype(BF16)

        def up(c):
            glu_cols = slice(c * MOE_COLS, (c + 1) * MOE_COLS)
            lin_cols = slice(D_EXPERT + c * MOE_COLS, D_EXPERT + (c + 1) * MOE_COLS)
            return (jnp.dot(x, wgu_b[:, glu_cols], preferred_element_type=F32) + bgu_ref[expert, :, glu_cols],
                    jnp.dot(x, wgu_b[:, lin_cols], preferred_element_type=F32) + bgu_ref[expert, :, lin_cols])

        n_chunks = D_EXPERT // MOE_COLS
        nxt = up(0)
        y = None
        for c in range(n_chunks):
            g, lin = nxt
            if c + 1 < n_chunks:
                nxt = up(c + 1)
            x_glu = jnp.minimum(g, SWIGLU_LIMIT)
            x_lin = jnp.clip(lin, -SWIGLU_LIMIT, SWIGLU_LIMIT)
            act = x_glu * jax.nn.sigmoid(SWIGLU_ALPHA * x_glu) * (x_lin + 1.0)
            part = jnp.dot(act.astype(BF16), wdn_b[c * MOE_COLS:(c + 1) * MOE_COLS, :],
                           preferred_element_type=F32)
            y = part if y is None else y + part
        ys_ref[...] = _pack_rows(y + bdn_ref[expert])

    @pl.when(i >= total)
    def _():
        ys_ref[...] = jnp.zeros(ys_ref.shape, jnp.uint32)


def _moe(tile_e, tile_next, n_tiles, xs_part, ys_prev, first_tile, n_rows, w_gu, b_gu, w_dn, b_dn):
    tm = MOE_TILE

    def resident(a):
        return pl.BlockSpec(a.shape, lambda i, te, tn, nt: (0, 0, 0))

    hbm = pl.BlockSpec(memory_space=pl.ANY)
    in_specs = [pl.BlockSpec((tm, PACKED), lambda i, te, tn, nt: (i, 0)),
                hbm, resident(b_gu), hbm, resident(b_dn)]
    operands = [tile_e, tile_next, n_tiles, xs_part, w_gu, b_gu, w_dn, b_dn]
    aliases = {}
    if ys_prev is not None:
        in_specs.append(hbm)
        aliases = {len(operands): 0}
        operands.append(ys_prev)
    grid_spec = pltpu.PrefetchScalarGridSpec(
        num_scalar_prefetch=3,
        grid=(xs_part.shape[0] // tm,),
        in_specs=in_specs,
        out_specs=pl.BlockSpec((tm, PACKED), lambda i, te, tn, nt: (first_tile + i, 0)),
        scratch_shapes=[pltpu.VMEM((D_MODEL, 2 * D_EXPERT), F32),
                        pltpu.VMEM((D_EXPERT, D_MODEL), F32),
                        pltpu.VMEM((D_MODEL, 2 * D_EXPERT), BF16),
                        pltpu.VMEM((D_EXPERT, D_MODEL), BF16),
                        pltpu.SemaphoreType.DMA((2,))],
    )
    return pl.pallas_call(
        functools.partial(_moe_kernel, first_tile=first_tile),
        grid_spec=grid_spec,
        out_shape=jax.ShapeDtypeStruct((n_rows, PACKED), jnp.uint32),
        input_output_aliases=aliases,
        compiler_params=_cparams(("arbitrary",)),
        name="moe",
    )(*operands)


def _combine_kernel(x2_ref, route_ref, gfin_ref, yt_ref, *rest):
    y_ref = rest[-1]
    route = route_ref[...]
    acc = x2_ref[...]
    for j in range(TOP_K):
        acc = acc + route[:, TOP_K + j:TOP_K + j + 1] * _unpack_rows(yt_ref[j])
    y_ref[...] = _rms(acc, gfin_ref[...])


def _combine(x2, route, g_final, ys_tok, y_prev, tok0, n_tok, ys_block0):
    tc = min(COMBINE_TILE, n_tok)
    b0 = tok0 // tc

    def tok(width):
        return pl.BlockSpec((tc, width), lambda i: (b0 + i, 0))

    in_specs = [tok(D_MODEL), tok(LANES), pl.BlockSpec((1, D_MODEL), lambda i: (0, 0)),
                pl.BlockSpec((TOP_K, tc, PACKED), lambda i: (0, ys_block0 + i, 0))]
    operands = [x2, route, g_final, ys_tok]
    aliases = {}
    if y_prev is not None:
        in_specs.append(pl.BlockSpec(memory_space=pl.ANY))
        aliases = {len(operands): 0}
        operands.append(y_prev)
    return pl.pallas_call(
        _combine_kernel,
        grid=(n_tok // tc,),
        in_specs=in_specs,
        out_specs=tok(D_MODEL),
        out_shape=jax.ShapeDtypeStruct(x2.shape, F32),
        input_output_aliases=aliases,
        compiler_params=_cparams(("arbitrary",)),
        name="combine",
    )(*operands)


def _routing_tables(idx_t, rank_t, counts, n_rows, part_tiles):
    tm = MOE_TILE
    n_tok = idx_t.shape[1]
    n_assign = TOP_K * n_tok
    tok_mask = (1 << TOKEN_BITS) - 1
    tiles_e = (counts + tm - 1) // tm
    tile_end = jnp.cumsum(tiles_e)
    row_start = (tile_end - tiles_e) * tm
    total = tile_end[-1]
    expert_ids = jnp.arange(N_EXPERTS, dtype=I32)

    def lookup(table, e):
        return jnp.sum(jnp.where(e[..., None] == expert_ids, table, 0), axis=-1)

    pos = lookup(row_start, idx_t) + rank_t
    keys_real = (idx_t * (1 << TOKEN_BITS) + jnp.arange(n_tok, dtype=I32)[None, :]).reshape(-1)
    k = jnp.arange(n_rows - n_assign, dtype=I32)
    pad_e, pad_s = k // tm, k % tm
    pad_needed = lookup(tiles_e * tm - counts, pad_e)
    pad_key_e = jnp.where((pad_e < N_EXPERTS) & (pad_s < pad_needed), pad_e, N_EXPERTS)
    keys = lax.sort(jnp.concatenate([keys_real, pad_key_e * (1 << TOKEN_BITS) + tok_mask]),
                    is_stable=False)
    src_tok = jnp.where((keys & tok_mask) == tok_mask, jnp.arange(n_rows, dtype=I32) % n_tok,
                        keys & tok_mask)
    tid = jnp.minimum(jnp.arange(n_rows // tm, dtype=I32), total - 1)
    tile_e = jnp.minimum(jnp.sum((tid[:, None] >= tile_end[None, :]).astype(I32), axis=1), N_EXPERTS - 1)
    later = (expert_ids[None, :] > expert_ids[:, None]) & (tiles_e[None, :] > 0)
    next_e = jnp.min(jnp.where(later, expert_ids[None, :], N_EXPERTS), axis=1)
    nxt = lookup(next_e, tile_e)
    nxt_first_tile = lookup(tile_end - tiles_e, jnp.minimum(nxt, N_EXPERTS - 1))
    bounds = jnp.cumsum(jnp.asarray(part_tiles, I32))
    part_end = jnp.min(jnp.where(bounds[None, :] > tid[:, None], bounds[None, :], n_rows), axis=1)
    tile_next = jnp.where((nxt < N_EXPERTS) & (nxt_first_tile < part_end), nxt, -1)
    return (tile_e.astype(I32), tile_next.astype(I32), total.reshape(1).astype(I32), src_tok.astype(I32),
            pos.reshape(-1).astype(I32))


def _pad_rows(a, rows):
    return jnp.concatenate([a, jnp.zeros((rows - a.shape[0],) + a.shape[1:], a.dtype)], axis=0)


def _pad_lanes(a, lanes=LANES):
    return jnp.concatenate([a, jnp.zeros(a.shape[:-1] + (lanes - a.shape[-1],), a.dtype)], axis=-1)


def kernel(x_prompt, mem_prompt, x_sample, state_conformer_conv, state_gdn_conv, state_gdn, cache_mem_k,
           cache_mem_v, w_in, b_glu, w_dw, b_dw, ln_g, ln_b, w_sc, a_log, dt_bias, g_onorm, w_out, g_mix,
           g_xattn, g_mem, w_xq, w_mk, w_mv, w_xo, g_moe, w_router, b_router, w_gu, b_gu, w_dn, b_dn,
           g_final):
    assert w_in.shape[0] == 1, "single-layer configuration"
    batch, seq, _ = x_prompt.shape
    n_s = x_sample.shape[0]
    n_p = batch * seq
    n_all = n_p + n_s
    assert seq % TOKEN_TILE == 0 and n_p % n_s == 0 and n_all < (1 << TOKEN_BITS) - 1
    assert (n_all * TOP_K) % (SC_CORES * SC_SUBCORES * SUBLANES) == 0

    wts = {
        "g_mix": g_mix[0][None], "g_xattn": g_xattn[0][None], "g_moe": g_moe[0][None],
        "g_onorm": g_onorm[0][None],
        "w_in_b": w_in[0].astype(BF16),
        "w_ab_b": _pad_lanes(w_in[0][:, OFF_A:]).astype(BF16),
        "b_glu": b_glu[0][None],
        "w_dw": _pad_rows(w_dw[0], 32), "b_dw": b_dw[0][None], "ln_g": ln_g[0][None], "ln_b": ln_b[0][None],
        "w_sc": _pad_rows(w_sc[0], 8),
        "gdn_cst": _pad_rows(_pad_lanes(jnp.stack([a_log[0], dt_bias[0]])), 8),
        "w_out_b": w_out[0].astype(BF16), "w_xq_b": w_xq[0].astype(BF16), "w_xo_b": w_xo[0].astype(BF16),
        "w_router": _pad_lanes(w_router[0]), "b_router": _pad_lanes(b_router[0][None]),
    }

    mk, mv, mk_b, mv_b = _mem_kv(mem_prompt.reshape(batch * N_MEM, D_MODEL), g_mem[0][None],
                                 w_mk[0].astype(BF16), w_mv[0].astype(BF16))
    xp = x_prompt.reshape(n_p, D_MODEL)
    conv_p, qkv_p, z_p, gb_p, cstate_p, sstate_p = _pre_prompt(xp, batch, seq, wts)
    o_p, gstate_p = _gdn_prompt(qkv_p, gb_p, wts["w_sc"], batch, seq)

    xs = x_sample.reshape(n_s, D_MODEL)
    conv_s, q_s, k_s, v_s, z_s, gb_s, cstate_s, sstate_s = _pre_sample(
        xs, state_conformer_conv[0], state_gdn_conv[0], wts)
    o_s, gstate_s = _gdn_sample(q_s, k_s, v_s, gb_s, state_gdn[0])
    x1_s, qx_s = _mix_sample(xs, conv_s, o_s, z_s, wts)
    att_s = _attn_sample(qx_s, cache_mem_k[0], cache_mem_v[0])
    x2_s, h3_s, route_s, rt_s, counts_s = _route_sample(x1_s, att_s, wts)

    x2_p, h3r, route_p, rt_p, counts_p = _post_prompt(xp, conv_p, o_p, z_p, mk_b, mv_b, h3_s, batch, seq,
                                                      wts)

    counts_p = counts_p[0, :N_EXPERTS].astype(I32)
    counts_s = counts_s[0, :N_EXPERTS].astype(I32)
    idx_s = rt_s[0:TOP_K].astype(I32)
    rank_s = rt_s[2 * TOP_K:3 * TOP_K].astype(I32) + jnp.sum(
        jnp.where(idx_s[..., None] == jnp.arange(N_EXPERTS, dtype=I32), counts_p, 0), axis=-1)
    idx_t = jnp.concatenate([rt_p[0:TOP_K].astype(I32), idx_s], axis=1)
    rank_t = jnp.concatenate([rt_p[2 * TOP_K:3 * TOP_K].astype(I32), rank_s], axis=1)
    min_tiles = -(-(n_all * TOP_K + N_EXPERTS * (MOE_TILE - 1)) // MOE_TILE)
    part_tiles = [_part_tiles(-(-min_tiles * w // sum(MOE_PART_WEIGHTS))) for w in MOE_PART_WEIGHTS]
    first_tiles = [sum(part_tiles[:k]) for k in range(len(part_tiles))]
    n_rows = sum(part_tiles) * MOE_TILE
    tile_e, tile_next, n_tiles, src_tok, pos = _routing_tables(idx_t, rank_t, counts_p + counts_s, n_rows,
                                                               part_tiles)
    pos_t = pos.reshape(TOP_K, n_all)
    xs_parts = [_sc_gather_rows(h3r, src_tok[f * MOE_TILE:(f + t) * MOE_TILE])
                for f, t in zip(first_tiles, part_tiles)]
    ys = None
    for k, f in enumerate(first_tiles):
        ys = _moe(tile_e, tile_next, n_tiles, xs_parts[k], ys, f, n_rows, w_gu[0],
                  b_gu[0][:, None, :], w_dn[0], b_dn[0][:, None, :])
    gfin = g_final[None]
    assert n_p % (sum(TOKEN_PART_WEIGHTS) * COMBINE_TILE) == 0
    tok_parts = [n_p * w // sum(TOKEN_PART_WEIGHTS) for w in TOKEN_PART_WEIGHTS]
    cut = -(-n_s // COMBINE_TILE) * COMBINE_TILE
    sample_part = 0 if (tok_parts[0] > cut and tok_parts[0] % min(COMBINE_TILE, n_s) == 0) \
        else len(tok_parts) - 1
    if sample_part == 0:
        tok_parts[0] -= cut
        tok_parts[-1] += cut
    y_p = None
    for k, n_tok in enumerate(tok_parts):
        tok0 = sum(tok_parts[:k])
        pos_k = pos_t[:, tok0:tok0 + n_tok]
        if k == sample_part:
            pos_k = jnp.concatenate([pos_k, pos_t[:, n_p:]], axis=1)
        ys_tok = _sc_gather_rows(ys, pos_k.reshape(-1)).reshape(TOP_K, pos_k.shape[1], PACKED)
        y_p = _combine(x2_p, route_p, gfin, ys_tok, y_p, tok0, n_tok, 0)
        if k == sample_part:
            y_s = _combine(x2_s, route_s, gfin, ys_tok, None, 0, n_s, n_tok // min(COMBINE_TILE, n_s))

    return (y_p.reshape(batch, seq, D_MODEL), y_s.reshape(n_s, 1, D_MODEL),
            cstate_p[None], sstate_p[None], gstate_p[None],
            mk[None], mv[None],
            cstate_s[None], sstate_s[None], gstate_s[None])
```
